```python
import math
import jax
import jax.numpy as jnp
from jax import lax
import numpy as np

D_MODEL = 1024
BATCH = 4
SEQ = 4096
DEPTH = 1

CTX_LEN = 256
GRID_W = 64
HEAD_DIM = 128
N_HEADS_GDN = 4
N_HEADS_RET = 4
W_GDN = N_HEADS_GDN * HEAD_DIM
W_RET = N_HEADS_RET * HEAD_DIM
MIX_WIDTH = W_GDN + W_RET
IN_COLS = 4 * W_GDN + 4 * N_HEADS_GDN + 4 * W_RET
CONV_K = 5
CHUNK = 64
D_FF = ((8 * D_MODEL + 3 * 256 - 1) // (3 * 256)) * 256
ROPE_THETA = 10000.0
ROPE_SEQ_PAIRS = 16
ROPE_ROW_PAIRS = 24
ROPE_COL_PAIRS = 24
HALF = HEAD_DIM // 2
NORM_EPS = 1e-6

kernel_name = "hybrid_gdn_retention_dit_block"


def rms_norm(x, g):
    xf = x.astype(jnp.float32)
    y = xf * lax.rsqrt(jnp.mean(xf * xf, axis=-1, keepdims=True) + NORM_EPS)
    return (y * g.astype(jnp.float32)).astype(x.dtype)


def head_group_norm(t, g):
    mu = jnp.mean(t, axis=-1, keepdims=True)
    var = jnp.mean(jnp.square(t - mu), axis=-1, keepdims=True)
    return (t - mu) * lax.rsqrt(var + NORM_EPS) * g.astype(jnp.float32)


def l2_normalize(t):
    return t * lax.rsqrt(jnp.sum(t * t, axis=-1, keepdims=True) + NORM_EPS)


def modulate(h, shift, scale):
    return h * (1.0 + scale) + shift


def adaln(cond, w, b):
    mod = (jax.nn.silu(cond) @ w + b)[:, None, :]
    return jnp.split(mod, 6, axis=-1)


def swiglu(h, w_in, w_out):
    gate, up = jnp.split(h @ w_in, 2, axis=-1)
    return (jax.nn.silu(gate) * up) @ w_out


def axis_angles(pos, n_pairs):
    inv_freq = ROPE_THETA ** (-jnp.arange(n_pairs, dtype=jnp.float32) / n_pairs)
    return pos[:, None] * inv_freq[None, :]


def rope_tables(rows):
    n_lat = rows * GRID_W
    row = jnp.repeat(jnp.arange(rows, dtype=jnp.float32), GRID_W)
    col = jnp.tile(jnp.arange(GRID_W, dtype=jnp.float32), rows)
    zeros_ctx = jnp.zeros((CTX_LEN,), jnp.float32)
    p_seq = jnp.concatenate([jnp.arange(CTX_LEN, dtype=jnp.float32),
                             jnp.full((n_lat,), float(CTX_LEN), jnp.float32)])
    p_row = jnp.concatenate([zeros_ctx, row])
    p_col = jnp.concatenate([zeros_ctx, col])
    ang = jnp.concatenate([axis_angles(p_seq, ROPE_SEQ_PAIRS),
                           axis_angles(p_row, ROPE_ROW_PAIRS),
                           axis_angles(p_col, ROPE_COL_PAIRS)], axis=-1)
    return jnp.cos(ang), jnp.sin(ang)


def apply_rope(t, cos, sin):
    c = cos[None, :, None, :]
    s = sin[None, :, None, :]
    t1, t2 = t[..., :HALF], t[..., HALF:]
    return jnp.concatenate([t1 * c - t2 * s, t1 * s + t2 * c], axis=-1)


def to_bwd(t):
    return jnp.concatenate([jnp.flip(t[:, :CTX_LEN], axis=1), jnp.flip(t[:, CTX_LEN:], axis=1)], axis=1)


def short_conv(u, w):
    n = u.shape[1]
    pad = (CONV_K - 1) // 2
    up = jnp.pad(u, ((0, 0), (pad, pad), (0, 0)))
    y = up[:, 0:n] * w[0]
    for i in range(1, CONV_K):
        y = y + up[:, i:i + n] * w[i]
    return jax.nn.silu(y)


def to_chunks(t):
    b, tl, h = t.shape[:3]
    t = t.reshape((b, tl // CHUNK, CHUNK, h) + t.shape[3:])
    return t.transpose((1, 0, 3, 2) + tuple(range(4, t.ndim)))


def from_chunks(o):
    nc, b, h, cl, d = o.shape
    return o.transpose(1, 0, 3, 2, 4).reshape(b, nc * cl, h, d)


def gated_delta_rule(q, k, v, g, beta):
    b, tl, h, dk = q.shape
    dv = v.shape[-1]
    qc, kc, vc = to_chunks(q), to_chunks(k), to_chunks(v)
    gc, bc = to_chunks(g), to_chunks(beta)
    G = jnp.cumsum(gc, axis=-1)
    idx = jnp.arange(CHUNK)
    incl = idx[:, None] >= idx[None, :]
    strict = idx[:, None] > idx[None, :]
    diff = G[..., :, None] - G[..., None, :]
    decay = jnp.where(incl, jnp.exp(jnp.where(incl, diff, 0.0)), 0.0)
    kb = kc * bc[..., None]
    A = jnp.where(strict, jnp.einsum('nbhid,nbhjd->nbhij', kb, kc) * decay, 0.0)
    rhs = jnp.concatenate([vc * bc[..., None], kb * jnp.exp(G)[..., None]], axis=-1)
    sol = lax.linalg.triangular_solve(A + jnp.eye(CHUNK, dtype=A.dtype), rhs,
                                      left_side=True, lower=True, unit_diagonal=True)
    u, w = sol[..., :dv], sol[..., dv:]
    qk = jnp.einsum('nbhid,nbhjd->nbhij', qc, kc) * decay
    q_dec = qc * jnp.exp(G)[..., None]
    k_dec = kc * jnp.exp(G[..., -1:] - G)[..., None]
    g_last = jnp.exp(G[..., -1])[..., None, None]

    def step(S, xs):
        u_c, w_c, q_c, qk_c, k_c, gl = xs
        v_new = u_c - jnp.einsum('bhck,bhkv->bhcv', w_c, S)
        o = jnp.einsum('bhck,bhkv->bhcv', q_c, S) + jnp.einsum('bhij,bhjv->bhiv', qk_c, v_new)
        S = gl * S + jnp.einsum('bhck,bhcv->bhkv', k_c, v_new)
        return S, o

    S0 = jnp.zeros((b, h, dk, dv), q.dtype)
    _, o = lax.scan(step, S0, (u, w, q_dec, qk, k_dec, g_last))
    return from_chunks(o)


def retention_chunkwise(q, k, v, log_gamma):
    b, tl, h, dk = q.shape
    dv = v.shape[-1]
    qc, kc, vc = to_chunks(q), to_chunks(k), to_chunks(v)
    pos = jnp.arange(CHUNK, dtype=jnp.float32)
    lg = log_gamma.astype(jnp.float32)[:, None]
    diff = pos[:, None] - pos[None, :]
    decay = jnp.where(diff >= 0, jnp.exp(lg[:, :, None] * jnp.maximum(diff, 0.0)), 0.0)
    xi = jnp.exp(lg * (pos + 1.0))
    zeta = jnp.exp(lg * (CHUNK - 1.0 - pos))
    g_chunk = jnp.exp(lg * CHUNK)[..., None]
    inner = jnp.einsum('nbhij,nbhjv->nbhiv', jnp.einsum('nbhid,nbhjd->nbhij', qc, kc) * decay, vc)

    def step(R, xs):
        q_c, k_c, v_c = xs
        o = jnp.einsum('bhik,bhkv->bhiv', q_c * xi[..., None], R)
        R = g_chunk * R + jnp.einsum('bhjk,bhjv->bhkv', k_c * zeta[..., None], v_c)
        return R, o

    R0 = jnp.zeros((b, h, dk, dv), q.dtype)
    _, cross = lax.scan(step, R0, (qc, kc, vc))
    return from_chunks(inner + cross)


def hybrid_mixer(h_ctx, h_lat, w_in, conv_w, gdn_a_log, gdn_dt_bias, gdn_norm_g,
                 ret_decay_logit, ret_norm_g, cos, sin):
    out_dtype = h_lat.dtype
    f32 = jnp.float32
    h = jnp.concatenate([h_ctx, h_lat], axis=1)
    p = (h @ w_in).astype(f32)
    b, tl, _ = p.shape
    splits = [3 * W_GDN, 4 * W_GDN, 4 * W_GDN + 4 * N_HEADS_GDN,
              4 * W_GDN + 4 * N_HEADS_GDN + W_RET,
              4 * W_GDN + 4 * N_HEADS_GDN + 2 * W_RET,
              4 * W_GDN + 4 * N_HEADS_GDN + 3 * W_RET]
    qkv_a, z, ab, rq, rk, rv, rg = jnp.split(p, splits, axis=-1)

    cw = conv_w.astype(f32)
    qkv_a = jnp.concatenate([short_conv(qkv_a[:, :CTX_LEN], cw), short_conv(qkv_a[:, CTX_LEN:], cw)], axis=1)
    q, k, v = [t.reshape(b, tl, N_HEADS_GDN, HEAD_DIM) for t in jnp.split(qkv_a, 3, axis=-1)]
    q = l2_normalize(q) * (HEAD_DIM ** -0.5)
    k = l2_normalize(k)
    a_f, a_b, b_f, b_b = jnp.split(ab, 4, axis=-1)
    A = jnp.exp(gdn_a_log.astype(f32))
    dtb = gdn_dt_bias.astype(f32)
    g_f = -A[0] * jax.nn.softplus(a_f + dtb[0])
    g_b = -A[1] * jax.nn.softplus(a_b + dtb[1])
    o_f = gated_delta_rule(q, k, v, g_f, jax.nn.sigmoid(b_f))
    o_b = to_bwd(gated_delta_rule(to_bwd(q), to_bwd(k), to_bwd(v), to_bwd(g_b), to_bwd(jax.nn.sigmoid(b_b))))
    o_gdn = rms_norm(o_f + o_b, gdn_norm_g) * jax.nn.silu(z.reshape(b, tl, N_HEADS_GDN, HEAD_DIM))

    rq = apply_rope(rq.reshape(b, tl, N_HEADS_RET, HEAD_DIM), cos, sin)
    rk = apply_rope(rk.reshape(b, tl, N_HEADS_RET, HEAD_DIM), cos, sin) * (HEAD_DIM ** -0.5)
    rv = rv.reshape(b, tl, N_HEADS_RET, HEAD_DIM)
    lg = jax.nn.log_sigmoid(ret_decay_logit.astype(f32))
    r_f = retention_chunkwise(rq, rk, rv, lg[0])
    r_b = to_bwd(retention_chunkwise(to_bwd(rq), to_bwd(rk), to_bwd(rv), lg[1]))
    o_ret = head_group_norm(r_f + r_b, ret_norm_g) * jax.nn.silu(rg.reshape(b, tl, N_HEADS_RET, HEAD_DIM))

    y = jnp.concatenate([o_gdn.reshape(b, tl, W_GDN), o_ret.reshape(b, tl, W_RET)], axis=-1)
    return y.astype(out_dtype)


def setup_inputs(seed: int = 0) -> dict:
    key = jax.random.key(seed)
    ks = jax.random.split(key, 19)
    f32 = jnp.float32

    def nrm(k, shape, scale):
        return scale * jax.random.normal(k, shape, f32)

    x = nrm(ks[0], (BATCH, SEQ, D_MODEL), 1.0)
    c = nrm(ks[1], (BATCH, D_MODEL), 1.0)
    ctx = nrm(ks[2], (BATCH, CTX_LEN, D_MODEL), 1.0)
    c_ctx = nrm(ks[3], (D_MODEL,), 1.0)
    ada_w = nrm(ks[4], (DEPTH, D_MODEL, 6 * D_MODEL), D_MODEL ** -0.5)
    ada_b = nrm(ks[5], (DEPTH, 6 * D_MODEL), 0.02)
    norm_mix_g = 1.0 + nrm(ks[6], (DEPTH, D_MODEL), 0.05)
    norm_ffn_g = 1.0 + nrm(ks[7], (DEPTH, D_MODEL), 0.05)
    w_in = nrm(ks[8], (DEPTH, D_MODEL, IN_COLS), D_MODEL ** -0.5)
    conv_w = nrm(ks[9], (DEPTH, CONV_K, 3 * W_GDN), CONV_K ** -0.5)
    gdn_a_log = jnp.log(jax.random.uniform(ks[10], (DEPTH, 2, N_HEADS_GDN), f32, 1.0, 16.0))
    dt = jnp.exp(jax.random.uniform(ks[11], (DEPTH, 2, N_HEADS_GDN), f32, math.log(1e-3), math.log(1e-1)))
    gdn_dt_bias = dt + jnp.log(-jnp.expm1(-dt))
    gdn_norm_g = 1.0 + nrm(ks[12], (DEPTH, HEAD_DIM), 0.05)
    heads = jnp.arange(N_HEADS_RET, dtype=f32)
    ret_decay_logit = jnp.log(2.0 ** (5.0 + heads) - 1.0) + nrm(ks[13], (DEPTH, 2, N_HEADS_RET), 0.01)
    ret_norm_g = 1.0 + nrm(ks[14], (DEPTH, HEAD_DIM), 0.05)
    w_out = nrm(ks[15], (DEPTH, MIX_WIDTH, D_MODEL), MIX_WIDTH ** -0.5)
    w_ffn_in = nrm(ks[16], (DEPTH, D_MODEL, 2 * D_FF), D_MODEL ** -0.5)
    w_ffn_out = nrm(ks[17], (DEPTH, D_FF, D_MODEL), D_FF ** -0.5)
    final_g = 1.0 + nrm(ks[18], (D_MODEL,), 0.05)
    return {"x": x, "c": c, "ctx": ctx, "c_ctx": c_ctx, "ada_w": ada_w, "ada_b": ada_b,
            "norm_mix_g": norm_mix_g, "norm_ffn_g": norm_ffn_g, "w_in": w_in, "conv_w": conv_w,
            "gdn_a_log": gdn_a_log, "gdn_dt_bias": gdn_dt_bias, "gdn_norm_g": gdn_norm_g,
            "ret_decay_logit": ret_decay_logit, "ret_norm_g": ret_norm_g, "w_out": w_out,
            "w_ffn_in": w_ffn_in, "w_ffn_out": w_ffn_out, "final_g": final_g}


def reference(x, c, ctx, c_ctx, ada_w, ada_b, norm_mix_g, norm_ffn_g, w_in, conv_w,
              gdn_a_log, gdn_dt_bias, gdn_norm_g, ret_decay_logit, ret_norm_g, w_out,
              w_ffn_in, w_ffn_out, final_g):
    n_lat = x.shape[1]
    rows = n_lat // GRID_W
    cos, sin = rope_tables(rows)
    x_ctx = ctx
    for layer in range(DEPTH):
        sh1, sc1, g1, sh2, sc2, g2 = adaln(c, ada_w[layer], ada_b[layer])
        csh1, csc1, cg1, csh2, csc2, cg2 = adaln(c_ctx[None, :], ada_w[layer], ada_b[layer])
        h_lat = modulate(rms_norm(x, norm_mix_g[layer]), sh1, sc1)
        h_ctx = modulate(rms_norm(x_ctx, norm_mix_g[layer]), csh1, csc1)
        y = hybrid_mixer(h_ctx, h_lat, w_in[layer], conv_w[layer], gdn_a_log[layer],
                         gdn_dt_bias[layer], gdn_norm_g[layer], ret_decay_logit[layer],
                         ret_norm_g[layer], cos, sin)
        x = x + g1 * (y[:, CTX_LEN:] @ w_out[layer])
        x = x + g2 * swiglu(modulate(rms_norm(x, norm_ffn_g[layer]), sh2, sc2),
                            w_ffn_in[layer], w_ffn_out[layer])
        if layer < DEPTH - 1:
            x_ctx = x_ctx + cg1 * (y[:, :CTX_LEN] @ w_out[layer])
            x_ctx = x_ctx + cg2 * swiglu(modulate(rms_norm(x_ctx, norm_ffn_g[layer]), csh2, csc2),
                                         w_ffn_in[layer], w_ffn_out[layer])
    return rms_norm(x, final_g)
```

```python
import functools
import math

import jax
import jax.numpy as jnp
from jax import lax
from jax.experimental import pallas as pl
from jax.experimental.pallas import tpu as pltpu

F32 = jnp.float32
BF16 = jnp.bfloat16

HEAD_DIM = 128
N_HEADS = 4
GROUP_W = N_HEADS * HEAD_DIM
CONV_K = 5
GRID_W = 64
ROPE_THETA = 10000.0
ROPE_PAIRS = (16, 24, 24)
NORM_EPS = 1e-6

CHUNK = 64
TILE = 256
CHUNKS_PER_TILE = TILE // CHUNK
HALO = 8
GATE_LANES = 128
FFN_TOK = 512
VMEM_LIMIT = 56 * 1024 * 1024


def _cparams(sem):
    return pltpu.CompilerParams(dimension_semantics=sem, vmem_limit_bytes=VMEM_LIMIT)


def _dot(a, b):
    return jnp.dot(a, b, preferred_element_type=F32)


def _dot_nt(a, b):
    return lax.dot_general(a, b, (((1,), (1,)), ((), ())), preferred_element_type=F32)


def _split(a):
    hi = a.astype(BF16)
    lo = (a - hi.astype(F32)).astype(BF16)
    return hi, lo


def _silu(x):
    return x * jax.nn.sigmoid(x)


def _lane_bcast(col, width):
    return jnp.broadcast_to(col, (col.shape[0], width))


def _ada_kernel(cond_ref, w_ref, b_ref, o_ref):
    a_hi, a_lo = _split(_silu(cond_ref[...]))
    w_hi, w_lo = _split(w_ref[...])
    o_ref[...] = _dot(a_hi, w_hi) + _dot(a_lo, w_hi) + _dot(a_hi, w_lo) + b_ref[...]


def _ada(cond, w, b):
    rows, d = cond.shape
    n = w.shape[1]
    bn = 1536
    return pl.pallas_call(
        _ada_kernel,
        grid=(n // bn,),
        in_specs=[pl.BlockSpec((rows, d), lambda j: (0, 0)),
                  pl.BlockSpec((d, bn), lambda j: (0, j)),
                  pl.BlockSpec((1, bn), lambda j: (0, j))],
        out_specs=pl.BlockSpec((rows, bn), lambda j: (0, j)),
        out_shape=jax.ShapeDtypeStruct((rows, n), F32),
        compiler_params=_cparams(("parallel",)),
        name="ada",
    )(cond, w, b)


def _rms_mod(x, g, shift, scale):
    ms = jnp.mean(x * x, axis=-1, keepdims=True)
    return (x * lax.rsqrt(ms + NORM_EPS) * g) * (1.0 + scale) + shift


def _inproj_kernel(x_ref, ctx_ref, mod_ref, g_ref, wa_ref, wab_ref, wr_ref, cos_ref, sin_ref,
                   pa_ref, ab_ref, pr_ref, *, d_model):
    t = pl.program_id(1)
    xin = jnp.where(t == 0, ctx_ref[0], x_ref[0])
    mod = mod_ref[0]
    h = _rms_mod(xin, g_ref[...], mod[:, 0:d_model], mod[:, d_model:2 * d_model]).astype(BF16)
    pa_ref[0] = _dot(h, wa_ref[...])
    ab_ref[0] = _dot(h, wab_ref[...])
    pr = _dot(h, wr_ref[...])
    cos2 = cos_ref[...]
    sin2 = sin_ref[...]
    k_scale = HEAD_DIM ** -0.5
    for hd in range(2 * N_HEADS):
        lo = hd * HEAD_DIM
        tt = pr[:, lo:lo + HEAD_DIM]
        rot = tt * cos2 + pltpu.roll(tt, HEAD_DIM // 2, 1) * sin2
        if hd >= N_HEADS:
            rot = rot * k_scale
        pr_ref[0, :, lo:lo + HEAD_DIM] = rot
    pr_ref[0, :, 2 * GROUP_W:] = pr[:, 2 * GROUP_W:]


def _inproj(x, ctx, mod3, g, wa, wab, wr, cos2, sin2):
    b, seq, d = x.shape
    n_tiles = (ctx.shape[1] + seq) // TILE
    t_total = n_tiles * TILE
    return pl.pallas_call(
        functools.partial(_inproj_kernel, d_model=d),
        grid=(b, n_tiles),
        in_specs=[pl.BlockSpec((1, TILE, d), lambda i, t: (i, jnp.maximum(t - 1, 0), 0)),
                  pl.BlockSpec((1, TILE, d), lambda i, t: (i, 0, 0)),
                  pl.BlockSpec((1, 1, mod3.shape[2]), lambda i, t: (jnp.where(t == 0, b, i), 0, 0)),
                  pl.BlockSpec((1, d), lambda i, t: (0, 0)),
                  pl.BlockSpec(wa.shape, lambda i, t: (0, 0)),
                  pl.BlockSpec(wab.shape, lambda i, t: (0, 0)),
                  pl.BlockSpec(wr.shape, lambda i, t: (0, 0)),
                  pl.BlockSpec((TILE, HEAD_DIM), lambda i, t: (t, 0)),
                  pl.BlockSpec((TILE, HEAD_DIM), lambda i, t: (t, 0))],
        out_specs=[pl.BlockSpec((1, TILE, wa.shape[1]), lambda i, t: (i, t, 0)),
                   pl.BlockSpec((1, TILE, GATE_LANES), lambda i, t: (i, t, 0)),
                   pl.BlockSpec((1, TILE, wr.shape[1]), lambda i, t: (i, t, 0))],
        out_shape=[jax.ShapeDtypeStruct((b, t_total, wa.shape[1]), F32),
                   jax.ShapeDtypeStruct((b, t_total, GATE_LANES), F32),
                   jax.ShapeDtypeStruct((b, t_total, wr.shape[1]), F32)],
        compiler_params=_cparams(("parallel", "parallel")),
        name="inproj",
    )(x, ctx, mod3, g, wa, wab, wr, cos2, sin2)


def _conv_kernel(cur_ref, prev_ref, next_ref, w_ref, o_ref, ext_ref):
    t = pl.program_id(1)
    last = pl.num_programs(1) - 1
    has_prev = t >= 2
    has_next = jnp.logical_and(t >= 1, t < last)
    ext_ref[0:HALO, :] = jnp.where(has_prev, prev_ref[0], 0.0)
    ext_ref[HALO:HALO + TILE, :] = cur_ref[0]
    ext_ref[HALO + TILE:, :] = jnp.where(has_next, next_ref[0], 0.0)
    pad = (CONV_K - 1) // 2
    q_scale = HEAD_DIM ** -0.5
    for part in range(3):
        for hd in range(N_HEADS):
            lo = part * GROUP_W + hd * HEAD_DIM
            acc = None
            for i in range(CONV_K):
                r0 = HALO - pad + i
                term = ext_ref[r0:r0 + TILE, lo:lo + HEAD_DIM] * w_ref[i:i + 1, lo:lo + HEAD_DIM]
                acc = term if acc is None else acc + term
            y = _silu(acc)
            if part < 2:
                y = y * lax.rsqrt(jnp.sum(y * y, axis=-1, keepdims=True) + NORM_EPS)
            if part == 0:
                y = y * q_scale
            o_ref[0, :, lo:lo + HEAD_DIM] = y


def _conv(pa, conv_w):
    b, t_total, _ = pa.shape
    n_tiles = t_total // TILE
    width = 3 * GROUP_W
    per = TILE // HALO
    n_halo = t_total // HALO
    return pl.pallas_call(
        _conv_kernel,
        grid=(b, n_tiles),
        in_specs=[pl.BlockSpec((1, TILE, width), lambda i, t: (i, t, 0)),
                  pl.BlockSpec((1, HALO, width), lambda i, t: (i, jnp.maximum(t * per - 1, 0), 0)),
                  pl.BlockSpec((1, HALO, width), lambda i, t: (i, jnp.minimum((t + 1) * per, n_halo - 1), 0)),
                  pl.BlockSpec((CONV_K, width), lambda i, t: (0, 0))],
        out_specs=pl.BlockSpec((1, TILE, width), lambda i, t: (i, t, 0)),
        out_shape=jax.ShapeDtypeStruct((b, t_total, width), F32),
        scratch_shapes=[pltpu.VMEM((TILE + 2 * HALO, width), F32)],
        compiler_params=_cparams(("parallel", "parallel")),
        name="conv",
    )(pa, pa, pa, conv_w)


def _packed_matmul(x, y, bd_mask):
    yb = y.astype(BF16)
    y_bd = jnp.where(bd_mask, jnp.concatenate([yb] * N_HEADS, axis=0), jnp.zeros((), BF16))
    return _dot(x.astype(BF16), y_bd)


def _unit_lower_inverse(a, eye, m16, m32, m64, bd_mask):
    pm = functools.partial(_packed_matmul, bd_mask=bd_mask)
    p = -jnp.where(m16, a, 0.0)
    x = eye + p
    for _ in range(3):
        p = pm(p, p)
        x = x + pm(x, p)
    for m in (m32, m64):
        off = jnp.where(m, a, 0.0)
        x = x - pm(x, pm(off, x))
    return x


def _gdn_chunk(q, k, v, gates, s_ref, slot0, rev, consts):
    tcum, mstrict, incl4, strict4, eye4, m16, m32, m64, bd_mask = consts
    g_lane0 = N_HEADS if rev else 0
    b_lane0 = 2 * N_HEADS + (N_HEADS if rev else 0)
    last = 0 if rev else CHUNK - 1

    g_hi, g_lo = _split(gates)
    gc = _dot(tcum, g_hi) + _dot(tcum, g_lo)
    g_last = gc[last:last + 1, :]
    e_g = jnp.exp(gc)
    e_gl = jnp.exp(g_last - gc)
    e_last = jnp.exp(g_last)

    y = jnp.concatenate(
        [jnp.where(mstrict, _lane_bcast(gates[:, g_lane0 + h:g_lane0 + h + 1], CHUNK), 0.0)
         for h in range(N_HEADS)], axis=1)
    y_hi, y_lo = _split(y)
    e = jnp.exp(_dot(tcum, y_hi) + _dot(tcum, y_lo))

    qk_parts, kk_parts, kb_parts, beta_cols = [], [], [], []
    for h in range(N_HEADS):
        lo = h * HEAD_DIM
        beta = _lane_bcast(gates[:, b_lane0 + h:b_lane0 + h + 1], HEAD_DIM)
        k_h = k[:, lo:lo + HEAD_DIM]
        kb_h = k_h * beta
        lhs = jnp.concatenate([q[:, lo:lo + HEAD_DIM], kb_h], axis=0).astype(BF16)
        m = _dot_nt(lhs, k_h.astype(BF16))
        qk_parts.append(m[:CHUNK])
        kk_parts.append(m[CHUNK:])
        kb_parts.append(kb_h)
        beta_cols.append(beta)
    a = jnp.where(strict4, jnp.concatenate(kk_parts, axis=1) * e, 0.0)
    qkd = jnp.where(incl4, jnp.concatenate(qk_parts, axis=1) * e, 0.0)
    t_inv = _unit_lower_inverse(a, eye4, m16, m32, m64, bd_mask)

    outs = []
    for h in range(N_HEADS):
        lo = h * HEAD_DIM
        gl = g_lane0 + h
        eg_h = _lane_bcast(e_g[:, gl:gl + 1], HEAD_DIM)
        egl_h = _lane_bcast(e_gl[:, gl:gl + 1], HEAD_DIM)
        rhs = jnp.concatenate([v[:, lo:lo + HEAD_DIM] * beta_cols[h], kb_parts[h] * eg_h], axis=1)
        sol = _dot(t_inv[:, h * CHUNK:(h + 1) * CHUNK].astype(BF16), rhs.astype(BF16))
        u_h = sol[:, :HEAD_DIM]
        w_h = sol[:, HEAD_DIM:]
        s = s_ref[slot0 + h]
        ws = _dot(jnp.concatenate([w_h, q[:, lo:lo + HEAD_DIM] * eg_h], axis=0).astype(BF16), s.astype(BF16))
        v_new = u_h - ws[:CHUNK]
        v_new_b = v_new.astype(BF16)
        outs.append(ws[CHUNK:] + _dot(qkd[:, h * CHUNK:(h + 1) * CHUNK].astype(BF16), v_new_b))
        kd_t = (k[:, lo:lo + HEAD_DIM] * egl_h).T.astype(BF16)
        s_ref[slot0 + h] = e_last[:, gl:gl + 1] * s + _dot(kd_t, v_new_b)
    return jnp.concatenate(outs, axis=1)


def _gate_tile(ab, dtb, a_log):
    z = ab + dtb
    softplus = jnp.maximum(z, 0.0) + jnp.log(1.0 + jnp.exp(-jnp.abs(z)))
    lane = lax.broadcasted_iota(jnp.int32, ab.shape, 1)
    return jnp.where(lane < 2 * N_HEADS, -jnp.exp(a_log) * softplus, jax.nn.sigmoid(ab))


def _gdn_kernel(qf_ref, kf_ref, vf_ref, abf_ref, qb_ref, kb_ref, vb_ref, abb_ref, dtb_ref, alog_ref,
                of_ref, ob_ref, s_ref):
    @pl.when(pl.program_id(1) == 0)
    def _():
        s_ref[...] = jnp.zeros_like(s_ref)

    ri = lax.broadcasted_iota(jnp.int32, (CHUNK, CHUNK), 0)
    ci = lax.broadcasted_iota(jnp.int32, (CHUNK, CHUNK), 1)
    ri4 = lax.broadcasted_iota(jnp.int32, (CHUNK, N_HEADS * CHUNK), 0)
    ci4 = lax.broadcasted_iota(jnp.int32, (CHUNK, N_HEADS * CHUNK), 1) % CHUNK
    rbd = lax.broadcasted_iota(jnp.int32, (N_HEADS * CHUNK, N_HEADS * CHUNK), 0) // CHUNK
    cbd = lax.broadcasted_iota(jnp.int32, (N_HEADS * CHUNK, N_HEADS * CHUNK), 1) // CHUNK
    bd_mask = rbd == cbd
    eye4 = (ri4 == ci4).astype(F32)
    m16 = (ri4 // 16) == (ci4 // 16)
    m32 = jnp.logical_and((ri4 // 32) == (ci4 // 32), jnp.logical_not(m16))
    m64 = (ri4 // 32) != (ci4 // 32)

    def consts(rev):
        if rev:
            return ((ci >= ri).astype(BF16), ri < ci, ri4 <= ci4, ri4 < ci4, eye4, m16, m32, m64, bd_mask)
        return ((ci <= ri).astype(BF16), ri > ci, ri4 >= ci4, ri4 > ci4, eye4, m16, m32, m64, bd_mask)

    cf = consts(False)
    cb = consts(True)
    dtb = dtb_ref[...]
    alog = alog_ref[...]

    def body(j, carry):
        rf = pl.multiple_of(j * CHUNK, CHUNK)
        rb = pl.multiple_of((CHUNKS_PER_TILE - 1 - j) * CHUNK, CHUNK)
        gf = _gate_tile(abf_ref[0, pl.ds(rf, CHUNK), :], dtb, alog)
        of_ref[0, pl.ds(rf, CHUNK), :] = _gdn_chunk(
            qf_ref[0, pl.ds(rf, CHUNK), :], kf_ref[0, pl.ds(rf, CHUNK), :], vf_ref[0, pl.ds(rf, CHUNK), :],
            gf, s_ref, 0, False, cf)
        gb = _gate_tile(abb_ref[0, pl.ds(rb, CHUNK), :], dtb, alog)
        ob_ref[0, pl.ds(rb, CHUNK), :] = _gdn_chunk(
            qb_ref[0, pl.ds(rb, CHUNK), :], kb_ref[0, pl.ds(rb, CHUNK), :], vb_ref[0, pl.ds(rb, CHUNK), :],
            gb, s_ref, N_HEADS, True, cb)
        return carry

    lax.fori_loop(0, CHUNKS_PER_TILE, body, 0)


def _bwd_tile(s, n_tiles):
    return jnp.where(s == 0, 0, n_tiles - s)


def _gdn(qkv, ab, dtb_row, alog_row):
    b, t_total, _ = qkv.shape
    n_tiles = t_total // TILE

    def col(c, bwd):
        if bwd:
            return pl.BlockSpec((1, TILE, GROUP_W), lambda i, s: (i, _bwd_tile(s, n_tiles), c))
        return pl.BlockSpec((1, TILE, GROUP_W), lambda i, s: (i, s, c))

    def gate(bwd):
        if bwd:
            return pl.BlockSpec((1, TILE, GATE_LANES), lambda i, s: (i, _bwd_tile(s, n_tiles), 0))
        return pl.BlockSpec((1, TILE, GATE_LANES), lambda i, s: (i, s, 0))

    row = pl.BlockSpec((1, GATE_LANES), lambda i, s: (0, 0))
    return pl.pallas_call(
        _gdn_kernel,
        grid=(b, n_tiles),
        in_specs=[col(0, False), col(1, False), col(2, False), gate(False),
                  col(0, True), col(1, True), col(2, True), gate(True), row, row],
        out_specs=[col(0, False), col(0, True)],
        out_shape=[jax.ShapeDtypeStruct((b, t_total, GROUP_W), F32)] * 2,
        scratch_shapes=[pltpu.VMEM((2 * N_HEADS, HEAD_DIM, HEAD_DIM), F32)],
        compiler_params=_cparams(("parallel", "arbitrary")),
        name="gdn",
    )(qkv, qkv, qkv, ab, qkv, qkv, qkv, ab, dtb_row, alog_row)


def _ret_dir(q, k, v, lg_row, r_ref, slot0, rev, o_ref):
    ri = lax.broadcasted_iota(jnp.int32, (TILE, TILE), 0)
    ci = lax.broadcasted_iota(jnp.int32, (TILE, TILE), 1)
    diff = ((ci - ri) if rev else (ri - ci)).astype(F32)
    pos = lax.broadcasted_iota(jnp.int32, (TILE, 1), 0).astype(F32)
    if rev:
        pos = (TILE - 1.0) - pos
    for h in range(N_HEADS):
        lo = h * HEAD_DIM
        lg = lg_row[:, slot0 + h:slot0 + h + 1]
        decay = jnp.where(diff >= 0, jnp.exp(lg * jnp.maximum(diff, 0.0)), 0.0)
        xi = jnp.exp(lg * (pos + 1.0))
        zeta = jnp.exp(lg * ((TILE - 1.0) - pos))
        g_chunk = jnp.exp(lg * float(TILE))
        q_h = q[:, lo:lo + HEAD_DIM]
        k_h = k[:, lo:lo + HEAD_DIM]
        v_b = v[:, lo:lo + HEAD_DIM].astype(BF16)
        r = r_ref[slot0 + h]
        qk = _dot_nt(q_h.astype(BF16), k_h.astype(BF16)) * decay
        o_ref[0, :, lo:lo + HEAD_DIM] = _dot(qk.astype(BF16), v_b) + _dot((q_h * xi).astype(BF16), r.astype(BF16))
        r_ref[slot0 + h] = g_chunk * r + _dot((k_h * zeta).T.astype(BF16), v_b)


def _ret_kernel(qf_ref, kf_ref, vf_ref, qb_ref, kb_ref, vb_ref, logit_ref, of_ref, ob_ref, r_ref):
    @pl.when(pl.program_id(1) == 0)
    def _():
        r_ref[...] = jnp.zeros_like(r_ref)

    x = logit_ref[...]
    lg_row = jnp.minimum(x, 0.0) - jnp.log(1.0 + jnp.exp(-jnp.abs(x)))
    _ret_dir(qf_ref[0], kf_ref[0], vf_ref[0], lg_row, r_ref, 0, False, of_ref)
    _ret_dir(qb_ref[0], kb_ref[0], vb_ref[0], lg_row, r_ref, N_HEADS, True, ob_ref)


def _ret(pr, logit_row):
    b, t_total, _ = pr.shape
    n_tiles = t_total // TILE

    def col(c, bwd):
        if bwd:
            return pl.BlockSpec((1, TILE, GROUP_W), lambda i, s: (i, _bwd_tile(s, n_tiles), c))
        return pl.BlockSpec((1, TILE, GROUP_W), lambda i, s: (i, s, c))

    return pl.pallas_call(
        _ret_kernel,
        grid=(b, n_tiles),
        in_specs=[col(0, False), col(1, False), col(2, False), col(0, True), col(1, True), col(2, True),
                  pl.BlockSpec((1, GATE_LANES), lambda i, s: (0, 0))],
        out_specs=[col(0, False), col(0, True)],
        out_shape=[jax.ShapeDtypeStruct((b, t_total, GROUP_W), F32)] * 2,
        scratch_shapes=[pltpu.VMEM((2 * N_HEADS, HEAD_DIM, HEAD_DIM), F32)],
        compiler_params=_cparams(("parallel", "arbitrary")),
        name="ret",
    )(pr, pr, pr, pr, pr, pr, logit_row)


def _out_kernel(gf_ref, gb_ref, z_ref, rf_ref, rb_ref, rg_ref, x_ref, mod_ref, gdn_g_ref, ret_g_ref,
                nffn_g_ref, w_ref, x1_ref, h2_ref, y_ref, *, d_model):
    o = gf_ref[0] + gb_ref[0]
    r = rf_ref[0] + rb_ref[0]
    z = z_ref[0]
    rg = rg_ref[0]
    for h in range(N_HEADS):
        lo = h * HEAD_DIM
        o_h = o[:, lo:lo + HEAD_DIM]
        o_n = o_h * lax.rsqrt(jnp.mean(o_h * o_h, axis=-1, keepdims=True) + NORM_EPS) * gdn_g_ref[...]
        y_ref[:, lo:lo + HEAD_DIM] = (o_n * _silu(z[:, lo:lo + HEAD_DIM])).astype(BF16)
        r_h = r[:, lo:lo + HEAD_DIM]
        mu = jnp.mean(r_h, axis=-1, keepdims=True)
        cen = r_h - mu
        var = jnp.mean(cen * cen, axis=-1, keepdims=True)
        r_n = cen * lax.rsqrt(var + NORM_EPS) * ret_g_ref[...]
        y_ref[:, GROUP_W + lo:GROUP_W + lo + HEAD_DIM] = (r_n * _silu(rg[:, lo:lo + HEAD_DIM])).astype(BF16)
    mod = mod_ref[0]
    x1 = x_ref[0] + mod[:, 2 * d_model:3 * d_model] * _dot(y_ref[...], w_ref[...])
    x1_ref[0] = x1
    h2_ref[0] = _rms_mod(x1, nffn_g_ref[...], mod[:, 3 * d_model:4 * d_model],
                         mod[:, 4 * d_model:5 * d_model]).astype(BF16)


def _out(gdn_f, gdn_b, pa, ret_f, ret_b, pr, x, mod3, gdn_g, ret_g, nffn_g, w_out):
    b, seq, d = x.shape
    n_lat = seq // TILE
    first = (pa.shape[1] - seq) // TILE

    def tok(c):
        return pl.BlockSpec((1, TILE, GROUP_W), lambda i, t: (i, t + first, c))

    vec = pl.BlockSpec((1, HEAD_DIM), lambda i, t: (0, 0))
    return pl.pallas_call(
        functools.partial(_out_kernel, d_model=d),
        grid=(b, n_lat),
        in_specs=[tok(0), tok(0), tok(3), tok(0), tok(0), tok(3),
                  pl.BlockSpec((1, TILE, d), lambda i, t: (i, t, 0)),
                  pl.BlockSpec((1, 1, mod3.shape[2]), lambda i, t: (i, 0, 0)),
                  vec, vec,
                  pl.BlockSpec((1, d), lambda i, t: (0, 0)),
                  pl.BlockSpec(w_out.shape, lambda i, t: (0, 0))],
        out_specs=[pl.BlockSpec((1, TILE, d), lambda i, t: (i, t, 0)),
                   pl.BlockSpec((1, TILE, d), lambda i, t: (i, t, 0))],
        out_shape=[jax.ShapeDtypeStruct((b, seq, d), F32), jax.ShapeDtypeStruct((b, seq, d), BF16)],
        scratch_shapes=[pltpu.VMEM((TILE, 2 * GROUP_W), BF16)],
        compiler_params=_cparams(("parallel", "parallel")),
        name="outproj",
    )(gdn_f, gdn_b, pa, ret_f, ret_b, pr, x, mod3, gdn_g, ret_g, nffn_g, w_out)


def _ffn_kernel(h_ref, x1_ref, mod_ref, wg_ref, wu_ref, wo_ref, fg_ref, o_ref, acc_ref, *, d_model):
    f = pl.program_id(2)

    @pl.when(f == 0)
    def _():
        acc_ref[...] = jnp.zeros_like(acc_ref)

    h = h_ref[0]
    act = (_silu(_dot(h, wg_ref[...])) * _dot(h, wu_ref[...])).astype(BF16)
    acc_ref[...] += _dot(act, wo_ref[...])

    @pl.when(f == pl.num_programs(2) - 1)
    def _():
        x2 = x1_ref[0] + mod_ref[0][:, 5 * d_model:6 * d_model] * acc_ref[...]
        ms = jnp.mean(x2 * x2, axis=-1, keepdims=True)
        o_ref[0] = x2 * lax.rsqrt(ms + NORM_EPS) * fg_ref[...]


def _ffn(h2, x1, mod3, w_in, w_out, final_g):
    b, seq, d = x1.shape
    d_ff = w_out.shape[0]
    n_f = 2
    bf = d_ff // n_f
    return pl.pallas_call(
        functools.partial(_ffn_kernel, d_model=d),
        grid=(b, seq // FFN_TOK, n_f),
        in_specs=[pl.BlockSpec((1, FFN_TOK, d), lambda i, t, f: (i, t, 0)),
                  pl.BlockSpec((1, FFN_TOK, d), lambda i, t, f: (i, t, 0)),
                  pl.BlockSpec((1, 1, mod3.shape[2]), lambda i, t, f: (i, 0, 0)),
                  pl.BlockSpec((d, bf), lambda i, t, f: (0, f)),
                  pl.BlockSpec((d, bf), lambda i, t, f: (0, f + n_f)),
                  pl.BlockSpec((bf, d), lambda i, t, f: (f, 0)),
                  pl.BlockSpec((1, d), lambda i, t, f: (0, 0))],
        out_specs=pl.BlockSpec((1, FFN_TOK, d), lambda i, t, f: (i, t, 0)),
        out_shape=jax.ShapeDtypeStruct((b, seq, d), F32),
        scratch_shapes=[pltpu.VMEM((FFN_TOK, d), F32)],
        compiler_params=_cparams(("parallel", "parallel", "arbitrary")),
        name="ffn",
    )(h2, x1, mod3, w_in, w_in, w_out, final_g)


def _rope_tables(ctx_len, n_lat):
    def angles(pos, n_pairs):
        inv = ROPE_THETA ** (-jnp.arange(n_pairs, dtype=F32) / n_pairs)
        return pos[:, None] * inv[None, :]

    rows = n_lat // GRID_W
    row = jnp.repeat(jnp.arange(rows, dtype=F32), GRID_W)
    col = jnp.tile(jnp.arange(GRID_W, dtype=F32), rows)
    zeros = jnp.zeros((ctx_len,), F32)
    p_seq = jnp.concatenate([jnp.arange(ctx_len, dtype=F32), jnp.full((n_lat,), float(ctx_len), F32)])
    ang = jnp.concatenate([angles(p_seq, ROPE_PAIRS[0]),
                           angles(jnp.concatenate([zeros, row]), ROPE_PAIRS[1]),
                           angles(jnp.concatenate([zeros, col]), ROPE_PAIRS[2])], axis=-1)
    cos, sin = jnp.cos(ang), jnp.sin(ang)
    return jnp.concatenate([cos, cos], axis=-1), jnp.concatenate([-sin, sin], axis=-1)


def _lane_row(values):
    flat = values.reshape(1, -1).astype(F32)
    return jnp.pad(flat, ((0, 0), (0, GATE_LANES - flat.shape[1])))


def kernel(x, c, ctx, c_ctx, ada_w, ada_b, norm_mix_g, norm_ffn_g, w_in, conv_w, gdn_a_log, gdn_dt_bias,
           gdn_norm_g, ret_decay_logit, ret_norm_g, w_out, w_ffn_in, w_ffn_out, final_g):
    assert ada_w.shape[0] == 1, "single-layer block"
    b, seq, d = x.shape
    ctx_len = ctx.shape[1]
    assert ctx_len == TILE and seq % FFN_TOK == 0 and b + 1 <= 8

    cond = jnp.concatenate([c, c_ctx[None, :], jnp.zeros((8 - b - 1, d), F32)], axis=0)
    mod = _ada(cond, ada_w[0], ada_b)
    mod3 = mod[:, None, :]

    n_qkvz = 4 * GROUP_W
    n_gate = 4 * N_HEADS
    w = w_in[0]
    wa = w[:, :n_qkvz].astype(BF16)
    wab = jnp.pad(w[:, n_qkvz:n_qkvz + n_gate], ((0, 0), (0, GATE_LANES - n_gate))).astype(BF16)
    wr = w[:, n_qkvz + n_gate:].astype(BF16)
    cos2, sin2 = _rope_tables(ctx_len, seq)
    pa, ab, pr = _inproj(x, ctx, mod3, norm_mix_g, wa, wab, wr, cos2, sin2)

    qkv = _conv(pa, conv_w[0])
    dtb_row = _lane_row(gdn_dt_bias[0])
    gdn_f, gdn_b = _gdn(qkv, ab, dtb_row, _lane_row(gdn_a_log[0]))
    ret_f, ret_b = _ret(pr, _lane_row(ret_decay_logit[0]))

    x1, h2 = _out(gdn_f, gdn_b, pa, ret_f, ret_b, pr, x, mod3, gdn_norm_g, ret_norm_g, norm_ffn_g,
                  w_out[0].astype(BF16))
    return _ffn(h2, x1, mod3, w_ffn_in[0].astype(BF16), w_ffn_out[0].astype(BF16), final_g[None, :])
```

```python
import functools
import math

import jax
import jax.numpy as jnp
from jax import lax
from jax.experimental import pallas as pl
from jax.experimental.pallas import tpu as pltpu

F32 = jnp.float32
BF16 = jnp.bfloat16

HEAD_DIM = 128
N_HEADS = 4
GROUP_W = N_HEADS * HEAD_DIM
CONV_K = 5
GRID_W = 64
ROPE_THETA = 10000.0
ROPE_PAIRS = (16, 24, 24)
NORM_EPS = 1e-6

CHUNK = 64
TILE = 256
CHUNKS_PER_TILE = TILE // CHUNK
HALO = 8
GATE_LANES = 128
FFN_TOK = 512
VMEM_LIMIT = 56 * 1024 * 1024


def _cparams(sem):
    return pltpu.CompilerParams(dimension_semantics=sem, vmem_limit_bytes=VMEM_LIMIT)


def _dot(a, b):
    return jnp.dot(a, b, preferred_element_type=F32)


def _dot_nt(a, b):
    return lax.dot_general(a, b, (((1,), (1,)), ((), ())), preferred_element_type=F32)


def _split(a):
    hi = a.astype(BF16)
    lo = (a - hi.astype(F32)).astype(BF16)
    return hi, lo


def _silu(x):
    return x * jax.nn.sigmoid(x)


def _lane_bcast(col, width):
    return jnp.broadcast_to(col, (col.shape[0], width))


def _ada_kernel(cond_ref, w_ref, b_ref, o_ref):
    a_hi, a_lo = _split(_silu(cond_ref[...]))
    w_hi, w_lo = _split(w_ref[...])
    o_ref[...] = _dot(a_hi, w_hi) + _dot(a_lo, w_hi) + _dot(a_hi, w_lo) + b_ref[...]


def _ada(cond, w, b):
    rows, d = cond.shape
    n = w.shape[1]
    bn = 1536
    return pl.pallas_call(
        _ada_kernel,
        grid=(n // bn,),
        in_specs=[pl.BlockSpec((rows, d), lambda j: (0, 0)),
                  pl.BlockSpec((d, bn), lambda j: (0, j)),
                  pl.BlockSpec((1, bn), lambda j: (0, j))],
        out_specs=pl.BlockSpec((rows, bn), lambda j: (0, j)),
        out_shape=jax.ShapeDtypeStruct((rows, n), F32),
        compiler_params=_cparams(("parallel",)),
        name="ada",
    )(cond, w, b)


def _rms_mod(x, g, shift, scale):
    ms = jnp.mean(x * x, axis=-1, keepdims=True)
    return (x * lax.rsqrt(ms + NORM_EPS) * g) * (1.0 + scale) + shift


def _inproj_kernel(x_ref, ctx_ref, mod_ref, g_ref, wa_ref, wab_ref, wr_ref, cos_ref, sin_ref,
                   pa_ref, ab_ref, pr_ref, *, d_model):
    t = pl.program_id(1)
    xin = jnp.where(t == 0, ctx_ref[0], x_ref[0])
    mod = mod_ref[0]
    h = _rms_mod(xin, g_ref[...], mod[:, 0:d_model], mod[:, d_model:2 * d_model]).astype(BF16)
    pa_ref[0] = _dot(h, wa_ref[...])
    ab_ref[0] = _dot(h, wab_ref[...])
    pr = _dot(h, wr_ref[...])
    cos2 = cos_ref[...]
    sin2 = sin_ref[...]
    k_scale = HEAD_DIM ** -0.5
    for hd in range(2 * N_HEADS):
        lo = hd * HEAD_DIM
        tt = pr[:, lo:lo + HEAD_DIM]
        rot = tt * cos2 + pltpu.roll(tt, HEAD_DIM // 2, 1) * sin2
        if hd >= N_HEADS:
            rot = rot * k_scale
        pr_ref[0, :, lo:lo + HEAD_DIM] = rot
    pr_ref[0, :, 2 * GROUP_W:] = pr[:, 2 * GROUP_W:]


def _inproj(x, ctx, mod3, g, wa, wab, wr, cos2, sin2):
    b, seq, d = x.shape
    n_tiles = (ctx.shape[1] + seq) // TILE
    t_total = n_tiles * TILE
    return pl.pallas_call(
        functools.partial(_inproj_kernel, d_model=d),
        grid=(b, n_tiles),
        in_specs=[pl.BlockSpec((1, TILE, d), lambda i, t: (i, jnp.maximum(t - 1, 0), 0)),
                  pl.BlockSpec((1, TILE, d), lambda i, t: (i, 0, 0)),
                  pl.BlockSpec((1, 1, mod3.shape[2]), lambda i, t: (jnp.where(t == 0, b, i), 0, 0)),
                  pl.BlockSpec((1, d), lambda i, t: (0, 0)),
                  pl.BlockSpec(wa.shape, lambda i, t: (0, 0)),
                  pl.BlockSpec(wab.shape, lambda i, t: (0, 0)),
                  pl.BlockSpec(wr.shape, lambda i, t: (0, 0)),
                  pl.BlockSpec((TILE, HEAD_DIM), lambda i, t: (t, 0)),
                  pl.BlockSpec((TILE, HEAD_DIM), lambda i, t: (t, 0))],
        out_specs=[pl.BlockSpec((1, TILE, wa.shape[1]), lambda i, t: (i, t, 0)),
                   pl.BlockSpec((1, TILE, GATE_LANES), lambda i, t: (i, t, 0)),
                   pl.BlockSpec((1, TILE, wr.shape[1]), lambda i, t: (i, t, 0))],
        out_shape=[jax.ShapeDtypeStruct((b, t_total, wa.shape[1]), F32),
                   jax.ShapeDtypeStruct((b, t_total, GATE_LANES), F32),
                   jax.ShapeDtypeStruct((b, t_total, wr.shape[1]), F32)],
        compiler_params=_cparams(("parallel", "parallel")),
        name="inproj",
    )(x, ctx, mod3, g, wa, wab, wr, cos2, sin2)


def _conv_kernel(cur_ref, prev_ref, next_ref, w_ref, o_ref, ext_ref):
    t = pl.program_id(1)
    last = pl.num_programs(1) - 1
    has_prev = t >= 2
    has_next = jnp.logical_and(t >= 1, t < last)
    ext_ref[0:HALO, :] = jnp.where(has_prev, prev_ref[0], 0.0)
    ext_ref[HALO:HALO + TILE, :] = cur_ref[0]
    ext_ref[HALO + TILE:, :] = jnp.where(has_next, next_ref[0], 0.0)
    pad = (CONV_K - 1) // 2
    q_scale = HEAD_DIM ** -0.5
    for part in range(3):
        for hd in range(N_HEADS):
            lo = part * GROUP_W + hd * HEAD_DIM
            acc = None
            for i in range(CONV_K):
                r0 = HALO - pad + i
                term = ext_ref[r0:r0 + TILE, lo:lo + HEAD_DIM] * w_ref[i:i + 1, lo:lo + HEAD_DIM]
                acc = term if acc is None else acc + term
            y = _silu(acc)
            if part < 2:
                y = y * lax.rsqrt(jnp.sum(y * y, axis=-1, keepdims=True) + NORM_EPS)
            if part == 0:
                y = y * q_scale
            o_ref[0, :, lo:lo + HEAD_DIM] = y


def _conv(pa, conv_w):
    b, t_total, _ = pa.shape
    n_tiles = t_total // TILE
    width = 3 * GROUP_W
    per = TILE // HALO
    n_halo = t_total // HALO
    return pl.pallas_call(
        _conv_kernel,
        grid=(b, n_tiles),
        in_specs=[pl.BlockSpec((1, TILE, width), lambda i, t: (i, t, 0)),
                  pl.BlockSpec((1, HALO, width), lambda i, t: (i, jnp.maximum(t * per - 1, 0), 0)),
                  pl.BlockSpec((1, HALO, width), lambda i, t: (i, jnp.minimum((t + 1) * per, n_halo - 1), 0)),
                  pl.BlockSpec((CONV_K, width), lambda i, t: (0, 0))],
        out_specs=pl.BlockSpec((1, TILE, width), lambda i, t: (i, t, 0)),
        out_shape=jax.ShapeDtypeStruct((b, t_total, width), F32),
        scratch_shapes=[pltpu.VMEM((TILE + 2 * HALO, width), F32)],
        compiler_params=_cparams(("parallel", "parallel")),
        name="conv",
    )(pa, pa, pa, conv_w)


def _packed_matmul(x, y, bd_mask):
    yb = y.astype(BF16)
    y_bd = jnp.where(bd_mask, jnp.concatenate([yb] * N_HEADS, axis=0), jnp.zeros((), BF16))
    return _dot(x.astype(BF16), y_bd)


def _unit_lower_inverses(a_list, eye, m16, m32, m64, bd_mask):
    pm = functools.partial(_packed_matmul, bd_mask=bd_mask)
    ps = [-jnp.where(m16, a, 0.0) for a in a_list]
    xs = [eye + p for p in ps]
    for _ in range(3):
        ps = [pm(p, p) for p in ps]
        xs = [x + pm(x, p) for x, p in zip(xs, ps)]
    for m in (m32, m64):
        zs = [pm(jnp.where(m, a, 0.0), x) for a, x in zip(a_list, xs)]
        xs = [x - pm(x, z) for x, z in zip(xs, zs)]
    return xs


def _gdn_prepare(groups, masks):
    eye4, m16, m32, m64, bd_mask = masks
    for g in groups:
        tcum, mstrict = g["consts"][:2]
        g_lane0 = N_HEADS if g["rev"] else 0
        last = 0 if g["rev"] else CHUNK - 1
        gates = g["gates"]
        g_hi, g_lo = _split(gates)
        gc = _dot(tcum, g_hi) + _dot(tcum, g_lo)
        g_last = gc[last:last + 1, :]
        g["e_g"] = jnp.exp(gc)
        g["e_gl"] = jnp.exp(g_last - gc)
        g["e_last"] = jnp.exp(g_last)
        y = jnp.concatenate(
            [jnp.where(mstrict, _lane_bcast(gates[:, g_lane0 + h:g_lane0 + h + 1], CHUNK), 0.0)
             for h in range(N_HEADS)], axis=1)
        y_hi, y_lo = _split(y)
        g["e"] = jnp.exp(_dot(tcum, y_hi) + _dot(tcum, y_lo))

    for g in groups:
        b_lane0 = 2 * N_HEADS + (N_HEADS if g["rev"] else 0)
        qk_parts, kk_parts, g["kb"], g["beta"] = [], [], [], []
        for h in range(N_HEADS):
            lo = h * HEAD_DIM
            beta = _lane_bcast(g["gates"][:, b_lane0 + h:b_lane0 + h + 1], HEAD_DIM)
            k_h = g["k"][:, lo:lo + HEAD_DIM]
            kb_h = k_h * beta
            lhs = jnp.concatenate([g["q"][:, lo:lo + HEAD_DIM], kb_h], axis=0).astype(BF16)
            m = _dot_nt(lhs, k_h.astype(BF16))
            qk_parts.append(m[:CHUNK])
            kk_parts.append(m[CHUNK:])
            g["kb"].append(kb_h)
            g["beta"].append(beta)
        incl4, strict4 = g["consts"][2:]
        g["a"] = jnp.where(strict4, jnp.concatenate(kk_parts, axis=1) * g["e"], 0.0)
        g["qkd"] = jnp.where(incl4, jnp.concatenate(qk_parts, axis=1) * g["e"], 0.0).astype(BF16)

    t_invs = _unit_lower_inverses([g["a"] for g in groups], eye4, m16, m32, m64, bd_mask)

    for g, t_inv in zip(groups, t_invs):
        g_lane0 = N_HEADS if g["rev"] else 0
        t_b = t_inv.astype(BF16)
        g["u"], g["wq"], g["kd_t"] = [], [], []
        for h in range(N_HEADS):
            lo = h * HEAD_DIM
            gl = g_lane0 + h
            eg_h = _lane_bcast(g["e_g"][:, gl:gl + 1], HEAD_DIM)
            egl_h = _lane_bcast(g["e_gl"][:, gl:gl + 1], HEAD_DIM)
            rhs = jnp.concatenate([g["v"][:, lo:lo + HEAD_DIM] * g["beta"][h], g["kb"][h] * eg_h], axis=1)
            sol = _dot(t_b[:, h * CHUNK:(h + 1) * CHUNK], rhs.astype(BF16))
            g["u"].append(sol[:, :HEAD_DIM])
            g["wq"].append(jnp.concatenate([sol[:, HEAD_DIM:], g["q"][:, lo:lo + HEAD_DIM] * eg_h],
                                           axis=0).astype(BF16))
            g["kd_t"].append((g["k"][:, lo:lo + HEAD_DIM] * egl_h).T.astype(BF16))


def _gdn_scan_step(pair, s_ref):
    units = [(g, h, (N_HEADS if g["rev"] else 0) + h) for g in pair for h in range(N_HEADS)]
    states = [s_ref[slot] for _, _, slot in units]
    wss = [_dot(g["wq"][h], s.astype(BF16)) for (g, h, _), s in zip(units, states)]
    v_news = [(g["u"][h] - ws[:CHUNK]).astype(BF16) for (g, h, _), ws in zip(units, wss)]
    outs = [ws[CHUNK:] + _dot(g["qkd"][:, h * CHUNK:(h + 1) * CHUNK], vn)
            for (g, h, _), ws, vn in zip(units, wss, v_news)]
    for (g, h, slot), s, vn in zip(units, states, v_news):
        s_ref[slot] = g["e_last"][:, slot:slot + 1] * s + _dot(g["kd_t"][h], vn)
    return [jnp.concatenate(outs[i * N_HEADS:(i + 1) * N_HEADS], axis=1) for i in range(len(pair))]


def _gate_tile(ab, dtb, a_log):
    z = ab + dtb
    softplus = jnp.maximum(z, 0.0) + jnp.log(1.0 + jnp.exp(-jnp.abs(z)))
    lane = lax.broadcasted_iota(jnp.int32, ab.shape, 1)
    return jnp.where(lane < 2 * N_HEADS, -jnp.exp(a_log) * softplus, jax.nn.sigmoid(ab))


def _gdn_kernel(qf_ref, kf_ref, vf_ref, abf_ref, qb_ref, kb_ref, vb_ref, abb_ref, dtb_ref, alog_ref,
                of_ref, ob_ref, s_ref):
    @pl.when(pl.program_id(1) == 0)
    def _():
        s_ref[...] = jnp.zeros_like(s_ref)

    ri = lax.broadcasted_iota(jnp.int32, (CHUNK, CHUNK), 0)
    ci = lax.broadcasted_iota(jnp.int32, (CHUNK, CHUNK), 1)
    ri4 = lax.broadcasted_iota(jnp.int32, (CHUNK, N_HEADS * CHUNK), 0)
    ci4 = lax.broadcasted_iota(jnp.int32, (CHUNK, N_HEADS * CHUNK), 1) % CHUNK
    rbd = lax.broadcasted_iota(jnp.int32, (N_HEADS * CHUNK, N_HEADS * CHUNK), 0) // CHUNK
    cbd = lax.broadcasted_iota(jnp.int32, (N_HEADS * CHUNK, N_HEADS * CHUNK), 1) // CHUNK
    bd_mask = rbd == cbd
    eye4 = (ri4 == ci4).astype(F32)
    m16 = (ri4 // 16) == (ci4 // 16)
    m32 = jnp.logical_and((ri4 // 32) == (ci4 // 32), jnp.logical_not(m16))
    m64 = (ri4 // 32) != (ci4 // 32)

    cf = ((ci <= ri).astype(BF16), ri > ci, ri4 >= ci4, ri4 > ci4)
    cb = ((ci >= ri).astype(BF16), ri < ci, ri4 <= ci4, ri4 < ci4)
    dtb = dtb_ref[...]
    alog = alog_ref[...]

    pairs = []
    for j in range(CHUNKS_PER_TILE):
        rf = j * CHUNK
        rb = (CHUNKS_PER_TILE - 1 - j) * CHUNK
        fwd = dict(q=qf_ref[0, rf:rf + CHUNK, :], k=kf_ref[0, rf:rf + CHUNK, :], v=vf_ref[0, rf:rf + CHUNK, :],
                   gates=_gate_tile(abf_ref[0, rf:rf + CHUNK, :], dtb, alog), rev=False, consts=cf, row=rf)
        bwd = dict(q=qb_ref[0, rb:rb + CHUNK, :], k=kb_ref[0, rb:rb + CHUNK, :], v=vb_ref[0, rb:rb + CHUNK, :],
                   gates=_gate_tile(abb_ref[0, rb:rb + CHUNK, :], dtb, alog), rev=True, consts=cb, row=rb)
        pairs.append((fwd, bwd))
    _gdn_prepare([g for pair in pairs for g in pair], (eye4, m16, m32, m64, bd_mask))
    for fwd, bwd in pairs:
        o_f, o_b = _gdn_scan_step((fwd, bwd), s_ref)
        of_ref[0, fwd["row"]:fwd["row"] + CHUNK, :] = o_f
        ob_ref[0, bwd["row"]:bwd["row"] + CHUNK, :] = o_b


def _bwd_tile(s, n_tiles):
    return jnp.where(s == 0, 0, n_tiles - s)


def _gdn(qkv, ab, dtb_row, alog_row):
    b, t_total, _ = qkv.shape
    n_tiles = t_total // TILE

    def col(c, bwd):
        if bwd:
            return pl.BlockSpec((1, TILE, GROUP_W), lambda i, s: (i, _bwd_tile(s, n_tiles), c))
        return pl.BlockSpec((1, TILE, GROUP_W), lambda i, s: (i, s, c))

    def gate(bwd):
        if bwd:
            return pl.BlockSpec((1, TILE, GATE_LANES), lambda i, s: (i, _bwd_tile(s, n_tiles), 0))
        return pl.BlockSpec((1, TILE, GATE_LANES), lambda i, s: (i, s, 0))

    row = pl.BlockSpec((1, GATE_LANES), lambda i, s: (0, 0))
    return pl.pallas_call(
        _gdn_kernel,
        grid=(b, n_tiles),
        in_specs=[col(0, False), col(1, False), col(2, False), gate(False),
                  col(0, True), col(1, True), col(2, True), gate(True), row, row],
        out_specs=[col(0, False), col(0, True)],
        out_shape=[jax.ShapeDtypeStruct((b, t_total, GROUP_W), F32)] * 2,
        scratch_shapes=[pltpu.VMEM((2 * N_HEADS, HEAD_DIM, HEAD_DIM), F32)],
        compiler_params=_cparams(("parallel", "arbitrary")),
        name="gdn",
    )(qkv, qkv, qkv, ab, qkv, qkv, qkv, ab, dtb_row, alog_row)


def _ret_dir(q, k, v, lg_row, r_ref, slot0, rev, o_ref):
    ri = lax.broadcasted_iota(jnp.int32, (TILE, TILE), 0)
    ci = lax.broadcasted_iota(jnp.int32, (TILE, TILE), 1)
    diff = ((ci - ri) if rev else (ri - ci)).astype(F32)
    pos = lax.broadcasted_iota(jnp.int32, (TILE, 1), 0).astype(F32)
    if rev:
        pos = (TILE - 1.0) - pos
    for h in range(N_HEADS):
        lo = h * HEAD_DIM
        lg = lg_row[:, slot0 + h:slot0 + h + 1]
        decay = jnp.where(diff >= 0, jnp.exp(lg * jnp.maximum(diff, 0.0)), 0.0)
        xi = jnp.exp(lg * (pos + 1.0))
        zeta = jnp.exp(lg * ((TILE - 1.0) - pos))
        g_chunk = jnp.exp(lg * float(TILE))
        q_h = q[:, lo:lo + HEAD_DIM]
        k_h = k[:, lo:lo + HEAD_DIM]
        v_b = v[:, lo:lo + HEAD_DIM].astype(BF16)
        r = r_ref[slot0 + h]
        qk = _dot_nt(q_h.astype(BF16), k_h.astype(BF16)) * decay
        o_ref[0, :, lo:lo + HEAD_DIM] = _dot(qk.astype(BF16), v_b) + _dot((q_h * xi).astype(BF16), r.astype(BF16))
        r_ref[slot0 + h] = g_chunk * r + _dot((k_h * zeta).T.astype(BF16), v_b)


def _ret_kernel(qf_ref, kf_ref, vf_ref, qb_ref, kb_ref, vb_ref, logit_ref, of_ref, ob_ref, r_ref):
    @pl.when(pl.program_id(1) == 0)
    def _():
        r_ref[...] = jnp.zeros_like(r_ref)

    x = logit_ref[...]
    lg_row = jnp.minimum(x, 0.0) - jnp.log(1.0 + jnp.exp(-jnp.abs(x)))
    _ret_dir(qf_ref[0], kf_ref[0], vf_ref[0], lg_row, r_ref, 0, False, of_ref)
    _ret_dir(qb_ref[0], kb_ref[0], vb_ref[0], lg_row, r_ref, N_HEADS, True, ob_ref)


def _ret(pr, logit_row):
    b, t_total, _ = pr.shape
    n_tiles = t_total // TILE

    def col(c, bwd):
        if bwd:
            return pl.BlockSpec((1, TILE, GROUP_W), lambda i, s: (i, _bwd_tile(s, n_tiles), c))
        return pl.BlockSpec((1, TILE, GROUP_W), lambda i, s: (i, s, c))

    return pl.pallas_call(
        _ret_kernel,
        grid=(b, n_tiles),
        in_specs=[col(0, False), col(1, False), col(2, False), col(0, True), col(1, True), col(2, True),
                  pl.BlockSpec((1, GATE_LANES), lambda i, s: (0, 0))],
        out_specs=[col(0, False), col(0, True)],
        out_shape=[jax.ShapeDtypeStruct((b, t_total, GROUP_W), F32)] * 2,
        scratch_shapes=[pltpu.VMEM((2 * N_HEADS, HEAD_DIM, HEAD_DIM), F32)],
        compiler_params=_cparams(("parallel", "arbitrary")),
        name="ret",
    )(pr, pr, pr, pr, pr, pr, logit_row)


def _out_kernel(gf_ref, gb_ref, z_ref, rf_ref, rb_ref, rg_ref, x_ref, mod_ref, gdn_g_ref, ret_g_ref,
                nffn_g_ref, w_ref, x1_ref, h2_ref, y_ref, *, d_model):
    o = gf_ref[0] + gb_ref[0]
    r = rf_ref[0] + rb_ref[0]
    z = z_ref[0]
    rg = rg_ref[0]
    for h in range(N_HEADS):
        lo = h * HEAD_DIM
        o_h = o[:, lo:lo + HEAD_DIM]
        o_n = o_h * lax.rsqrt(jnp.mean(o_h * o_h, axis=-1, keepdims=True) + NORM_EPS) * gdn_g_ref[...]
        y_ref[:, lo:lo + HEAD_DIM] = (o_n * _silu(z[:, lo:lo + HEAD_DIM])).astype(BF16)
        r_h = r[:, lo:lo + HEAD_DIM]
        mu = jnp.mean(r_h, axis=-1, keepdims=True)
        cen = r_h - mu
        var = jnp.mean(cen * cen, axis=-1, keepdims=True)
        r_n = cen * lax.rsqrt(var + NORM_EPS) * ret_g_ref[...]
        y_ref[:, GROUP_W + lo:GROUP_W + lo + HEAD_DIM] = (r_n * _silu(rg[:, lo:lo + HEAD_DIM])).astype(BF16)
    mod = mod_ref[0]
    x1 = x_ref[0] + mod[:, 2 * d_model:3 * d_model] * _dot(y_ref[...], w_ref[...])
    x1_ref[0] = x1
    h2_ref[0] = _rms_mod(x1, nffn_g_ref[...], mod[:, 3 * d_model:4 * d_model],
                         mod[:, 4 * d_model:5 * d_model]).astype(BF16)


def _out(gdn_f, gdn_b, pa, ret_f, ret_b, pr, x, mod3, gdn_g, ret_g, nffn_g, w_out):
    b, seq, d = x.shape
    n_lat = seq // TILE
    first = (pa.shape[1] - seq) // TILE

    def tok(c):
        return pl.BlockSpec((1, TILE, GROUP_W), lambda i, t: (i, t + first, c))

    vec = pl.BlockSpec((1, HEAD_DIM), lambda i, t: (0, 0))
    return pl.pallas_call(
        functools.partial(_out_kernel, d_model=d),
        grid=(b, n_lat),
        in_specs=[tok(0), tok(0), tok(3), tok(0), tok(0), tok(3),
                  pl.BlockSpec((1, TILE, d), lambda i, t: (i, t, 0)),
                  pl.BlockSpec((1, 1, mod3.shape[2]), lambda i, t: (i, 0, 0)),
                  vec, vec,
                  pl.BlockSpec((1, d), lambda i, t: (0, 0)),
                  pl.BlockSpec(w_out.shape, lambda i, t: (0, 0))],
        out_specs=[pl.BlockSpec((1, TILE, d), lambda i, t: (i, t, 0)),
                   pl.BlockSpec((1, TILE, d), lambda i, t: (i, t, 0))],
        out_shape=[jax.ShapeDtypeStruct((b, seq, d), F32), jax.ShapeDtypeStruct((b, seq, d), BF16)],
        scratch_shapes=[pltpu.VMEM((TILE, 2 * GROUP_W), BF16)],
        compiler_params=_cparams(("parallel", "parallel")),
        name="outproj",
    )(gdn_f, gdn_b, pa, ret_f, ret_b, pr, x, mod3, gdn_g, ret_g, nffn_g, w_out)


def _ffn_kernel(h_ref, x1_ref, mod_ref, wg_ref, wu_ref, wo_ref, fg_ref, o_ref, acc_ref, *, d_model):
    f = pl.program_id(2)

    @pl.when(f == 0)
    def _():
        acc_ref[...] = jnp.zeros_like(acc_ref)

    h = h_ref[0]
    act = (_silu(_dot(h, wg_ref[...])) * _dot(h, wu_ref[...])).astype(BF16)
    acc_ref[...] += _dot(act, wo_ref[...])

    @pl.when(f == pl.num_programs(2) - 1)
    def _():
        x2 = x1_ref[0] + mod_ref[0][:, 5 * d_model:6 * d_model] * acc_ref[...]
        ms = jnp.mean(x2 * x2, axis=-1, keepdims=True)
        o_ref[0] = x2 * lax.rsqrt(ms + NORM_EPS) * fg_ref[...]


def _ffn(h2, x1, mod3, w_in, w_out, final_g):
    b, seq, d = x1.shape
    d_ff = w_out.shape[0]
    n_f = 2
    bf = d_ff // n_f
    return pl.pallas_call(
        functools.partial(_ffn_kernel, d_model=d),
        grid=(b, seq // FFN_TOK, n_f),
        in_specs=[pl.BlockSpec((1, FFN_TOK, d), lambda i, t, f: (i, t, 0)),
                  pl.BlockSpec((1, FFN_TOK, d), lambda i, t, f: (i, t, 0)),
                  pl.BlockSpec((1, 1, mod3.shape[2]), lambda i, t, f: (i, 0, 0)),
                  pl.BlockSpec((d, bf), lambda i, t, f: (0, f)),
                  pl.BlockSpec((d, bf), lambda i, t, f: (0, f + n_f)),
                  pl.BlockSpec((bf, d), lambda i, t, f: (f, 0)),
                  pl.BlockSpec((1, d), lambda i, t, f: (0, 0))],
        out_specs=pl.BlockSpec((1, FFN_TOK, d), lambda i, t, f: (i, t, 0)),
        out_shape=jax.ShapeDtypeStruct((b, seq, d), F32),
        scratch_shapes=[pltpu.VMEM((FFN_TOK, d), F32)],
        compiler_params=_cparams(("parallel", "parallel", "arbitrary")),
        name="ffn",
    )(h2, x1, mod3, w_in, w_in, w_out, final_g)


def _rope_tables(ctx_len, n_lat):
    def angles(pos, n_pairs):
        inv = ROPE_THETA ** (-jnp.arange(n_pairs, dtype=F32) / n_pairs)
        return pos[:, None] * inv[None, :]

    rows = n_lat // GRID_W
    row = jnp.repeat(jnp.arange(rows, dtype=F32), GRID_W)
    col = jnp.tile(jnp.arange(GRID_W, dtype=F32), rows)
    zeros = jnp.zeros((ctx_len,), F32)
    p_seq = jnp.concatenate([jnp.arange(ctx_len, dtype=F32), jnp.full((n_lat,), float(ctx_len), F32)])
    ang = jnp.concatenate([angles(p_seq, ROPE_PAIRS[0]),
                           angles(jnp.concatenate([zeros, row]), ROPE_PAIRS[1]),
                           angles(jnp.concatenate([zeros, col]), ROPE_PAIRS[2])], axis=-1)
    cos, sin = jnp.cos(ang), jnp.sin(ang)
    return jnp.concatenate([cos, cos], axis=-1), jnp.concatenate([-sin, sin], axis=-1)


def _lane_row(values):
    flat = values.reshape(1, -1).astype(F32)
    return jnp.pad(flat, ((0, 0), (0, GATE_LANES - flat.shape[1])))


def kernel(x, c, ctx, c_ctx, ada_w, ada_b, norm_mix_g, norm_ffn_g, w_in, conv_w, gdn_a_log, gdn_dt_bias,
           gdn_norm_g, ret_decay_logit, ret_norm_g, w_out, w_ffn_in, w_ffn_out, final_g):
    assert ada_w.shape[0] == 1, "single-layer block"
    b, seq, d = x.shape
    ctx_len = ctx.shape[1]
    assert ctx_len == TILE and seq % FFN_TOK == 0 and b + 1 <= 8

    cond = jnp.concatenate([c, c_ctx[None, :], jnp.zeros((8 - b - 1, d), F32)], axis=0)
    mod = _ada(cond, ada_w[0], ada_b)
    mod3 = mod[:, None, :]

    n_qkvz = 4 * GROUP_W
    n_gate = 4 * N_HEADS
    w = w_in[0]
    wa = w[:, :n_qkvz].astype(BF16)
    wab = jnp.pad(w[:, n_qkvz:n_qkvz + n_gate], ((0, 0), (0, GATE_LANES - n_gate))).astype(BF16)
    wr = w[:, n_qkvz + n_gate:].astype(BF16)
    cos2, sin2 = _rope_tables(ctx_len, seq)
    pa, ab, pr = _inproj(x, ctx, mod3, norm_mix_g, wa, wab, wr, cos2, sin2)

    qkv = _conv(pa, conv_w[0])
    dtb_row = _lane_row(gdn_dt_bias[0])
    gdn_f, gdn_b = _gdn(qkv, ab, dtb_row, _lane_row(gdn_a_log[0]))
    ret_f, ret_b = _ret(pr, _lane_row(ret_decay_logit[0]))

    x1, h2 = _out(gdn_f, gdn_b, pa, ret_f, ret_b, pr, x, mod3, gdn_norm_g, ret_norm_g, norm_ffn_g,
                  w_out[0].astype(BF16))
    return _ffn(h2, x1, mod3, w_ffn_in[0].astype(BF16), w_ffn_out[0].astype(BF16), final_g[None, :])
```

```python
import functools
import math

import jax
import jax.numpy as jnp
from jax import lax
from jax.experimental import pallas as pl
from jax.experimental.pallas import tpu as pltpu

F32 = jnp.float32
BF16 = jnp.bfloat16

HEAD_DIM = 128
N_HEADS = 4
GROUP_W = N_HEADS * HEAD_DIM
CONV_K = 5
GRID_W = 64
ROPE_THETA = 10000.0
ROPE_PAIRS = (16, 24, 24)
NORM_EPS = 1e-6

CHUNK = 64
TILE = 256
CHUNKS_PER_TILE = TILE // CHUNK
HALO = 16
GATE_LANES = 128
FFN_TOK = 512
VMEM_LIMIT = 56 * 1024 * 1024


def _cparams(sem):
    return pltpu.CompilerParams(dimension_semantics=sem, vmem_limit_bytes=VMEM_LIMIT)


def _dot(a, b):
    return jnp.dot(a, b, preferred_element_type=F32)


def _dot_nt(a, b):
    return lax.dot_general(a, b, (((1,), (1,)), ((), ())), preferred_element_type=F32)


def _split(a):
    hi = a.astype(BF16)
    lo = (a - hi.astype(F32)).astype(BF16)
    return hi, lo


def _silu(x):
    return x * jax.nn.sigmoid(x)


def _lane_bcast(col, width):
    return jnp.broadcast_to(col, (col.shape[0], width))


def _ada_kernel(cond_ref, w_ref, b_ref, o_ref):
    a_hi, a_lo = _split(_silu(cond_ref[...]))
    w_hi, w_lo = _split(w_ref[...])
    o_ref[...] = _dot(a_hi, w_hi) + _dot(a_lo, w_hi) + _dot(a_hi, w_lo) + b_ref[...]


def _ada(cond, w, b):
    rows, d = cond.shape
    n = w.shape[1]
    bn = 1536
    return pl.pallas_call(
        _ada_kernel,
        grid=(n // bn,),
        in_specs=[pl.BlockSpec((rows, d), lambda j: (0, 0)),
                  pl.BlockSpec((d, bn), lambda j: (0, j)),
                  pl.BlockSpec((1, bn), lambda j: (0, j))],
        out_specs=pl.BlockSpec((rows, bn), lambda j: (0, j)),
        out_shape=jax.ShapeDtypeStruct((rows, n), F32),
        compiler_params=_cparams(("parallel",)),
        name="ada",
    )(cond, w, b)


def _rms_mod(x, g, shift, scale):
    ms = jnp.mean(x * x, axis=-1, keepdims=True)
    return (x * lax.rsqrt(ms + NORM_EPS) * g) * (1.0 + scale) + shift


def _inproj_kernel(x_ref, ctx_ref, mod_ref, g_ref, wa_ref, wab_ref, wr_ref, cos_ref, sin_ref,
                   pa_ref, ab_ref, pr_ref, *, d_model):
    t = pl.program_id(1)
    xin = jnp.where(t == 0, ctx_ref[0], x_ref[0])
    mod = mod_ref[0]
    h = _rms_mod(xin, g_ref[...], mod[:, 0:d_model], mod[:, d_model:2 * d_model]).astype(BF16)
    pa_ref[0] = _dot(h, wa_ref[...]).astype(BF16)
    ab_ref[0] = _dot(h, wab_ref[...])
    pr = _dot(h, wr_ref[...])
    cos2 = cos_ref[...]
    sin2 = sin_ref[...]
    k_scale = HEAD_DIM ** -0.5
    for hd in range(2 * N_HEADS):
        lo = hd * HEAD_DIM
        tt = pr[:, lo:lo + HEAD_DIM]
        rot = tt * cos2 + pltpu.roll(tt, HEAD_DIM // 2, 1) * sin2
        if hd >= N_HEADS:
            rot = rot * k_scale
        pr_ref[0, :, lo:lo + HEAD_DIM] = rot.astype(BF16)
    pr_ref[0, :, 2 * GROUP_W:] = pr[:, 2 * GROUP_W:].astype(BF16)


def _inproj(x, ctx, mod3, g, wa, wab, wr, cos2, sin2):
    b, seq, d = x.shape
    n_tiles = (ctx.shape[1] + seq) // TILE
    t_total = n_tiles * TILE
    return pl.pallas_call(
        functools.partial(_inproj_kernel, d_model=d),
        grid=(b, n_tiles),
        in_specs=[pl.BlockSpec((1, TILE, d), lambda i, t: (i, jnp.maximum(t - 1, 0), 0)),
                  pl.BlockSpec((1, TILE, d), lambda i, t: (i, 0, 0)),
                  pl.BlockSpec((1, 1, mod3.shape[2]), lambda i, t: (jnp.where(t == 0, b, i), 0, 0)),
                  pl.BlockSpec((1, d), lambda i, t: (0, 0)),
                  pl.BlockSpec(wa.shape, lambda i, t: (0, 0)),
                  pl.BlockSpec(wab.shape, lambda i, t: (0, 0)),
                  pl.BlockSpec(wr.shape, lambda i, t: (0, 0)),
                  pl.BlockSpec((TILE, HEAD_DIM), lambda i, t: (t, 0)),
                  pl.BlockSpec((TILE, HEAD_DIM), lambda i, t: (t, 0))],
        out_specs=[pl.BlockSpec((1, TILE, wa.shape[1]), lambda i, t: (i, t, 0)),
                   pl.BlockSpec((1, TILE, GATE_LANES), lambda i, t: (i, t, 0)),
                   pl.BlockSpec((1, TILE, wr.shape[1]), lambda i, t: (i, t, 0))],
        out_shape=[jax.ShapeDtypeStruct((b, t_total, wa.shape[1]), BF16),
                   jax.ShapeDtypeStruct((b, t_total, GATE_LANES), F32),
                   jax.ShapeDtypeStruct((b, t_total, wr.shape[1]), BF16)],
        compiler_params=_cparams(("parallel", "parallel")),
        name="inproj",
    )(x, ctx, mod3, g, wa, wab, wr, cos2, sin2)


def _conv_kernel(cur_ref, prev_ref, next_ref, w_ref, o_ref, ext_ref):
    t = pl.program_id(1)
    last = pl.num_programs(1) - 1
    has_prev = t >= 2
    has_next = jnp.logical_and(t >= 1, t < last)
    ext_ref[0:HALO, :] = jnp.where(has_prev, prev_ref[0].astype(F32), 0.0)
    ext_ref[HALO:HALO + TILE, :] = cur_ref[0].astype(F32)
    ext_ref[HALO + TILE:, :] = jnp.where(has_next, next_ref[0].astype(F32), 0.0)
    pad = (CONV_K - 1) // 2
    q_scale = HEAD_DIM ** -0.5
    for part in range(3):
        for hd in range(N_HEADS):
            lo = part * GROUP_W + hd * HEAD_DIM
            acc = None
            for i in range(CONV_K):
                r0 = HALO - pad + i
                term = ext_ref[r0:r0 + TILE, lo:lo + HEAD_DIM] * w_ref[i:i + 1, lo:lo + HEAD_DIM]
                acc = term if acc is None else acc + term
            y = _silu(acc)
            if part < 2:
                y = y * lax.rsqrt(jnp.sum(y * y, axis=-1, keepdims=True) + NORM_EPS)
            if part == 0:
                y = y * q_scale
            o_ref[0, :, lo:lo + HEAD_DIM] = y.astype(BF16)


def _conv(pa, conv_w):
    b, t_total, _ = pa.shape
    n_tiles = t_total // TILE
    width = 3 * GROUP_W
    per = TILE // HALO
    n_halo = t_total // HALO
    return pl.pallas_call(
        _conv_kernel,
        grid=(b, n_tiles),
        in_specs=[pl.BlockSpec((1, TILE, width), lambda i, t: (i, t, 0)),
                  pl.BlockSpec((1, HALO, width), lambda i, t: (i, jnp.maximum(t * per - 1, 0), 0)),
                  pl.BlockSpec((1, HALO, width), lambda i, t: (i, jnp.minimum((t + 1) * per, n_halo - 1), 0)),
                  pl.BlockSpec((CONV_K, width), lambda i, t: (0, 0))],
        out_specs=pl.BlockSpec((1, TILE, width), lambda i, t: (i, t, 0)),
        out_shape=jax.ShapeDtypeStruct((b, t_total, width), BF16),
        scratch_shapes=[pltpu.VMEM((TILE + 2 * HALO, width), F32)],
        compiler_params=_cparams(("parallel", "parallel")),
        name="conv",
    )(pa, pa, pa, conv_w)


def _packed_matmul(x, y, bd_mask):
    yb = y.astype(BF16)
    y_bd = jnp.where(bd_mask, jnp.concatenate([yb] * N_HEADS, axis=0), jnp.zeros((), BF16))
    return _dot(x.astype(BF16), y_bd)


def _unit_lower_inverses(a_list, eye, m16, m32, m64, bd_mask):
    pm = functools.partial(_packed_matmul, bd_mask=bd_mask)
    ps = [-jnp.where(m16, a, 0.0) for a in a_list]
    xs = [eye + p for p in ps]
    for _ in range(3):
        ps = [pm(p, p) for p in ps]
        xs = [x + pm(x, p) for x, p in zip(xs, ps)]
    for m in (m32, m64):
        zs = [pm(jnp.where(m, a, 0.0), x) for a, x in zip(a_list, xs)]
        xs = [x - pm(x, z) for x, z in zip(xs, zs)]
    return xs


def _gdn_prepare(groups, masks):
    eye4, m16, m32, m64, bd_mask = masks
    for g in groups:
        tcum, mstrict = g["consts"][:2]
        g_lane0 = N_HEADS if g["rev"] else 0
        last = 0 if g["rev"] else CHUNK - 1
        gates = g["gates"]
        g_hi, g_lo = _split(gates)
        gc = _dot(tcum, g_hi) + _dot(tcum, g_lo)
        g_last = gc[last:last + 1, :]
        g["e_g"] = jnp.exp(gc)
        g["e_gl"] = jnp.exp(g_last - gc)
        g["e_last"] = jnp.exp(g_last)
        y = jnp.concatenate(
            [jnp.where(mstrict, _lane_bcast(gates[:, g_lane0 + h:g_lane0 + h + 1], CHUNK), 0.0)
             for h in range(N_HEADS)], axis=1)
        y_hi, y_lo = _split(y)
        g["e"] = jnp.exp(_dot(tcum, y_hi) + _dot(tcum, y_lo))

    for g in groups:
        b_lane0 = 2 * N_HEADS + (N_HEADS if g["rev"] else 0)
        qk_parts, kk_parts, g["kb"], g["beta"] = [], [], [], []
        for h in range(N_HEADS):
            lo = h * HEAD_DIM
            beta = _lane_bcast(g["gates"][:, b_lane0 + h:b_lane0 + h + 1], HEAD_DIM)
            k_h = g["k"][:, lo:lo + HEAD_DIM]
            kb_h = k_h * beta
            lhs = jnp.concatenate([g["q"][:, lo:lo + HEAD_DIM], kb_h.astype(BF16)], axis=0)
            m = _dot_nt(lhs, k_h)
            qk_parts.append(m[:CHUNK])
            kk_parts.append(m[CHUNK:])
            g["kb"].append(kb_h)
            g["beta"].append(beta)
        incl4, strict4 = g["consts"][2:]
        g["a"] = jnp.where(strict4, jnp.concatenate(kk_parts, axis=1) * g["e"], 0.0)
        g["qkd"] = jnp.where(incl4, jnp.concatenate(qk_parts, axis=1) * g["e"], 0.0).astype(BF16)

    t_invs = _unit_lower_inverses([g["a"] for g in groups], eye4, m16, m32, m64, bd_mask)

    for g, t_inv in zip(groups, t_invs):
        g_lane0 = N_HEADS if g["rev"] else 0
        t_b = t_inv.astype(BF16)
        g["u"], g["wq"], g["kd_t"] = [], [], []
        for h in range(N_HEADS):
            lo = h * HEAD_DIM
            gl = g_lane0 + h
            eg_h = _lane_bcast(g["e_g"][:, gl:gl + 1], HEAD_DIM)
            egl_h = _lane_bcast(g["e_gl"][:, gl:gl + 1], HEAD_DIM)
            rhs = jnp.concatenate([g["v"][:, lo:lo + HEAD_DIM] * g["beta"][h], g["kb"][h] * eg_h], axis=1)
            sol = _dot(t_b[:, h * CHUNK:(h + 1) * CHUNK], rhs.astype(BF16))
            g["u"].append(sol[:, :HEAD_DIM])
            g["wq"].append(jnp.concatenate([sol[:, HEAD_DIM:], g["q"][:, lo:lo + HEAD_DIM] * eg_h],
                                           axis=0).astype(BF16))
            g["kd_t"].append((g["k"][:, lo:lo + HEAD_DIM] * egl_h).T.astype(BF16))


def _gdn_scan_step(pair, s_ref):
    units = [(g, h, (N_HEADS if g["rev"] else 0) + h) for g in pair for h in range(N_HEADS)]
    states = [s_ref[slot] for _, _, slot in units]
    wss = [_dot(g["wq"][h], s.astype(BF16)) for (g, h, _), s in zip(units, states)]
    v_news = [(g["u"][h] - ws[:CHUNK]).astype(BF16) for (g, h, _), ws in zip(units, wss)]
    outs = [ws[CHUNK:] + _dot(g["qkd"][:, h * CHUNK:(h + 1) * CHUNK], vn)
            for (g, h, _), ws, vn in zip(units, wss, v_news)]
    for (g, h, slot), s, vn in zip(units, states, v_news):
        s_ref[slot] = g["e_last"][:, slot:slot + 1] * s + _dot(g["kd_t"][h], vn)
    return [jnp.concatenate(outs[i * N_HEADS:(i + 1) * N_HEADS], axis=1) for i in range(len(pair))]


def _gate_tile(ab, dtb, a_log):
    z = ab + dtb
    softplus = jnp.maximum(z, 0.0) + jnp.log(1.0 + jnp.exp(-jnp.abs(z)))
    lane = lax.broadcasted_iota(jnp.int32, ab.shape, 1)
    return jnp.where(lane < 2 * N_HEADS, -jnp.exp(a_log) * softplus, jax.nn.sigmoid(ab))


def _gdn_kernel(qf_ref, kf_ref, vf_ref, abf_ref, qb_ref, kb_ref, vb_ref, abb_ref, dtb_ref, alog_ref,
                of_ref, ob_ref, s_ref):
    @pl.when(pl.program_id(1) == 0)
    def _():
        s_ref[...] = jnp.zeros_like(s_ref)

    ri = lax.broadcasted_iota(jnp.int32, (CHUNK, CHUNK), 0)
    ci = lax.broadcasted_iota(jnp.int32, (CHUNK, CHUNK), 1)
    ri4 = lax.broadcasted_iota(jnp.int32, (CHUNK, N_HEADS * CHUNK), 0)
    ci4 = lax.broadcasted_iota(jnp.int32, (CHUNK, N_HEADS * CHUNK), 1) % CHUNK
    rbd = lax.broadcasted_iota(jnp.int32, (N_HEADS * CHUNK, N_HEADS * CHUNK), 0) // CHUNK
    cbd = lax.broadcasted_iota(jnp.int32, (N_HEADS * CHUNK, N_HEADS * CHUNK), 1) // CHUNK
    bd_mask = rbd == cbd
    eye4 = (ri4 == ci4).astype(F32)
    m16 = (ri4 // 16) == (ci4 // 16)
    m32 = jnp.logical_and((ri4 // 32) == (ci4 // 32), jnp.logical_not(m16))
    m64 = (ri4 // 32) != (ci4 // 32)

    cf = ((ci <= ri).astype(BF16), ri > ci, ri4 >= ci4, ri4 > ci4)
    cb = ((ci >= ri).astype(BF16), ri < ci, ri4 <= ci4, ri4 < ci4)
    dtb = dtb_ref[...]
    alog = alog_ref[...]

    pairs = []
    for j in range(CHUNKS_PER_TILE):
        rf = j * CHUNK
        rb = (CHUNKS_PER_TILE - 1 - j) * CHUNK
        fwd = dict(q=qf_ref[0, rf:rf + CHUNK, :], k=kf_ref[0, rf:rf + CHUNK, :], v=vf_ref[0, rf:rf + CHUNK, :],
                   gates=_gate_tile(abf_ref[0, rf:rf + CHUNK, :], dtb, alog), rev=False, consts=cf, row=rf)
        bwd = dict(q=qb_ref[0, rb:rb + CHUNK, :], k=kb_ref[0, rb:rb + CHUNK, :], v=vb_ref[0, rb:rb + CHUNK, :],
                   gates=_gate_tile(abb_ref[0, rb:rb + CHUNK, :], dtb, alog), rev=True, consts=cb, row=rb)
        pairs.append((fwd, bwd))
    _gdn_prepare([g for pair in pairs for g in pair], (eye4, m16, m32, m64, bd_mask))
    for fwd, bwd in pairs:
        o_f, o_b = _gdn_scan_step((fwd, bwd), s_ref)
        of_ref[0, fwd["row"]:fwd["row"] + CHUNK, :] = o_f.astype(BF16)
        ob_ref[0, bwd["row"]:bwd["row"] + CHUNK, :] = o_b.astype(BF16)


def _bwd_tile(s, n_tiles):
    return jnp.where(s == 0, 0, n_tiles - s)


def _gdn(qkv, ab, dtb_row, alog_row):
    b, t_total, _ = qkv.shape
    n_tiles = t_total // TILE

    def col(c, bwd):
        if bwd:
            return pl.BlockSpec((1, TILE, GROUP_W), lambda i, s: (i, _bwd_tile(s, n_tiles), c))
        return pl.BlockSpec((1, TILE, GROUP_W), lambda i, s: (i, s, c))

    def gate(bwd):
        if bwd:
            return pl.BlockSpec((1, TILE, GATE_LANES), lambda i, s: (i, _bwd_tile(s, n_tiles), 0))
        return pl.BlockSpec((1, TILE, GATE_LANES), lambda i, s: (i, s, 0))

    row = pl.BlockSpec((1, GATE_LANES), lambda i, s: (0, 0))
    return pl.pallas_call(
        _gdn_kernel,
        grid=(b, n_tiles),
        in_specs=[col(0, False), col(1, False), col(2, False), gate(False),
                  col(0, True), col(1, True), col(2, True), gate(True), row, row],
        out_specs=[col(0, False), col(0, True)],
        out_shape=[jax.ShapeDtypeStruct((b, t_total, GROUP_W), BF16)] * 2,
        scratch_shapes=[pltpu.VMEM((2 * N_HEADS, HEAD_DIM, HEAD_DIM), F32)],
        compiler_params=_cparams(("parallel", "arbitrary")),
        name="gdn",
    )(qkv, qkv, qkv, ab, qkv, qkv, qkv, ab, dtb_row, alog_row)


def _ret_kernel(qf_ref, kf_ref, vf_ref, qb_ref, kb_ref, vb_ref, logit_ref, of_ref, ob_ref,
                r_ref, decay_ref, xi_ref, zeta_ref):
    x = logit_ref[...]
    lg_row = jnp.minimum(x, 0.0) - jnp.log(1.0 + jnp.exp(-jnp.abs(x)))

    @pl.when(pl.program_id(1) == 0)
    def _():
        r_ref[...] = jnp.zeros_like(r_ref)
        ri = lax.broadcasted_iota(jnp.int32, (TILE, TILE), 0)
        ci = lax.broadcasted_iota(jnp.int32, (TILE, TILE), 1)
        row = lax.broadcasted_iota(jnp.int32, (TILE, HEAD_DIM), 0).astype(F32)
        for slot in range(2 * N_HEADS):
            rev = slot >= N_HEADS
            lg = lg_row[:, slot:slot + 1]
            diff = ((ci - ri) if rev else (ri - ci)).astype(F32)
            pos = ((TILE - 1.0) - row) if rev else row
            decay_ref[slot] = jnp.where(diff >= 0, jnp.exp(lg * jnp.maximum(diff, 0.0)), 0.0)
            xi_ref[slot] = jnp.exp(lg * (pos + 1.0))
            zeta_ref[slot] = jnp.exp(lg * ((TILE - 1.0) - pos))

    units = [(refs, (N_HEADS if rev else 0) + h, h * HEAD_DIM)
             for refs, rev in (((qf_ref, kf_ref, vf_ref, of_ref), False), ((qb_ref, kb_ref, vb_ref, ob_ref), True))
             for h in range(N_HEADS)]
    qs = [refs[0][0, :, lo:lo + HEAD_DIM] for refs, _, lo in units]
    ks = [refs[1][0, :, lo:lo + HEAD_DIM] for refs, _, lo in units]
    vs = [refs[2][0, :, lo:lo + HEAD_DIM] for refs, _, lo in units]
    qks = [_dot_nt(q, k) for q, k in zip(qs, ks)]
    qkds = [(qk * decay_ref[slot]).astype(BF16) for qk, (_, slot, _) in zip(qks, units)]
    rs = [r_ref[slot] for _, slot, _ in units]
    for (refs, slot, lo), q, v, qkd, r in zip(units, qs, vs, qkds, rs):
        refs[3][0, :, lo:lo + HEAD_DIM] = (
            _dot(qkd, v) + _dot((q * xi_ref[slot]).astype(BF16), r.astype(BF16))).astype(BF16)
    for (_, slot, _), k, v, r in zip(units, ks, vs, rs):
        g_chunk = jnp.exp(lg_row[:, slot:slot + 1] * float(TILE))
        r_ref[slot] = g_chunk * r + _dot((k * zeta_ref[slot]).T.astype(BF16), v)


def _ret(pr, logit_row):
    b, t_total, _ = pr.shape
    n_tiles = t_total // TILE

    def col(c, bwd):
        if bwd:
            return pl.BlockSpec((1, TILE, GROUP_W), lambda i, s: (i, _bwd_tile(s, n_tiles), c))
        return pl.BlockSpec((1, TILE, GROUP_W), lambda i, s: (i, s, c))

    return pl.pallas_call(
        _ret_kernel,
        grid=(b, n_tiles),
        in_specs=[col(0, False), col(1, False), col(2, False), col(0, True), col(1, True), col(2, True),
                  pl.BlockSpec((1, GATE_LANES), lambda i, s: (0, 0))],
        out_specs=[col(0, False), col(0, True)],
        out_shape=[jax.ShapeDtypeStruct((b, t_total, GROUP_W), BF16)] * 2,
        scratch_shapes=[pltpu.VMEM((2 * N_HEADS, HEAD_DIM, HEAD_DIM), F32),
                        pltpu.VMEM((2 * N_HEADS, TILE, TILE), F32),
                        pltpu.VMEM((2 * N_HEADS, TILE, HEAD_DIM), F32),
                        pltpu.VMEM((2 * N_HEADS, TILE, HEAD_DIM), F32)],
        compiler_params=_cparams(("parallel", "arbitrary")),
        name="ret",
    )(pr, pr, pr, pr, pr, pr, logit_row)


def _out_kernel(gf_ref, gb_ref, z_ref, rf_ref, rb_ref, rg_ref, x_ref, mod_ref, gdn_g_ref, ret_g_ref,
                nffn_g_ref, w_ref, x1_ref, h2_ref, y_ref, *, d_model):
    o = gf_ref[0].astype(F32) + gb_ref[0].astype(F32)
    r = rf_ref[0].astype(F32) + rb_ref[0].astype(F32)
    z = z_ref[0].astype(F32)
    rg = rg_ref[0].astype(F32)
    for h in range(N_HEADS):
        lo = h * HEAD_DIM
        o_h = o[:, lo:lo + HEAD_DIM]
        o_n = o_h * lax.rsqrt(jnp.mean(o_h * o_h, axis=-1, keepdims=True) + NORM_EPS) * gdn_g_ref[...]
        y_ref[:, lo:lo + HEAD_DIM] = (o_n * _silu(z[:, lo:lo + HEAD_DIM])).astype(BF16)
        r_h = r[:, lo:lo + HEAD_DIM]
        mu = jnp.mean(r_h, axis=-1, keepdims=True)
        cen = r_h - mu
        var = jnp.mean(cen * cen, axis=-1, keepdims=True)
        r_n = cen * lax.rsqrt(var + NORM_EPS) * ret_g_ref[...]
        y_ref[:, GROUP_W + lo:GROUP_W + lo + HEAD_DIM] = (r_n * _silu(rg[:, lo:lo + HEAD_DIM])).astype(BF16)
    mod = mod_ref[0]
    x1 = x_ref[0] + mod[:, 2 * d_model:3 * d_model] * _dot(y_ref[...], w_ref[...])
    x1_ref[0] = x1
    h2_ref[0] = _rms_mod(x1, nffn_g_ref[...], mod[:, 3 * d_model:4 * d_model],
                         mod[:, 4 * d_model:5 * d_model]).astype(BF16)


def _out(gdn_f, gdn_b, pa, ret_f, ret_b, pr, x, mod3, gdn_g, ret_g, nffn_g, w_out):
    b, seq, d = x.shape
    n_lat = seq // TILE
    first = (pa.shape[1] - seq) // TILE

    def tok(c):
        return pl.BlockSpec((1, TILE, GROUP_W), lambda i, t: (i, t + first, c))

    vec = pl.BlockSpec((1, HEAD_DIM), lambda i, t: (0, 0))
    return pl.pallas_call(
        functools.partial(_out_kernel, d_model=d),
        grid=(b, n_lat),
        in_specs=[tok(0), tok(0), tok(3), tok(0), tok(0), tok(3),
                  pl.BlockSpec((1, TILE, d), lambda i, t: (i, t, 0)),
                  pl.BlockSpec((1, 1, mod3.shape[2]), lambda i, t: (i, 0, 0)),
                  vec, vec,
                  pl.BlockSpec((1, d), lambda i, t: (0, 0)),
                  pl.BlockSpec(w_out.shape, lambda i, t: (0, 0))],
        out_specs=[pl.BlockSpec((1, TILE, d), lambda i, t: (i, t, 0)),
                   pl.BlockSpec((1, TILE, d), lambda i, t: (i, t, 0))],
        out_shape=[jax.ShapeDtypeStruct((b, seq, d), F32), jax.ShapeDtypeStruct((b, seq, d), BF16)],
        scratch_shapes=[pltpu.VMEM((TILE, 2 * GROUP_W), BF16)],
        compiler_params=_cparams(("parallel", "parallel")),
        name="outproj",
    )(gdn_f, gdn_b, pa, ret_f, ret_b, pr, x, mod3, gdn_g, ret_g, nffn_g, w_out)


def _ffn_kernel(h_ref, x1_ref, mod_ref, wg_ref, wu_ref, wo_ref, fg_ref, o_ref, acc_ref, *, d_model):
    f = pl.program_id(2)

    @pl.when(f == 0)
    def _():
        acc_ref[...] = jnp.zeros_like(acc_ref)

    h = h_ref[0]
    act = (_silu(_dot(h, wg_ref[...])) * _dot(h, wu_ref[...])).astype(BF16)
    acc_ref[...] += _dot(act, wo_ref[...])

    @pl.when(f == pl.num_programs(2) - 1)
    def _():
        x2 = x1_ref[0] + mod_ref[0][:, 5 * d_model:6 * d_model] * acc_ref[...]
        ms = jnp.mean(x2 * x2, axis=-1, keepdims=True)
        o_ref[0] = x2 * lax.rsqrt(ms + NORM_EPS) * fg_ref[...]


def _ffn(h2, x1, mod3, w_in, w_out, final_g):
    b, seq, d = x1.shape
    d_ff = w_out.shape[0]
    n_f = 2
    bf = d_ff // n_f
    return pl.pallas_call(
        functools.partial(_ffn_kernel, d_model=d),
        grid=(b, seq // FFN_TOK, n_f),
        in_specs=[pl.BlockSpec((1, FFN_TOK, d), lambda i, t, f: (i, t, 0)),
                  pl.BlockSpec((1, FFN_TOK, d), lambda i, t, f: (i, t, 0)),
                  pl.BlockSpec((1, 1, mod3.shape[2]), lambda i, t, f: (i, 0, 0)),
                  pl.BlockSpec((d, bf), lambda i, t, f: (0, f)),
                  pl.BlockSpec((d, bf), lambda i, t, f: (0, f + n_f)),
                  pl.BlockSpec((bf, d), lambda i, t, f: (f, 0)),
                  pl.BlockSpec((1, d), lambda i, t, f: (0, 0))],
        out_specs=pl.BlockSpec((1, FFN_TOK, d), lambda i, t, f: (i, t, 0)),
        out_shape=jax.ShapeDtypeStruct((b, seq, d), F32),
        scratch_shapes=[pltpu.VMEM((FFN_TOK, d), F32)],
        compiler_params=_cparams(("parallel", "parallel", "arbitrary")),
        name="ffn",
    )(h2, x1, mod3, w_in, w_in, w_out, final_g)


def _rope_tables(ctx_len, n_lat):
    def angles(pos, n_pairs):
        inv = ROPE_THETA ** (-jnp.arange(n_pairs, dtype=F32) / n_pairs)
        return pos[:, None] * inv[None, :]

    rows = n_lat // GRID_W
    row = jnp.repeat(jnp.arange(rows, dtype=F32), GRID_W)
    col = jnp.tile(jnp.arange(GRID_W, dtype=F32), rows)
    zeros = jnp.zeros((ctx_len,), F32)
    p_seq = jnp.concatenate([jnp.arange(ctx_len, dtype=F32), jnp.full((n_lat,), float(ctx_len), F32)])
    ang = jnp.concatenate([angles(p_seq, ROPE_PAIRS[0]),
                           angles(jnp.concatenate([zeros, row]), ROPE_PAIRS[1]),
                           angles(jnp.concatenate([zeros, col]), ROPE_PAIRS[2])], axis=-1)
    cos, sin = jnp.cos(ang), jnp.sin(ang)
    return jnp.concatenate([cos, cos], axis=-1), jnp.concatenate([-sin, sin], axis=-1)


def _lane_row(values):
    flat = values.reshape(1, -1).astype(F32)
    return jnp.pad(flat, ((0, 0), (0, GATE_LANES - flat.shape[1])))


def kernel(x, c, ctx, c_ctx, ada_w, ada_b, norm_mix_g, norm_ffn_g, w_in, conv_w, gdn_a_log, gdn_dt_bias,
           gdn_norm_g, ret_decay_logit, ret_norm_g, w_out, w_ffn_in, w_ffn_out, final_g):
    assert ada_w.shape[0] == 1, "single-layer block"
    b, seq, d = x.shape
    ctx_len = ctx.shape[1]
    assert ctx_len == TILE and seq % FFN_TOK == 0 and b + 1 <= 8

    cond = jnp.concatenate([c, c_ctx[None, :], jnp.zeros((8 - b - 1, d), F32)], axis=0)
    mod = _ada(cond, ada_w[0], ada_b)
    mod3 = mod[:, None, :]

    n_qkvz = 4 * GROUP_W
    n_gate = 4 * N_HEADS
    w = w_in[0]
    wa = w[:, :n_qkvz].astype(BF16)
    wab = jnp.pad(w[:, n_qkvz:n_qkvz + n_gate], ((0, 0), (0, GATE_LANES - n_gate))).astype(BF16)
    wr = w[:, n_qkvz + n_gate:].astype(BF16)
    cos2, sin2 = _rope_tables(ctx_len, seq)
    pa, ab, pr = _inproj(x, ctx, mod3, norm_mix_g, wa, wab, wr, cos2, sin2)

    qkv = _conv(pa, conv_w[0])
    dtb_row = _lane_row(gdn_dt_bias[0])
    gdn_f, gdn_b = _gdn(qkv, ab, dtb_row, _lane_row(gdn_a_log[0]))
    ret_f, ret_b = _ret(pr, _lane_row(ret_decay_logit[0]))

    x1, h2 = _out(gdn_f, gdn_b, pa, ret_f, ret_b, pr, x, mod3, gdn_norm_g, ret_norm_g, norm_ffn_g,
                  w_out[0].astype(BF16))
    return _ffn(h2, x1, mod3, w_ffn_in[0].astype(BF16), w_ffn_out[0].astype(BF16), final_g[None, :])
```

```python
import functools
import math

import jax
import jax.numpy as jnp
from jax import lax
from jax.experimental import pallas as pl
from jax.experimental.pallas import tpu as pltpu

F32 = jnp.float32
BF16 = jnp.bfloat16

HEAD_DIM = 128
N_HEADS = 4
GROUP_W = N_HEADS * HEAD_DIM
CONV_K = 5
GRID_W = 64
ROPE_THETA = 10000.0
ROPE_PAIRS = (16, 24, 24)
NORM_EPS = 1e-6

CHUNK = 64
TILE = 256
CHUNKS_PER_TILE = TILE // CHUNK
HALO = 16
GATE_LANES = 128
TAIL_TOK = 256
FFN_CHUNK = 1024
VMEM_LIMIT = 56 * 1024 * 1024


def _cparams(sem):
    return pltpu.CompilerParams(dimension_semantics=sem, vmem_limit_bytes=VMEM_LIMIT)


def _dot(a, b):
    return jnp.dot(a, b, preferred_element_type=F32)


def _dot_nt(a, b):
    return lax.dot_general(a, b, (((1,), (1,)), ((), ())), preferred_element_type=F32)


def _split(a):
    hi = a.astype(BF16)
    lo = (a - hi.astype(F32)).astype(BF16)
    return hi, lo


def _silu(x):
    return x * jax.nn.sigmoid(x)


def _lane_bcast(col, width):
    return jnp.broadcast_to(col, (col.shape[0], width))


def _ada_kernel(cond_ref, w_ref, b_ref, o_ref):
    a_hi, a_lo = _split(_silu(cond_ref[...]))
    w_hi, w_lo = _split(w_ref[...])
    o_ref[...] = _dot(a_hi, w_hi) + _dot(a_lo, w_hi) + _dot(a_hi, w_lo) + b_ref[...]


def _ada(cond, w, b):
    rows, d = cond.shape
    n = w.shape[1]
    bn = 1536
    return pl.pallas_call(
        _ada_kernel,
        grid=(n // bn,),
        in_specs=[pl.BlockSpec((rows, d), lambda j: (0, 0)),
                  pl.BlockSpec((d, bn), lambda j: (0, j)),
                  pl.BlockSpec((1, bn), lambda j: (0, j))],
        out_specs=pl.BlockSpec((rows, bn), lambda j: (0, j)),
        out_shape=jax.ShapeDtypeStruct((rows, n), F32),
        compiler_params=_cparams(("parallel",)),
        name="ada",
    )(cond, w, b)


def _rms_mod(x, g, shift, scale):
    ms = jnp.mean(x * x, axis=-1, keepdims=True)
    return (x * lax.rsqrt(ms + NORM_EPS) * g) * (1.0 + scale) + shift


def _inproj_kernel(x_ref, ctx_ref, mod_ref, g_ref, wa_ref, wab_ref, wr_ref, cos_ref, sin_ref,
                   pa_ref, ab_ref, pr_ref, *, d_model):
    t = pl.program_id(1)
    xin = jnp.where(t == 0, ctx_ref[0], x_ref[0])
    mod = mod_ref[0]
    h = _rms_mod(xin, g_ref[...], mod[:, 0:d_model], mod[:, d_model:2 * d_model]).astype(BF16)
    pa_ref[0] = _dot(h, wa_ref[...]).astype(BF16)
    ab_ref[0] = _dot(h, wab_ref[...])
    pr = _dot(h, wr_ref[...])
    cos2 = cos_ref[...]
    sin2 = sin_ref[...]
    k_scale = HEAD_DIM ** -0.5
    for hd in range(2 * N_HEADS):
        lo = hd * HEAD_DIM
        tt = pr[:, lo:lo + HEAD_DIM]
        rot = tt * cos2 + pltpu.roll(tt, HEAD_DIM // 2, 1) * sin2
        if hd >= N_HEADS:
            rot = rot * k_scale
        pr_ref[0, :, lo:lo + HEAD_DIM] = rot.astype(BF16)
    pr_ref[0, :, 2 * GROUP_W:] = pr[:, 2 * GROUP_W:].astype(BF16)


def _inproj(x, ctx, mod3, g, wa, wab, wr, cos2, sin2):
    b, seq, d = x.shape
    n_tiles = (ctx.shape[1] + seq) // TILE
    t_total = n_tiles * TILE
    return pl.pallas_call(
        functools.partial(_inproj_kernel, d_model=d),
        grid=(b, n_tiles),
        in_specs=[pl.BlockSpec((1, TILE, d), lambda i, t: (i, jnp.maximum(t - 1, 0), 0)),
                  pl.BlockSpec((1, TILE, d), lambda i, t: (i, 0, 0)),
                  pl.BlockSpec((1, 1, mod3.shape[2]), lambda i, t: (jnp.where(t == 0, b, i), 0, 0)),
                  pl.BlockSpec((1, d), lambda i, t: (0, 0)),
                  pl.BlockSpec(wa.shape, lambda i, t: (0, 0)),
                  pl.BlockSpec(wab.shape, lambda i, t: (0, 0)),
                  pl.BlockSpec(wr.shape, lambda i, t: (0, 0)),
                  pl.BlockSpec((TILE, HEAD_DIM), lambda i, t: (t, 0)),
                  pl.BlockSpec((TILE, HEAD_DIM), lambda i, t: (t, 0))],
        out_specs=[pl.BlockSpec((1, TILE, wa.shape[1]), lambda i, t: (i, t, 0)),
                   pl.BlockSpec((1, TILE, GATE_LANES), lambda i, t: (i, t, 0)),
                   pl.BlockSpec((1, TILE, wr.shape[1]), lambda i, t: (i, t, 0))],
        out_shape=[jax.ShapeDtypeStruct((b, t_total, wa.shape[1]), BF16),
                   jax.ShapeDtypeStruct((b, t_total, GATE_LANES), F32),
                   jax.ShapeDtypeStruct((b, t_total, wr.shape[1]), BF16)],
        compiler_params=_cparams(("parallel", "parallel")),
        name="inproj",
    )(x, ctx, mod3, g, wa, wab, wr, cos2, sin2)


def _conv_kernel(cur_ref, prev_ref, next_ref, w_ref, o_ref, ext_ref):
    t = pl.program_id(1)
    last = pl.num_programs(1) - 1
    has_prev = t >= 2
    has_next = jnp.logical_and(t >= 1, t < last)
    ext_ref[0:HALO, :] = jnp.where(has_prev, prev_ref[0].astype(F32), 0.0)
    ext_ref[HALO:HALO + TILE, :] = cur_ref[0].astype(F32)
    ext_ref[HALO + TILE:, :] = jnp.where(has_next, next_ref[0].astype(F32), 0.0)
    pad = (CONV_K - 1) // 2
    q_scale = HEAD_DIM ** -0.5
    for part in range(3):
        for hd in range(N_HEADS):
            lo = part * GROUP_W + hd * HEAD_DIM
            acc = None
            for i in range(CONV_K):
                r0 = HALO - pad + i
                term = ext_ref[r0:r0 + TILE, lo:lo + HEAD_DIM] * w_ref[i:i + 1, lo:lo + HEAD_DIM]
                acc = term if acc is None else acc + term
            y = _silu(acc)
            if part < 2:
                y = y * lax.rsqrt(jnp.sum(y * y, axis=-1, keepdims=True) + NORM_EPS)
            if part == 0:
                y = y * q_scale
            o_ref[0, :, lo:lo + HEAD_DIM] = y.astype(BF16)


def _conv(pa, conv_w):
    b, t_total, _ = pa.shape
    n_tiles = t_total // TILE
    width = 3 * GROUP_W
    per = TILE // HALO
    n_halo = t_total // HALO
    return pl.pallas_call(
        _conv_kernel,
        grid=(b, n_tiles),
        in_specs=[pl.BlockSpec((1, TILE, width), lambda i, t: (i, t, 0)),
                  pl.BlockSpec((1, HALO, width), lambda i, t: (i, jnp.maximum(t * per - 1, 0), 0)),
                  pl.BlockSpec((1, HALO, width), lambda i, t: (i, jnp.minimum((t + 1) * per, n_halo - 1), 0)),
                  pl.BlockSpec((CONV_K, width), lambda i, t: (0, 0))],
        out_specs=pl.BlockSpec((1, TILE, width), lambda i, t: (i, t, 0)),
        out_shape=jax.ShapeDtypeStruct((b, t_total, width), BF16),
        scratch_shapes=[pltpu.VMEM((TILE + 2 * HALO, width), F32)],
        compiler_params=_cparams(("parallel", "parallel")),
        name="conv",
    )(pa, pa, pa, conv_w)


def _packed_matmul(x, y, bd_mask):
    yb = y.astype(BF16)
    y_bd = jnp.where(bd_mask, jnp.concatenate([yb] * N_HEADS, axis=0), jnp.zeros((), BF16))
    return _dot(x.astype(BF16), y_bd)


def _unit_lower_inverses(a_list, eye, m16, m32, m64, bd_mask):
    pm = functools.partial(_packed_matmul, bd_mask=bd_mask)
    ps = [-jnp.where(m16, a, 0.0) for a in a_list]
    xs = [eye + p for p in ps]
    for _ in range(3):
        ps = [pm(p, p) for p in ps]
        xs = [x + pm(x, p) for x, p in zip(xs, ps)]
    for m in (m32, m64):
        zs = [pm(jnp.where(m, a, 0.0), x) for a, x in zip(a_list, xs)]
        xs = [x - pm(x, z) for x, z in zip(xs, zs)]
    return xs


def _gdn_prepare(groups, masks):
    eye4, m16, m32, m64, bd_mask = masks
    for g in groups:
        tcum, mstrict = g["consts"][:2]
        g_lane0 = N_HEADS if g["rev"] else 0
        last = 0 if g["rev"] else CHUNK - 1
        gates = g["gates"]
        g_hi, g_lo = _split(gates)
        gc = _dot(tcum, g_hi) + _dot(tcum, g_lo)
        g_last = gc[last:last + 1, :]
        g["e_g"] = jnp.exp(gc)
        g["e_gl"] = jnp.exp(g_last - gc)
        g["e_last"] = jnp.exp(g_last)
        y = jnp.concatenate(
            [jnp.where(mstrict, _lane_bcast(gates[:, g_lane0 + h:g_lane0 + h + 1], CHUNK), 0.0)
             for h in range(N_HEADS)], axis=1)
        y_hi, y_lo = _split(y)
        g["e"] = jnp.exp(_dot(tcum, y_hi) + _dot(tcum, y_lo))

    for g in groups:
        b_lane0 = 2 * N_HEADS + (N_HEADS if g["rev"] else 0)
        qk_parts, kk_parts, g["kb"], g["beta"] = [], [], [], []
        for h in range(N_HEADS):
            lo = h * HEAD_DIM
            beta = _lane_bcast(g["gates"][:, b_lane0 + h:b_lane0 + h + 1], HEAD_DIM)
            k_h = g["k"][:, lo:lo + HEAD_DIM]
            kb_h = k_h * beta
            lhs = jnp.concatenate([g["q"][:, lo:lo + HEAD_DIM], kb_h.astype(BF16)], axis=0)
            m = _dot_nt(lhs, k_h)
            qk_parts.append(m[:CHUNK])
            kk_parts.append(m[CHUNK:])
            g["kb"].append(kb_h)
            g["beta"].append(beta)
        incl4, strict4 = g["consts"][2:]
        g["a"] = jnp.where(strict4, jnp.concatenate(kk_parts, axis=1) * g["e"], 0.0)
        g["qkd"] = jnp.where(incl4, jnp.concatenate(qk_parts, axis=1) * g["e"], 0.0).astype(BF16)

    t_invs = _unit_lower_inverses([g["a"] for g in groups], eye4, m16, m32, m64, bd_mask)

    for g, t_inv in zip(groups, t_invs):
        g_lane0 = N_HEADS if g["rev"] else 0
        t_b = t_inv.astype(BF16)
        g["u"], g["wq"], g["kd_t"] = [], [], []
        for h in range(N_HEADS):
            lo = h * HEAD_DIM
            gl = g_lane0 + h
            eg_h = _lane_bcast(g["e_g"][:, gl:gl + 1], HEAD_DIM)
            egl_h = _lane_bcast(g["e_gl"][:, gl:gl + 1], HEAD_DIM)
            rhs = jnp.concatenate([g["v"][:, lo:lo + HEAD_DIM] * g["beta"][h], g["kb"][h] * eg_h], axis=1)
            sol = _dot(t_b[:, h * CHUNK:(h + 1) * CHUNK], rhs.astype(BF16))
            g["u"].append(sol[:, :HEAD_DIM])
            g["wq"].append(jnp.concatenate([sol[:, HEAD_DIM:], g["q"][:, lo:lo + HEAD_DIM] * eg_h],
                                           axis=0).astype(BF16))
            g["kd_t"].append((g["k"][:, lo:lo + HEAD_DIM] * egl_h).T.astype(BF16))


def _gdn_scan_step(pair, s_ref):
    units = [(g, h, (N_HEADS if g["rev"] else 0) + h) for g in pair for h in range(N_HEADS)]
    states = [s_ref[slot] for _, _, slot in units]
    wss = [_dot(g["wq"][h], s.astype(BF16)) for (g, h, _), s in zip(units, states)]
    v_news = [(g["u"][h] - ws[:CHUNK]).astype(BF16) for (g, h, _), ws in zip(units, wss)]
    outs = [ws[CHUNK:] + _dot(g["qkd"][:, h * CHUNK:(h + 1) * CHUNK], vn)
            for (g, h, _), ws, vn in zip(units, wss, v_news)]
    for (g, h, slot), s, vn in zip(units, states, v_news):
        s_ref[slot] = g["e_last"][:, slot:slot + 1] * s + _dot(g["kd_t"][h], vn)
    return [jnp.concatenate(outs[i * N_HEADS:(i + 1) * N_HEADS], axis=1) for i in range(len(pair))]


def _gate_tile(ab, dtb, a_log):
    z = ab + dtb
    softplus = jnp.maximum(z, 0.0) + jnp.log(1.0 + jnp.exp(-jnp.abs(z)))
    lane = lax.broadcasted_iota(jnp.int32, ab.shape, 1)
    return jnp.where(lane < 2 * N_HEADS, -jnp.exp(a_log) * softplus, jax.nn.sigmoid(ab))


def _gdn_kernel(qf_ref, kf_ref, vf_ref, abf_ref, qb_ref, kb_ref, vb_ref, abb_ref, dtb_ref, alog_ref,
                of_ref, ob_ref, s_ref):
    @pl.when(pl.program_id(1) == 0)
    def _():
        s_ref[...] = jnp.zeros_like(s_ref)

    ri = lax.broadcasted_iota(jnp.int32, (CHUNK, CHUNK), 0)
    ci = lax.broadcasted_iota(jnp.int32, (CHUNK, CHUNK), 1)
    ri4 = lax.broadcasted_iota(jnp.int32, (CHUNK, N_HEADS * CHUNK), 0)
    ci4 = lax.broadcasted_iota(jnp.int32, (CHUNK, N_HEADS * CHUNK), 1) % CHUNK
    rbd = lax.broadcasted_iota(jnp.int32, (N_HEADS * CHUNK, N_HEADS * CHUNK), 0) // CHUNK
    cbd = lax.broadcasted_iota(jnp.int32, (N_HEADS * CHUNK, N_HEADS * CHUNK), 1) // CHUNK
    bd_mask = rbd == cbd
    eye4 = (ri4 == ci4).astype(F32)
    m16 = (ri4 // 16) == (ci4 // 16)
    m32 = jnp.logical_and((ri4 // 32) == (ci4 // 32), jnp.logical_not(m16))
    m64 = (ri4 // 32) != (ci4 // 32)

    cf = ((ci <= ri).astype(BF16), ri > ci, ri4 >= ci4, ri4 > ci4)
    cb = ((ci >= ri).astype(BF16), ri < ci, ri4 <= ci4, ri4 < ci4)
    dtb = dtb_ref[...]
    alog = alog_ref[...]

    pairs = []
    for j in range(CHUNKS_PER_TILE):
        rf = j * CHUNK
        rb = (CHUNKS_PER_TILE - 1 - j) * CHUNK
        fwd = dict(q=qf_ref[0, rf:rf + CHUNK, :], k=kf_ref[0, rf:rf + CHUNK, :], v=vf_ref[0, rf:rf + CHUNK, :],
                   gates=_gate_tile(abf_ref[0, rf:rf + CHUNK, :], dtb, alog), rev=False, consts=cf, row=rf)
        bwd = dict(q=qb_ref[0, rb:rb + CHUNK, :], k=kb_ref[0, rb:rb + CHUNK, :], v=vb_ref[0, rb:rb + CHUNK, :],
                   gates=_gate_tile(abb_ref[0, rb:rb + CHUNK, :], dtb, alog), rev=True, consts=cb, row=rb)
        pairs.append((fwd, bwd))
    _gdn_prepare([g for pair in pairs for g in pair], (eye4, m16, m32, m64, bd_mask))
    for fwd, bwd in pairs:
        o_f, o_b = _gdn_scan_step((fwd, bwd), s_ref)
        of_ref[0, fwd["row"]:fwd["row"] + CHUNK, :] = o_f.astype(BF16)
        ob_ref[0, bwd["row"]:bwd["row"] + CHUNK, :] = o_b.astype(BF16)


def _bwd_tile(s, n_tiles):
    return jnp.where(s == 0, 0, n_tiles - s)


def _gdn(qkv, ab, dtb_row, alog_row):
    b, t_total, _ = qkv.shape
    n_tiles = t_total // TILE

    def col(c, bwd):
        if bwd:
            return pl.BlockSpec((1, TILE, GROUP_W), lambda i, s: (i, _bwd_tile(s, n_tiles), c))
        return pl.BlockSpec((1, TILE, GROUP_W), lambda i, s: (i, s, c))

    def gate(bwd):
        if bwd:
            return pl.BlockSpec((1, TILE, GATE_LANES), lambda i, s: (i, _bwd_tile(s, n_tiles), 0))
        return pl.BlockSpec((1, TILE, GATE_LANES), lambda i, s: (i, s, 0))

    row = pl.BlockSpec((1, GATE_LANES), lambda i, s: (0, 0))
    return pl.pallas_call(
        _gdn_kernel,
        grid=(b, n_tiles),
        in_specs=[col(0, False), col(1, False), col(2, False), gate(False),
                  col(0, True), col(1, True), col(2, True), gate(True), row, row],
        out_specs=[col(0, False), col(0, True)],
        out_shape=[jax.ShapeDtypeStruct((b, t_total, GROUP_W), BF16)] * 2,
        scratch_shapes=[pltpu.VMEM((2 * N_HEADS, HEAD_DIM, HEAD_DIM), F32)],
        compiler_params=_cparams(("parallel", "arbitrary")),
        name="gdn",
    )(qkv, qkv, qkv, ab, qkv, qkv, qkv, ab, dtb_row, alog_row)


def _ret_kernel(qf_ref, kf_ref, vf_ref, qb_ref, kb_ref, vb_ref, logit_ref, of_ref, ob_ref,
                r_ref, decay_ref, xi_ref, zeta_ref):
    x = logit_ref[...]
    lg_row = jnp.minimum(x, 0.0) - jnp.log(1.0 + jnp.exp(-jnp.abs(x)))

    @pl.when(pl.program_id(1) == 0)
    def _():
        r_ref[...] = jnp.zeros_like(r_ref)
        ri = lax.broadcasted_iota(jnp.int32, (TILE, TILE), 0)
        ci = lax.broadcasted_iota(jnp.int32, (TILE, TILE), 1)
        row = lax.broadcasted_iota(jnp.int32, (TILE, HEAD_DIM), 0).astype(F32)
        for slot in range(2 * N_HEADS):
            rev = slot >= N_HEADS
            lg = lg_row[:, slot:slot + 1]
            diff = ((ci - ri) if rev else (ri - ci)).astype(F32)
            pos = ((TILE - 1.0) - row) if rev else row
            decay_ref[slot] = jnp.where(diff >= 0, jnp.exp(lg * jnp.maximum(diff, 0.0)), 0.0)
            xi_ref[slot] = jnp.exp(lg * (pos + 1.0))
            zeta_ref[slot] = jnp.exp(lg * ((TILE - 1.0) - pos))

    units = [(refs, (N_HEADS if rev else 0) + h, h * HEAD_DIM)
             for refs, rev in (((qf_ref, kf_ref, vf_ref, of_ref), False), ((qb_ref, kb_ref, vb_ref, ob_ref), True))
             for h in range(N_HEADS)]
    qs = [refs[0][0, :, lo:lo + HEAD_DIM] for refs, _, lo in units]
    ks = [refs[1][0, :, lo:lo + HEAD_DIM] for refs, _, lo in units]
    vs = [refs[2][0, :, lo:lo + HEAD_DIM] for refs, _, lo in units]
    qks = [_dot_nt(q, k) for q, k in zip(qs, ks)]
    qkds = [(qk * decay_ref[slot]).astype(BF16) for qk, (_, slot, _) in zip(qks, units)]
    rs = [r_ref[slot] for _, slot, _ in units]
    for (refs, slot, lo), q, v, qkd, r in zip(units, qs, vs, qkds, rs):
        refs[3][0, :, lo:lo + HEAD_DIM] = (
            _dot(qkd, v) + _dot((q * xi_ref[slot]).astype(BF16), r.astype(BF16))).astype(BF16)
    for (_, slot, _), k, v, r in zip(units, ks, vs, rs):
        g_chunk = jnp.exp(lg_row[:, slot:slot + 1] * float(TILE))
        r_ref[slot] = g_chunk * r + _dot((k * zeta_ref[slot]).T.astype(BF16), v)


def _ret(pr, logit_row):
    b, t_total, _ = pr.shape
    n_tiles = t_total // TILE

    def col(c, bwd):
        if bwd:
            return pl.BlockSpec((1, TILE, GROUP_W), lambda i, s: (i, _bwd_tile(s, n_tiles), c))
        return pl.BlockSpec((1, TILE, GROUP_W), lambda i, s: (i, s, c))

    return pl.pallas_call(
        _ret_kernel,
        grid=(b, n_tiles),
        in_specs=[col(0, False), col(1, False), col(2, False), col(0, True), col(1, True), col(2, True),
                  pl.BlockSpec((1, GATE_LANES), lambda i, s: (0, 0))],
        out_specs=[col(0, False), col(0, True)],
        out_shape=[jax.ShapeDtypeStruct((b, t_total, GROUP_W), BF16)] * 2,
        scratch_shapes=[pltpu.VMEM((2 * N_HEADS, HEAD_DIM, HEAD_DIM), F32),
                        pltpu.VMEM((2 * N_HEADS, TILE, TILE), F32),
                        pltpu.VMEM((2 * N_HEADS, TILE, HEAD_DIM), F32),
                        pltpu.VMEM((2 * N_HEADS, TILE, HEAD_DIM), F32)],
        compiler_params=_cparams(("parallel", "arbitrary")),
        name="ret",
    )(pr, pr, pr, pr, pr, pr, logit_row)


def _tail_kernel(gf_ref, gb_ref, z_ref, rf_ref, rb_ref, rg_ref, x_ref, mod_ref, gdn_g_ref, ret_g_ref,
                 nffn_g_ref, w_ref, wi_ref, wo_ref, fg_ref, o_ref, y_ref, *, d_model, d_ff):
    o = gf_ref[0].astype(F32) + gb_ref[0].astype(F32)
    r = rf_ref[0].astype(F32) + rb_ref[0].astype(F32)
    z = z_ref[0].astype(F32)
    rg = rg_ref[0].astype(F32)
    for h in range(N_HEADS):
        lo = h * HEAD_DIM
        o_h = o[:, lo:lo + HEAD_DIM]
        o_n = o_h * lax.rsqrt(jnp.mean(o_h * o_h, axis=-1, keepdims=True) + NORM_EPS) * gdn_g_ref[...]
        y_ref[:, lo:lo + HEAD_DIM] = (o_n * _silu(z[:, lo:lo + HEAD_DIM])).astype(BF16)
        r_h = r[:, lo:lo + HEAD_DIM]
        mu = jnp.mean(r_h, axis=-1, keepdims=True)
        cen = r_h - mu
        var = jnp.mean(cen * cen, axis=-1, keepdims=True)
        r_n = cen * lax.rsqrt(var + NORM_EPS) * ret_g_ref[...]
        y_ref[:, GROUP_W + lo:GROUP_W + lo + HEAD_DIM] = (r_n * _silu(rg[:, lo:lo + HEAD_DIM])).astype(BF16)
    mod = mod_ref[0]
    x1 = x_ref[0] + mod[:, 2 * d_model:3 * d_model] * _dot(y_ref[...], w_ref[...])
    h2 = _rms_mod(x1, nffn_g_ref[...], mod[:, 3 * d_model:4 * d_model],
                  mod[:, 4 * d_model:5 * d_model]).astype(BF16)
    acc = None
    for lo in range(0, d_ff, FFN_CHUNK):
        hi = min(lo + FFN_CHUNK, d_ff)
        gate = _dot(h2, wi_ref[:, lo:hi])
        up = _dot(h2, wi_ref[:, d_ff + lo:d_ff + hi])
        part = _dot((_silu(gate) * up).astype(BF16), wo_ref[lo:hi, :])
        acc = part if acc is None else acc + part
    x2 = x1 + mod[:, 5 * d_model:6 * d_model] * acc
    ms = jnp.mean(x2 * x2, axis=-1, keepdims=True)
    o_ref[0] = x2 * lax.rsqrt(ms + NORM_EPS) * fg_ref[...]


def _tail(gdn_f, gdn_b, pa, ret_f, ret_b, pr, x, mod3, gdn_g, ret_g, nffn_g, w_out, w_ffn_in, w_ffn_out,
          final_g):
    b, seq, d = x.shape
    d_ff = w_ffn_out.shape[0]
    first = (pa.shape[1] - seq) // TAIL_TOK

    def tok(c):
        return pl.BlockSpec((1, TAIL_TOK, GROUP_W), lambda i, t: (i, t + first, c))

    def resident(shape):
        return pl.BlockSpec(shape, lambda i, t: (0,) * len(shape), pipeline_mode=pl.Buffered(1))

    return pl.pallas_call(
        functools.partial(_tail_kernel, d_model=d, d_ff=d_ff),
        grid=(b, seq // TAIL_TOK),
        in_specs=[tok(0), tok(0), tok(3), tok(0), tok(0), tok(3),
                  pl.BlockSpec((1, TAIL_TOK, d), lambda i, t: (i, t, 0)),
                  pl.BlockSpec((1, 1, mod3.shape[2]), lambda i, t: (i, 0, 0)),
                  resident((1, HEAD_DIM)), resident((1, HEAD_DIM)), resident((1, d)),
                  resident(w_out.shape), resident(w_ffn_in.shape), resident(w_ffn_out.shape),
                  resident((1, d))],
        out_specs=pl.BlockSpec((1, TAIL_TOK, d), lambda i, t: (i, t, 0)),
        out_shape=jax.ShapeDtypeStruct((b, seq, d), F32),
        scratch_shapes=[pltpu.VMEM((TAIL_TOK, 2 * GROUP_W), BF16)],
        compiler_params=_cparams(("parallel", "parallel")),
        name="tail",
    )(gdn_f, gdn_b, pa, ret_f, ret_b, pr, x, mod3, gdn_g, ret_g, nffn_g, w_out, w_ffn_in, w_ffn_out, final_g)


def _rope_tables(ctx_len, n_lat):
    def angles(pos, n_pairs):
        inv = ROPE_THETA ** (-jnp.arange(n_pairs, dtype=F32) / n_pairs)
        return pos[:, None] * inv[None, :]

    rows = n_lat // GRID_W
    row = jnp.repeat(jnp.arange(rows, dtype=F32), GRID_W)
    col = jnp.tile(jnp.arange(GRID_W, dtype=F32), rows)
    zeros = jnp.zeros((ctx_len,), F32)
    p_seq = jnp.concatenate([jnp.arange(ctx_len, dtype=F32), jnp.full((n_lat,), float(ctx_len), F32)])
    ang = jnp.concatenate([angles(p_seq, ROPE_PAIRS[0]),
                           angles(jnp.concatenate([zeros, row]), ROPE_PAIRS[1]),
                           angles(jnp.concatenate([zeros, col]), ROPE_PAIRS[2])], axis=-1)
    cos, sin = jnp.cos(ang), jnp.sin(ang)
    return jnp.concatenate([cos, cos], axis=-1), jnp.concatenate([-sin, sin], axis=-1)


def _lane_row(values):
    flat = values.reshape(1, -1).astype(F32)
    return jnp.pad(flat, ((0, 0), (0, GATE_LANES - flat.shape[1])))


def kernel(x, c, ctx, c_ctx, ada_w, ada_b, norm_mix_g, norm_ffn_g, w_in, conv_w, gdn_a_log, gdn_dt_bias,
           gdn_norm_g, ret_decay_logit, ret_norm_g, w_out, w_ffn_in, w_ffn_out, final_g):
    assert ada_w.shape[0] == 1, "single-layer block"
    b, seq, d = x.shape
    ctx_len = ctx.shape[1]
    assert ctx_len == TILE and seq % TILE == 0 and TILE % TAIL_TOK == 0 and b + 1 <= 8

    cond = jnp.concatenate([c, c_ctx[None, :], jnp.zeros((8 - b - 1, d), F32)], axis=0)
    mod = _ada(cond, ada_w[0], ada_b)
    mod3 = mod[:, None, :]

    n_qkvz = 4 * GROUP_W
    n_gate = 4 * N_HEADS
    w = w_in[0]
    wa = w[:, :n_qkvz].astype(BF16)
    wab = jnp.pad(w[:, n_qkvz:n_qkvz + n_gate], ((0, 0), (0, GATE_LANES - n_gate))).astype(BF16)
    wr = w[:, n_qkvz + n_gate:].astype(BF16)
    cos2, sin2 = _rope_tables(ctx_len, seq)
    pa, ab, pr = _inproj(x, ctx, mod3, norm_mix_g, wa, wab, wr, cos2, sin2)

    qkv = _conv(pa, conv_w[0])
    dtb_row = _lane_row(gdn_dt_bias[0])
    gdn_f, gdn_b = _gdn(qkv, ab, dtb_row, _lane_row(gdn_a_log[0]))
    ret_f, ret_b = _ret(pr, _lane_row(ret_decay_logit[0]))

    return _tail(gdn_f, gdn_b, pa, ret_f, ret_b, pr, x, mod3, gdn_norm_g, ret_norm_g, norm_ffn_g,
                 w_out[0].astype(BF16), w_ffn_in[0].astype(BF16), w_ffn_out[0].astype(BF16), final_g[None, :])
```

```python
import functools
import math

import jax
import jax.numpy as jnp
from jax import lax
from jax.experimental import pallas as pl
from jax.experimental.pallas import tpu as pltpu

F32 = jnp.float32
BF16 = jnp.bfloat16

HEAD_DIM = 128
N_HEADS = 4
GROUP_W = N_HEADS * HEAD_DIM
CONV_K = 5
GRID_W = 64
ROPE_THETA = 10000.0
ROPE_PAIRS = (16, 24, 24)
NORM_EPS = 1e-6

CHUNK = 64
TILE = 256
CHUNKS_PER_TILE = TILE // CHUNK
HALO = 16
GATE_LANES = 128
TAIL_TOK = 256
FFN_CHUNK = 1024
VMEM_LIMIT = 56 * 1024 * 1024


def _cparams(sem):
    return pltpu.CompilerParams(dimension_semantics=sem, vmem_limit_bytes=VMEM_LIMIT)


def _dot(a, b):
    return jnp.dot(a, b, preferred_element_type=F32)


def _dot_nt(a, b):
    return lax.dot_general(a, b, (((1,), (1,)), ((), ())), preferred_element_type=F32)


def _split(a):
    hi = a.astype(BF16)
    lo = (a - hi.astype(F32)).astype(BF16)
    return hi, lo


def _silu(x):
    return x * jax.nn.sigmoid(x)


def _lane_bcast(col, width):
    return jnp.broadcast_to(col, (col.shape[0], width))


def _ada_kernel(cond_ref, w_ref, b_ref, o_ref):
    a_hi, a_lo = _split(_silu(cond_ref[...]))
    w_hi, w_lo = _split(w_ref[...])
    o_ref[...] = _dot(a_hi, w_hi) + _dot(a_lo, w_hi) + _dot(a_hi, w_lo) + b_ref[...]


def _ada(cond, w, b):
    rows, d = cond.shape
    n = w.shape[1]
    bn = 1536
    return pl.pallas_call(
        _ada_kernel,
        grid=(n // bn,),
        in_specs=[pl.BlockSpec((rows, d), lambda j: (0, 0)),
                  pl.BlockSpec((d, bn), lambda j: (0, j)),
                  pl.BlockSpec((1, bn), lambda j: (0, j))],
        out_specs=pl.BlockSpec((rows, bn), lambda j: (0, j)),
        out_shape=jax.ShapeDtypeStruct((rows, n), F32),
        compiler_params=_cparams(("parallel",)),
        name="ada",
    )(cond, w, b)


def _rms_mod(x, g, shift, scale):
    ms = jnp.mean(x * x, axis=-1, keepdims=True)
    return (x * lax.rsqrt(ms + NORM_EPS) * g) * (1.0 + scale) + shift


def _inproj_kernel(x_ref, ctx_ref, mod_ref, g_ref, wa_ref, wab_ref, wr_ref, cos_ref, sin_ref,
                   pa_ref, ab_ref, pr_ref, *, d_model):
    t = pl.program_id(1)
    xin = jnp.where(t == 0, ctx_ref[0], x_ref[0])
    mod = mod_ref[0]
    h = _rms_mod(xin, g_ref[...], mod[:, 0:d_model], mod[:, d_model:2 * d_model]).astype(BF16)
    pa_ref[0] = _dot(h, wa_ref[...]).astype(BF16)
    ab_ref[0] = _dot(h, wab_ref[...])
    pr = _dot(h, wr_ref[...])
    cos2 = cos_ref[...]
    sin2 = sin_ref[...]
    k_scale = HEAD_DIM ** -0.5
    for hd in range(2 * N_HEADS):
        lo = hd * HEAD_DIM
        tt = pr[:, lo:lo + HEAD_DIM]
        rot = tt * cos2 + pltpu.roll(tt, HEAD_DIM // 2, 1) * sin2
        if hd >= N_HEADS:
            rot = rot * k_scale
        pr_ref[0, :, lo:lo + HEAD_DIM] = rot.astype(BF16)
    pr_ref[0, :, 2 * GROUP_W:] = pr[:, 2 * GROUP_W:].astype(BF16)


def _inproj(x, ctx, mod3, g, wa, wab, wr, cos2, sin2):
    b, seq, d = x.shape
    n_tiles = (ctx.shape[1] + seq) // TILE
    t_total = n_tiles * TILE
    return pl.pallas_call(
        functools.partial(_inproj_kernel, d_model=d),
        grid=(b, n_tiles),
        in_specs=[pl.BlockSpec((1, TILE, d), lambda i, t: (i, jnp.maximum(t - 1, 0), 0)),
                  pl.BlockSpec((1, TILE, d), lambda i, t: (i, 0, 0)),
                  pl.BlockSpec((1, 1, mod3.shape[2]), lambda i, t: (jnp.where(t == 0, b, i), 0, 0)),
                  pl.BlockSpec((1, d), lambda i, t: (0, 0)),
                  pl.BlockSpec(wa.shape, lambda i, t: (0, 0)),
                  pl.BlockSpec(wab.shape, lambda i, t: (0, 0)),
                  pl.BlockSpec(wr.shape, lambda i, t: (0, 0)),
                  pl.BlockSpec((TILE, HEAD_DIM), lambda i, t: (t, 0)),
                  pl.BlockSpec((TILE, HEAD_DIM), lambda i, t: (t, 0))],
        out_specs=[pl.BlockSpec((1, TILE, wa.shape[1]), lambda i, t: (i, t, 0)),
                   pl.BlockSpec((1, TILE, GATE_LANES), lambda i, t: (i, t, 0)),
                   pl.BlockSpec((1, TILE, wr.shape[1]), lambda i, t: (i, t, 0))],
        out_shape=[jax.ShapeDtypeStruct((b, t_total, wa.shape[1]), BF16),
                   jax.ShapeDtypeStruct((b, t_total, GATE_LANES), F32),
                   jax.ShapeDtypeStruct((b, t_total, wr.shape[1]), BF16)],
        compiler_params=_cparams(("parallel", "parallel")),
        name="inproj",
    )(x, ctx, mod3, g, wa, wab, wr, cos2, sin2)


def _conv_kernel(cur_ref, prev_ref, next_ref, w_ref, o_ref, ext_ref):
    t = pl.program_id(1)
    last = pl.num_programs(1) - 1
    has_prev = t >= 2
    has_next = jnp.logical_and(t >= 1, t < last)
    ext_ref[0:HALO, :] = jnp.where(has_prev, prev_ref[0].astype(F32), 0.0)
    ext_ref[HALO:HALO + TILE, :] = cur_ref[0].astype(F32)
    ext_ref[HALO + TILE:, :] = jnp.where(has_next, next_ref[0].astype(F32), 0.0)
    pad = (CONV_K - 1) // 2
    q_scale = HEAD_DIM ** -0.5
    for part in range(3):
        for hd in range(N_HEADS):
            lo = part * GROUP_W + hd * HEAD_DIM
            acc = None
            for i in range(CONV_K):
                r0 = HALO - pad + i
                term = ext_ref[r0:r0 + TILE, lo:lo + HEAD_DIM] * w_ref[i:i + 1, lo:lo + HEAD_DIM]
                acc = term if acc is None else acc + term
            y = _silu(acc)
            if part < 2:
                y = y * lax.rsqrt(jnp.sum(y * y, axis=-1, keepdims=True) + NORM_EPS)
            if part == 0:
                y = y * q_scale
            o_ref[0, :, lo:lo + HEAD_DIM] = y.astype(BF16)


def _conv(pa, conv_w):
    b, t_total, _ = pa.shape
    n_tiles = t_total // TILE
    width = 3 * GROUP_W
    per = TILE // HALO
    n_halo = t_total // HALO
    return pl.pallas_call(
        _conv_kernel,
        grid=(b, n_tiles),
        in_specs=[pl.BlockSpec((1, TILE, width), lambda i, t: (i, t, 0)),
                  pl.BlockSpec((1, HALO, width), lambda i, t: (i, jnp.maximum(t * per - 1, 0), 0)),
                  pl.BlockSpec((1, HALO, width), lambda i, t: (i, jnp.minimum((t + 1) * per, n_halo - 1), 0)),
                  pl.BlockSpec((CONV_K, width), lambda i, t: (0, 0))],
        out_specs=pl.BlockSpec((1, TILE, width), lambda i, t: (i, t, 0)),
        out_shape=jax.ShapeDtypeStruct((b, t_total, width), BF16),
        scratch_shapes=[pltpu.VMEM((TILE + 2 * HALO, width), F32)],
        compiler_params=_cparams(("parallel", "parallel")),
        name="conv",
    )(pa, pa, pa, conv_w)


def _packed_matmul(x, y, bd_mask):
    yb = y.astype(BF16)
    y_bd = jnp.where(bd_mask, jnp.concatenate([yb] * N_HEADS, axis=0), jnp.zeros((), BF16))
    return _dot(x.astype(BF16), y_bd)


def _unit_lower_inverse_stages(a_list, eye, m16, m32, m64, bd_mask, out):
    pm = functools.partial(_packed_matmul, bd_mask=bd_mask)
    ps = [-jnp.where(m16, a, 0.0) for a in a_list]
    xs = [eye + p for p in ps]
    for _ in range(3):
        ps = [pm(p, p) for p in ps]
        yield
        xs = [x + pm(x, p) for x, p in zip(xs, ps)]
        yield
    for m in (m32, m64):
        zs = [pm(jnp.where(m, a, 0.0), x) for a, x in zip(a_list, xs)]
        yield
        xs = [x - pm(x, z) for x, z in zip(xs, zs)]
        yield
    out.extend(xs)


def _gdn_front_stages(groups, ring_base, mid_base, a_scr, rhs_scr, brow_scr, wq_scr, qkd_scr, kdt_scr,
                      el_scr):
    lane = lax.broadcasted_iota(jnp.int32, (CHUNK, GATE_LANES), 1)
    low_half = lax.broadcasted_iota(jnp.int32, (CHUNK, HEAD_DIM), 1) < CHUNK
    zero_k = jnp.zeros((CHUNK, HEAD_DIM), BF16)
    for g in groups:
        tcum = g["consts"][0]
        g_hi, g_lo = _split(g["gates"])
        g["gc"] = _dot(tcum, g_hi) + _dot(tcum, g_lo)
    yield

    for gi, g in enumerate(groups):
        g_lane0 = N_HEADS if g["rev"] else 0
        b_lane0 = 2 * N_HEADS + g_lane0
        last = 0 if g["rev"] else CHUNK - 1
        gc, gates = g["gc"], g["gates"]
        ct = jnp.where(lane < 2 * N_HEADS, gc, gates).T
        egl_t = jnp.exp(ct[:, last:last + 1] - ct)
        gcb, kb, ms = [], [], []
        for h in range(N_HEADS):
            lo = h * HEAD_DIM
            gl = g_lane0 + h
            ring_unit = (ring_base + gi) * N_HEADS + h
            gcb.append(_lane_bcast(gc[:, gl:gl + 1], HEAD_DIM))
            beta = _lane_bcast(gates[:, b_lane0 + h:b_lane0 + h + 1], HEAD_DIM)
            eg_h = jnp.exp(gcb[h])
            q_h = g["q"][:, lo:lo + HEAD_DIM]
            k_h = g["k"][:, lo:lo + HEAD_DIM].astype(F32)
            kb.append((k_h * beta).astype(BF16))
            rhs_scr[(mid_base + gi) * N_HEADS + h] = jnp.concatenate(
                [g["v"][:, lo:lo + HEAD_DIM], (k_h * eg_h).astype(BF16)], axis=1)
            wq_scr[ring_unit, CHUNK:, :] = (q_h * eg_h).astype(BF16)
            kdt_scr[ring_unit] = (k_h.T * egl_t[gl:gl + 1, :]).astype(BF16)
        d_parts = []
        for p in range(N_HEADS // 2):
            h0, h1 = 2 * p, 2 * p + 1
            lo = h0 * HEAD_DIM
            lhs = jnp.concatenate([g["q"][:, lo:lo + 2 * HEAD_DIM], jnp.concatenate([kb[h0], kb[h1]], axis=1)],
                                  axis=0)
            k0 = g["k"][:, lo:lo + HEAD_DIM]
            k1 = g["k"][:, lo + HEAD_DIM:lo + 2 * HEAD_DIM]
            rhs_bd = jnp.concatenate([jnp.concatenate([k0, zero_k], axis=1),
                                      jnp.concatenate([zero_k, k1], axis=1)], axis=0)
            ms.append(_dot_nt(lhs, rhs_bd))
            g_row = jnp.concatenate([ct[g_lane0 + h0:g_lane0 + h0 + 1, :], ct[g_lane0 + h1:g_lane0 + h1 + 1, :]],
                                    axis=1)
            d_parts.append(jnp.where(low_half, gcb[h0], gcb[h1]) - g_row)
        incl4, strict4 = g["consts"][2:]
        e = jnp.exp(jnp.where(incl4, jnp.concatenate(d_parts, axis=1), 0.0))
        a_scr[mid_base + gi] = jnp.where(strict4, jnp.concatenate([m[CHUNK:] for m in ms], axis=1) * e, 0.0)
        qkd_scr[ring_base + gi] = jnp.where(incl4, jnp.concatenate([m[:CHUNK] for m in ms], axis=1) * e,
                                            0.0).astype(BF16)
        el_scr[ring_base + gi] = jnp.exp(gc[last:last + 1, :])
        brow_scr[mid_base + gi] = jnp.concatenate(
            [ct[b_lane0 + h:b_lane0 + h + 1, :] for h in range(N_HEADS)], axis=1)
        yield


def _gdn_solve_stages(n_groups, masks, ring_base, mid_base, a_scr, rhs_scr, brow_scr, u_scr, wq_scr):
    eye4, m16, m32, m64, bd_mask = masks
    t_invs = []
    yield from _unit_lower_inverse_stages([a_scr[mid_base + gi] for gi in range(n_groups)],
                                          eye4, m16, m32, m64, bd_mask, t_invs)
    for gi, t_inv in enumerate(t_invs):
        t_b = (t_inv * brow_scr[mid_base + gi]).astype(BF16)
        for h in range(N_HEADS):
            sol = _dot(t_b[:, h * CHUNK:(h + 1) * CHUNK], rhs_scr[(mid_base + gi) * N_HEADS + h])
            ring_unit = (ring_base + gi) * N_HEADS + h
            u_scr[ring_unit] = sol[:, :HEAD_DIM]
            wq_scr[ring_unit, :CHUNK, :] = sol[:, HEAD_DIM:].astype(BF16)


def _gdn_scan_stages(slot_base, u_scr, wq_scr, qkd_scr, kdt_scr, el_scr, s_ref, of_ref, ob_ref):
    for j in range(CHUNKS_PER_TILE):
        units = [(slot_base + 2 * j + d, h, d * N_HEADS + h) for d in range(2) for h in range(N_HEADS)]
        states = [s_ref[slot] for _, _, slot in units]
        wss = [_dot(wq_scr[gi * N_HEADS + h], s.astype(BF16)) for (gi, h, _), s in zip(units, states)]
        yield
        v_news = [(u_scr[gi * N_HEADS + h] - ws[:CHUNK]).astype(BF16) for (gi, h, _), ws in zip(units, wss)]
        outs = [ws[CHUNK:] + _dot(qkd_scr[gi][:, h * CHUNK:(h + 1) * CHUNK], vn)
                for (gi, h, _), ws, vn in zip(units, wss, v_news)]
        for (gi, h, slot), s, vn in zip(units, states, v_news):
            s_ref[slot] = el_scr[gi][:, slot:slot + 1] * s + _dot(kdt_scr[gi * N_HEADS + h], vn)
        rf = j * CHUNK
        rb = (CHUNKS_PER_TILE - 1 - j) * CHUNK
        of_ref[0, rf:rf + CHUNK, :] = jnp.concatenate(outs[:N_HEADS], axis=1).astype(BF16)
        ob_ref[0, rb:rb + CHUNK, :] = jnp.concatenate(outs[N_HEADS:], axis=1).astype(BF16)
        yield


def _interleave(*stage_generators):
    live = list(stage_generators)
    while live:
        for gen in list(live):
            try:
                next(gen)
            except StopIteration:
                live.remove(gen)


def _gate_tile(ab, dtb, a_log):
    z = ab + dtb
    softplus = jnp.maximum(z, 0.0) + jnp.log(1.0 + jnp.exp(-jnp.abs(z)))
    lane = lax.broadcasted_iota(jnp.int32, ab.shape, 1)
    return jnp.where(lane < 2 * N_HEADS, -jnp.exp(a_log) * softplus, jax.nn.sigmoid(ab))


def _gdn_kernel(qf_ref, kf_ref, vf_ref, abf_ref, qb_ref, kb_ref, vb_ref, abb_ref, dtb_ref, alog_ref,
                of_ref, ob_ref, s_ref, a_scr, rhs_scr, brow_scr, u_scr, wq_scr, qkd_scr, kdt_scr, el_scr):
    step = pl.program_id(1)
    groups_per_tile = 2 * CHUNKS_PER_TILE
    front_ring = lax.rem(step, 3) * groups_per_tile
    solve_ring = lax.rem(step + 2, 3) * groups_per_tile
    scan_ring = lax.rem(step + 1, 3) * groups_per_tile
    front_mid = lax.rem(step, 2) * groups_per_tile
    solve_mid = groups_per_tile - front_mid

    @pl.when(step == 0)
    def _():
        for scr in (s_ref, a_scr, rhs_scr, brow_scr, u_scr, wq_scr, qkd_scr, kdt_scr, el_scr):
            scr[...] = jnp.zeros_like(scr)

    ri = lax.broadcasted_iota(jnp.int32, (CHUNK, CHUNK), 0)
    ci = lax.broadcasted_iota(jnp.int32, (CHUNK, CHUNK), 1)
    ri4 = lax.broadcasted_iota(jnp.int32, (CHUNK, N_HEADS * CHUNK), 0)
    ci4 = lax.broadcasted_iota(jnp.int32, (CHUNK, N_HEADS * CHUNK), 1) % CHUNK
    rbd = lax.broadcasted_iota(jnp.int32, (N_HEADS * CHUNK, N_HEADS * CHUNK), 0) // CHUNK
    cbd = lax.broadcasted_iota(jnp.int32, (N_HEADS * CHUNK, N_HEADS * CHUNK), 1) // CHUNK
    bd_mask = rbd == cbd
    eye4 = (ri4 == ci4).astype(F32)
    m16 = (ri4 // 16) == (ci4 // 16)
    m32 = jnp.logical_and((ri4 // 32) == (ci4 // 32), jnp.logical_not(m16))
    m64 = (ri4 // 32) != (ci4 // 32)

    cf = ((ci <= ri).astype(BF16), None, ri4 >= ci4, ri4 > ci4)
    cb = ((ci >= ri).astype(BF16), None, ri4 <= ci4, ri4 < ci4)
    dtb = dtb_ref[...]
    alog = alog_ref[...]

    groups = []
    for j in range(CHUNKS_PER_TILE):
        rf = j * CHUNK
        rb = (CHUNKS_PER_TILE - 1 - j) * CHUNK
        groups.append(dict(q=qf_ref[0, rf:rf + CHUNK, :], k=kf_ref[0, rf:rf + CHUNK, :],
                           v=vf_ref[0, rf:rf + CHUNK, :],
                           gates=_gate_tile(abf_ref[0, rf:rf + CHUNK, :], dtb, alog), rev=False, consts=cf))
        groups.append(dict(q=qb_ref[0, rb:rb + CHUNK, :], k=kb_ref[0, rb:rb + CHUNK, :],
                           v=vb_ref[0, rb:rb + CHUNK, :],
                           gates=_gate_tile(abb_ref[0, rb:rb + CHUNK, :], dtb, alog), rev=True, consts=cb))
    _interleave(
        _gdn_solve_stages(len(groups), (eye4, m16, m32, m64, bd_mask), solve_ring, solve_mid,
                          a_scr, rhs_scr, brow_scr, u_scr, wq_scr),
        _gdn_scan_stages(scan_ring, u_scr, wq_scr, qkd_scr, kdt_scr, el_scr, s_ref, of_ref, ob_ref),
        _gdn_front_stages(groups, front_ring, front_mid, a_scr, rhs_scr, brow_scr, wq_scr, qkd_scr, kdt_scr,
                          el_scr))


def _bwd_tile(s, n_tiles):
    return jnp.where(s == 0, 0, n_tiles - s)


def _gdn(qkv, ab, dtb_row, alog_row):
    b, t_total, _ = qkv.shape
    n_tiles = t_total // TILE

    def tile_of(s, bwd, lag):
        pos = jnp.maximum(s - 2, 0) if lag else jnp.minimum(s, n_tiles - 1)
        return _bwd_tile(pos, n_tiles) if bwd else pos

    def col(c, bwd, lag=False):
        return pl.BlockSpec((1, TILE, GROUP_W), lambda i, s: (i, tile_of(s, bwd, lag), c))

    def gate(bwd):
        return pl.BlockSpec((1, TILE, GATE_LANES), lambda i, s: (i, tile_of(s, bwd, False), 0))

    row = pl.BlockSpec((1, GATE_LANES), lambda i, s: (0, 0))
    per_tile = 2 * CHUNKS_PER_TILE
    n_groups = 3 * per_tile
    n_units = n_groups * N_HEADS
    n_mid = 2 * per_tile
    return pl.pallas_call(
        _gdn_kernel,
        grid=(b, n_tiles + 2),
        in_specs=[col(0, False), col(1, False), col(2, False), gate(False),
                  col(0, True), col(1, True), col(2, True), gate(True), row, row],
        out_specs=[col(0, False, lag=True), col(0, True, lag=True)],
        out_shape=[jax.ShapeDtypeStruct((b, t_total, GROUP_W), BF16)] * 2,
        scratch_shapes=[pltpu.VMEM((2 * N_HEADS, HEAD_DIM, HEAD_DIM), F32),
                        pltpu.VMEM((n_mid, CHUNK, N_HEADS * CHUNK), F32),
                        pltpu.VMEM((n_mid * N_HEADS, CHUNK, 2 * HEAD_DIM), BF16),
                        pltpu.VMEM((n_mid, 1, N_HEADS * CHUNK), F32),
                        pltpu.VMEM((n_units, CHUNK, HEAD_DIM), F32),
                        pltpu.VMEM((n_units, 2 * CHUNK, HEAD_DIM), BF16),
                        pltpu.VMEM((n_groups, CHUNK, N_HEADS * CHUNK), BF16),
                        pltpu.VMEM((n_units, HEAD_DIM, CHUNK), BF16),
                        pltpu.VMEM((n_groups, 1, GATE_LANES), F32)],
        compiler_params=_cparams(("parallel", "arbitrary")),
        name="gdn",
    )(qkv, qkv, qkv, ab, qkv, qkv, qkv, ab, dtb_row, alog_row)


def _ret_kernel(qf_ref, kf_ref, vf_ref, qb_ref, kb_ref, vb_ref, logit_ref, of_ref, ob_ref,
                r_ref, decay_ref, xi_ref, zeta_ref):
    x = logit_ref[...]
    lg_row = jnp.minimum(x, 0.0) - jnp.log(1.0 + jnp.exp(-jnp.abs(x)))

    @pl.when(pl.program_id(1) == 0)
    def _():
        r_ref[...] = jnp.zeros_like(r_ref)
        ri = lax.broadcasted_iota(jnp.int32, (TILE, TILE), 0)
        ci = lax.broadcasted_iota(jnp.int32, (TILE, TILE), 1)
        row = lax.broadcasted_iota(jnp.int32, (TILE, HEAD_DIM), 0).astype(F32)
        for slot in range(2 * N_HEADS):
            rev = slot >= N_HEADS
            lg = lg_row[:, slot:slot + 1]
            diff = ((ci - ri) if rev else (ri - ci)).astype(F32)
            pos = ((TILE - 1.0) - row) if rev else row
            decay_ref[slot] = jnp.where(diff >= 0, jnp.exp(lg * jnp.maximum(diff, 0.0)), 0.0)
            xi_ref[slot] = jnp.exp(lg * (pos + 1.0))
            zeta_ref[slot] = jnp.exp(lg * ((TILE - 1.0) - pos))

    units = [(refs, (N_HEADS if rev else 0) + h, h * HEAD_DIM)
             for refs, rev in (((qf_ref, kf_ref, vf_ref, of_ref), False), ((qb_ref, kb_ref, vb_ref, ob_ref), True))
             for h in range(N_HEADS)]
    qs = [refs[0][0, :, lo:lo + HEAD_DIM] for refs, _, lo in units]
    ks = [refs[1][0, :, lo:lo + HEAD_DIM] for refs, _, lo in units]
    vs = [refs[2][0, :, lo:lo + HEAD_DIM] for refs, _, lo in units]
    qks = [_dot_nt(q, k) for q, k in zip(qs, ks)]
    qkds = [(qk * decay_ref[slot]).astype(BF16) for qk, (_, slot, _) in zip(qks, units)]
    rs = [r_ref[slot] for _, slot, _ in units]
    for (refs, slot, lo), q, v, qkd, r in zip(units, qs, vs, qkds, rs):
        refs[3][0, :, lo:lo + HEAD_DIM] = (
            _dot(qkd, v) + _dot((q * xi_ref[slot]).astype(BF16), r.astype(BF16))).astype(BF16)
    for (_, slot, _), k, v, r in zip(units, ks, vs, rs):
        g_chunk = jnp.exp(lg_row[:, slot:slot + 1] * float(TILE))
        r_ref[slot] = g_chunk * r + _dot((k * zeta_ref[slot]).T.astype(BF16), v)


def _ret(pr, logit_row):
    b, t_total, _ = pr.shape
    n_tiles = t_total // TILE

    def col(c, bwd):
        if bwd:
            return pl.BlockSpec((1, TILE, GROUP_W), lambda i, s: (i, _bwd_tile(s, n_tiles), c))
        return pl.BlockSpec((1, TILE, GROUP_W), lambda i, s: (i, s, c))

    return pl.pallas_call(
        _ret_kernel,
        grid=(b, n_tiles),
        in_specs=[col(0, False), col(1, False), col(2, False), col(0, True), col(1, True), col(2, True),
                  pl.BlockSpec((1, GATE_LANES), lambda i, s: (0, 0))],
        out_specs=[col(0, False), col(0, True)],
        out_shape=[jax.ShapeDtypeStruct((b, t_total, GROUP_W), BF16)] * 2,
        scratch_shapes=[pltpu.VMEM((2 * N_HEADS, HEAD_DIM, HEAD_DIM), F32),
                        pltpu.VMEM((2 * N_HEADS, TILE, TILE), F32),
                        pltpu.VMEM((2 * N_HEADS, TILE, HEAD_DIM), F32),
                        pltpu.VMEM((2 * N_HEADS, TILE, HEAD_DIM), F32)],
        compiler_params=_cparams(("parallel", "arbitrary")),
        name="ret",
    )(pr, pr, pr, pr, pr, pr, logit_row)


def _tail_kernel(gf_ref, gb_ref, z_ref, rf_ref, rb_ref, rg_ref, x_ref, mod_ref, gdn_g_ref, ret_g_ref,
                 nffn_g_ref, w_ref, wi_ref, wo_ref, fg_ref, o_ref, y_ref, *, d_model, d_ff):
    o = gf_ref[0].astype(F32) + gb_ref[0].astype(F32)
    r = rf_ref[0].astype(F32) + rb_ref[0].astype(F32)
    z = z_ref[0].astype(F32)
    rg = rg_ref[0].astype(F32)
    for h in range(N_HEADS):
        lo = h * HEAD_DIM
        o_h = o[:, lo:lo + HEAD_DIM]
        o_n = o_h * lax.rsqrt(jnp.mean(o_h * o_h, axis=-1, keepdims=True) + NORM_EPS) * gdn_g_ref[...]
        y_ref[:, lo:lo + HEAD_DIM] = (o_n * _silu(z[:, lo:lo + HEAD_DIM])).astype(BF16)
        r_h = r[:, lo:lo + HEAD_DIM]
        mu = jnp.mean(r_h, axis=-1, keepdims=True)
        cen = r_h - mu
        var = jnp.mean(cen * cen, axis=-1, keepdims=True)
        r_n = cen * lax.rsqrt(var + NORM_EPS) * ret_g_ref[...]
        y_ref[:, GROUP_W + lo:GROUP_W + lo + HEAD_DIM] = (r_n * _silu(rg[:, lo:lo + HEAD_DIM])).astype(BF16)
    mod = mod_ref[0]
    x1 = x_ref[0] + mod[:, 2 * d_model:3 * d_model] * _dot(y_ref[...], w_ref[...])
    h2 = _rms_mod(x1, nffn_g_ref[...], mod[:, 3 * d_model:4 * d_model],
                  mod[:, 4 * d_model:5 * d_model]).astype(BF16)
    acc = None
    for lo in range(0, d_ff, FFN_CHUNK):
        hi = min(lo + FFN_CHUNK, d_ff)
        gate = _dot(h2, wi_ref[:, lo:hi])
        up = _dot(h2, wi_ref[:, d_ff + lo:d_ff + hi])
        part = _dot((_silu(gate) * up).astype(BF16), wo_ref[lo:hi, :])
        acc = part if acc is None else acc + part
    x2 = x1 + mod[:, 5 * d_model:6 * d_model] * acc
    ms = jnp.mean(x2 * x2, axis=-1, keepdims=True)
    o_ref[0] = x2 * lax.rsqrt(ms + NORM_EPS) * fg_ref[...]


def _tail(gdn_f, gdn_b, pa, ret_f, ret_b, pr, x, mod3, gdn_g, ret_g, nffn_g, w_out, w_ffn_in, w_ffn_out,
          final_g):
    b, seq, d = x.shape
    d_ff = w_ffn_out.shape[0]
    first = (pa.shape[1] - seq) // TAIL_TOK

    def tok(c):
        return pl.BlockSpec((1, TAIL_TOK, GROUP_W), lambda i, t: (i, t + first, c))

    def resident(shape):
        return pl.BlockSpec(shape, lambda i, t: (0,) * len(shape), pipeline_mode=pl.Buffered(1))

    return pl.pallas_call(
        functools.partial(_tail_kernel, d_model=d, d_ff=d_ff),
        grid=(b, seq // TAIL_TOK),
        in_specs=[tok(0), tok(0), tok(3), tok(0), tok(0), tok(3),
                  pl.BlockSpec((1, TAIL_TOK, d), lambda i, t: (i, t, 0)),
                  pl.BlockSpec((1, 1, mod3.shape[2]), lambda i, t: (i, 0, 0)),
                  resident((1, HEAD_DIM)), resident((1, HEAD_DIM)), resident((1, d)),
                  resident(w_out.shape), resident(w_ffn_in.shape), resident(w_ffn_out.shape),
                  resident((1, d))],
        out_specs=pl.BlockSpec((1, TAIL_TOK, d), lambda i, t: (i, t, 0)),
        out_shape=jax.ShapeDtypeStruct((b, seq, d), F32),
        scratch_shapes=[pltpu.VMEM((TAIL_TOK, 2 * GROUP_W), BF16)],
        compiler_params=_cparams(("parallel", "parallel")),
        name="tail",
    )(gdn_f, gdn_b, pa, ret_f, ret_b, pr, x, mod3, gdn_g, ret_g, nffn_g, w_out, w_ffn_in, w_ffn_out, final_g)


def _rope_tables(ctx_len, n_lat):
    def angles(pos, n_pairs):
        inv = ROPE_THETA ** (-jnp.arange(n_pairs, dtype=F32) / n_pairs)
        return pos[:, None] * inv[None, :]

    rows = n_lat // GRID_W
    row = jnp.repeat(jnp.arange(rows, dtype=F32), GRID_W)
    col = jnp.tile(jnp.arange(GRID_W, dtype=F32), rows)
    zeros = jnp.zeros((ctx_len,), F32)
    p_seq = jnp.concatenate([jnp.arange(ctx_len, dtype=F32), jnp.full((n_lat,), float(ctx_len), F32)])
    ang = jnp.concatenate([angles(p_seq, ROPE_PAIRS[0]),
                           angles(jnp.concatenate([zeros, row]), ROPE_PAIRS[1]),
                           angles(jnp.concatenate([zeros, col]), ROPE_PAIRS[2])], axis=-1)
    cos, sin = jnp.cos(ang), jnp.sin(ang)
    return jnp.concatenate([cos, cos], axis=-1), jnp.concatenate([-sin, sin], axis=-1)


def _lane_row(values):
    flat = values.reshape(1, -1).astype(F32)
    return jnp.pad(flat, ((0, 0), (0, GATE_LANES - flat.shape[1])))


def kernel(x, c, ctx, c_ctx, ada_w, ada_b, norm_mix_g, norm_ffn_g, w_in, conv_w, gdn_a_log, gdn_dt_bias,
           gdn_norm_g, ret_decay_logit, ret_norm_g, w_out, w_ffn_in, w_ffn_out, final_g):
    assert ada_w.shape[0] == 1, "single-layer block"
    b, seq, d = x.shape
    ctx_len = ctx.shape[1]
    assert ctx_len == TILE and seq % TILE == 0 and TILE % TAIL_TOK == 0 and b + 1 <= 8

    cond = jnp.concatenate([c, c_ctx[None, :], jnp.zeros((8 - b - 1, d), F32)], axis=0)
    mod = _ada(cond, ada_w[0], ada_b)
    mod3 = mod[:, None, :]

    n_qkvz = 4 * GROUP_W
    n_gate = 4 * N_HEADS
    w = w_in[0]
    wa = w[:, :n_qkvz].astype(BF16)
    wab = jnp.pad(w[:, n_qkvz:n_qkvz + n_gate], ((0, 0), (0, GATE_LANES - n_gate))).astype(BF16)
    wr = w[:, n_qkvz + n_gate:].astype(BF16)
    cos2, sin2 = _rope_tables(ctx_len, seq)
    pa, ab, pr = _inproj(x, ctx, mod3, norm_mix_g, wa, wab, wr, cos2, sin2)

    qkv = _conv(pa, conv_w[0])
    dtb_row = _lane_row(gdn_dt_bias[0])
    gdn_f, gdn_b = _gdn(qkv, ab, dtb_row, _lane_row(gdn_a_log[0]))
    ret_f, ret_b = _ret(pr, _lane_row(ret_decay_logit[0]))

    return _tail(gdn_f, gdn_b, pa, ret_f, ret_b, pr, x, mod3, gdn_norm_g, ret_norm_g, norm_ffn_g,
                 w_out[0].astype(BF16), w_ffn_in[0].astype(BF16), w_ffn_out[0].astype(BF16), final_g[None, :])
```

```python
import functools
import math

import jax
import jax.numpy as jnp
import numpy as np
from jax import lax
from jax.experimental import pallas as pl
from jax.experimental.pallas import tpu as pltpu

F32 = jnp.float32
BF16 = jnp.bfloat16

HEAD_DIM = 128
N_HEADS = 4
GROUP_W = N_HEADS * HEAD_DIM
CONV_K = 5
GRID_W = 64
ROPE_THETA = 10000.0
ROPE_PAIRS = (16, 24, 24)
NORM_EPS = 1e-6

CHUNK = 64
TILE = 256
CHUNKS_PER_TILE = TILE // CHUNK
HALO = 16
GATE_LANES = 128
TAIL_TOK = 256
FFN_CHUNK = 1024
VMEM_LIMIT = 56 * 1024 * 1024


def _cparams(sem):
    return pltpu.CompilerParams(dimension_semantics=sem, vmem_limit_bytes=VMEM_LIMIT)


def _dot(a, b):
    return jnp.dot(a, b, preferred_element_type=F32)


def _dot_nt(a, b):
    return lax.dot_general(a, b, (((1,), (1,)), ((), ())), preferred_element_type=F32)


def _split(a):
    hi = a.astype(BF16)
    lo = (a - hi.astype(F32)).astype(BF16)
    return hi, lo


def _silu(x):
    return x * jax.nn.sigmoid(x)


def _lane_bcast(col, width):
    return jnp.broadcast_to(col, (col.shape[0], width))


def _ada_kernel(cond_ref, w_ref, b_ref, o_ref):
    a_hi, a_lo = _split(_silu(cond_ref[...]))
    w_hi, w_lo = _split(w_ref[0])
    o_ref[...] = _dot(a_hi, w_hi) + _dot(a_lo, w_hi) + _dot(a_hi, w_lo) + b_ref[...]


def _ada(cond, w, b):
    rows, d = cond.shape
    n = w.shape[2]
    bn = 1536
    return pl.pallas_call(
        _ada_kernel,
        grid=(n // bn,),
        in_specs=[pl.BlockSpec((rows, d), lambda j: (0, 0)),
                  pl.BlockSpec((1, d, bn), lambda j: (0, 0, j)),
                  pl.BlockSpec((1, bn), lambda j: (0, j))],
        out_specs=pl.BlockSpec((rows, bn), lambda j: (0, j)),
        out_shape=jax.ShapeDtypeStruct((rows, n), F32),
        compiler_params=_cparams(("parallel",)),
        name="ada",
    )(cond, w, b)


def _rms_mod(x, g, shift, scale):
    ms = jnp.mean(x * x, axis=-1, keepdims=True)
    return (x * lax.rsqrt(ms + NORM_EPS) * g) * (1.0 + scale) + shift


def _inproj_kernel(x_ref, ctx_ref, mod_ref, g_ref, wa_ref, wab_ref, wr_ref, cos_ref, sin_ref,
                   pa_ref, ab_ref, pr_ref, *, d_model):
    t = pl.program_id(1)
    xin = jnp.where(t == 0, ctx_ref[0], x_ref[0])
    mod = mod_ref[0]
    h = _rms_mod(xin, g_ref[...], mod[:, 0:d_model], mod[:, d_model:2 * d_model]).astype(BF16)
    pa_ref[0] = _dot(h, wa_ref[...]).astype(BF16)
    ab_ref[0] = _dot(h, wab_ref[...])
    pr = _dot(h, wr_ref[...])
    cos2 = cos_ref[...]
    sin2 = sin_ref[...]
    k_scale = HEAD_DIM ** -0.5
    for hd in range(2 * N_HEADS):
        lo = hd * HEAD_DIM
        tt = pr[:, lo:lo + HEAD_DIM]
        rot = tt * cos2 + pltpu.roll(tt, HEAD_DIM // 2, 1) * sin2
        if hd >= N_HEADS:
            rot = rot * k_scale
        pr_ref[0, :, lo:lo + HEAD_DIM] = rot.astype(BF16)
    pr_ref[0, :, 2 * GROUP_W:] = pr[:, 2 * GROUP_W:].astype(BF16)


def _inproj(x, ctx, mod3, g, wa, wab, wr, cos2, sin2):
    b, seq, d = x.shape
    n_tiles = (ctx.shape[1] + seq) // TILE
    t_total = n_tiles * TILE
    return pl.pallas_call(
        functools.partial(_inproj_kernel, d_model=d),
        grid=(b, n_tiles),
        in_specs=[pl.BlockSpec((1, TILE, d), lambda i, t: (i, jnp.maximum(t - 1, 0), 0)),
                  pl.BlockSpec((1, TILE, d), lambda i, t: (i, 0, 0)),
                  pl.BlockSpec((1, 1, mod3.shape[2]), lambda i, t: (jnp.where(t == 0, b, i), 0, 0)),
                  pl.BlockSpec((1, d), lambda i, t: (0, 0)),
                  pl.BlockSpec(wa.shape, lambda i, t: (0, 0)),
                  pl.BlockSpec(wab.shape, lambda i, t: (0, 0)),
                  pl.BlockSpec(wr.shape, lambda i, t: (0, 0)),
                  pl.BlockSpec((TILE, HEAD_DIM), lambda i, t: (t, 0)),
                  pl.BlockSpec((TILE, HEAD_DIM), lambda i, t: (t, 0))],
        out_specs=[pl.BlockSpec((1, TILE, wa.shape[1]), lambda i, t: (i, t, 0)),
                   pl.BlockSpec((1, TILE, GATE_LANES), lambda i, t: (i, t, 0)),
                   pl.BlockSpec((1, TILE, wr.shape[1]), lambda i, t: (i, t, 0))],
        out_shape=[jax.ShapeDtypeStruct((b, t_total, wa.shape[1]), BF16),
                   jax.ShapeDtypeStruct((b, t_total, GATE_LANES), F32),
                   jax.ShapeDtypeStruct((b, t_total, wr.shape[1]), BF16)],
        compiler_params=_cparams(("parallel", "parallel")),
        name="inproj",
    )(x, ctx, mod3, g, wa, wab, wr, cos2, sin2)


def _conv_kernel(cur_ref, prev_ref, next_ref, w_ref, o_ref, ext_ref):
    t = pl.program_id(1)
    last = pl.num_programs(1) - 1
    has_prev = t >= 2
    has_next = jnp.logical_and(t >= 1, t < last)
    ext_ref[0:HALO, :] = jnp.where(has_prev, prev_ref[0].astype(F32), 0.0)
    ext_ref[HALO:HALO + TILE, :] = cur_ref[0].astype(F32)
    ext_ref[HALO + TILE:, :] = jnp.where(has_next, next_ref[0].astype(F32), 0.0)
    pad = (CONV_K - 1) // 2
    q_scale = HEAD_DIM ** -0.5
    for part in range(3):
        for hd in range(N_HEADS):
            lo = part * GROUP_W + hd * HEAD_DIM
            acc = None
            for i in range(CONV_K):
                r0 = HALO - pad + i
                term = ext_ref[r0:r0 + TILE, lo:lo + HEAD_DIM] * w_ref[i:i + 1, lo:lo + HEAD_DIM]
                acc = term if acc is None else acc + term
            y = _silu(acc)
            if part < 2:
                y = y * lax.rsqrt(jnp.sum(y * y, axis=-1, keepdims=True) + NORM_EPS)
            if part == 0:
                y = y * q_scale
            o_ref[0, :, lo:lo + HEAD_DIM] = y.astype(BF16)


def _conv(pa, conv_w):
    b, t_total, _ = pa.shape
    n_tiles = t_total // TILE
    width = 3 * GROUP_W
    per = TILE // HALO
    n_halo = t_total // HALO
    return pl.pallas_call(
        _conv_kernel,
        grid=(b, n_tiles),
        in_specs=[pl.BlockSpec((1, TILE, width), lambda i, t: (i, t, 0)),
                  pl.BlockSpec((1, HALO, width), lambda i, t: (i, jnp.maximum(t * per - 1, 0), 0)),
                  pl.BlockSpec((1, HALO, width), lambda i, t: (i, jnp.minimum((t + 1) * per, n_halo - 1), 0)),
                  pl.BlockSpec((CONV_K, width), lambda i, t: (0, 0))],
        out_specs=pl.BlockSpec((1, TILE, width), lambda i, t: (i, t, 0)),
        out_shape=jax.ShapeDtypeStruct((b, t_total, width), BF16),
        scratch_shapes=[pltpu.VMEM((TILE + 2 * HALO, width), F32)],
        compiler_params=_cparams(("parallel", "parallel")),
        name="conv",
    )(pa, pa, pa, conv_w)


def _packed_matmul(x, y, bd_mask):
    yb = y.astype(BF16)
    y_bd = jnp.where(bd_mask, jnp.concatenate([yb] * N_HEADS, axis=0), jnp.zeros((), BF16))
    return _dot(x.astype(BF16), y_bd)


def _unit_lower_inverse_stages(a_list, eye, m16, m32, m64, bd_mask, out):
    pm = functools.partial(_packed_matmul, bd_mask=bd_mask)
    ps = [-jnp.where(m16, a, 0.0) for a in a_list]
    xs = [eye + p for p in ps]
    for _ in range(3):
        ps = [pm(p, p) for p in ps]
        yield
        xs = [x + pm(x, p) for x, p in zip(xs, ps)]
        yield
    for m in (m32, m64):
        zs = [pm(jnp.where(m, a, 0.0), x) for a, x in zip(a_list, xs)]
        yield
        xs = [x - pm(x, z) for x, z in zip(xs, zs)]
        yield
    out.extend(xs)


def _gdn_front_stages(groups, ring_base, mid_base, a_scr, rhs_scr, brow_scr, wq_scr, qkd_scr, kdt_scr,
                      el_scr):
    lane = lax.broadcasted_iota(jnp.int32, (CHUNK, GATE_LANES), 1)
    low_half = lax.broadcasted_iota(jnp.int32, (CHUNK, HEAD_DIM), 1) < CHUNK
    zero_k = jnp.zeros((CHUNK, HEAD_DIM), BF16)
    for g in groups:
        tcum = g["consts"][0]
        g_hi, g_lo = _split(g["gates"])
        g["gc"] = _dot(tcum, g_hi) + _dot(tcum, g_lo)
    yield

    for gi, g in enumerate(groups):
        g_lane0 = N_HEADS if g["rev"] else 0
        b_lane0 = 2 * N_HEADS + g_lane0
        last = 0 if g["rev"] else CHUNK - 1
        gc, gates = g["gc"], g["gates"]
        ct = jnp.where(lane < 2 * N_HEADS, gc, gates).T
        egl_t = jnp.exp(ct[:, last:last + 1] - ct)
        gcb, kb, ms = [], [], []
        for h in range(N_HEADS):
            lo = h * HEAD_DIM
            gl = g_lane0 + h
            ring_unit = (ring_base + gi) * N_HEADS + h
            gcb.append(_lane_bcast(gc[:, gl:gl + 1], HEAD_DIM))
            beta = _lane_bcast(gates[:, b_lane0 + h:b_lane0 + h + 1], HEAD_DIM)
            eg_h = jnp.exp(gcb[h])
            q_h = g["q"][:, lo:lo + HEAD_DIM]
            k_h = g["k"][:, lo:lo + HEAD_DIM].astype(F32)
            kb.append((k_h * beta).astype(BF16))
            rhs_scr[(mid_base + gi) * N_HEADS + h] = jnp.concatenate(
                [g["v"][:, lo:lo + HEAD_DIM], (k_h * eg_h).astype(BF16)], axis=1)
            wq_scr[ring_unit, CHUNK:, :] = (q_h * eg_h).astype(BF16)
            kdt_scr[ring_unit] = (k_h.T * egl_t[gl:gl + 1, :]).astype(BF16)
        d_parts = []
        for p in range(N_HEADS // 2):
            h0, h1 = 2 * p, 2 * p + 1
            lo = h0 * HEAD_DIM
            lhs = jnp.concatenate([g["q"][:, lo:lo + 2 * HEAD_DIM], jnp.concatenate([kb[h0], kb[h1]], axis=1)],
                                  axis=0)
            k0 = g["k"][:, lo:lo + HEAD_DIM]
            k1 = g["k"][:, lo + HEAD_DIM:lo + 2 * HEAD_DIM]
            rhs_bd = jnp.concatenate([jnp.concatenate([k0, zero_k], axis=1),
                                      jnp.concatenate([zero_k, k1], axis=1)], axis=0)
            ms.append(_dot_nt(lhs, rhs_bd))
            g_row = jnp.concatenate([ct[g_lane0 + h0:g_lane0 + h0 + 1, :], ct[g_lane0 + h1:g_lane0 + h1 + 1, :]],
                                    axis=1)
            d_parts.append(jnp.where(low_half, gcb[h0], gcb[h1]) - g_row)
        incl4, strict4 = g["consts"][2:]
        e = jnp.exp(jnp.where(incl4, jnp.concatenate(d_parts, axis=1), 0.0))
        a_scr[mid_base + gi] = jnp.where(strict4, jnp.concatenate([m[CHUNK:] for m in ms], axis=1) * e, 0.0)
        qkd_scr[ring_base + gi] = jnp.where(incl4, jnp.concatenate([m[:CHUNK] for m in ms], axis=1) * e,
                                            0.0).astype(BF16)
        el_scr[ring_base + gi] = jnp.exp(gc[last:last + 1, :])
        brow_scr[mid_base + gi] = jnp.concatenate(
            [ct[b_lane0 + h:b_lane0 + h + 1, :] for h in range(N_HEADS)], axis=1)
        yield


def _gdn_solve_stages(n_groups, masks, ring_base, mid_base, a_scr, rhs_scr, brow_scr, u_scr, wq_scr):
    eye4, m16, m32, m64, bd_mask = masks
    t_invs = []
    yield from _unit_lower_inverse_stages([a_scr[mid_base + gi] for gi in range(n_groups)],
                                          eye4, m16, m32, m64, bd_mask, t_invs)
    for gi, t_inv in enumerate(t_invs):
        t_b = (t_inv * brow_scr[mid_base + gi]).astype(BF16)
        for h in range(N_HEADS):
            sol = _dot(t_b[:, h * CHUNK:(h + 1) * CHUNK], rhs_scr[(mid_base + gi) * N_HEADS + h])
            ring_unit = (ring_base + gi) * N_HEADS + h
            u_scr[ring_unit] = sol[:, :HEAD_DIM]
            wq_scr[ring_unit, :CHUNK, :] = sol[:, HEAD_DIM:].astype(BF16)


def _gdn_scan_stages(slot_base, u_scr, wq_scr, qkd_scr, kdt_scr, el_scr, s_ref, of_ref, ob_ref):
    for j in range(CHUNKS_PER_TILE):
        units = [(slot_base + 2 * j + d, h, d * N_HEADS + h) for d in range(2) for h in range(N_HEADS)]
        states = [s_ref[slot] for _, _, slot in units]
        wss = [_dot(wq_scr[gi * N_HEADS + h], s.astype(BF16)) for (gi, h, _), s in zip(units, states)]
        yield
        v_news = [(u_scr[gi * N_HEADS + h] - ws[:CHUNK]).astype(BF16) for (gi, h, _), ws in zip(units, wss)]
        outs = [ws[CHUNK:] + _dot(qkd_scr[gi][:, h * CHUNK:(h + 1) * CHUNK], vn)
                for (gi, h, _), ws, vn in zip(units, wss, v_news)]
        for (gi, h, slot), s, vn in zip(units, states, v_news):
            s_ref[slot] = el_scr[gi][:, slot:slot + 1] * s + _dot(kdt_scr[gi * N_HEADS + h], vn)
        rf = j * CHUNK
        rb = (CHUNKS_PER_TILE - 1 - j) * CHUNK
        of_ref[0, rf:rf + CHUNK, :] = jnp.concatenate(outs[:N_HEADS], axis=1).astype(BF16)
        ob_ref[0, rb:rb + CHUNK, :] = jnp.concatenate(outs[N_HEADS:], axis=1).astype(BF16)
        yield


def _interleave(*stage_generators):
    live = list(stage_generators)
    while live:
        for gen in list(live):
            try:
                next(gen)
            except StopIteration:
                live.remove(gen)


def _gate_tile(ab, dtb, a_log):
    z = ab + dtb
    softplus = jnp.maximum(z, 0.0) + jnp.log(1.0 + jnp.exp(-jnp.abs(z)))
    lane = lax.broadcasted_iota(jnp.int32, ab.shape, 1)
    return jnp.where(lane < 2 * N_HEADS, -jnp.exp(a_log) * softplus, jax.nn.sigmoid(ab))


def _gdn_kernel(qf_ref, kf_ref, vf_ref, abf_ref, qb_ref, kb_ref, vb_ref, abb_ref, dtb_ref, alog_ref,
                of_ref, ob_ref, s_ref, a_scr, rhs_scr, brow_scr, u_scr, wq_scr, qkd_scr, kdt_scr, el_scr):
    step = pl.program_id(1)
    groups_per_tile = 2 * CHUNKS_PER_TILE
    front_ring = lax.rem(step, 3) * groups_per_tile
    solve_ring = lax.rem(step + 2, 3) * groups_per_tile
    scan_ring = lax.rem(step + 1, 3) * groups_per_tile
    front_mid = lax.rem(step, 2) * groups_per_tile
    solve_mid = groups_per_tile - front_mid

    @pl.when(step == 0)
    def _():
        for scr in (s_ref, a_scr, rhs_scr, brow_scr, u_scr, wq_scr, qkd_scr, kdt_scr, el_scr):
            scr[...] = jnp.zeros_like(scr)

    ri = lax.broadcasted_iota(jnp.int32, (CHUNK, CHUNK), 0)
    ci = lax.broadcasted_iota(jnp.int32, (CHUNK, CHUNK), 1)
    ri4 = lax.broadcasted_iota(jnp.int32, (CHUNK, N_HEADS * CHUNK), 0)
    ci4 = lax.broadcasted_iota(jnp.int32, (CHUNK, N_HEADS * CHUNK), 1) % CHUNK
    rbd = lax.broadcasted_iota(jnp.int32, (N_HEADS * CHUNK, N_HEADS * CHUNK), 0) // CHUNK
    cbd = lax.broadcasted_iota(jnp.int32, (N_HEADS * CHUNK, N_HEADS * CHUNK), 1) // CHUNK
    bd_mask = rbd == cbd
    eye4 = (ri4 == ci4).astype(F32)
    m16 = (ri4 // 16) == (ci4 // 16)
    m32 = jnp.logical_and((ri4 // 32) == (ci4 // 32), jnp.logical_not(m16))
    m64 = (ri4 // 32) != (ci4 // 32)

    cf = ((ci <= ri).astype(BF16), None, ri4 >= ci4, ri4 > ci4)
    cb = ((ci >= ri).astype(BF16), None, ri4 <= ci4, ri4 < ci4)
    dtb = dtb_ref[...]
    alog = alog_ref[...]

    groups = []
    for j in range(CHUNKS_PER_TILE):
        rf = j * CHUNK
        rb = (CHUNKS_PER_TILE - 1 - j) * CHUNK
        groups.append(dict(q=qf_ref[0, rf:rf + CHUNK, :], k=kf_ref[0, rf:rf + CHUNK, :],
                           v=vf_ref[0, rf:rf + CHUNK, :],
                           gates=_gate_tile(abf_ref[0, rf:rf + CHUNK, :], dtb, alog), rev=False, consts=cf))
        groups.append(dict(q=qb_ref[0, rb:rb + CHUNK, :], k=kb_ref[0, rb:rb + CHUNK, :],
                           v=vb_ref[0, rb:rb + CHUNK, :],
                           gates=_gate_tile(abb_ref[0, rb:rb + CHUNK, :], dtb, alog), rev=True, consts=cb))
    _interleave(
        _gdn_solve_stages(len(groups), (eye4, m16, m32, m64, bd_mask), solve_ring, solve_mid,
                          a_scr, rhs_scr, brow_scr, u_scr, wq_scr),
        _gdn_scan_stages(scan_ring, u_scr, wq_scr, qkd_scr, kdt_scr, el_scr, s_ref, of_ref, ob_ref),
        _gdn_front_stages(groups, front_ring, front_mid, a_scr, rhs_scr, brow_scr, wq_scr, qkd_scr, kdt_scr,
                          el_scr))


def _bwd_tile(s, n_tiles):
    return jnp.where(s == 0, 0, n_tiles - s)


def _gdn(qkv, ab, dtb_row, alog_row):
    b, t_total, _ = qkv.shape
    n_tiles = t_total // TILE

    def tile_of(s, bwd, lag):
        pos = jnp.maximum(s - 2, 0) if lag else jnp.minimum(s, n_tiles - 1)
        return _bwd_tile(pos, n_tiles) if bwd else pos

    def col(c, bwd, lag=False):
        return pl.BlockSpec((1, TILE, GROUP_W), lambda i, s: (i, tile_of(s, bwd, lag), c))

    def gate(bwd):
        return pl.BlockSpec((1, TILE, GATE_LANES), lambda i, s: (i, tile_of(s, bwd, False), 0))

    row = pl.BlockSpec((1, GATE_LANES), lambda i, s: (0, 0))
    per_tile = 2 * CHUNKS_PER_TILE
    n_groups = 3 * per_tile
    n_units = n_groups * N_HEADS
    n_mid = 2 * per_tile
    return pl.pallas_call(
        _gdn_kernel,
        grid=(b, n_tiles + 2),
        in_specs=[col(0, False), col(1, False), col(2, False), gate(False),
                  col(0, True), col(1, True), col(2, True), gate(True), row, row],
        out_specs=[col(0, False, lag=True), col(0, True, lag=True)],
        out_shape=[jax.ShapeDtypeStruct((b, t_total, GROUP_W), BF16)] * 2,
        scratch_shapes=[pltpu.VMEM((2 * N_HEADS, HEAD_DIM, HEAD_DIM), F32),
                        pltpu.VMEM((n_mid, CHUNK, N_HEADS * CHUNK), F32),
                        pltpu.VMEM((n_mid * N_HEADS, CHUNK, 2 * HEAD_DIM), BF16),
                        pltpu.VMEM((n_mid, 1, N_HEADS * CHUNK), F32),
                        pltpu.VMEM((n_units, CHUNK, HEAD_DIM), F32),
                        pltpu.VMEM((n_units, 2 * CHUNK, HEAD_DIM), BF16),
                        pltpu.VMEM((n_groups, CHUNK, N_HEADS * CHUNK), BF16),
                        pltpu.VMEM((n_units, HEAD_DIM, CHUNK), BF16),
                        pltpu.VMEM((n_groups, 1, GATE_LANES), F32)],
        compiler_params=_cparams(("parallel", "arbitrary")),
        name="gdn",
    )(qkv, qkv, qkv, ab, qkv, qkv, qkv, ab, dtb_row, alog_row)


def _ret_kernel(qf_ref, kf_ref, vf_ref, qb_ref, kb_ref, vb_ref, logit_ref, of_ref, ob_ref,
                r_ref, decay_ref, xi_ref, zeta_ref):
    x = logit_ref[...]
    lg_row = jnp.minimum(x, 0.0) - jnp.log(1.0 + jnp.exp(-jnp.abs(x)))

    @pl.when(pl.program_id(1) == 0)
    def _():
        r_ref[...] = jnp.zeros_like(r_ref)
        ri = lax.broadcasted_iota(jnp.int32, (TILE, TILE), 0)
        ci = lax.broadcasted_iota(jnp.int32, (TILE, TILE), 1)
        row = lax.broadcasted_iota(jnp.int32, (TILE, HEAD_DIM), 0).astype(F32)
        for slot in range(2 * N_HEADS):
            rev = slot >= N_HEADS
            lg = lg_row[:, slot:slot + 1]
            diff = ((ci - ri) if rev else (ri - ci)).astype(F32)
            pos = ((TILE - 1.0) - row) if rev else row
            decay_ref[slot] = jnp.where(diff >= 0, jnp.exp(lg * jnp.maximum(diff, 0.0)), 0.0)
            xi_ref[slot] = jnp.exp(lg * (pos + 1.0))
            zeta_ref[slot] = jnp.exp(lg * ((TILE - 1.0) - pos))

    units = [(refs, (N_HEADS if rev else 0) + h, h * HEAD_DIM)
             for refs, rev in (((qf_ref, kf_ref, vf_ref, of_ref), False), ((qb_ref, kb_ref, vb_ref, ob_ref), True))
             for h in range(N_HEADS)]
    qs = [refs[0][0, :, lo:lo + HEAD_DIM] for refs, _, lo in units]
    ks = [refs[1][0, :, lo:lo + HEAD_DIM] for refs, _, lo in units]
    vs = [refs[2][0, :, lo:lo + HEAD_DIM] for refs, _, lo in units]
    qks = [_dot_nt(q, k) for q, k in zip(qs, ks)]
    qkds = [(qk * decay_ref[slot]).astype(BF16) for qk, (_, slot, _) in zip(qks, units)]
    rs = [r_ref[slot] for _, slot, _ in units]
    for (refs, slot, lo), q, v, qkd, r in zip(units, qs, vs, qkds, rs):
        refs[3][0, :, lo:lo + HEAD_DIM] = (
            _dot(qkd, v) + _dot((q * xi_ref[slot]).astype(BF16), r.astype(BF16))).astype(BF16)
    for (_, slot, _), k, v, r in zip(units, ks, vs, rs):
        g_chunk = jnp.exp(lg_row[:, slot:slot + 1] * float(TILE))
        r_ref[slot] = g_chunk * r + _dot((k * zeta_ref[slot]).T.astype(BF16), v)


def _ret(pr, logit_row):
    b, t_total, _ = pr.shape
    n_tiles = t_total // TILE

    def col(c, bwd):
        if bwd:
            return pl.BlockSpec((1, TILE, GROUP_W), lambda i, s: (i, _bwd_tile(s, n_tiles), c))
        return pl.BlockSpec((1, TILE, GROUP_W), lambda i, s: (i, s, c))

    return pl.pallas_call(
        _ret_kernel,
        grid=(b, n_tiles),
        in_specs=[col(0, False), col(1, False), col(2, False), col(0, True), col(1, True), col(2, True),
                  pl.BlockSpec((1, GATE_LANES), lambda i, s: (0, 0))],
        out_specs=[col(0, False), col(0, True)],
        out_shape=[jax.ShapeDtypeStruct((b, t_total, GROUP_W), BF16)] * 2,
        scratch_shapes=[pltpu.VMEM((2 * N_HEADS, HEAD_DIM, HEAD_DIM), F32),
                        pltpu.VMEM((2 * N_HEADS, TILE, TILE), F32),
                        pltpu.VMEM((2 * N_HEADS, TILE, HEAD_DIM), F32),
                        pltpu.VMEM((2 * N_HEADS, TILE, HEAD_DIM), F32)],
        compiler_params=_cparams(("parallel", "arbitrary")),
        name="ret",
    )(pr, pr, pr, pr, pr, pr, logit_row)


def _tail_kernel(gf_ref, gb_ref, z_ref, rf_ref, rb_ref, rg_ref, x_ref, mod_ref, modp_ref, gdn_g_ref, ret_g_ref,
                 nffn_g_ref, w_ref, wi_ref, wo_ref, fg_ref, o_ref, y_ref, x1_scr, h2_scr, *, d_model, d_ff):
    step = pl.program_id(0)
    cur = lax.rem(step, 2)
    prev = 1 - cur

    @pl.when(step == 0)
    def _():
        x1_scr[...] = jnp.zeros_like(x1_scr)
        h2_scr[...] = jnp.zeros_like(h2_scr)

    h2_prev = h2_scr[prev]
    x1_prev = x1_scr[prev]

    def swiglu_chunk(lo):
        hi = min(lo + FFN_CHUNK, d_ff)
        gate = _dot(h2_prev, wi_ref[:, lo:hi])
        up = _dot(h2_prev, wi_ref[:, d_ff + lo:d_ff + hi])
        return _dot((_silu(gate) * up).astype(BF16), wo_ref[lo:hi, :])

    acc = swiglu_chunk(0)

    o = gf_ref[0].astype(F32) + gb_ref[0].astype(F32)
    r = rf_ref[0].astype(F32) + rb_ref[0].astype(F32)
    z = z_ref[0].astype(F32)
    rg = rg_ref[0].astype(F32)
    for h in range(N_HEADS):
        lo = h * HEAD_DIM
        o_h = o[:, lo:lo + HEAD_DIM]
        o_n = o_h * lax.rsqrt(jnp.mean(o_h * o_h, axis=-1, keepdims=True) + NORM_EPS) * gdn_g_ref[...]
        y_ref[:, lo:lo + HEAD_DIM] = (o_n * _silu(z[:, lo:lo + HEAD_DIM])).astype(BF16)
        r_h = r[:, lo:lo + HEAD_DIM]
        mu = jnp.mean(r_h, axis=-1, keepdims=True)
        cen = r_h - mu
        var = jnp.mean(cen * cen, axis=-1, keepdims=True)
        r_n = cen * lax.rsqrt(var + NORM_EPS) * ret_g_ref[...]
        y_ref[:, GROUP_W + lo:GROUP_W + lo + HEAD_DIM] = (r_n * _silu(rg[:, lo:lo + HEAD_DIM])).astype(BF16)
    mod = mod_ref[0]
    x1 = x_ref[0] + mod[:, 2 * d_model:3 * d_model] * _dot(y_ref[...], w_ref[...])
    x1_scr[cur] = x1
    h2_scr[cur] = _rms_mod(x1, nffn_g_ref[...], mod[:, 3 * d_model:4 * d_model],
                           mod[:, 4 * d_model:5 * d_model]).astype(BF16)

    for lo in range(FFN_CHUNK, d_ff, FFN_CHUNK):
        acc = acc + swiglu_chunk(lo)
    x2 = x1_prev + modp_ref[0][:, 5 * d_model:6 * d_model] * acc
    ms = jnp.mean(x2 * x2, axis=-1, keepdims=True)
    o_ref[0] = x2 * lax.rsqrt(ms + NORM_EPS) * fg_ref[...]


def _tail(gdn_f, gdn_b, pa, ret_f, ret_b, pr, x, mod3, gdn_g, ret_g, nffn_g, w_out, w_ffn_in, w_ffn_out,
          final_g):
    b, seq, d = x.shape
    d_ff = w_ffn_out.shape[0]
    n_lat = seq // TAIL_TOK
    total = b * n_lat
    first = (pa.shape[1] - seq) // TAIL_TOK

    def cur(s):
        return jnp.minimum(s, total - 1)

    def prev(s):
        return jnp.maximum(s - 1, 0)

    def tok(c):
        return pl.BlockSpec((1, TAIL_TOK, GROUP_W), lambda s: (cur(s) // n_lat, cur(s) % n_lat + first, c))

    def resident(shape):
        return pl.BlockSpec(shape, lambda s: (0,) * len(shape), pipeline_mode=pl.Buffered(1))

    return pl.pallas_call(
        functools.partial(_tail_kernel, d_model=d, d_ff=d_ff),
        grid=(total + 1,),
        in_specs=[tok(0), tok(0), tok(3), tok(0), tok(0), tok(3),
                  pl.BlockSpec((1, TAIL_TOK, d), lambda s: (cur(s) // n_lat, cur(s) % n_lat, 0)),
                  pl.BlockSpec((1, 1, mod3.shape[2]), lambda s: (cur(s) // n_lat, 0, 0)),
                  pl.BlockSpec((1, 1, mod3.shape[2]), lambda s: (prev(s) // n_lat, 0, 0)),
                  resident((1, HEAD_DIM)), resident((1, HEAD_DIM)), resident((1, d)),
                  resident(w_out.shape), resident(w_ffn_in.shape), resident(w_ffn_out.shape),
                  resident((1, d))],
        out_specs=pl.BlockSpec((1, TAIL_TOK, d), lambda s: (prev(s) // n_lat, prev(s) % n_lat, 0)),
        out_shape=jax.ShapeDtypeStruct((b, seq, d), F32),
        scratch_shapes=[pltpu.VMEM((TAIL_TOK, 2 * GROUP_W), BF16),
                        pltpu.VMEM((2, TAIL_TOK, d), F32),
                        pltpu.VMEM((2, TAIL_TOK, d), BF16)],
        compiler_params=_cparams(("arbitrary",)),
        name="tail",
    )(gdn_f, gdn_b, pa, ret_f, ret_b, pr, x, mod3, mod3, gdn_g, ret_g, nffn_g, w_out, w_ffn_in, w_ffn_out,
      final_g)


def _rope_tables(ctx_len, n_lat):
    def angles(pos, n_pairs):
        inv = ROPE_THETA ** (-np.arange(n_pairs, dtype=np.float64) / n_pairs)
        return pos[:, None] * inv[None, :]

    rows = n_lat // GRID_W
    row = np.repeat(np.arange(rows, dtype=np.float64), GRID_W)
    col = np.tile(np.arange(GRID_W, dtype=np.float64), rows)
    zeros = np.zeros((ctx_len,), np.float64)
    p_seq = np.concatenate([np.arange(ctx_len, dtype=np.float64), np.full((n_lat,), float(ctx_len))])
    ang = np.concatenate([angles(p_seq, ROPE_PAIRS[0]),
                          angles(np.concatenate([zeros, row]), ROPE_PAIRS[1]),
                          angles(np.concatenate([zeros, col]), ROPE_PAIRS[2])], axis=-1)
    cos, sin = np.cos(ang).astype(np.float32), np.sin(ang).astype(np.float32)
    return jnp.asarray(np.concatenate([cos, cos], axis=-1)), jnp.asarray(np.concatenate([-sin, sin], axis=-1))


def _lane_row(values):
    flat = values.reshape(1, -1).astype(F32)
    return jnp.pad(flat, ((0, 0), (0, GATE_LANES - flat.shape[1])))


def kernel(x, c, ctx, c_ctx, ada_w, ada_b, norm_mix_g, norm_ffn_g, w_in, conv_w, gdn_a_log, gdn_dt_bias,
           gdn_norm_g, ret_decay_logit, ret_norm_g, w_out, w_ffn_in, w_ffn_out, final_g):
    assert ada_w.shape[0] == 1, "single-layer block"
    b, seq, d = x.shape
    ctx_len = ctx.shape[1]
    assert ctx_len == TILE and seq % TILE == 0 and TILE % TAIL_TOK == 0 and b + 1 <= 8

    cond = jnp.concatenate([c, c_ctx[None, :], jnp.zeros((8 - b - 1, d), F32)], axis=0)
    mod = _ada(cond, ada_w, ada_b)
    mod3 = mod[:, None, :]

    n_qkvz = 4 * GROUP_W
    n_gate = 4 * N_HEADS
    w = w_in[0]
    wa = w[:, :n_qkvz].astype(BF16)
    wab = jnp.pad(w[:, n_qkvz:n_qkvz + n_gate], ((0, 0), (0, GATE_LANES - n_gate))).astype(BF16)
    wr = w[:, n_qkvz + n_gate:].astype(BF16)
    cos2, sin2 = _rope_tables(ctx_len, seq)
    pa, ab, pr = _inproj(x, ctx, mod3, norm_mix_g, wa, wab, wr, cos2, sin2)

    qkv = _conv(pa, conv_w[0])
    dtb_row = _lane_row(gdn_dt_bias[0])
    gdn_f, gdn_b = _gdn(qkv, ab, dtb_row, _lane_row(gdn_a_log[0]))
    ret_f, ret_b = _ret(pr, _lane_row(ret_decay_logit[0]))

    return _tail(gdn_f, gdn_b, pa, ret_f, ret_b, pr, x, mod3, gdn_norm_g, ret_norm_g, norm_ffn_g,
                 w_out[0].astype(BF16), w_ffn_in[0].astype(BF16), w_ffn_out[0].astype(BF16), final_g[None, :])
```

```python
import functools
import math

import jax
import jax.numpy as jnp
import numpy as np
from jax import lax
from jax.experimental import pallas as pl
from jax.experimental.pallas import tpu as pltpu

F32 = jnp.float32
BF16 = jnp.bfloat16

HEAD_DIM = 128
N_HEADS = 4
GROUP_W = N_HEADS * HEAD_DIM
CONV_K = 5
GRID_W = 64
ROPE_THETA = 10000.0
ROPE_PAIRS = (16, 24, 24)
NORM_EPS = 1e-6

CHUNK = 64
TILE = 256
CHUNKS_PER_TILE = TILE // CHUNK
HALO = 16
GATE_LANES = 128
TAIL_TOK = 256
FFN_CHUNK = 1024
VMEM_LIMIT = 56 * 1024 * 1024


def _cparams(sem):
    return pltpu.CompilerParams(dimension_semantics=sem, vmem_limit_bytes=VMEM_LIMIT)


def _dot(a, b):
    return jnp.dot(a, b, preferred_element_type=F32)


def _dot_nt(a, b):
    return lax.dot_general(a, b, (((1,), (1,)), ((), ())), preferred_element_type=F32)


def _split(a):
    hi = a.astype(BF16)
    lo = (a - hi.astype(F32)).astype(BF16)
    return hi, lo


def _silu(x):
    return x * jax.nn.sigmoid(x)


def _lane_bcast(col, width):
    return jnp.broadcast_to(col, (col.shape[0], width))


def _ada_kernel(cond_ref, w_ref, b_ref, o_ref):
    a_hi, a_lo = _split(_silu(cond_ref[...]))
    w_hi, w_lo = _split(w_ref[0])
    o_ref[...] = _dot(a_hi, w_hi) + _dot(a_lo, w_hi) + _dot(a_hi, w_lo) + b_ref[...]


def _ada(cond, w, b):
    rows, d = cond.shape
    n = w.shape[2]
    bn = 1536
    return pl.pallas_call(
        _ada_kernel,
        grid=(n // bn,),
        in_specs=[pl.BlockSpec((rows, d), lambda j: (0, 0)),
                  pl.BlockSpec((1, d, bn), lambda j: (0, 0, j)),
                  pl.BlockSpec((1, bn), lambda j: (0, j))],
        out_specs=pl.BlockSpec((rows, bn), lambda j: (0, j)),
        out_shape=jax.ShapeDtypeStruct((rows, n), F32),
        compiler_params=_cparams(("parallel",)),
        name="ada",
    )(cond, w, b)


def _rms_mod(x, g, shift, scale):
    ms = jnp.mean(x * x, axis=-1, keepdims=True)
    return (x * lax.rsqrt(ms + NORM_EPS) * g) * (1.0 + scale) + shift


def _inproj_kernel(x_ref, ctx_ref, mod_ref, g_ref, w_ref, cos_ref, sin_ref,
                   pa_ref, ab_ref, pr_ref, wa_ref, wab_ref, wr_ref, *, d_model, n_half, n_gate):
    t = pl.program_id(1)

    @pl.when(jnp.logical_and(pl.program_id(0) == 0, t == 0))
    def _():
        wa_ref[...] = w_ref[0, :, :n_half].astype(BF16)
        wr_ref[...] = w_ref[0, :, n_half + n_gate:].astype(BF16)
        lane = lax.broadcasted_iota(jnp.int32, (d_model, GATE_LANES), 1)
        wab_ref[...] = jnp.where(lane < n_gate, w_ref[0, :, n_half:n_half + GATE_LANES], 0.0).astype(BF16)

    xin = jnp.where(t == 0, ctx_ref[0], x_ref[0])
    mod = mod_ref[0]
    h = _rms_mod(xin, g_ref[...], mod[:, 0:d_model], mod[:, d_model:2 * d_model]).astype(BF16)
    pa_ref[0] = _dot(h, wa_ref[...]).astype(BF16)
    ab_ref[0] = _dot(h, wab_ref[...])
    pr = _dot(h, wr_ref[...])
    cos2 = cos_ref[...]
    sin2 = sin_ref[...]
    k_scale = HEAD_DIM ** -0.5
    for hd in range(2 * N_HEADS):
        lo = hd * HEAD_DIM
        tt = pr[:, lo:lo + HEAD_DIM]
        rot = tt * cos2 + pltpu.roll(tt, HEAD_DIM // 2, 1) * sin2
        if hd >= N_HEADS:
            rot = rot * k_scale
        pr_ref[0, :, lo:lo + HEAD_DIM] = rot.astype(BF16)
    pr_ref[0, :, 2 * GROUP_W:] = pr[:, 2 * GROUP_W:].astype(BF16)


def _inproj(x, ctx, mod3, g, w_in, n_half, n_gate, cos2, sin2):
    b, seq, d = x.shape
    n_tiles = (ctx.shape[1] + seq) // TILE
    t_total = n_tiles * TILE
    assert w_in.shape[2] == 2 * n_half + n_gate and n_gate <= GATE_LANES
    return pl.pallas_call(
        functools.partial(_inproj_kernel, d_model=d, n_half=n_half, n_gate=n_gate),
        grid=(b, n_tiles),
        in_specs=[pl.BlockSpec((1, TILE, d), lambda i, t: (i, jnp.maximum(t - 1, 0), 0)),
                  pl.BlockSpec((1, TILE, d), lambda i, t: (i, 0, 0)),
                  pl.BlockSpec((1, 1, mod3.shape[2]), lambda i, t: (jnp.where(t == 0, b, i), 0, 0)),
                  pl.BlockSpec((1, d), lambda i, t: (0, 0)),
                  pl.BlockSpec(w_in.shape, lambda i, t: (0, 0, 0), pipeline_mode=pl.Buffered(1)),
                  pl.BlockSpec((TILE, HEAD_DIM), lambda i, t: (t, 0)),
                  pl.BlockSpec((TILE, HEAD_DIM), lambda i, t: (t, 0))],
        out_specs=[pl.BlockSpec((1, TILE, n_half), lambda i, t: (i, t, 0)),
                   pl.BlockSpec((1, TILE, GATE_LANES), lambda i, t: (i, t, 0)),
                   pl.BlockSpec((1, TILE, n_half), lambda i, t: (i, t, 0))],
        out_shape=[jax.ShapeDtypeStruct((b, t_total, n_half), BF16),
                   jax.ShapeDtypeStruct((b, t_total, GATE_LANES), F32),
                   jax.ShapeDtypeStruct((b, t_total, n_half), BF16)],
        scratch_shapes=[pltpu.VMEM((d, n_half), BF16), pltpu.VMEM((d, GATE_LANES), BF16),
                        pltpu.VMEM((d, n_half), BF16)],
        compiler_params=_cparams(("arbitrary", "arbitrary")),
        name="inproj",
    )(x, ctx, mod3, g, w_in, cos2, sin2)


def _conv_kernel(cur_ref, prev_ref, next_ref, w_ref, o_ref, ext_ref):
    t = pl.program_id(1)
    last = pl.num_programs(1) - 1
    has_prev = t >= 2
    has_next = jnp.logical_and(t >= 1, t < last)
    ext_ref[0:HALO, :] = jnp.where(has_prev, prev_ref[0].astype(F32), 0.0)
    ext_ref[HALO:HALO + TILE, :] = cur_ref[0].astype(F32)
    ext_ref[HALO + TILE:, :] = jnp.where(has_next, next_ref[0].astype(F32), 0.0)
    pad = (CONV_K - 1) // 2
    q_scale = HEAD_DIM ** -0.5
    for part in range(3):
        for hd in range(N_HEADS):
            lo = part * GROUP_W + hd * HEAD_DIM
            acc = None
            for i in range(CONV_K):
                r0 = HALO - pad + i
                term = ext_ref[r0:r0 + TILE, lo:lo + HEAD_DIM] * w_ref[i:i + 1, lo:lo + HEAD_DIM]
                acc = term if acc is None else acc + term
            y = _silu(acc)
            if part < 2:
                y = y * lax.rsqrt(jnp.sum(y * y, axis=-1, keepdims=True) + NORM_EPS)
            if part == 0:
                y = y * q_scale
            o_ref[0, :, lo:lo + HEAD_DIM] = y.astype(BF16)


def _conv(pa, conv_w):
    b, t_total, _ = pa.shape
    n_tiles = t_total // TILE
    width = 3 * GROUP_W
    per = TILE // HALO
    n_halo = t_total // HALO
    return pl.pallas_call(
        _conv_kernel,
        grid=(b, n_tiles),
        in_specs=[pl.BlockSpec((1, TILE, width), lambda i, t: (i, t, 0)),
                  pl.BlockSpec((1, HALO, width), lambda i, t: (i, jnp.maximum(t * per - 1, 0), 0)),
                  pl.BlockSpec((1, HALO, width), lambda i, t: (i, jnp.minimum((t + 1) * per, n_halo - 1), 0)),
                  pl.BlockSpec((CONV_K, width), lambda i, t: (0, 0))],
        out_specs=pl.BlockSpec((1, TILE, width), lambda i, t: (i, t, 0)),
        out_shape=jax.ShapeDtypeStruct((b, t_total, width), BF16),
        scratch_shapes=[pltpu.VMEM((TILE + 2 * HALO, width), F32)],
        compiler_params=_cparams(("parallel", "parallel")),
        name="conv",
    )(pa, pa, pa, conv_w)


def _packed_matmul(x, y, bd_mask):
    yb = y.astype(BF16)
    y_bd = jnp.where(bd_mask, jnp.concatenate([yb] * N_HEADS, axis=0), jnp.zeros((), BF16))
    return _dot(x.astype(BF16), y_bd)


def _unit_lower_inverse_stages(a_list, eye, m16, m32, m64, bd_mask, out):
    pm = functools.partial(_packed_matmul, bd_mask=bd_mask)
    ps = [-jnp.where(m16, a, 0.0) for a in a_list]
    xs = [eye + p for p in ps]
    for _ in range(3):
        ps = [pm(p, p) for p in ps]
        yield
        xs = [x + pm(x, p) for x, p in zip(xs, ps)]
        yield
    for m in (m32, m64):
        zs = [pm(jnp.where(m, a, 0.0), x) for a, x in zip(a_list, xs)]
        yield
        xs = [x - pm(x, z) for x, z in zip(xs, zs)]
        yield
    out.extend(xs)


def _gdn_front_stages(groups, ring_base, mid_base, a_scr, rhs_scr, brow_scr, wq_scr, qkd_scr, kdt_scr,
                      el_scr):
    lane = lax.broadcasted_iota(jnp.int32, (CHUNK, GATE_LANES), 1)
    low_half = lax.broadcasted_iota(jnp.int32, (CHUNK, HEAD_DIM), 1) < CHUNK
    zero_k = jnp.zeros((CHUNK, HEAD_DIM), BF16)
    for g in groups:
        tcum = g["consts"][0]
        g_hi, g_lo = _split(g["gates"])
        g["gc"] = _dot(tcum, g_hi) + _dot(tcum, g_lo)
    yield

    for gi, g in enumerate(groups):
        g_lane0 = N_HEADS if g["rev"] else 0
        b_lane0 = 2 * N_HEADS + g_lane0
        last = 0 if g["rev"] else CHUNK - 1
        gc, gates = g["gc"], g["gates"]
        ct = jnp.where(lane < 2 * N_HEADS, gc, gates).T
        egl_t = jnp.exp(ct[:, last:last + 1] - ct)
        gcb, kb, ms = [], [], []
        for h in range(N_HEADS):
            lo = h * HEAD_DIM
            gl = g_lane0 + h
            ring_unit = (ring_base + gi) * N_HEADS + h
            gcb.append(_lane_bcast(gc[:, gl:gl + 1], HEAD_DIM))
            beta = _lane_bcast(gates[:, b_lane0 + h:b_lane0 + h + 1], HEAD_DIM)
            eg_h = jnp.exp(gcb[h])
            q_h = g["q"][:, lo:lo + HEAD_DIM]
            k_h = g["k"][:, lo:lo + HEAD_DIM].astype(F32)
            kb.append((k_h * beta).astype(BF16))
            rhs_scr[(mid_base + gi) * N_HEADS + h] = jnp.concatenate(
                [g["v"][:, lo:lo + HEAD_DIM], (k_h * eg_h).astype(BF16)], axis=1)
            wq_scr[ring_unit, CHUNK:, :] = (q_h * eg_h).astype(BF16)
            kdt_scr[ring_unit] = (k_h.T * egl_t[gl:gl + 1, :]).astype(BF16)
        d_parts = []
        for p in range(N_HEADS // 2):
            h0, h1 = 2 * p, 2 * p + 1
            lo = h0 * HEAD_DIM
            lhs = jnp.concatenate([g["q"][:, lo:lo + 2 * HEAD_DIM], jnp.concatenate([kb[h0], kb[h1]], axis=1)],
                                  axis=0)
            k0 = g["k"][:, lo:lo + HEAD_DIM]
            k1 = g["k"][:, lo + HEAD_DIM:lo + 2 * HEAD_DIM]
            rhs_bd = jnp.concatenate([jnp.concatenate([k0, zero_k], axis=1),
                                      jnp.concatenate([zero_k, k1], axis=1)], axis=0)
            ms.append(_dot_nt(lhs, rhs_bd))
            g_row = jnp.concatenate([ct[g_lane0 + h0:g_lane0 + h0 + 1, :], ct[g_lane0 + h1:g_lane0 + h1 + 1, :]],
                                    axis=1)
            d_parts.append(jnp.where(low_half, gcb[h0], gcb[h1]) - g_row)
        incl4, strict4 = g["consts"][2:]
        e = jnp.exp(jnp.where(incl4, jnp.concatenate(d_parts, axis=1), 0.0))
        a_scr[mid_base + gi] = jnp.where(strict4, jnp.concatenate([m[CHUNK:] for m in ms], axis=1) * e, 0.0)
        qkd_scr[ring_base + gi] = jnp.where(incl4, jnp.concatenate([m[:CHUNK] for m in ms], axis=1) * e,
                                            0.0).astype(BF16)
        el_scr[ring_base + gi] = jnp.exp(gc[last:last + 1, :])
        brow_scr[mid_base + gi] = jnp.concatenate(
            [ct[b_lane0 + h:b_lane0 + h + 1, :] for h in range(N_HEADS)], axis=1)
        yield


def _gdn_solve_stages(n_groups, masks, ring_base, mid_base, a_scr, rhs_scr, brow_scr, u_scr, wq_scr):
    eye4, m16, m32, m64, bd_mask = masks
    t_invs = []
    yield from _unit_lower_inverse_stages([a_scr[mid_base + gi] for gi in range(n_groups)],
                                          eye4, m16, m32, m64, bd_mask, t_invs)
    for gi, t_inv in enumerate(t_invs):
        t_b = (t_inv * brow_scr[mid_base + gi]).astype(BF16)
        for h in range(N_HEADS):
            sol = _dot(t_b[:, h * CHUNK:(h + 1) * CHUNK], rhs_scr[(mid_base + gi) * N_HEADS + h])
            ring_unit = (ring_base + gi) * N_HEADS + h
            u_scr[ring_unit] = sol[:, :HEAD_DIM]
            wq_scr[ring_unit, :CHUNK, :] = sol[:, HEAD_DIM:].astype(BF16)


def _gdn_scan_stages(slot_base, keep_state, u_scr, wq_scr, qkd_scr, kdt_scr, el_scr, s_ref, of_ref, ob_ref):
    for j in range(CHUNKS_PER_TILE):
        units = [(slot_base + 2 * j + d, h, d * N_HEADS + h) for d in range(2) for h in range(N_HEADS)]
        states = [jnp.where(keep_state, s_ref[slot], 0.0) if j == 0 else s_ref[slot] for _, _, slot in units]
        wss = [_dot(wq_scr[gi * N_HEADS + h], s.astype(BF16)) for (gi, h, _), s in zip(units, states)]
        yield
        v_news = [(u_scr[gi * N_HEADS + h] - ws[:CHUNK]).astype(BF16) for (gi, h, _), ws in zip(units, wss)]
        outs = [ws[CHUNK:] + _dot(qkd_scr[gi][:, h * CHUNK:(h + 1) * CHUNK], vn)
                for (gi, h, _), ws, vn in zip(units, wss, v_news)]
        for (gi, h, slot), s, vn in zip(units, states, v_news):
            s_ref[slot] = el_scr[gi][:, slot:slot + 1] * s + _dot(kdt_scr[gi * N_HEADS + h], vn)
        rf = j * CHUNK
        rb = (CHUNKS_PER_TILE - 1 - j) * CHUNK
        of_ref[0, rf:rf + CHUNK, :] = jnp.concatenate(outs[:N_HEADS], axis=1).astype(BF16)
        ob_ref[0, rb:rb + CHUNK, :] = jnp.concatenate(outs[N_HEADS:], axis=1).astype(BF16)
        yield


def _interleave(*stage_generators):
    live = list(stage_generators)
    while live:
        for gen in list(live):
            try:
                next(gen)
            except StopIteration:
                live.remove(gen)


def _gate_tile(ab, dtb, a_log):
    z = ab + dtb
    softplus = jnp.maximum(z, 0.0) + jnp.log(1.0 + jnp.exp(-jnp.abs(z)))
    lane = lax.broadcasted_iota(jnp.int32, ab.shape, 1)
    return jnp.where(lane < 2 * N_HEADS, -jnp.exp(a_log) * softplus, jax.nn.sigmoid(ab))


def _gdn_kernel(qf_ref, kf_ref, vf_ref, abf_ref, qb_ref, kb_ref, vb_ref, abb_ref, dtb_ref, alog_ref,
                of_ref, ob_ref, s_ref, a_scr, rhs_scr, brow_scr, u_scr, wq_scr, qkd_scr, kdt_scr, el_scr, *,
                tiles_per_sample):
    step = pl.program_id(0)
    keep_state = lax.rem(step + tiles_per_sample - 2, tiles_per_sample) != 0
    groups_per_tile = 2 * CHUNKS_PER_TILE
    front_ring = lax.rem(step, 3) * groups_per_tile
    solve_ring = lax.rem(step + 2, 3) * groups_per_tile
    scan_ring = lax.rem(step + 1, 3) * groups_per_tile
    front_mid = lax.rem(step, 2) * groups_per_tile
    solve_mid = groups_per_tile - front_mid

    @pl.when(step == 0)
    def _():
        for scr in (s_ref, a_scr, rhs_scr, brow_scr, u_scr, wq_scr, qkd_scr, kdt_scr, el_scr):
            scr[...] = jnp.zeros_like(scr)

    ri = lax.broadcasted_iota(jnp.int32, (CHUNK, CHUNK), 0)
    ci = lax.broadcasted_iota(jnp.int32, (CHUNK, CHUNK), 1)
    ri4 = lax.broadcasted_iota(jnp.int32, (CHUNK, N_HEADS * CHUNK), 0)
    ci4 = lax.broadcasted_iota(jnp.int32, (CHUNK, N_HEADS * CHUNK), 1) % CHUNK
    rbd = lax.broadcasted_iota(jnp.int32, (N_HEADS * CHUNK, N_HEADS * CHUNK), 0) // CHUNK
    cbd = lax.broadcasted_iota(jnp.int32, (N_HEADS * CHUNK, N_HEADS * CHUNK), 1) // CHUNK
    bd_mask = rbd == cbd
    eye4 = (ri4 == ci4).astype(F32)
    m16 = (ri4 // 16) == (ci4 // 16)
    m32 = jnp.logical_and((ri4 // 32) == (ci4 // 32), jnp.logical_not(m16))
    m64 = (ri4 // 32) != (ci4 // 32)

    cf = ((ci <= ri).astype(BF16), None, ri4 >= ci4, ri4 > ci4)
    cb = ((ci >= ri).astype(BF16), None, ri4 <= ci4, ri4 < ci4)
    dtb = dtb_ref[...]
    alog = alog_ref[...]

    groups = []
    for j in range(CHUNKS_PER_TILE):
        rf = j * CHUNK
        rb = (CHUNKS_PER_TILE - 1 - j) * CHUNK
        groups.append(dict(q=qf_ref[0, rf:rf + CHUNK, :], k=kf_ref[0, rf:rf + CHUNK, :],
                           v=vf_ref[0, rf:rf + CHUNK, :],
                           gates=_gate_tile(abf_ref[0, rf:rf + CHUNK, :], dtb, alog), rev=False, consts=cf))
        groups.append(dict(q=qb_ref[0, rb:rb + CHUNK, :], k=kb_ref[0, rb:rb + CHUNK, :],
                           v=vb_ref[0, rb:rb + CHUNK, :],
                           gates=_gate_tile(abb_ref[0, rb:rb + CHUNK, :], dtb, alog), rev=True, consts=cb))
    _interleave(
        _gdn_solve_stages(len(groups), (eye4, m16, m32, m64, bd_mask), solve_ring, solve_mid,
                          a_scr, rhs_scr, brow_scr, u_scr, wq_scr),
        _gdn_scan_stages(scan_ring, keep_state, u_scr, wq_scr, qkd_scr, kdt_scr, el_scr, s_ref, of_ref, ob_ref),
        _gdn_front_stages(groups, front_ring, front_mid, a_scr, rhs_scr, brow_scr, wq_scr, qkd_scr, kdt_scr,
                          el_scr))


def _bwd_tile(s, n_tiles):
    return jnp.where(s == 0, 0, n_tiles - s)


def _gdn(qkv, ab, dtb_row, alog_row):
    b, t_total, _ = qkv.shape
    n_tiles = t_total // TILE

    total = b * n_tiles

    def block_of(f, bwd, lag, c):
        flat = jnp.maximum(f - 2, 0) if lag else jnp.minimum(f, total - 1)
        pos = flat % n_tiles
        return (flat // n_tiles, _bwd_tile(pos, n_tiles) if bwd else pos, c)

    def col(c, bwd, lag=False):
        return pl.BlockSpec((1, TILE, GROUP_W), lambda f: block_of(f, bwd, lag, c))

    def gate(bwd):
        return pl.BlockSpec((1, TILE, GATE_LANES), lambda f: block_of(f, bwd, False, 0))

    row = pl.BlockSpec((1, GATE_LANES), lambda f: (0, 0))
    per_tile = 2 * CHUNKS_PER_TILE
    n_groups = 3 * per_tile
    n_units = n_groups * N_HEADS
    n_mid = 2 * per_tile
    return pl.pallas_call(
        functools.partial(_gdn_kernel, tiles_per_sample=n_tiles),
        grid=(total + 2,),
        in_specs=[col(0, False), col(1, False), col(2, False), gate(False),
                  col(0, True), col(1, True), col(2, True), gate(True), row, row],
        out_specs=[col(0, False, lag=True), col(0, True, lag=True)],
        out_shape=[jax.ShapeDtypeStruct((b, t_total, GROUP_W), BF16)] * 2,
        scratch_shapes=[pltpu.VMEM((2 * N_HEADS, HEAD_DIM, HEAD_DIM), F32),
                        pltpu.VMEM((n_mid, CHUNK, N_HEADS * CHUNK), F32),
                        pltpu.VMEM((n_mid * N_HEADS, CHUNK, 2 * HEAD_DIM), BF16),
                        pltpu.VMEM((n_mid, 1, N_HEADS * CHUNK), F32),
                        pltpu.VMEM((n_units, CHUNK, HEAD_DIM), F32),
                        pltpu.VMEM((n_units, 2 * CHUNK, HEAD_DIM), BF16),
                        pltpu.VMEM((n_groups, CHUNK, N_HEADS * CHUNK), BF16),
                        pltpu.VMEM((n_units, HEAD_DIM, CHUNK), BF16),
                        pltpu.VMEM((n_groups, 1, GATE_LANES), F32)],
        compiler_params=_cparams(("arbitrary",)),
        name="gdn",
    )(qkv, qkv, qkv, ab, qkv, qkv, qkv, ab, dtb_row, alog_row)


def _ret_kernel(qf_ref, kf_ref, vf_ref, qb_ref, kb_ref, vb_ref, logit_ref, of_ref, ob_ref,
                r_ref, decay_ref, xi_ref, zeta_ref):
    x = logit_ref[...]
    lg_row = jnp.minimum(x, 0.0) - jnp.log(1.0 + jnp.exp(-jnp.abs(x)))

    @pl.when(pl.program_id(1) == 0)
    def _():
        r_ref[...] = jnp.zeros_like(r_ref)
        ri = lax.broadcasted_iota(jnp.int32, (TILE, TILE), 0)
        ci = lax.broadcasted_iota(jnp.int32, (TILE, TILE), 1)
        row = lax.broadcasted_iota(jnp.int32, (TILE, HEAD_DIM), 0).astype(F32)
        for slot in range(2 * N_HEADS):
            rev = slot >= N_HEADS
            lg = lg_row[:, slot:slot + 1]
            diff = ((ci - ri) if rev else (ri - ci)).astype(F32)
            pos = ((TILE - 1.0) - row) if rev else row
            decay_ref[slot] = jnp.where(diff >= 0, jnp.exp(lg * jnp.maximum(diff, 0.0)), 0.0)
            xi_ref[slot] = jnp.exp(lg * (pos + 1.0))
            zeta_ref[slot] = jnp.exp(lg * ((TILE - 1.0) - pos))

    units = [(refs, (N_HEADS if rev else 0) + h, h * HEAD_DIM)
             for refs, rev in (((qf_ref, kf_ref, vf_ref, of_ref), False), ((qb_ref, kb_ref, vb_ref, ob_ref), True))
             for h in range(N_HEADS)]
    qs = [refs[0][0, :, lo:lo + HEAD_DIM] for refs, _, lo in units]
    ks = [refs[1][0, :, lo:lo + HEAD_DIM] for refs, _, lo in units]
    vs = [refs[2][0, :, lo:lo + HEAD_DIM] for refs, _, lo in units]
    qks = [_dot_nt(q, k) for q, k in zip(qs, ks)]
    qkds = [(qk * decay_ref[slot]).astype(BF16) for qk, (_, slot, _) in zip(qks, units)]
    rs = [r_ref[slot] for _, slot, _ in units]
    for (refs, slot, lo), q, v, qkd, r in zip(units, qs, vs, qkds, rs):
        refs[3][0, :, lo:lo + HEAD_DIM] = (
            _dot(qkd, v) + _dot((q * xi_ref[slot]).astype(BF16), r.astype(BF16))).astype(BF16)
    for (_, slot, _), k, v, r in zip(units, ks, vs, rs):
        g_chunk = jnp.exp(lg_row[:, slot:slot + 1] * float(TILE))
        r_ref[slot] = g_chunk * r + _dot((k * zeta_ref[slot]).T.astype(BF16), v)


def _ret(pr, logit_row):
    b, t_total, _ = pr.shape
    n_tiles = t_total // TILE

    def col(c, bwd):
        if bwd:
            return pl.BlockSpec((1, TILE, GROUP_W), lambda i, s: (i, _bwd_tile(s, n_tiles), c))
        return pl.BlockSpec((1, TILE, GROUP_W), lambda i, s: (i, s, c))

    return pl.pallas_call(
        _ret_kernel,
        grid=(b, n_tiles),
        in_specs=[col(0, False), col(1, False), col(2, False), col(0, True), col(1, True), col(2, True),
                  pl.BlockSpec((1, GATE_LANES), lambda i, s: (0, 0))],
        out_specs=[col(0, False), col(0, True)],
        out_shape=[jax.ShapeDtypeStruct((b, t_total, GROUP_W), BF16)] * 2,
        scratch_shapes=[pltpu.VMEM((2 * N_HEADS, HEAD_DIM, HEAD_DIM), F32),
                        pltpu.VMEM((2 * N_HEADS, TILE, TILE), F32),
                        pltpu.VMEM((2 * N_HEADS, TILE, HEAD_DIM), F32),
                        pltpu.VMEM((2 * N_HEADS, TILE, HEAD_DIM), F32)],
        compiler_params=_cparams(("parallel", "arbitrary")),
        name="ret",
    )(pr, pr, pr, pr, pr, pr, logit_row)


def _tail_kernel(gf_ref, gb_ref, z_ref, rf_ref, rb_ref, rg_ref, x_ref, mod_ref, gdn_g_ref, ret_g_ref,
                 nffn_g_ref, w_ref, wi_ref, wo_ref, fg_ref, o_ref, y_ref, *, d_model, d_ff):
    o = gf_ref[0].astype(F32) + gb_ref[0].astype(F32)
    r = rf_ref[0].astype(F32) + rb_ref[0].astype(F32)
    z = z_ref[0].astype(F32)
    rg = rg_ref[0].astype(F32)
    for h in range(N_HEADS):
        lo = h * HEAD_DIM
        o_h = o[:, lo:lo + HEAD_DIM]
        o_n = o_h * lax.rsqrt(jnp.mean(o_h * o_h, axis=-1, keepdims=True) + NORM_EPS) * gdn_g_ref[...]
        y_ref[:, lo:lo + HEAD_DIM] = (o_n * _silu(z[:, lo:lo + HEAD_DIM])).astype(BF16)
        r_h = r[:, lo:lo + HEAD_DIM]
        mu = jnp.mean(r_h, axis=-1, keepdims=True)
        cen = r_h - mu
        var = jnp.mean(cen * cen, axis=-1, keepdims=True)
        r_n = cen * lax.rsqrt(var + NORM_EPS) * ret_g_ref[...]
        y_ref[:, GROUP_W + lo:GROUP_W + lo + HEAD_DIM] = (r_n * _silu(rg[:, lo:lo + HEAD_DIM])).astype(BF16)
    mod = mod_ref[0]
    x1 = x_ref[0] + mod[:, 2 * d_model:3 * d_model] * _dot(y_ref[...], w_ref[...])
    h2 = _rms_mod(x1, nffn_g_ref[...], mod[:, 3 * d_model:4 * d_model],
                  mod[:, 4 * d_model:5 * d_model]).astype(BF16)
    acc = None
    for lo in range(0, d_ff, FFN_CHUNK):
        hi = min(lo + FFN_CHUNK, d_ff)
        gate = _dot(h2, wi_ref[:, lo:hi])
        up = _dot(h2, wi_ref[:, d_ff + lo:d_ff + hi])
        part = _dot((_silu(gate) * up).astype(BF16), wo_ref[lo:hi, :])
        acc = part if acc is None else acc + part
    x2 = x1 + mod[:, 5 * d_model:6 * d_model] * acc
    ms = jnp.mean(x2 * x2, axis=-1, keepdims=True)
    o_ref[0] = x2 * lax.rsqrt(ms + NORM_EPS) * fg_ref[...]


def _tail(gdn_f, gdn_b, pa, ret_f, ret_b, pr, x, mod3, gdn_g, ret_g, nffn_g, w_out, w_ffn_in, w_ffn_out,
          final_g):
    b, seq, d = x.shape
    d_ff = w_ffn_out.shape[0]
    first = (pa.shape[1] - seq) // TAIL_TOK

    def tok(c):
        return pl.BlockSpec((1, TAIL_TOK, GROUP_W), lambda i, t: (i, t + first, c))

    def resident(shape):
        return pl.BlockSpec(shape, lambda i, t: (0,) * len(shape), pipeline_mode=pl.Buffered(1))

    return pl.pallas_call(
        functools.partial(_tail_kernel, d_model=d, d_ff=d_ff),
        grid=(b, seq // TAIL_TOK),
        in_specs=[tok(0), tok(0), tok(3), tok(0), tok(0), tok(3),
                  pl.BlockSpec((1, TAIL_TOK, d), lambda i, t: (i, t, 0)),
                  pl.BlockSpec((1, 1, mod3.shape[2]), lambda i, t: (i, 0, 0)),
                  resident((1, HEAD_DIM)), resident((1, HEAD_DIM)), resident((1, d)),
                  resident(w_out.shape), resident(w_ffn_in.shape), resident(w_ffn_out.shape),
                  resident((1, d))],
        out_specs=pl.BlockSpec((1, TAIL_TOK, d), lambda i, t: (i, t, 0)),
        out_shape=jax.ShapeDtypeStruct((b, seq, d), F32),
        scratch_shapes=[pltpu.VMEM((TAIL_TOK, 2 * GROUP_W), BF16)],
        compiler_params=_cparams(("parallel", "parallel")),
        name="tail",
    )(gdn_f, gdn_b, pa, ret_f, ret_b, pr, x, mod3, gdn_g, ret_g, nffn_g, w_out, w_ffn_in, w_ffn_out, final_g)


def _rope_tables(ctx_len, n_lat):
    def angles(pos, n_pairs):
        inv = ROPE_THETA ** (-np.arange(n_pairs, dtype=np.float64) / n_pairs)
        return pos[:, None] * inv[None, :]

    rows = n_lat // GRID_W
    row = np.repeat(np.arange(rows, dtype=np.float64), GRID_W)
    col = np.tile(np.arange(GRID_W, dtype=np.float64), rows)
    zeros = np.zeros((ctx_len,), np.float64)
    p_seq = np.concatenate([np.arange(ctx_len, dtype=np.float64), np.full((n_lat,), float(ctx_len))])
    ang = np.concatenate([angles(p_seq, ROPE_PAIRS[0]),
                          angles(np.concatenate([zeros, row]), ROPE_PAIRS[1]),
                          angles(np.concatenate([zeros, col]), ROPE_PAIRS[2])], axis=-1)
    cos, sin = np.cos(ang).astype(np.float32), np.sin(ang).astype(np.float32)
    return jnp.asarray(np.concatenate([cos, cos], axis=-1)), jnp.asarray(np.concatenate([-sin, sin], axis=-1))


def _lane_row(values):
    flat = values.reshape(1, -1).astype(F32)
    return jnp.pad(flat, ((0, 0), (0, GATE_LANES - flat.shape[1])))


def kernel(x, c, ctx, c_ctx, ada_w, ada_b, norm_mix_g, norm_ffn_g, w_in, conv_w, gdn_a_log, gdn_dt_bias,
           gdn_norm_g, ret_decay_logit, ret_norm_g, w_out, w_ffn_in, w_ffn_out, final_g):
    assert ada_w.shape[0] == 1, "single-layer block"
    b, seq, d = x.shape
    ctx_len = ctx.shape[1]
    assert ctx_len == TILE and seq % TILE == 0 and TILE % TAIL_TOK == 0 and b + 1 <= 8

    cond = jnp.concatenate([c, c_ctx[None, :], jnp.zeros((8 - b - 1, d), F32)], axis=0)
    mod = _ada(cond, ada_w, ada_b)
    mod3 = mod[:, None, :]

    n_qkvz = 4 * GROUP_W
    n_gate = 4 * N_HEADS
    cos2, sin2 = _rope_tables(ctx_len, seq)
    pa, ab, pr = _inproj(x, ctx, mod3, norm_mix_g, w_in, n_qkvz, n_gate, cos2, sin2)

    qkv = _conv(pa, conv_w[0])
    dtb_row = _lane_row(gdn_dt_bias[0])
    gdn_f, gdn_b = _gdn(qkv, ab, dtb_row, _lane_row(gdn_a_log[0]))
    ret_f, ret_b = _ret(pr, _lane_row(ret_decay_logit[0]))

    return _tail(gdn_f, gdn_b, pa, ret_f, ret_b, pr, x, mod3, gdn_norm_g, ret_norm_g, norm_ffn_g,
                 w_out[0].astype(BF16), w_ffn_in[0].astype(BF16), w_ffn_out[0].astype(BF16), final_g[None, :])
```

```python
import functools
import math

import jax
import jax.numpy as jnp
import numpy as np
from jax import lax
from jax.experimental import pallas as pl
from jax.experimental.pallas import tpu as pltpu

F32 = jnp.float32
BF16 = jnp.bfloat16

HEAD_DIM = 128
N_HEADS = 4
GROUP_W = N_HEADS * HEAD_DIM
CONV_K = 5
GRID_W = 64
ROPE_THETA = 10000.0
ROPE_PAIRS = (16, 24, 24)
NORM_EPS = 1e-6

CHUNK = 64
TILE = 256
CHUNKS_PER_TILE = TILE // CHUNK
HALO = 16
GATE_LANES = 128
TAIL_TOK = 256
FFN_CHUNK = 1024
VMEM_LIMIT = 56 * 1024 * 1024


def _cparams(sem):
    return pltpu.CompilerParams(dimension_semantics=sem, vmem_limit_bytes=VMEM_LIMIT)


def _dot(a, b):
    return jnp.dot(a, b, preferred_element_type=F32)


def _dot_nt(a, b):
    return lax.dot_general(a, b, (((1,), (1,)), ((), ())), preferred_element_type=F32)


def _split(a):
    hi = a.astype(BF16)
    lo = (a - hi.astype(F32)).astype(BF16)
    return hi, lo


def _silu(x):
    return x * jax.nn.sigmoid(x)


def _lane_bcast(col, width):
    return jnp.broadcast_to(col, (col.shape[0], width))


def _ada_kernel(cond_ref, w_ref, b_ref, o_ref):
    a_hi, a_lo = _split(_silu(cond_ref[...]))
    w_hi, w_lo = _split(w_ref[0])
    o_ref[...] = _dot(a_hi, w_hi) + _dot(a_lo, w_hi) + _dot(a_hi, w_lo) + b_ref[...]


def _ada(cond, w, b):
    rows, d = cond.shape
    n = w.shape[2]
    bn = 1536
    return pl.pallas_call(
        _ada_kernel,
        grid=(n // bn,),
        in_specs=[pl.BlockSpec((rows, d), lambda j: (0, 0)),
                  pl.BlockSpec((1, d, bn), lambda j: (0, 0, j)),
                  pl.BlockSpec((1, bn), lambda j: (0, j))],
        out_specs=pl.BlockSpec((rows, bn), lambda j: (0, j)),
        out_shape=jax.ShapeDtypeStruct((rows, n), F32),
        compiler_params=_cparams(("parallel",)),
        name="ada",
    )(cond, w, b)


def _rms_mod(x, g, shift, scale):
    ms = jnp.mean(x * x, axis=-1, keepdims=True)
    return (x * lax.rsqrt(ms + NORM_EPS) * g) * (1.0 + scale) + shift


def _conv_tile(ext_ref, w_ref, o_ref):
    pad = (CONV_K - 1) // 2
    q_scale = HEAD_DIM ** -0.5
    for part in range(3):
        for hd in range(N_HEADS):
            lo = part * GROUP_W + hd * HEAD_DIM
            acc = None
            for i in range(CONV_K):
                r0 = HALO - pad + i
                term = ext_ref[r0:r0 + TILE, lo:lo + HEAD_DIM] * w_ref[0, i:i + 1, lo:lo + HEAD_DIM]
                acc = term if acc is None else acc + term
            y = _silu(acc)
            if part < 2:
                y = y * lax.rsqrt(jnp.sum(y * y, axis=-1, keepdims=True) + NORM_EPS)
            if part == 0:
                y = y * q_scale
            o_ref[0, :, lo:lo + HEAD_DIM] = y.astype(BF16)


def _inproj_kernel(x_ref, ctx_ref, mod_ref, g_ref, w_ref, cw_ref, cos_ref, sin_ref,
                   qkv_ref, z_ref, ab_ref, pr_ref, wa_ref, wab_ref, wr_ref, ext_ref,
                   *, d_model, n_half, n_gate, tiles_per_sample):
    step = pl.program_id(0)
    t = lax.rem(jnp.minimum(step, pl.num_programs(0) - 2), tiles_per_sample)
    t_conv = lax.rem(jnp.maximum(step - 1, 0), tiles_per_sample)
    width = 3 * GROUP_W

    @pl.when(step == 0)
    def _():
        wa_ref[...] = w_ref[0, :, :n_half].astype(BF16)
        wr_ref[...] = w_ref[0, :, n_half + n_gate:].astype(BF16)
        lane = lax.broadcasted_iota(jnp.int32, (d_model, GATE_LANES), 1)
        wab_ref[...] = jnp.where(lane < n_gate, w_ref[0, :, n_half:n_half + GATE_LANES], 0.0).astype(BF16)
        ext_ref[...] = jnp.zeros_like(ext_ref)

    xin = jnp.where(t == 0, ctx_ref[0], x_ref[0])
    mod = mod_ref[0]
    h = _rms_mod(xin, g_ref[...], mod[:, 0:d_model], mod[:, d_model:2 * d_model]).astype(BF16)
    pa = _dot(h, wa_ref[...])
    z_ref[0] = pa[:, width:].astype(BF16)

    has_next = jnp.logical_and(t_conv >= 1, t_conv < tiles_per_sample - 1)
    ext_ref[HALO + TILE:, :] = jnp.where(has_next, pa[:HALO, :width], 0.0)
    _conv_tile(ext_ref, cw_ref, qkv_ref)
    ext_ref[0:HALO, :] = jnp.where(t >= 2, ext_ref[TILE:HALO + TILE, :], 0.0)
    ext_ref[HALO:HALO + TILE, :] = pa[:, :width]

    ab_ref[0] = _dot(h, wab_ref[...])
    pr = _dot(h, wr_ref[...])
    cos2 = cos_ref[...]
    sin2 = sin_ref[...]
    k_scale = HEAD_DIM ** -0.5
    for hd in range(2 * N_HEADS):
        lo = hd * HEAD_DIM
        tt = pr[:, lo:lo + HEAD_DIM]
        rot = tt * cos2 + pltpu.roll(tt, HEAD_DIM // 2, 1) * sin2
        if hd >= N_HEADS:
            rot = rot * k_scale
        pr_ref[0, :, lo:lo + HEAD_DIM] = rot.astype(BF16)
    pr_ref[0, :, 2 * GROUP_W:] = pr[:, 2 * GROUP_W:].astype(BF16)


def _inproj(x, ctx, mod3, g, w_in, conv_w, n_half, n_gate, cos2, sin2):
    b, seq, d = x.shape
    n_tiles = (ctx.shape[1] + seq) // TILE
    t_total = n_tiles * TILE
    total = b * n_tiles
    width = 3 * GROUP_W
    assert w_in.shape[2] == 2 * n_half + n_gate and n_gate <= GATE_LANES and n_half == width + GROUP_W

    def now(f):
        flat = jnp.minimum(f, total - 1)
        return flat // n_tiles, flat % n_tiles

    def lagged(f):
        flat = jnp.maximum(f - 1, 0)
        return flat // n_tiles, flat % n_tiles

    def tok(cols, c=0):
        return pl.BlockSpec((1, TILE, cols), lambda f: (now(f)[0], now(f)[1], c))

    return pl.pallas_call(
        functools.partial(_inproj_kernel, d_model=d, n_half=n_half, n_gate=n_gate, tiles_per_sample=n_tiles),
        grid=(total + 1,),
        in_specs=[pl.BlockSpec((1, TILE, d), lambda f: (now(f)[0], jnp.maximum(now(f)[1] - 1, 0), 0)),
                  pl.BlockSpec((1, TILE, d), lambda f: (now(f)[0], 0, 0)),
                  pl.BlockSpec((1, 1, mod3.shape[2]), lambda f: (jnp.where(now(f)[1] == 0, b, now(f)[0]), 0, 0)),
                  pl.BlockSpec((1, d), lambda f: (0, 0)),
                  pl.BlockSpec(w_in.shape, lambda f: (0, 0, 0), pipeline_mode=pl.Buffered(1)),
                  pl.BlockSpec(conv_w.shape, lambda f: (0, 0, 0)),
                  pl.BlockSpec((TILE, HEAD_DIM), lambda f: (now(f)[1], 0)),
                  pl.BlockSpec((TILE, HEAD_DIM), lambda f: (now(f)[1], 0))],
        out_specs=[pl.BlockSpec((1, TILE, width), lambda f: (lagged(f)[0], lagged(f)[1], 0)),
                   tok(GROUP_W), tok(GATE_LANES), tok(n_half)],
        out_shape=[jax.ShapeDtypeStruct((b, t_total, width), BF16),
                   jax.ShapeDtypeStruct((b, t_total, GROUP_W), BF16),
                   jax.ShapeDtypeStruct((b, t_total, GATE_LANES), F32),
                   jax.ShapeDtypeStruct((b, t_total, n_half), BF16)],
        scratch_shapes=[pltpu.VMEM((d, n_half), BF16), pltpu.VMEM((d, GATE_LANES), BF16),
                        pltpu.VMEM((d, n_half), BF16), pltpu.VMEM((TILE + 2 * HALO, width), F32)],
        compiler_params=_cparams(("arbitrary",)),
        name="inproj",
    )(x, ctx, mod3, g, w_in, conv_w, cos2, sin2)


def _packed_matmul(x, y, bd_mask):
    yb = y.astype(BF16)
    y_bd = jnp.where(bd_mask, jnp.concatenate([yb] * N_HEADS, axis=0), jnp.zeros((), BF16))
    return _dot(x.astype(BF16), y_bd)


def _unit_lower_inverse_stages(a_list, eye, m16, m32, m64, bd_mask, out):
    pm = functools.partial(_packed_matmul, bd_mask=bd_mask)
    ps = [-jnp.where(m16, a, 0.0) for a in a_list]
    xs = [eye + p for p in ps]
    for _ in range(3):
        ps = [pm(p, p) for p in ps]
        yield
        xs = [x + pm(x, p) for x, p in zip(xs, ps)]
        yield
    for m in (m32, m64):
        zs = [pm(jnp.where(m, a, 0.0), x) for a, x in zip(a_list, xs)]
        yield
        xs = [x - pm(x, z) for x, z in zip(xs, zs)]
        yield
    out.extend(xs)


def _gdn_front_stages(groups, ring_base, mid_base, a_scr, rhs_scr, brow_scr, wq_scr, qkd_scr, kdt_scr,
                      el_scr):
    lane = lax.broadcasted_iota(jnp.int32, (CHUNK, GATE_LANES), 1)
    low_half = lax.broadcasted_iota(jnp.int32, (CHUNK, HEAD_DIM), 1) < CHUNK
    zero_k = jnp.zeros((CHUNK, HEAD_DIM), BF16)
    for g in groups:
        tcum = g["consts"][0]
        g_hi, g_lo = _split(g["gates"])
        g["gc"] = _dot(tcum, g_hi) + _dot(tcum, g_lo)
    yield

    for gi, g in enumerate(groups):
        g_lane0 = N_HEADS if g["rev"] else 0
        b_lane0 = 2 * N_HEADS + g_lane0
        last = 0 if g["rev"] else CHUNK - 1
        gc, gates = g["gc"], g["gates"]
        ct = jnp.where(lane < 2 * N_HEADS, gc, gates).T
        egl_t = jnp.exp(ct[:, last:last + 1] - ct)
        gcb, kb, ms = [], [], []
        for h in range(N_HEADS):
            lo = h * HEAD_DIM
            gl = g_lane0 + h
            ring_unit = (ring_base + gi) * N_HEADS + h
            gcb.append(_lane_bcast(gc[:, gl:gl + 1], HEAD_DIM))
            beta = _lane_bcast(gates[:, b_lane0 + h:b_lane0 + h + 1], HEAD_DIM)
            eg_h = jnp.exp(gcb[h])
            q_h = g["q"][:, lo:lo + HEAD_DIM]
            k_h = g["k"][:, lo:lo + HEAD_DIM].astype(F32)
            kb.append((k_h * beta).astype(BF16))
            rhs_scr[(mid_base + gi) * N_HEADS + h] = jnp.concatenate(
                [g["v"][:, lo:lo + HEAD_DIM], (k_h * eg_h).astype(BF16)], axis=1)
            wq_scr[ring_unit, CHUNK:, :] = (q_h * eg_h).astype(BF16)
            kdt_scr[ring_unit] = (k_h.T * egl_t[gl:gl + 1, :]).astype(BF16)
        d_parts = []
        for p in range(N_HEADS // 2):
            h0, h1 = 2 * p, 2 * p + 1
            lo = h0 * HEAD_DIM
            lhs = jnp.concatenate([g["q"][:, lo:lo + 2 * HEAD_DIM], jnp.concatenate([kb[h0], kb[h1]], axis=1)],
                                  axis=0)
            k0 = g["k"][:, lo:lo + HEAD_DIM]
            k1 = g["k"][:, lo + HEAD_DIM:lo + 2 * HEAD_DIM]
            rhs_bd = jnp.concatenate([jnp.concatenate([k0, zero_k], axis=1),
                                      jnp.concatenate([zero_k, k1], axis=1)], axis=0)
            ms.append(_dot_nt(lhs, rhs_bd))
            g_row = jnp.concatenate([ct[g_lane0 + h0:g_lane0 + h0 + 1, :], ct[g_lane0 + h1:g_lane0 + h1 + 1, :]],
                                    axis=1)
            d_parts.append(jnp.where(low_half, gcb[h0], gcb[h1]) - g_row)
        incl4, strict4 = g["consts"][2:]
        e = jnp.exp(jnp.where(incl4, jnp.concatenate(d_parts, axis=1), 0.0))
        a_scr[mid_base + gi] = jnp.where(strict4, jnp.concatenate([m[CHUNK:] for m in ms], axis=1) * e, 0.0)
        qkd_scr[ring_base + gi] = jnp.where(incl4, jnp.concatenate([m[:CHUNK] for m in ms], axis=1) * e,
                                            0.0).astype(BF16)
        el_scr[ring_base + gi] = jnp.exp(gc[last:last + 1, :])
        brow_scr[mid_base + gi] = jnp.concatenate(
            [ct[b_lane0 + h:b_lane0 + h + 1, :] for h in range(N_HEADS)], axis=1)
        yield


def _gdn_solve_stages(n_groups, masks, ring_base, mid_base, a_scr, rhs_scr, brow_scr, u_scr, wq_scr):
    eye4, m16, m32, m64, bd_mask = masks
    t_invs = []
    yield from _unit_lower_inverse_stages([a_scr[mid_base + gi] for gi in range(n_groups)],
                                          eye4, m16, m32, m64, bd_mask, t_invs)
    for gi, t_inv in enumerate(t_invs):
        t_b = (t_inv * brow_scr[mid_base + gi]).astype(BF16)
        for h in range(N_HEADS):
            sol = _dot(t_b[:, h * CHUNK:(h + 1) * CHUNK], rhs_scr[(mid_base + gi) * N_HEADS + h])
            ring_unit = (ring_base + gi) * N_HEADS + h
            u_scr[ring_unit] = sol[:, :HEAD_DIM]
            wq_scr[ring_unit, :CHUNK, :] = sol[:, HEAD_DIM:].astype(BF16)


def _gdn_scan_stages(slot_base, keep_state, u_scr, wq_scr, qkd_scr, kdt_scr, el_scr, s_ref, of_ref, ob_ref):
    for j in range(CHUNKS_PER_TILE):
        units = [(slot_base + 2 * j + d, h, d * N_HEADS + h) for d in range(2) for h in range(N_HEADS)]
        states = [jnp.where(keep_state, s_ref[slot], 0.0) if j == 0 else s_ref[slot] for _, _, slot in units]
        wss = [_dot(wq_scr[gi * N_HEADS + h], s.astype(BF16)) for (gi, h, _), s in zip(units, states)]
        yield
        v_news = [(u_scr[gi * N_HEADS + h] - ws[:CHUNK]).astype(BF16) for (gi, h, _), ws in zip(units, wss)]
        outs = [ws[CHUNK:] + _dot(qkd_scr[gi][:, h * CHUNK:(h + 1) * CHUNK], vn)
                for (gi, h, _), ws, vn in zip(units, wss, v_news)]
        for (gi, h, slot), s, vn in zip(units, states, v_news):
            s_ref[slot] = el_scr[gi][:, slot:slot + 1] * s + _dot(kdt_scr[gi * N_HEADS + h], vn)
        rf = j * CHUNK
        rb = (CHUNKS_PER_TILE - 1 - j) * CHUNK
        of_ref[0, rf:rf + CHUNK, :] = jnp.concatenate(outs[:N_HEADS], axis=1).astype(BF16)
        ob_ref[0, rb:rb + CHUNK, :] = jnp.concatenate(outs[N_HEADS:], axis=1).astype(BF16)
        yield


def _interleave(*stage_generators):
    live = list(stage_generators)
    while live:
        for gen in list(live):
            try:
                next(gen)
            except StopIteration:
                live.remove(gen)


def _gate_tile(ab, dtb, a_log):
    z = ab + dtb
    softplus = jnp.maximum(z, 0.0) + jnp.log(1.0 + jnp.exp(-jnp.abs(z)))
    lane = lax.broadcasted_iota(jnp.int32, ab.shape, 1)
    return jnp.where(lane < 2 * N_HEADS, -jnp.exp(a_log) * softplus, jax.nn.sigmoid(ab))


def _gdn_kernel(qf_ref, kf_ref, vf_ref, abf_ref, qb_ref, kb_ref, vb_ref, abb_ref, dtb_ref, alog_ref,
                of_ref, ob_ref, s_ref, a_scr, rhs_scr, brow_scr, u_scr, wq_scr, qkd_scr, kdt_scr, el_scr, *,
                tiles_per_sample):
    step = pl.program_id(0)
    keep_state = lax.rem(step + tiles_per_sample - 2, tiles_per_sample) != 0
    groups_per_tile = 2 * CHUNKS_PER_TILE
    front_ring = lax.rem(step, 3) * groups_per_tile
    solve_ring = lax.rem(step + 2, 3) * groups_per_tile
    scan_ring = lax.rem(step + 1, 3) * groups_per_tile
    front_mid = lax.rem(step, 2) * groups_per_tile
    solve_mid = groups_per_tile - front_mid

    @pl.when(step == 0)
    def _():
        for scr in (s_ref, a_scr, rhs_scr, brow_scr, u_scr, wq_scr, qkd_scr, kdt_scr, el_scr):
            scr[...] = jnp.zeros_like(scr)

    ri = lax.broadcasted_iota(jnp.int32, (CHUNK, CHUNK), 0)
    ci = lax.broadcasted_iota(jnp.int32, (CHUNK, CHUNK), 1)
    ri4 = lax.broadcasted_iota(jnp.int32, (CHUNK, N_HEADS * CHUNK), 0)
    ci4 = lax.broadcasted_iota(jnp.int32, (CHUNK, N_HEADS * CHUNK), 1) % CHUNK
    rbd = lax.broadcasted_iota(jnp.int32, (N_HEADS * CHUNK, N_HEADS * CHUNK), 0) // CHUNK
    cbd = lax.broadcasted_iota(jnp.int32, (N_HEADS * CHUNK, N_HEADS * CHUNK), 1) // CHUNK
    bd_mask = rbd == cbd
    eye4 = (ri4 == ci4).astype(F32)
    m16 = (ri4 // 16) == (ci4 // 16)
    m32 = jnp.logical_and((ri4 // 32) == (ci4 // 32), jnp.logical_not(m16))
    m64 = (ri4 // 32) != (ci4 // 32)

    cf = ((ci <= ri).astype(BF16), None, ri4 >= ci4, ri4 > ci4)
    cb = ((ci >= ri).astype(BF16), None, ri4 <= ci4, ri4 < ci4)
    dtb = dtb_ref[...]
    alog = alog_ref[...]

    groups = []
    for j in range(CHUNKS_PER_TILE):
        rf = j * CHUNK
        rb = (CHUNKS_PER_TILE - 1 - j) * CHUNK
        groups.append(dict(q=qf_ref[0, rf:rf + CHUNK, :], k=kf_ref[0, rf:rf + CHUNK, :],
                           v=vf_ref[0, rf:rf + CHUNK, :],
                           gates=_gate_tile(abf_ref[0, rf:rf + CHUNK, :], dtb, alog), rev=False, consts=cf))
        groups.append(dict(q=qb_ref[0, rb:rb + CHUNK, :], k=kb_ref[0, rb:rb + CHUNK, :],
                           v=vb_ref[0, rb:rb + CHUNK, :],
                           gates=_gate_tile(abb_ref[0, rb:rb + CHUNK, :], dtb, alog), rev=True, consts=cb))
    _interleave(
        _gdn_solve_stages(len(groups), (eye4, m16, m32, m64, bd_mask), solve_ring, solve_mid,
                          a_scr, rhs_scr, brow_scr, u_scr, wq_scr),
        _gdn_scan_stages(scan_ring, keep_state, u_scr, wq_scr, qkd_scr, kdt_scr, el_scr, s_ref, of_ref, ob_ref),
        _gdn_front_stages(groups, front_ring, front_mid, a_scr, rhs_scr, brow_scr, wq_scr, qkd_scr, kdt_scr,
                          el_scr))


def _bwd_tile(s, n_tiles):
    return jnp.where(s == 0, 0, n_tiles - s)


def _gdn(qkv, ab, dtb_row, alog_row):
    b, t_total, _ = qkv.shape
    n_tiles = t_total // TILE

    total = b * n_tiles

    def block_of(f, bwd, lag, c):
        flat = jnp.maximum(f - 2, 0) if lag else jnp.minimum(f, total - 1)
        pos = flat % n_tiles
        return (flat // n_tiles, _bwd_tile(pos, n_tiles) if bwd else pos, c)

    def col(c, bwd, lag=False):
        return pl.BlockSpec((1, TILE, GROUP_W), lambda f: block_of(f, bwd, lag, c))

    def gate(bwd):
        return pl.BlockSpec((1, TILE, GATE_LANES), lambda f: block_of(f, bwd, False, 0))

    row = pl.BlockSpec((1, GATE_LANES), lambda f: (0, 0))
    per_tile = 2 * CHUNKS_PER_TILE
    n_groups = 3 * per_tile
    n_units = n_groups * N_HEADS
    n_mid = 2 * per_tile
    return pl.pallas_call(
        functools.partial(_gdn_kernel, tiles_per_sample=n_tiles),
        grid=(total + 2,),
        in_specs=[col(0, False), col(1, False), col(2, False), gate(False),
                  col(0, True), col(1, True), col(2, True), gate(True), row, row],
        out_specs=[col(0, False, lag=True), col(0, True, lag=True)],
        out_shape=[jax.ShapeDtypeStruct((b, t_total, GROUP_W), BF16)] * 2,
        scratch_shapes=[pltpu.VMEM((2 * N_HEADS, HEAD_DIM, HEAD_DIM), F32),
                        pltpu.VMEM((n_mid, CHUNK, N_HEADS * CHUNK), F32),
                        pltpu.VMEM((n_mid * N_HEADS, CHUNK, 2 * HEAD_DIM), BF16),
                        pltpu.VMEM((n_mid, 1, N_HEADS * CHUNK), F32),
                        pltpu.VMEM((n_units, CHUNK, HEAD_DIM), F32),
                        pltpu.VMEM((n_units, 2 * CHUNK, HEAD_DIM), BF16),
                        pltpu.VMEM((n_groups, CHUNK, N_HEADS * CHUNK), BF16),
                        pltpu.VMEM((n_units, HEAD_DIM, CHUNK), BF16),
                        pltpu.VMEM((n_groups, 1, GATE_LANES), F32)],
        compiler_params=_cparams(("arbitrary",)),
        name="gdn",
    )(qkv, qkv, qkv, ab, qkv, qkv, qkv, ab, dtb_row, alog_row)


def _ret_kernel(qf_ref, kf_ref, vf_ref, qb_ref, kb_ref, vb_ref, logit_ref, of_ref, ob_ref,
                r_ref, decay_ref, xi_ref, zeta_ref):
    x = logit_ref[...]
    lg_row = jnp.minimum(x, 0.0) - jnp.log(1.0 + jnp.exp(-jnp.abs(x)))

    @pl.when(pl.program_id(1) == 0)
    def _():
        r_ref[...] = jnp.zeros_like(r_ref)
        ri = lax.broadcasted_iota(jnp.int32, (TILE, TILE), 0)
        ci = lax.broadcasted_iota(jnp.int32, (TILE, TILE), 1)
        row = lax.broadcasted_iota(jnp.int32, (TILE, HEAD_DIM), 0).astype(F32)
        for slot in range(2 * N_HEADS):
            rev = slot >= N_HEADS
            lg = lg_row[:, slot:slot + 1]
            diff = ((ci - ri) if rev else (ri - ci)).astype(F32)
            pos = ((TILE - 1.0) - row) if rev else row
            decay_ref[slot] = jnp.where(diff >= 0, jnp.exp(lg * jnp.maximum(diff, 0.0)), 0.0)
            xi_ref[slot] = jnp.exp(lg * (pos + 1.0))
            zeta_ref[slot] = jnp.exp(lg * ((TILE - 1.0) - pos))

    units = [(refs, (N_HEADS if rev else 0) + h, h * HEAD_DIM)
             for refs, rev in (((qf_ref, kf_ref, vf_ref, of_ref), False), ((qb_ref, kb_ref, vb_ref, ob_ref), True))
             for h in range(N_HEADS)]
    qs = [refs[0][0, :, lo:lo + HEAD_DIM] for refs, _, lo in units]
    ks = [refs[1][0, :, lo:lo + HEAD_DIM] for refs, _, lo in units]
    vs = [refs[2][0, :, lo:lo + HEAD_DIM] for refs, _, lo in units]
    qks = [_dot_nt(q, k) for q, k in zip(qs, ks)]
    qkds = [(qk * decay_ref[slot]).astype(BF16) for qk, (_, slot, _) in zip(qks, units)]
    rs = [r_ref[slot] for _, slot, _ in units]
    for (refs, slot, lo), q, v, qkd, r in zip(units, qs, vs, qkds, rs):
        refs[3][0, :, lo:lo + HEAD_DIM] = (
            _dot(qkd, v) + _dot((q * xi_ref[slot]).astype(BF16), r.astype(BF16))).astype(BF16)
    for (_, slot, _), k, v, r in zip(units, ks, vs, rs):
        g_chunk = jnp.exp(lg_row[:, slot:slot + 1] * float(TILE))
        r_ref[slot] = g_chunk * r + _dot((k * zeta_ref[slot]).T.astype(BF16), v)


def _ret(pr, logit_row):
    b, t_total, _ = pr.shape
    n_tiles = t_total // TILE

    def col(c, bwd):
        if bwd:
            return pl.BlockSpec((1, TILE, GROUP_W), lambda i, s: (i, _bwd_tile(s, n_tiles), c))
        return pl.BlockSpec((1, TILE, GROUP_W), lambda i, s: (i, s, c))

    return pl.pallas_call(
        _ret_kernel,
        grid=(b, n_tiles),
        in_specs=[col(0, False), col(1, False), col(2, False), col(0, True), col(1, True), col(2, True),
                  pl.BlockSpec((1, GATE_LANES), lambda i, s: (0, 0))],
        out_specs=[col(0, False), col(0, True)],
        out_shape=[jax.ShapeDtypeStruct((b, t_total, GROUP_W), BF16)] * 2,
        scratch_shapes=[pltpu.VMEM((2 * N_HEADS, HEAD_DIM, HEAD_DIM), F32),
                        pltpu.VMEM((2 * N_HEADS, TILE, TILE), F32),
                        pltpu.VMEM((2 * N_HEADS, TILE, HEAD_DIM), F32),
                        pltpu.VMEM((2 * N_HEADS, TILE, HEAD_DIM), F32)],
        compiler_params=_cparams(("parallel", "arbitrary")),
        name="ret",
    )(pr, pr, pr, pr, pr, pr, logit_row)


def _tail_kernel(gf_ref, gb_ref, z_ref, rf_ref, rb_ref, rg_ref, x_ref, mod_ref, gdn_g_ref, ret_g_ref,
                 nffn_g_ref, w_ref, wi_ref, wo_ref, fg_ref, o_ref, y_ref, *, d_model, d_ff):
    o = gf_ref[0].astype(F32) + gb_ref[0].astype(F32)
    r = rf_ref[0].astype(F32) + rb_ref[0].astype(F32)
    z = z_ref[0].astype(F32)
    rg = rg_ref[0].astype(F32)
    for h in range(N_HEADS):
        lo = h * HEAD_DIM
        o_h = o[:, lo:lo + HEAD_DIM]
        o_n = o_h * lax.rsqrt(jnp.mean(o_h * o_h, axis=-1, keepdims=True) + NORM_EPS) * gdn_g_ref[...]
        y_ref[:, lo:lo + HEAD_DIM] = (o_n * _silu(z[:, lo:lo + HEAD_DIM])).astype(BF16)
        r_h = r[:, lo:lo + HEAD_DIM]
        mu = jnp.mean(r_h, axis=-1, keepdims=True)
        cen = r_h - mu
        var = jnp.mean(cen * cen, axis=-1, keepdims=True)
        r_n = cen * lax.rsqrt(var + NORM_EPS) * ret_g_ref[...]
        y_ref[:, GROUP_W + lo:GROUP_W + lo + HEAD_DIM] = (r_n * _silu(rg[:, lo:lo + HEAD_DIM])).astype(BF16)
    mod = mod_ref[0]
    x1 = x_ref[0] + mod[:, 2 * d_model:3 * d_model] * _dot(y_ref[...], w_ref[...])
    h2 = _rms_mod(x1, nffn_g_ref[...], mod[:, 3 * d_model:4 * d_model],
                  mod[:, 4 * d_model:5 * d_model]).astype(BF16)
    acc = None
    for lo in range(0, d_ff, FFN_CHUNK):
        hi = min(lo + FFN_CHUNK, d_ff)
        gate = _dot(h2, wi_ref[:, lo:hi])
        up = _dot(h2, wi_ref[:, d_ff + lo:d_ff + hi])
        part = _dot((_silu(gate) * up).astype(BF16), wo_ref[lo:hi, :])
        acc = part if acc is None else acc + part
    x2 = x1 + mod[:, 5 * d_model:6 * d_model] * acc
    ms = jnp.mean(x2 * x2, axis=-1, keepdims=True)
    o_ref[0] = x2 * lax.rsqrt(ms + NORM_EPS) * fg_ref[...]


def _tail(gdn_f, gdn_b, z, ret_f, ret_b, pr, x, mod3, gdn_g, ret_g, nffn_g, w_out, w_ffn_in, w_ffn_out,
          final_g):
    b, seq, d = x.shape
    d_ff = w_ffn_out.shape[0]
    first = (z.shape[1] - seq) // TAIL_TOK

    def tok(c):
        return pl.BlockSpec((1, TAIL_TOK, GROUP_W), lambda i, t: (i, t + first, c))

    def resident(shape):
        return pl.BlockSpec(shape, lambda i, t: (0,) * len(shape), pipeline_mode=pl.Buffered(1))

    return pl.pallas_call(
        functools.partial(_tail_kernel, d_model=d, d_ff=d_ff),
        grid=(b, seq // TAIL_TOK),
        in_specs=[tok(0), tok(0), tok(0), tok(0), tok(0), tok(3),
                  pl.BlockSpec((1, TAIL_TOK, d), lambda i, t: (i, t, 0)),
                  pl.BlockSpec((1, 1, mod3.shape[2]), lambda i, t: (i, 0, 0)),
                  resident((1, HEAD_DIM)), resident((1, HEAD_DIM)), resident((1, d)),
                  resident(w_out.shape), resident(w_ffn_in.shape), resident(w_ffn_out.shape),
                  resident((1, d))],
        out_specs=pl.BlockSpec((1, TAIL_TOK, d), lambda i, t: (i, t, 0)),
        out_shape=jax.ShapeDtypeStruct((b, seq, d), F32),
        scratch_shapes=[pltpu.VMEM((TAIL_TOK, 2 * GROUP_W), BF16)],
        compiler_params=_cparams(("parallel", "parallel")),
        name="tail",
    )(gdn_f, gdn_b, z, ret_f, ret_b, pr, x, mod3, gdn_g, ret_g, nffn_g, w_out, w_ffn_in, w_ffn_out, final_g)


def _rope_tables(ctx_len, n_lat):
    def angles(pos, n_pairs):
        inv = ROPE_THETA ** (-np.arange(n_pairs, dtype=np.float64) / n_pairs)
        return pos[:, None] * inv[None, :]

    rows = n_lat // GRID_W
    row = np.repeat(np.arange(rows, dtype=np.float64), GRID_W)
    col = np.tile(np.arange(GRID_W, dtype=np.float64), rows)
    zeros = np.zeros((ctx_len,), np.float64)
    p_seq = np.concatenate([np.arange(ctx_len, dtype=np.float64), np.full((n_lat,), float(ctx_len))])
    ang = np.concatenate([angles(p_seq, ROPE_PAIRS[0]),
                          angles(np.concatenate([zeros, row]), ROPE_PAIRS[1]),
                          angles(np.concatenate([zeros, col]), ROPE_PAIRS[2])], axis=-1)
    cos, sin = np.cos(ang).astype(np.float32), np.sin(ang).astype(np.float32)
    return jnp.asarray(np.concatenate([cos, cos], axis=-1)), jnp.asarray(np.concatenate([-sin, sin], axis=-1))


def _lane_row(values):
    flat = values.reshape(1, -1).astype(F32)
    return jnp.pad(flat, ((0, 0), (0, GATE_LANES - flat.shape[1])))


def kernel(x, c, ctx, c_ctx, ada_w, ada_b, norm_mix_g, norm_ffn_g, w_in, conv_w, gdn_a_log, gdn_dt_bias,
           gdn_norm_g, ret_decay_logit, ret_norm_g, w_out, w_ffn_in, w_ffn_out, final_g):
    assert ada_w.shape[0] == 1, "single-layer block"
    b, seq, d = x.shape
    ctx_len = ctx.shape[1]
    assert ctx_len == TILE and seq % TILE == 0 and TILE % TAIL_TOK == 0 and b + 1 <= 8

    cond = jnp.concatenate([c, c_ctx[None, :], jnp.zeros((8 - b - 1, d), F32)], axis=0)
    mod = _ada(cond, ada_w, ada_b)
    mod3 = mod[:, None, :]

    n_qkvz = 4 * GROUP_W
    n_gate = 4 * N_HEADS
    cos2, sin2 = _rope_tables(ctx_len, seq)
    qkv, z, ab, pr = _inproj(x, ctx, mod3, norm_mix_g, w_in, conv_w, n_qkvz, n_gate, cos2, sin2)

    dtb_row = _lane_row(gdn_dt_bias[0])
    gdn_f, gdn_b = _gdn(qkv, ab, dtb_row, _lane_row(gdn_a_log[0]))
    ret_f, ret_b = _ret(pr, _lane_row(ret_decay_logit[0]))

    return _tail(gdn_f, gdn_b, z, ret_f, ret_b, pr, x, mod3, gdn_norm_g, ret_norm_g, norm_ffn_g,
                 w_out[0].astype(BF16), w_ffn_in[0].astype(BF16), w_ffn_out[0].astype(BF16), final_g[None, :])
```

```python
import functools
import math

import jax
import jax.numpy as jnp
import numpy as np
from jax import lax
from jax.experimental import pallas as pl
from jax.experimental.pallas import tpu as pltpu

F32 = jnp.float32
BF16 = jnp.bfloat16

HEAD_DIM = 128
N_HEADS = 4
GROUP_W = N_HEADS * HEAD_DIM
CONV_K = 5
GRID_W = 64
ROPE_THETA = 10000.0
ROPE_PAIRS = (16, 24, 24)
NORM_EPS = 1e-6

CHUNK = 64
TILE = 256
CHUNKS_PER_TILE = TILE // CHUNK
HALO = 16
GATE_LANES = 128
TAIL_TOK = 256
FFN_CHUNK = 1024
VMEM_LIMIT = 56 * 1024 * 1024


def _cparams(sem):
    return pltpu.CompilerParams(dimension_semantics=sem, vmem_limit_bytes=VMEM_LIMIT)


def _dot(a, b):
    return jnp.dot(a, b, preferred_element_type=F32)


def _dot_nt(a, b):
    return lax.dot_general(a, b, (((1,), (1,)), ((), ())), preferred_element_type=F32)


def _split(a):
    hi = a.astype(BF16)
    lo = (a - hi.astype(F32)).astype(BF16)
    return hi, lo


def _silu(x):
    return x * jax.nn.sigmoid(x)


def _lane_bcast(col, width):
    return jnp.broadcast_to(col, (col.shape[0], width))


def _ada_kernel(cond_ref, w_ref, b_ref, o_ref):
    a_hi, a_lo = _split(_silu(cond_ref[...]))
    w_hi, w_lo = _split(w_ref[0])
    o_ref[...] = _dot(a_hi, w_hi) + _dot(a_lo, w_hi) + _dot(a_hi, w_lo) + b_ref[...]


def _ada(cond, w, b):
    rows, d = cond.shape
    n = w.shape[2]
    bn = 1536
    return pl.pallas_call(
        _ada_kernel,
        grid=(n // bn,),
        in_specs=[pl.BlockSpec((rows, d), lambda j: (0, 0)),
                  pl.BlockSpec((1, d, bn), lambda j: (0, 0, j)),
                  pl.BlockSpec((1, bn), lambda j: (0, j))],
        out_specs=pl.BlockSpec((rows, bn), lambda j: (0, j)),
        out_shape=jax.ShapeDtypeStruct((rows, n), F32),
        compiler_params=_cparams(("parallel",)),
        name="ada",
    )(cond, w, b)


def _rms_mod(x, g, shift, scale):
    ms = jnp.mean(x * x, axis=-1, keepdims=True)
    return (x * lax.rsqrt(ms + NORM_EPS) * g) * (1.0 + scale) + shift


def _conv_tile(ext_ref, w_ref, o_ref):
    pad = (CONV_K - 1) // 2
    rows = TILE + 2 * HALO
    q_scale = HEAD_DIM ** -0.5
    for part in range(3):
        for hd in range(N_HEADS):
            lo = part * GROUP_W + hd * HEAD_DIM
            slab = ext_ref[:, lo:lo + HEAD_DIM]
            acc = None
            for i in range(CONV_K):
                shifted = slab if i == pad else pltpu.roll(slab, (pad - i) % rows, 0)
                term = shifted[HALO:HALO + TILE] * w_ref[0, i:i + 1, lo:lo + HEAD_DIM]
                acc = term if acc is None else acc + term
            y = _silu(acc)
            if part < 2:
                y = y * lax.rsqrt(jnp.sum(y * y, axis=-1, keepdims=True) + NORM_EPS)
            if part == 0:
                y = y * q_scale
            o_ref[0, :, lo:lo + HEAD_DIM] = y.astype(BF16)


def _inproj_kernel(x_ref, ctx_ref, mod_ref, g_ref, w_ref, cw_ref, rope_ref,
                   qkv_ref, z_ref, ab_ref, pr_ref, wa_ref, wab_ref, wr_ref, ext_ref,
                   *, d_model, n_half, n_gate, tiles_per_sample):
    step = pl.program_id(0)
    t = lax.rem(jnp.minimum(step, pl.num_programs(0) - 2), tiles_per_sample)
    t_conv = lax.rem(jnp.maximum(step - 1, 0), tiles_per_sample)
    width = 3 * GROUP_W

    @pl.when(step == 0)
    def _():
        wa_ref[...] = w_ref[0, :, :n_half].astype(BF16)
        wr_ref[...] = w_ref[0, :, n_half + n_gate:].astype(BF16)
        lane = lax.broadcasted_iota(jnp.int32, (d_model, GATE_LANES), 1)
        wab_ref[...] = jnp.where(lane < n_gate, w_ref[0, :, n_half:n_half + GATE_LANES], 0.0).astype(BF16)
        ext_ref[...] = jnp.zeros_like(ext_ref)

    xin = jnp.where(t == 0, ctx_ref[0], x_ref[0])
    mod = mod_ref[0]
    h = _rms_mod(xin, g_ref[...], mod[:, 0:d_model], mod[:, d_model:2 * d_model]).astype(BF16)
    pa = _dot(h, wa_ref[...])
    z_ref[0] = pa[:, width:].astype(BF16)

    has_next = jnp.logical_and(t_conv >= 1, t_conv < tiles_per_sample - 1)
    ext_ref[HALO + TILE:, :] = jnp.where(has_next, pa[:HALO, :width], 0.0)
    _conv_tile(ext_ref, cw_ref, qkv_ref)
    ext_ref[0:HALO, :] = jnp.where(t >= 2, ext_ref[TILE:HALO + TILE, :], 0.0)
    ext_ref[HALO:HALO + TILE, :] = pa[:, :width]

    ab_ref[0] = _dot(h, wab_ref[...])
    pr = _dot(h, wr_ref[...])
    cos2 = rope_ref[:, :HEAD_DIM]
    sin2 = rope_ref[:, HEAD_DIM:]
    k_scale = HEAD_DIM ** -0.5
    for hd in range(2 * N_HEADS):
        lo = hd * HEAD_DIM
        tt = pr[:, lo:lo + HEAD_DIM]
        rot = tt * cos2 + pltpu.roll(tt, HEAD_DIM // 2, 1) * sin2
        if hd >= N_HEADS:
            rot = rot * k_scale
        pr_ref[0, :, lo:lo + HEAD_DIM] = rot.astype(BF16)
    pr_ref[0, :, 2 * GROUP_W:] = pr[:, 2 * GROUP_W:].astype(BF16)


def _inproj(x, ctx, mod3, g, w_in, conv_w, n_half, n_gate, rope):
    b, seq, d = x.shape
    n_tiles = (ctx.shape[1] + seq) // TILE
    t_total = n_tiles * TILE
    total = b * n_tiles
    width = 3 * GROUP_W
    assert w_in.shape[2] == 2 * n_half + n_gate and n_gate <= GATE_LANES and n_half == width + GROUP_W

    def now(f):
        flat = jnp.minimum(f, total - 1)
        return flat // n_tiles, flat % n_tiles

    def lagged(f):
        flat = jnp.maximum(f - 1, 0)
        return flat // n_tiles, flat % n_tiles

    def tok(cols, c=0):
        return pl.BlockSpec((1, TILE, cols), lambda f: (now(f)[0], now(f)[1], c))

    return pl.pallas_call(
        functools.partial(_inproj_kernel, d_model=d, n_half=n_half, n_gate=n_gate, tiles_per_sample=n_tiles),
        grid=(total + 1,),
        in_specs=[pl.BlockSpec((1, TILE, d), lambda f: (now(f)[0], jnp.maximum(now(f)[1] - 1, 0), 0)),
                  pl.BlockSpec((1, TILE, d), lambda f: (now(f)[0], 0, 0)),
                  pl.BlockSpec((1, 1, mod3.shape[2]), lambda f: (jnp.where(now(f)[1] == 0, b, now(f)[0]), 0, 0)),
                  pl.BlockSpec((1, d), lambda f: (0, 0)),
                  pl.BlockSpec(w_in.shape, lambda f: (0, 0, 0), pipeline_mode=pl.Buffered(1)),
                  pl.BlockSpec(conv_w.shape, lambda f: (0, 0, 0)),
                  pl.BlockSpec((TILE, 2 * HEAD_DIM), lambda f: (now(f)[1], 0))],
        out_specs=[pl.BlockSpec((1, TILE, width), lambda f: (lagged(f)[0], lagged(f)[1], 0)),
                   tok(GROUP_W), tok(GATE_LANES), tok(n_half)],
        out_shape=[jax.ShapeDtypeStruct((b, t_total, width), BF16),
                   jax.ShapeDtypeStruct((b, t_total, GROUP_W), BF16),
                   jax.ShapeDtypeStruct((b, t_total, GATE_LANES), F32),
                   jax.ShapeDtypeStruct((b, t_total, n_half), BF16)],
        scratch_shapes=[pltpu.VMEM((d, n_half), BF16), pltpu.VMEM((d, GATE_LANES), BF16),
                        pltpu.VMEM((d, n_half), BF16), pltpu.VMEM((TILE + 2 * HALO, width), F32)],
        compiler_params=_cparams(("arbitrary",)),
        name="inproj",
    )(x, ctx, mod3, g, w_in, conv_w, rope)


def _block_diag(y, bd_mask):
    yb = y.astype(BF16)
    return jnp.where(bd_mask, jnp.concatenate([yb] * N_HEADS, axis=0), jnp.zeros((), BF16))


def _packed_matmul(x, y_bd):
    return _dot(x.astype(BF16), y_bd)


def _unit_lower_inverse_stages(a_list, eye, m16, m32, m64, bd_mask, out):
    pm = _packed_matmul
    ps = [-jnp.where(m16, a, 0.0) for a in a_list]
    xs = [eye + p for p in ps]
    ps = [p.astype(BF16) for p in ps]
    bds = [_block_diag(p, bd_mask) for p in ps]
    for _ in range(3):
        ps = [pm(p, bd).astype(BF16) for p, bd in zip(ps, bds)]
        yield
        bds = [_block_diag(p, bd_mask) for p in ps]
        xs = [x + pm(x, bd) for x, bd in zip(xs, bds)]
        yield
    for m in (m32, m64):
        zs = [pm(jnp.where(m, a, 0.0), _block_diag(x, bd_mask)).astype(BF16) for a, x in zip(a_list, xs)]
        yield
        xs = [x - pm(x, _block_diag(z, bd_mask)) for x, z in zip(xs, zs)]
        yield
    out.extend(xs)


def _gdn_front_stages(groups, ring_base, mid_base, a_scr, rhs_scr, brow_scr, wq_scr, qkd_scr, kdt_scr,
                      el_scr):
    lane = lax.broadcasted_iota(jnp.int32, (CHUNK, GATE_LANES), 1)
    low_half = lax.broadcasted_iota(jnp.int32, (CHUNK, HEAD_DIM), 1) < CHUNK
    zero_k = jnp.zeros((CHUNK, HEAD_DIM), BF16)
    for g in groups:
        tcum = g["consts"][0]
        g_hi, g_lo = _split(g["gates"])
        g["gc"] = _dot(tcum, g_hi) + _dot(tcum, g_lo)
    yield

    for gi, g in enumerate(groups):
        g_lane0 = N_HEADS if g["rev"] else 0
        b_lane0 = 2 * N_HEADS + g_lane0
        last = 0 if g["rev"] else CHUNK - 1
        gc, gates = g["gc"], g["gates"]
        ct = jnp.where(lane < 2 * N_HEADS, gc, gates).T
        egl_t = jnp.exp(ct[:, last:last + 1] - ct)
        gcb, kb, ms = [], [], []
        for h in range(N_HEADS):
            lo = h * HEAD_DIM
            gl = g_lane0 + h
            ring_unit = (ring_base + gi) * N_HEADS + h
            gcb.append(_lane_bcast(gc[:, gl:gl + 1], HEAD_DIM))
            beta = _lane_bcast(gates[:, b_lane0 + h:b_lane0 + h + 1], HEAD_DIM)
            eg_h = jnp.exp(gcb[h])
            q_h = g["q"][:, lo:lo + HEAD_DIM]
            k_h = g["k"][:, lo:lo + HEAD_DIM].astype(F32)
            kb.append((k_h * beta).astype(BF16))
            rhs_scr[(mid_base + gi) * N_HEADS + h] = jnp.concatenate(
                [g["v"][:, lo:lo + HEAD_DIM], (k_h * eg_h).astype(BF16)], axis=1)
            wq_scr[ring_unit, CHUNK:, :] = (q_h * eg_h).astype(BF16)
            kdt_scr[ring_unit] = (k_h.T * egl_t[gl:gl + 1, :]).astype(BF16)
        d_parts = []
        for p in range(N_HEADS // 2):
            h0, h1 = 2 * p, 2 * p + 1
            lo = h0 * HEAD_DIM
            lhs = jnp.concatenate([g["q"][:, lo:lo + 2 * HEAD_DIM], jnp.concatenate([kb[h0], kb[h1]], axis=1)],
                                  axis=0)
            k0 = g["k"][:, lo:lo + HEAD_DIM]
            k1 = g["k"][:, lo + HEAD_DIM:lo + 2 * HEAD_DIM]
            rhs_bd = jnp.concatenate([jnp.concatenate([k0, zero_k], axis=1),
                                      jnp.concatenate([zero_k, k1], axis=1)], axis=0)
            ms.append(_dot_nt(lhs, rhs_bd))
            g_row = jnp.concatenate([ct[g_lane0 + h0:g_lane0 + h0 + 1, :], ct[g_lane0 + h1:g_lane0 + h1 + 1, :]],
                                    axis=1)
            d_parts.append(jnp.where(low_half, gcb[h0], gcb[h1]) - g_row)
        incl4, strict4 = g["consts"][2:]
        e = jnp.exp(jnp.where(incl4, jnp.concatenate(d_parts, axis=1), 0.0))
        a_scr[mid_base + gi] = jnp.where(strict4, jnp.concatenate([m[CHUNK:] for m in ms], axis=1) * e, 0.0)
        qkd_scr[ring_base + gi] = jnp.where(incl4, jnp.concatenate([m[:CHUNK] for m in ms], axis=1) * e,
                                            0.0).astype(BF16)
        el_scr[ring_base + gi] = jnp.exp(gc[last:last + 1, :])
        brow_scr[mid_base + gi] = jnp.concatenate(
            [ct[b_lane0 + h:b_lane0 + h + 1, :] for h in range(N_HEADS)], axis=1)
        yield


def _gdn_solve_stages(n_groups, masks, ring_base, mid_base, a_scr, rhs_scr, brow_scr, u_scr, wq_scr):
    eye4, m16, m32, m64, bd_mask = masks
    t_invs = []
    yield from _unit_lower_inverse_stages([a_scr[mid_base + gi] for gi in range(n_groups)],
                                          eye4, m16, m32, m64, bd_mask, t_invs)
    for gi, t_inv in enumerate(t_invs):
        t_b = (t_inv * brow_scr[mid_base + gi]).astype(BF16)
        for h in range(N_HEADS):
            sol = _dot(t_b[:, h * CHUNK:(h + 1) * CHUNK], rhs_scr[(mid_base + gi) * N_HEADS + h])
            ring_unit = (ring_base + gi) * N_HEADS + h
            u_scr[ring_unit] = sol[:, :HEAD_DIM]
            wq_scr[ring_unit, :CHUNK, :] = sol[:, HEAD_DIM:].astype(BF16)


def _gdn_scan_stages(slot_base, keep_state, u_scr, wq_scr, qkd_scr, kdt_scr, el_scr, s_ref, of_ref, ob_ref):
    for j in range(CHUNKS_PER_TILE):
        units = [(slot_base + 2 * j + d, h, d * N_HEADS + h) for d in range(2) for h in range(N_HEADS)]
        states = [jnp.where(keep_state, s_ref[slot], 0.0) if j == 0 else s_ref[slot] for _, _, slot in units]
        wss = [_dot(wq_scr[gi * N_HEADS + h], s.astype(BF16)) for (gi, h, _), s in zip(units, states)]
        yield
        v_news = [(u_scr[gi * N_HEADS + h] - ws[:CHUNK]).astype(BF16) for (gi, h, _), ws in zip(units, wss)]
        outs = [ws[CHUNK:] + _dot(qkd_scr[gi][:, h * CHUNK:(h + 1) * CHUNK], vn)
                for (gi, h, _), ws, vn in zip(units, wss, v_news)]
        for (gi, h, slot), s, vn in zip(units, states, v_news):
            s_ref[slot] = el_scr[gi][:, slot:slot + 1] * s + _dot(kdt_scr[gi * N_HEADS + h], vn)
        rf = j * CHUNK
        rb = (CHUNKS_PER_TILE - 1 - j) * CHUNK
        of_ref[0, rf:rf + CHUNK, :] = jnp.concatenate(outs[:N_HEADS], axis=1).astype(BF16)
        ob_ref[0, rb:rb + CHUNK, :] = jnp.concatenate(outs[N_HEADS:], axis=1).astype(BF16)
        yield


def _interleave(*stage_generators):
    live = list(stage_generators)
    while live:
        for gen in list(live):
            try:
                next(gen)
            except StopIteration:
                live.remove(gen)


def _gate_tile(ab, dtb, a_log):
    z = ab + dtb
    softplus = jnp.maximum(z, 0.0) + jnp.log(1.0 + jnp.exp(-jnp.abs(z)))
    lane = lax.broadcasted_iota(jnp.int32, ab.shape, 1)
    return jnp.where(lane < 2 * N_HEADS, -jnp.exp(a_log) * softplus, jax.nn.sigmoid(ab))


def _gdn_kernel(qkvf_ref, abf_ref, qkvb_ref, abb_ref, dtb_ref, alog_ref,
                of_ref, ob_ref, s_ref, a_scr, rhs_scr, brow_scr, u_scr, wq_scr, qkd_scr, kdt_scr, el_scr, *,
                tiles_per_sample):
    step = pl.program_id(0)
    keep_state = lax.rem(step + tiles_per_sample - 2, tiles_per_sample) != 0
    groups_per_tile = 2 * CHUNKS_PER_TILE
    front_ring = lax.rem(step, 3) * groups_per_tile
    solve_ring = lax.rem(step + 2, 3) * groups_per_tile
    scan_ring = lax.rem(step + 1, 3) * groups_per_tile
    front_mid = lax.rem(step, 2) * groups_per_tile
    solve_mid = groups_per_tile - front_mid

    @pl.when(step == 0)
    def _():
        for scr in (s_ref, a_scr, rhs_scr, brow_scr, u_scr, wq_scr, qkd_scr, kdt_scr, el_scr):
            scr[...] = jnp.zeros_like(scr)

    ri = lax.broadcasted_iota(jnp.int32, (CHUNK, CHUNK), 0)
    ci = lax.broadcasted_iota(jnp.int32, (CHUNK, CHUNK), 1)
    ri4 = lax.broadcasted_iota(jnp.int32, (CHUNK, N_HEADS * CHUNK), 0)
    ci4 = lax.broadcasted_iota(jnp.int32, (CHUNK, N_HEADS * CHUNK), 1) % CHUNK
    rbd = lax.broadcasted_iota(jnp.int32, (N_HEADS * CHUNK, N_HEADS * CHUNK), 0) // CHUNK
    cbd = lax.broadcasted_iota(jnp.int32, (N_HEADS * CHUNK, N_HEADS * CHUNK), 1) // CHUNK
    bd_mask = rbd == cbd
    eye4 = (ri4 == ci4).astype(F32)
    m16 = (ri4 // 16) == (ci4 // 16)
    m32 = jnp.logical_and((ri4 // 32) == (ci4 // 32), jnp.logical_not(m16))
    m64 = (ri4 // 32) != (ci4 // 32)

    cf = ((ci <= ri).astype(BF16), None, ri4 >= ci4, ri4 > ci4)
    cb = ((ci >= ri).astype(BF16), None, ri4 <= ci4, ri4 < ci4)
    dtb = dtb_ref[...]
    alog = alog_ref[...]

    groups = []
    for j in range(CHUNKS_PER_TILE):
        rf = j * CHUNK
        rb = (CHUNKS_PER_TILE - 1 - j) * CHUNK
        for ref, ab_ref, r0, rev, consts in ((qkvf_ref, abf_ref, rf, False, cf), (qkvb_ref, abb_ref, rb, True, cb)):
            groups.append(dict(q=ref[0, r0:r0 + CHUNK, 0:GROUP_W], k=ref[0, r0:r0 + CHUNK, GROUP_W:2 * GROUP_W],
                               v=ref[0, r0:r0 + CHUNK, 2 * GROUP_W:],
                               gates=_gate_tile(ab_ref[0, r0:r0 + CHUNK, :], dtb, alog), rev=rev, consts=consts))
    _interleave(
        _gdn_solve_stages(len(groups), (eye4, m16, m32, m64, bd_mask), solve_ring, solve_mid,
                          a_scr, rhs_scr, brow_scr, u_scr, wq_scr),
        _gdn_scan_stages(scan_ring, keep_state, u_scr, wq_scr, qkd_scr, kdt_scr, el_scr, s_ref, of_ref, ob_ref),
        _gdn_front_stages(groups, front_ring, front_mid, a_scr, rhs_scr, brow_scr, wq_scr, qkd_scr, kdt_scr,
                          el_scr))


def _bwd_tile(s, n_tiles):
    return jnp.where(s == 0, 0, n_tiles - s)


def _gdn(qkv, ab, dtb_row, alog_row):
    b, t_total, _ = qkv.shape
    n_tiles = t_total // TILE

    total = b * n_tiles

    def block_of(f, bwd, lag, c):
        flat = jnp.maximum(f - 2, 0) if lag else jnp.minimum(f, total - 1)
        pos = flat % n_tiles
        return (flat // n_tiles, _bwd_tile(pos, n_tiles) if bwd else pos, c)

    def tok(width, bwd, lag=False):
        return pl.BlockSpec((1, TILE, width), lambda f: block_of(f, bwd, lag, 0))

    row = pl.BlockSpec((1, GATE_LANES), lambda f: (0, 0))
    per_tile = 2 * CHUNKS_PER_TILE
    n_groups = 3 * per_tile
    n_units = n_groups * N_HEADS
    n_mid = 2 * per_tile
    return pl.pallas_call(
        functools.partial(_gdn_kernel, tiles_per_sample=n_tiles),
        grid=(total + 2,),
        in_specs=[tok(3 * GROUP_W, False), tok(GATE_LANES, False),
                  tok(3 * GROUP_W, True), tok(GATE_LANES, True), row, row],
        out_specs=[tok(GROUP_W, False, lag=True), tok(GROUP_W, True, lag=True)],
        out_shape=[jax.ShapeDtypeStruct((b, t_total, GROUP_W), BF16)] * 2,
        scratch_shapes=[pltpu.VMEM((2 * N_HEADS, HEAD_DIM, HEAD_DIM), F32),
                        pltpu.VMEM((n_mid, CHUNK, N_HEADS * CHUNK), F32),
                        pltpu.VMEM((n_mid * N_HEADS, CHUNK, 2 * HEAD_DIM), BF16),
                        pltpu.VMEM((n_mid, 1, N_HEADS * CHUNK), F32),
                        pltpu.VMEM((n_units, CHUNK, HEAD_DIM), F32),
                        pltpu.VMEM((n_units, 2 * CHUNK, HEAD_DIM), BF16),
                        pltpu.VMEM((n_groups, CHUNK, N_HEADS * CHUNK), BF16),
                        pltpu.VMEM((n_units, HEAD_DIM, CHUNK), BF16),
                        pltpu.VMEM((n_groups, 1, GATE_LANES), F32)],
        compiler_params=_cparams(("arbitrary",)),
        name="gdn",
    )(qkv, ab, qkv, ab, dtb_row, alog_row)


def _ret_kernel(qkvf_ref, qkvb_ref, logit_ref, of_ref, ob_ref, r_ref, decay_ref, xi_ref, zeta_ref):
    x = logit_ref[...]
    lg_row = jnp.minimum(x, 0.0) - jnp.log(1.0 + jnp.exp(-jnp.abs(x)))

    @pl.when(pl.program_id(1) == 0)
    def _():
        r_ref[...] = jnp.zeros_like(r_ref)

    @pl.when(jnp.logical_and(pl.program_id(0) == 0, pl.program_id(1) == 0))
    def _():
        ri = lax.broadcasted_iota(jnp.int32, (TILE, TILE), 0)
        ci = lax.broadcasted_iota(jnp.int32, (TILE, TILE), 1)
        row = lax.broadcasted_iota(jnp.int32, (TILE, HEAD_DIM), 0).astype(F32)
        for slot in range(2 * N_HEADS):
            rev = slot >= N_HEADS
            lg = lg_row[:, slot:slot + 1]
            diff = ((ci - ri) if rev else (ri - ci)).astype(F32)
            pos = ((TILE - 1.0) - row) if rev else row
            decay_ref[slot] = jnp.where(diff >= 0, jnp.exp(lg * jnp.maximum(diff, 0.0)), 0.0)
            xi_ref[slot] = jnp.exp(lg * (pos + 1.0))
            zeta_ref[slot] = jnp.exp(lg * ((TILE - 1.0) - pos))

    units = [(refs, (N_HEADS if rev else 0) + h, h * HEAD_DIM)
             for refs, rev in (((qkvf_ref, of_ref), False), ((qkvb_ref, ob_ref), True))
             for h in range(N_HEADS)]
    qs = [refs[0][0, :, lo:lo + HEAD_DIM] for refs, _, lo in units]
    ks = [refs[0][0, :, GROUP_W + lo:GROUP_W + lo + HEAD_DIM] for refs, _, lo in units]
    vs = [refs[0][0, :, 2 * GROUP_W + lo:2 * GROUP_W + lo + HEAD_DIM] for refs, _, lo in units]
    qks = [_dot_nt(q, k) for q, k in zip(qs, ks)]
    qkds = [(qk * decay_ref[slot]).astype(BF16) for qk, (_, slot, _) in zip(qks, units)]
    rs = [r_ref[slot] for _, slot, _ in units]
    for (refs, slot, lo), q, v, qkd, r in zip(units, qs, vs, qkds, rs):
        refs[1][0, :, lo:lo + HEAD_DIM] = (
            _dot(qkd, v) + _dot((q * xi_ref[slot]).astype(BF16), r.astype(BF16))).astype(BF16)
    for (_, slot, _), k, v, r in zip(units, ks, vs, rs):
        g_chunk = jnp.exp(lg_row[:, slot:slot + 1] * float(TILE))
        r_ref[slot] = g_chunk * r + _dot((k * zeta_ref[slot]).T.astype(BF16), v)


def _ret(pr, logit_row):
    b, t_total, _ = pr.shape
    n_tiles = t_total // TILE

    def tok(width, bwd):
        if bwd:
            return pl.BlockSpec((1, TILE, width), lambda i, s: (i, _bwd_tile(s, n_tiles), 0))
        return pl.BlockSpec((1, TILE, width), lambda i, s: (i, s, 0))

    return pl.pallas_call(
        _ret_kernel,
        grid=(b, n_tiles),
        in_specs=[tok(3 * GROUP_W, False), tok(3 * GROUP_W, True),
                  pl.BlockSpec((1, GATE_LANES), lambda i, s: (0, 0))],
        out_specs=[tok(GROUP_W, False), tok(GROUP_W, True)],
        out_shape=[jax.ShapeDtypeStruct((b, t_total, GROUP_W), BF16)] * 2,
        scratch_shapes=[pltpu.VMEM((2 * N_HEADS, HEAD_DIM, HEAD_DIM), F32),
                        pltpu.VMEM((2 * N_HEADS, TILE, TILE), F32),
                        pltpu.VMEM((2 * N_HEADS, TILE, HEAD_DIM), F32),
                        pltpu.VMEM((2 * N_HEADS, TILE, HEAD_DIM), F32)],
        compiler_params=_cparams(("arbitrary", "arbitrary")),
        name="ret",
    )(pr, pr, logit_row)


def _tail_kernel(gf_ref, gb_ref, z_ref, rf_ref, rb_ref, rg_ref, x_ref, mod_ref, gdn_g_ref, ret_g_ref,
                 nffn_g_ref, w_ref, wi_ref, wo_ref, fg_ref, o_ref, y_ref, *, d_model, d_ff):
    o = gf_ref[0].astype(F32) + gb_ref[0].astype(F32)
    r = rf_ref[0].astype(F32) + rb_ref[0].astype(F32)
    z = z_ref[0].astype(F32)
    rg = rg_ref[0].astype(F32)
    for h in range(N_HEADS):
        lo = h * HEAD_DIM
        o_h = o[:, lo:lo + HEAD_DIM]
        o_n = o_h * lax.rsqrt(jnp.mean(o_h * o_h, axis=-1, keepdims=True) + NORM_EPS) * gdn_g_ref[...]
        y_ref[:, lo:lo + HEAD_DIM] = (o_n * _silu(z[:, lo:lo + HEAD_DIM])).astype(BF16)
        r_h = r[:, lo:lo + HEAD_DIM]
        mu = jnp.mean(r_h, axis=-1, keepdims=True)
        cen = r_h - mu
        var = jnp.mean(cen * cen, axis=-1, keepdims=True)
        r_n = cen * lax.rsqrt(var + NORM_EPS) * ret_g_ref[...]
        y_ref[:, GROUP_W + lo:GROUP_W + lo + HEAD_DIM] = (r_n * _silu(rg[:, lo:lo + HEAD_DIM])).astype(BF16)
    mod = mod_ref[0]
    x1 = x_ref[0] + mod[:, 2 * d_model:3 * d_model] * _dot(y_ref[...], w_ref[...])
    h2 = _rms_mod(x1, nffn_g_ref[...], mod[:, 3 * d_model:4 * d_model],
                  mod[:, 4 * d_model:5 * d_model]).astype(BF16)
    acc = None
    for lo in range(0, d_ff, FFN_CHUNK):
        hi = min(lo + FFN_CHUNK, d_ff)
        gate = _dot(h2, wi_ref[:, lo:hi])
        up = _dot(h2, wi_ref[:, d_ff + lo:d_ff + hi])
        part = _dot((_silu(gate) * up).astype(BF16), wo_ref[lo:hi, :])
        acc = part if acc is None else acc + part
    x2 = x1 + mod[:, 5 * d_model:6 * d_model] * acc
    ms = jnp.mean(x2 * x2, axis=-1, keepdims=True)
    o_ref[0] = x2 * lax.rsqrt(ms + NORM_EPS) * fg_ref[...]


def _tail(gdn_f, gdn_b, z, ret_f, ret_b, pr, x, mod3, gdn_g, ret_g, nffn_g, w_out, w_ffn_in, w_ffn_out,
          final_g):
    b, seq, d = x.shape
    d_ff = w_ffn_out.shape[0]
    first = (z.shape[1] - seq) // TAIL_TOK

    def tok(c):
        return pl.BlockSpec((1, TAIL_TOK, GROUP_W), lambda i, t: (i, t + first, c))

    def resident(shape):
        return pl.BlockSpec(shape, lambda i, t: (0,) * len(shape), pipeline_mode=pl.Buffered(1))

    return pl.pallas_call(
        functools.partial(_tail_kernel, d_model=d, d_ff=d_ff),
        grid=(b, seq // TAIL_TOK),
        in_specs=[tok(0), tok(0), tok(0), tok(0), tok(0), tok(3),
                  pl.BlockSpec((1, TAIL_TOK, d), lambda i, t: (i, t, 0)),
                  pl.BlockSpec((1, 1, mod3.shape[2]), lambda i, t: (i, 0, 0)),
                  resident((1, HEAD_DIM)), resident((1, HEAD_DIM)), resident((1, d)),
                  resident(w_out.shape), resident(w_ffn_in.shape), resident(w_ffn_out.shape),
                  resident((1, d))],
        out_specs=pl.BlockSpec((1, TAIL_TOK, d), lambda i, t: (i, t, 0)),
        out_shape=jax.ShapeDtypeStruct((b, seq, d), F32),
        scratch_shapes=[pltpu.VMEM((TAIL_TOK, 2 * GROUP_W), BF16)],
        compiler_params=_cparams(("parallel", "parallel")),
        name="tail",
    )(gdn_f, gdn_b, z, ret_f, ret_b, pr, x, mod3, gdn_g, ret_g, nffn_g, w_out, w_ffn_in, w_ffn_out, final_g)


def _rope_tables(ctx_len, n_lat):
    def angles(pos, n_pairs):
        inv = ROPE_THETA ** (-np.arange(n_pairs, dtype=np.float64) / n_pairs)
        return pos[:, None] * inv[None, :]

    rows = n_lat // GRID_W
    row = np.repeat(np.arange(rows, dtype=np.float64), GRID_W)
    col = np.tile(np.arange(GRID_W, dtype=np.float64), rows)
    zeros = np.zeros((ctx_len,), np.float64)
    p_seq = np.concatenate([np.arange(ctx_len, dtype=np.float64), np.full((n_lat,), float(ctx_len))])
    ang = np.concatenate([angles(p_seq, ROPE_PAIRS[0]),
                          angles(np.concatenate([zeros, row]), ROPE_PAIRS[1]),
                          angles(np.concatenate([zeros, col]), ROPE_PAIRS[2])], axis=-1)
    cos, sin = np.cos(ang).astype(np.float32), np.sin(ang).astype(np.float32)
    return jnp.asarray(np.concatenate([cos, cos, -sin, sin], axis=-1))


def _lane_row(values):
    flat = values.reshape(1, -1).astype(F32)
    return jnp.pad(flat, ((0, 0), (0, GATE_LANES - flat.shape[1])))


def kernel(x, c, ctx, c_ctx, ada_w, ada_b, norm_mix_g, norm_ffn_g, w_in, conv_w, gdn_a_log, gdn_dt_bias,
           gdn_norm_g, ret_decay_logit, ret_norm_g, w_out, w_ffn_in, w_ffn_out, final_g):
    assert ada_w.shape[0] == 1, "single-layer block"
    b, seq, d = x.shape
    ctx_len = ctx.shape[1]
    assert ctx_len == TILE and seq % TILE == 0 and TILE % TAIL_TOK == 0 and b + 1 <= 8

    cond = jnp.concatenate([c, c_ctx[None, :], jnp.zeros((8 - b - 1, d), F32)], axis=0)
    mod = _ada(cond, ada_w, ada_b)
    mod3 = mod[:, None, :]

    n_qkvz = 4 * GROUP_W
    n_gate = 4 * N_HEADS
    qkv, z, ab, pr = _inproj(x, ctx, mod3, norm_mix_g, w_in, conv_w, n_qkvz, n_gate, _rope_tables(ctx_len, seq))

    dtb_row = _lane_row(gdn_dt_bias[0])
    gdn_f, gdn_b = _gdn(qkv, ab, dtb_row, _lane_row(gdn_a_log[0]))
    ret_f, ret_b = _ret(pr, _lane_row(ret_decay_logit[0]))

    return _tail(gdn_f, gdn_b, z, ret_f, ret_b, pr, x, mod3, gdn_norm_g, ret_norm_g, norm_ffn_g,
                 w_out[0].astype(BF16), w_ffn_in[0].astype(BF16), w_ffn_out[0].astype(BF16), final_g[None, :])
```

```python
import functools
import math

import jax
import jax.numpy as jnp
import numpy as np
from jax import lax
from jax.experimental import pallas as pl
from jax.experimental.pallas import tpu as pltpu

F32 = jnp.float32
BF16 = jnp.bfloat16

HEAD_DIM = 128
N_HEADS = 4
GROUP_W = N_HEADS * HEAD_DIM
CONV_K = 5
GRID_W = 64
ROPE_THETA = 10000.0
ROPE_PAIRS = (16, 24, 24)
NORM_EPS = 1e-6

CHUNK = 64
TILE = 256
CHUNKS_PER_TILE = TILE // CHUNK
HALO = 16
GATE_LANES = 128
TAIL_TOK = 512
FFN_CHUNK = 1024
VMEM_LIMIT = 56 * 1024 * 1024


def _cparams(sem):
    return pltpu.CompilerParams(dimension_semantics=sem, vmem_limit_bytes=VMEM_LIMIT)


def _dot(a, b):
    return jnp.dot(a, b, preferred_element_type=F32)


def _dot_nt(a, b):
    return lax.dot_general(a, b, (((1,), (1,)), ((), ())), preferred_element_type=F32)


def _split(a):
    hi = a.astype(BF16)
    lo = (a - hi.astype(F32)).astype(BF16)
    return hi, lo


def _silu(x):
    return x * jax.nn.sigmoid(x)


def _lane_bcast(col, width):
    return jnp.broadcast_to(col, (col.shape[0], width))


def _ada_kernel(cond_ref, w_ref, b_ref, o_ref):
    a_hi, a_lo = _split(_silu(cond_ref[...]))
    w_hi, w_lo = _split(w_ref[0])
    o_ref[...] = _dot(a_hi, w_hi) + _dot(a_lo, w_hi) + _dot(a_hi, w_lo) + b_ref[...]


def _ada(cond, w, b):
    rows, d = cond.shape
    n = w.shape[2]
    bn = 1536
    return pl.pallas_call(
        _ada_kernel,
        grid=(n // bn,),
        in_specs=[pl.BlockSpec((rows, d), lambda j: (0, 0)),
                  pl.BlockSpec((1, d, bn), lambda j: (0, 0, j)),
                  pl.BlockSpec((1, bn), lambda j: (0, j))],
        out_specs=pl.BlockSpec((rows, bn), lambda j: (0, j)),
        out_shape=jax.ShapeDtypeStruct((rows, n), F32),
        compiler_params=_cparams(("parallel",)),
        name="ada",
    )(cond, w, b)


def _rms_mod(x, g, shift, scale):
    ms = jnp.mean(x * x, axis=-1, keepdims=True)
    return (x * lax.rsqrt(ms + NORM_EPS) * g) * (1.0 + scale) + shift


def _conv_tile(ext_ref, w_ref, o_ref):
    pad = (CONV_K - 1) // 2
    rows = TILE + 2 * HALO
    q_scale = HEAD_DIM ** -0.5
    for part in range(3):
        for hd in range(N_HEADS):
            lo = part * GROUP_W + hd * HEAD_DIM
            slab = ext_ref[:, lo:lo + HEAD_DIM]
            acc = None
            for i in range(CONV_K):
                shifted = slab if i == pad else pltpu.roll(slab, (pad - i) % rows, 0)
                term = shifted[HALO:HALO + TILE] * w_ref[0, i:i + 1, lo:lo + HEAD_DIM]
                acc = term if acc is None else acc + term
            y = _silu(acc)
            if part < 2:
                y = y * lax.rsqrt(jnp.sum(y * y, axis=-1, keepdims=True) + NORM_EPS)
            if part == 0:
                y = y * q_scale
            o_ref[0, :, lo:lo + HEAD_DIM] = y.astype(BF16)


def _inproj_kernel(x_ref, ctx_ref, mod_ref, g_ref, w_ref, cw_ref, rope_ref,
                   qkv_ref, z_ref, ab_ref, pr_ref, rg_ref, wa_ref, wab_ref, wr_ref, ext_ref,
                   *, d_model, n_half, n_gate, tiles_per_sample):
    step = pl.program_id(0)
    t = lax.rem(jnp.minimum(step, pl.num_programs(0) - 2), tiles_per_sample)
    t_conv = lax.rem(jnp.maximum(step - 1, 0), tiles_per_sample)
    width = 3 * GROUP_W

    @pl.when(step == 0)
    def _():
        wa_ref[...] = w_ref[0, :n_half, :].T.astype(BF16)
        wr_ref[...] = w_ref[0, n_half + n_gate:, :].T.astype(BF16)
        lane = lax.broadcasted_iota(jnp.int32, (d_model, GATE_LANES), 1)
        wab_ref[...] = jnp.where(lane < n_gate, w_ref[0, n_half:n_half + GATE_LANES, :].T, 0.0).astype(BF16)
        ext_ref[...] = jnp.zeros_like(ext_ref)

    xin = jnp.where(t == 0, ctx_ref[0], x_ref[0])
    mod = mod_ref[0]
    h = _rms_mod(xin, g_ref[...], mod[:, 0:d_model], mod[:, d_model:2 * d_model]).astype(BF16)
    pa = _dot(h, wa_ref[...])
    z_ref[0] = pa[:, width:].astype(BF16)

    has_next = jnp.logical_and(t_conv >= 1, t_conv < tiles_per_sample - 1)
    ext_ref[HALO + TILE:, :] = jnp.where(has_next, pa[:HALO, :width], 0.0)
    _conv_tile(ext_ref, cw_ref, qkv_ref)
    ext_ref[0:HALO, :] = jnp.where(t >= 2, ext_ref[TILE:HALO + TILE, :], 0.0)
    ext_ref[HALO:HALO + TILE, :] = pa[:, :width]

    ab_ref[0] = _dot(h, wab_ref[...])
    pr = _dot(h, wr_ref[...])
    cos2 = rope_ref[:, :HEAD_DIM]
    sin2 = rope_ref[:, HEAD_DIM:]
    k_scale = HEAD_DIM ** -0.5
    for hd in range(2 * N_HEADS):
        lo = hd * HEAD_DIM
        tt = pr[:, lo:lo + HEAD_DIM]
        rot = tt * cos2 + pltpu.roll(tt, HEAD_DIM // 2, 1) * sin2
        if hd >= N_HEADS:
            rot = rot * k_scale
        pr_ref[0, :, lo:lo + HEAD_DIM] = rot.astype(BF16)
    pr_ref[0, :, 2 * GROUP_W:] = pr[:, 2 * GROUP_W:width].astype(BF16)
    rg_ref[0] = pr[:, width:].astype(BF16)


def _inproj(x, ctx, mod3, g, w_in, conv_w, n_half, n_gate, rope):
    b, seq, d = x.shape
    n_tiles = (ctx.shape[1] + seq) // TILE
    t_total = n_tiles * TILE
    total = b * n_tiles
    width = 3 * GROUP_W
    assert w_in.shape[1] == 2 * n_half + n_gate and n_gate <= GATE_LANES and n_half == width + GROUP_W

    def now(f):
        flat = jnp.minimum(f, total - 1)
        return flat // n_tiles, flat % n_tiles

    def lagged(f):
        flat = jnp.maximum(f - 1, 0)
        return flat // n_tiles, flat % n_tiles

    def tok(cols):
        return pl.BlockSpec((1, TILE, cols), lambda f: (now(f)[0], now(f)[1], 0))

    def lat(cols):
        return pl.BlockSpec((1, TILE, cols), lambda f: (now(f)[0], jnp.maximum(now(f)[1] - 1, 0), 0))

    return pl.pallas_call(
        functools.partial(_inproj_kernel, d_model=d, n_half=n_half, n_gate=n_gate, tiles_per_sample=n_tiles),
        grid=(total + 1,),
        in_specs=[pl.BlockSpec((1, TILE, d), lambda f: (now(f)[0], jnp.maximum(now(f)[1] - 1, 0), 0)),
                  pl.BlockSpec((1, TILE, d), lambda f: (now(f)[0], 0, 0)),
                  pl.BlockSpec((1, 1, mod3.shape[2]), lambda f: (jnp.where(now(f)[1] == 0, b, now(f)[0]), 0, 0)),
                  pl.BlockSpec((1, d), lambda f: (0, 0)),
                  pl.BlockSpec(w_in.shape, lambda f: (0, 0, 0), pipeline_mode=pl.Buffered(1)),
                  pl.BlockSpec(conv_w.shape, lambda f: (0, 0, 0)),
                  pl.BlockSpec((TILE, 2 * HEAD_DIM), lambda f: (now(f)[1], 0))],
        out_specs=[pl.BlockSpec((1, TILE, width), lambda f: (lagged(f)[0], lagged(f)[1], 0)),
                   lat(GROUP_W), tok(GATE_LANES), tok(width), lat(GROUP_W)],
        out_shape=[jax.ShapeDtypeStruct((b, t_total, width), BF16),
                   jax.ShapeDtypeStruct((b, seq, GROUP_W), BF16),
                   jax.ShapeDtypeStruct((b, t_total, GATE_LANES), F32),
                   jax.ShapeDtypeStruct((b, t_total, width), BF16),
                   jax.ShapeDtypeStruct((b, seq, GROUP_W), BF16)],
        scratch_shapes=[pltpu.VMEM((d, n_half), BF16), pltpu.VMEM((d, GATE_LANES), BF16),
                        pltpu.VMEM((d, n_half), BF16), pltpu.VMEM((TILE + 2 * HALO, width), F32)],
        compiler_params=_cparams(("arbitrary",)),
        name="inproj",
    )(x, ctx, mod3, g, w_in, conv_w, rope)


def _block_diag(y, bd_mask):
    yb = y.astype(BF16)
    return jnp.where(bd_mask, jnp.concatenate([yb] * N_HEADS, axis=0), jnp.zeros((), BF16))


def _packed_matmul(x, y_bd):
    return _dot(x.astype(BF16), y_bd)


def _unit_lower_inverse_stages(a_list, eye, m16, m32, m64, bd_mask, out):
    pm = _packed_matmul
    ps = [-jnp.where(m16, a, 0.0) for a in a_list]
    xs = [eye + p for p in ps]
    ps = [p.astype(BF16) for p in ps]
    bds = [_block_diag(p, bd_mask) for p in ps]
    for _ in range(3):
        ps = [pm(p, bd).astype(BF16) for p, bd in zip(ps, bds)]
        yield
        bds = [_block_diag(p, bd_mask) for p in ps]
        xs = [x + pm(x, bd) for x, bd in zip(xs, bds)]
        yield
    for m in (m32, m64):
        zs = [pm(jnp.where(m, a, 0.0), _block_diag(x, bd_mask)).astype(BF16) for a, x in zip(a_list, xs)]
        yield
        xs = [x - pm(x, _block_diag(z, bd_mask)) for x, z in zip(xs, zs)]
        yield
    out.extend(xs)


def _gdn_front_stages(groups, ring_base, mid_base, a_scr, rhs_scr, brow_scr, wq_scr, qkd_scr, kdt_scr,
                      el_scr):
    lane = lax.broadcasted_iota(jnp.int32, (CHUNK, GATE_LANES), 1)
    low_half = lax.broadcasted_iota(jnp.int32, (CHUNK, HEAD_DIM), 1) < CHUNK
    zero_k = jnp.zeros((CHUNK, HEAD_DIM), BF16)
    for g in groups:
        tcum = g["consts"][0]
        g_hi, g_lo = _split(g["gates"])
        g["gc"] = _dot(tcum, g_hi) + _dot(tcum, g_lo)
    yield

    for gi, g in enumerate(groups):
        g_lane0 = N_HEADS if g["rev"] else 0
        b_lane0 = 2 * N_HEADS + g_lane0
        last = 0 if g["rev"] else CHUNK - 1
        gc, gates = g["gc"], g["gates"]
        ct = jnp.where(lane < 2 * N_HEADS, gc, gates).T
        egl_t = jnp.exp(ct[:, last:last + 1] - ct)
        gcb, kb, ms = [], [], []
        for h in range(N_HEADS):
            lo = h * HEAD_DIM
            gl = g_lane0 + h
            ring_unit = (ring_base + gi) * N_HEADS + h
            gcb.append(_lane_bcast(gc[:, gl:gl + 1], HEAD_DIM))
            beta = _lane_bcast(gates[:, b_lane0 + h:b_lane0 + h + 1], HEAD_DIM)
            eg_h = jnp.exp(gcb[h])
            q_h = g["q"][:, lo:lo + HEAD_DIM]
            k_h = g["k"][:, lo:lo + HEAD_DIM].astype(F32)
            kb.append((k_h * beta).astype(BF16))
            rhs_scr[(mid_base + gi) * N_HEADS + h] = jnp.concatenate(
                [g["v"][:, lo:lo + HEAD_DIM], (k_h * eg_h).astype(BF16)], axis=1)
            wq_scr[ring_unit, CHUNK:, :] = (q_h * eg_h).astype(BF16)
            kdt_scr[ring_unit] = (k_h.T * egl_t[gl:gl + 1, :]).astype(BF16)
        d_parts = []
        for p in range(N_HEADS // 2):
            h0, h1 = 2 * p, 2 * p + 1
            lo = h0 * HEAD_DIM
            lhs = jnp.concatenate([g["q"][:, lo:lo + 2 * HEAD_DIM], jnp.concatenate([kb[h0], kb[h1]], axis=1)],
                                  axis=0)
            k0 = g["k"][:, lo:lo + HEAD_DIM]
            k1 = g["k"][:, lo + HEAD_DIM:lo + 2 * HEAD_DIM]
            rhs_bd = jnp.concatenate([jnp.concatenate([k0, zero_k], axis=1),
                                      jnp.concatenate([zero_k, k1], axis=1)], axis=0)
            ms.append(_dot_nt(lhs, rhs_bd))
            g_row = jnp.concatenate([ct[g_lane0 + h0:g_lane0 + h0 + 1, :], ct[g_lane0 + h1:g_lane0 + h1 + 1, :]],
                                    axis=1)
            d_parts.append(jnp.where(low_half, gcb[h0], gcb[h1]) - g_row)
        incl4, strict4 = g["consts"][2:]
        e = jnp.exp(jnp.where(incl4, jnp.concatenate(d_parts, axis=1), 0.0))
        a_scr[mid_base + gi] = jnp.where(strict4, jnp.concatenate([m[CHUNK:] for m in ms], axis=1) * e, 0.0)
        qkd_scr[ring_base + gi] = jnp.where(incl4, jnp.concatenate([m[:CHUNK] for m in ms], axis=1) * e,
                                            0.0).astype(BF16)
        el_scr[ring_base + gi] = jnp.exp(gc[last:last + 1, :])
        brow_scr[mid_base + gi] = jnp.concatenate(
            [ct[b_lane0 + h:b_lane0 + h + 1, :] for h in range(N_HEADS)], axis=1)
        yield


def _gdn_solve_stages(n_groups, masks, ring_base, mid_base, a_scr, rhs_scr, brow_scr, u_scr, wq_scr):
    eye4, m16, m32, m64, bd_mask = masks
    t_invs = []
    yield from _unit_lower_inverse_stages([a_scr[mid_base + gi] for gi in range(n_groups)],
                                          eye4, m16, m32, m64, bd_mask, t_invs)
    for gi, t_inv in enumerate(t_invs):
        t_b = (t_inv * brow_scr[mid_base + gi]).astype(BF16)
        for h in range(N_HEADS):
            sol = _dot(t_b[:, h * CHUNK:(h + 1) * CHUNK], rhs_scr[(mid_base + gi) * N_HEADS + h])
            ring_unit = (ring_base + gi) * N_HEADS + h
            u_scr[ring_unit] = sol[:, :HEAD_DIM]
            wq_scr[ring_unit, :CHUNK, :] = sol[:, HEAD_DIM:].astype(BF16)


def _gdn_scan_stages(slot_base, keep_state, u_scr, wq_scr, qkd_scr, kdt_scr, el_scr, s_ref, of_ref, ob_ref):
    for j in range(CHUNKS_PER_TILE):
        units = [(slot_base + 2 * j + d, h, d * N_HEADS + h) for d in range(2) for h in range(N_HEADS)]
        states = [jnp.where(keep_state, s_ref[slot], 0.0) if j == 0 else s_ref[slot] for _, _, slot in units]
        wss = [_dot(wq_scr[gi * N_HEADS + h], s.astype(BF16)) for (gi, h, _), s in zip(units, states)]
        yield
        v_news = [(u_scr[gi * N_HEADS + h] - ws[:CHUNK]).astype(BF16) for (gi, h, _), ws in zip(units, wss)]
        outs = [ws[CHUNK:] + _dot(qkd_scr[gi][:, h * CHUNK:(h + 1) * CHUNK], vn)
                for (gi, h, _), ws, vn in zip(units, wss, v_news)]
        for (gi, h, slot), s, vn in zip(units, states, v_news):
            s_ref[slot] = el_scr[gi][:, slot:slot + 1] * s + _dot(kdt_scr[gi * N_HEADS + h], vn)
        rf = j * CHUNK
        rb = (CHUNKS_PER_TILE - 1 - j) * CHUNK
        of_ref[0, rf:rf + CHUNK, :] = jnp.concatenate(outs[:N_HEADS], axis=1).astype(BF16)
        ob_ref[0, rb:rb + CHUNK, :] = jnp.concatenate(outs[N_HEADS:], axis=1).astype(BF16)
        yield


def _interleave(*stage_generators):
    live = list(stage_generators)
    while live:
        for gen in list(live):
            try:
                next(gen)
            except StopIteration:
                live.remove(gen)


def _gate_tile(ab, dtb, a_log):
    z = ab + dtb
    softplus = jnp.maximum(z, 0.0) + jnp.log(1.0 + jnp.exp(-jnp.abs(z)))
    lane = lax.broadcasted_iota(jnp.int32, ab.shape, 1)
    return jnp.where(lane < 2 * N_HEADS, -jnp.exp(a_log) * softplus, jax.nn.sigmoid(ab))


def _gdn_kernel(qkvf_ref, abf_ref, qkvb_ref, abb_ref, dtb_ref, alog_ref,
                of_ref, ob_ref, s_ref, a_scr, rhs_scr, brow_scr, u_scr, wq_scr, qkd_scr, kdt_scr, el_scr, *,
                tiles_per_sample):
    step = pl.program_id(0)
    keep_state = lax.rem(step + tiles_per_sample - 2, tiles_per_sample) != 0
    groups_per_tile = 2 * CHUNKS_PER_TILE
    front_ring = lax.rem(step, 3) * groups_per_tile
    solve_ring = lax.rem(step + 2, 3) * groups_per_tile
    scan_ring = lax.rem(step + 1, 3) * groups_per_tile
    front_mid = lax.rem(step, 2) * groups_per_tile
    solve_mid = groups_per_tile - front_mid

    @pl.when(step == 0)
    def _():
        for scr in (s_ref, a_scr, rhs_scr, brow_scr, u_scr, wq_scr, qkd_scr, kdt_scr, el_scr):
            scr[...] = jnp.zeros_like(scr)

    ri = lax.broadcasted_iota(jnp.int32, (CHUNK, CHUNK), 0)
    ci = lax.broadcasted_iota(jnp.int32, (CHUNK, CHUNK), 1)
    ri4 = lax.broadcasted_iota(jnp.int32, (CHUNK, N_HEADS * CHUNK), 0)
    ci4 = lax.broadcasted_iota(jnp.int32, (CHUNK, N_HEADS * CHUNK), 1) % CHUNK
    rbd = lax.broadcasted_iota(jnp.int32, (N_HEADS * CHUNK, N_HEADS * CHUNK), 0) // CHUNK
    cbd = lax.broadcasted_iota(jnp.int32, (N_HEADS * CHUNK, N_HEADS * CHUNK), 1) // CHUNK
    bd_mask = rbd == cbd
    eye4 = (ri4 == ci4).astype(F32)
    m16 = (ri4 // 16) == (ci4 // 16)
    m32 = jnp.logical_and((ri4 // 32) == (ci4 // 32), jnp.logical_not(m16))
    m64 = (ri4 // 32) != (ci4 // 32)

    cf = ((ci <= ri).astype(BF16), None, ri4 >= ci4, ri4 > ci4)
    cb = ((ci >= ri).astype(BF16), None, ri4 <= ci4, ri4 < ci4)
    dtb = dtb_ref[...]
    alog = alog_ref[...]

    groups = []
    for j in range(CHUNKS_PER_TILE):
        rf = j * CHUNK
        rb = (CHUNKS_PER_TILE - 1 - j) * CHUNK
        for ref, ab_ref, r0, rev, consts in ((qkvf_ref, abf_ref, rf, False, cf), (qkvb_ref, abb_ref, rb, True, cb)):
            groups.append(dict(q=ref[0, r0:r0 + CHUNK, 0:GROUP_W], k=ref[0, r0:r0 + CHUNK, GROUP_W:2 * GROUP_W],
                               v=ref[0, r0:r0 + CHUNK, 2 * GROUP_W:],
                               gates=_gate_tile(ab_ref[0, r0:r0 + CHUNK, :], dtb, alog), rev=rev, consts=consts))
    _interleave(
        _gdn_solve_stages(len(groups), (eye4, m16, m32, m64, bd_mask), solve_ring, solve_mid,
                          a_scr, rhs_scr, brow_scr, u_scr, wq_scr),
        _gdn_scan_stages(scan_ring, keep_state, u_scr, wq_scr, qkd_scr, kdt_scr, el_scr, s_ref, of_ref, ob_ref),
        _gdn_front_stages(groups, front_ring, front_mid, a_scr, rhs_scr, brow_scr, wq_scr, qkd_scr, kdt_scr,
                          el_scr))


def _bwd_tile(s, n_tiles):
    return jnp.where(s == 0, 0, n_tiles - s)


def _latent_block(tile, bwd, n_tiles):
    return jnp.where(tile == 0, n_tiles - 2 if bwd else 0, tile - 1)


def _gdn(qkv, ab, dtb_row, alog_row):
    b, t_total, _ = qkv.shape
    n_tiles = t_total // TILE

    total = b * n_tiles

    def block_of(f, bwd, lag):
        flat = jnp.maximum(f - 2, 0) if lag else jnp.minimum(f, total - 1)
        pos = flat % n_tiles
        tile = _bwd_tile(pos, n_tiles) if bwd else pos
        return (flat // n_tiles, _latent_block(tile, bwd, n_tiles) if lag else tile, 0)

    def tok(width, bwd, lag=False):
        return pl.BlockSpec((1, TILE, width), lambda f: block_of(f, bwd, lag))

    row = pl.BlockSpec((1, GATE_LANES), lambda f: (0, 0))
    per_tile = 2 * CHUNKS_PER_TILE
    n_groups = 3 * per_tile
    n_units = n_groups * N_HEADS
    n_mid = 2 * per_tile
    return pl.pallas_call(
        functools.partial(_gdn_kernel, tiles_per_sample=n_tiles),
        grid=(total + 2,),
        in_specs=[tok(3 * GROUP_W, False), tok(GATE_LANES, False),
                  tok(3 * GROUP_W, True), tok(GATE_LANES, True), row, row],
        out_specs=[tok(GROUP_W, False, lag=True), tok(GROUP_W, True, lag=True)],
        out_shape=[jax.ShapeDtypeStruct((b, t_total - TILE, GROUP_W), BF16)] * 2,
        scratch_shapes=[pltpu.VMEM((2 * N_HEADS, HEAD_DIM, HEAD_DIM), F32),
                        pltpu.VMEM((n_mid, CHUNK, N_HEADS * CHUNK), F32),
                        pltpu.VMEM((n_mid * N_HEADS, CHUNK, 2 * HEAD_DIM), BF16),
                        pltpu.VMEM((n_mid, 1, N_HEADS * CHUNK), F32),
                        pltpu.VMEM((n_units, CHUNK, HEAD_DIM), F32),
                        pltpu.VMEM((n_units, 2 * CHUNK, HEAD_DIM), BF16),
                        pltpu.VMEM((n_groups, CHUNK, N_HEADS * CHUNK), BF16),
                        pltpu.VMEM((n_units, HEAD_DIM, CHUNK), BF16),
                        pltpu.VMEM((n_groups, 1, GATE_LANES), F32)],
        compiler_params=_cparams(("arbitrary",)),
        name="gdn",
    )(qkv, ab, qkv, ab, dtb_row, alog_row)


def _ret_kernel(qkvf_ref, qkvb_ref, logit_ref, of_ref, ob_ref, r_ref, decay_ref, xi_ref, zeta_ref):
    x = logit_ref[...]
    lg_row = jnp.minimum(x, 0.0) - jnp.log(1.0 + jnp.exp(-jnp.abs(x)))

    @pl.when(pl.program_id(1) == 0)
    def _():
        r_ref[...] = jnp.zeros_like(r_ref)

    @pl.when(jnp.logical_and(pl.program_id(0) == 0, pl.program_id(1) == 0))
    def _():
        ri = lax.broadcasted_iota(jnp.int32, (TILE, TILE), 0)
        ci = lax.broadcasted_iota(jnp.int32, (TILE, TILE), 1)
        row = lax.broadcasted_iota(jnp.int32, (TILE, HEAD_DIM), 0).astype(F32)
        for slot in range(2 * N_HEADS):
            rev = slot >= N_HEADS
            lg = lg_row[:, slot:slot + 1]
            diff = ((ci - ri) if rev else (ri - ci)).astype(F32)
            pos = ((TILE - 1.0) - row) if rev else row
            decay_ref[slot] = jnp.where(diff >= 0, jnp.exp(lg * jnp.maximum(diff, 0.0)), 0.0)
            xi_ref[slot] = jnp.exp(lg * (pos + 1.0))
            zeta_ref[slot] = jnp.exp(lg * ((TILE - 1.0) - pos))

    units = [(refs, (N_HEADS if rev else 0) + h, h * HEAD_DIM)
             for refs, rev in (((qkvf_ref, of_ref), False), ((qkvb_ref, ob_ref), True))
             for h in range(N_HEADS)]
    qs = [refs[0][0, :, lo:lo + HEAD_DIM] for refs, _, lo in units]
    ks = [refs[0][0, :, GROUP_W + lo:GROUP_W + lo + HEAD_DIM] for refs, _, lo in units]
    vs = [refs[0][0, :, 2 * GROUP_W + lo:2 * GROUP_W + lo + HEAD_DIM] for refs, _, lo in units]
    qks = [_dot_nt(q, k) for q, k in zip(qs, ks)]
    qkds = [(qk * decay_ref[slot]).astype(BF16) for qk, (_, slot, _) in zip(qks, units)]
    rs = [r_ref[slot] for _, slot, _ in units]
    for (refs, slot, lo), q, v, qkd, r in zip(units, qs, vs, qkds, rs):
        refs[1][0, :, lo:lo + HEAD_DIM] = (
            _dot(qkd, v) + _dot((q * xi_ref[slot]).astype(BF16), r.astype(BF16))).astype(BF16)
    for (_, slot, _), k, v, r in zip(units, ks, vs, rs):
        g_chunk = jnp.exp(lg_row[:, slot:slot + 1] * float(TILE))
        r_ref[slot] = g_chunk * r + _dot((k * zeta_ref[slot]).T.astype(BF16), v)


def _ret(pr, logit_row):
    b, t_total, _ = pr.shape
    n_tiles = t_total // TILE

    def tok(width, bwd, latent=False):
        def index(i, s):
            tile = _bwd_tile(s, n_tiles) if bwd else s
            return (i, _latent_block(tile, bwd, n_tiles) if latent else tile, 0)
        return pl.BlockSpec((1, TILE, width), index)

    return pl.pallas_call(
        _ret_kernel,
        grid=(b, n_tiles),
        in_specs=[tok(3 * GROUP_W, False), tok(3 * GROUP_W, True),
                  pl.BlockSpec((1, GATE_LANES), lambda i, s: (0, 0))],
        out_specs=[tok(GROUP_W, False, latent=True), tok(GROUP_W, True, latent=True)],
        out_shape=[jax.ShapeDtypeStruct((b, t_total - TILE, GROUP_W), BF16)] * 2,
        scratch_shapes=[pltpu.VMEM((2 * N_HEADS, HEAD_DIM, HEAD_DIM), F32),
                        pltpu.VMEM((2 * N_HEADS, TILE, TILE), F32),
                        pltpu.VMEM((2 * N_HEADS, TILE, HEAD_DIM), F32),
                        pltpu.VMEM((2 * N_HEADS, TILE, HEAD_DIM), F32)],
        compiler_params=_cparams(("arbitrary", "arbitrary")),
        name="ret",
    )(pr, pr, logit_row)


def _tail_kernel(gf_ref, gb_ref, z_ref, rf_ref, rb_ref, rg_ref, x_ref, mod_ref, gdn_g_ref, ret_g_ref,
                 nffn_g_ref, w_ref, wi_ref, wo_ref, fg_ref, o_ref, y_ref, *, d_model, d_ff):
    o = gf_ref[0].astype(F32) + gb_ref[0].astype(F32)
    r = rf_ref[0].astype(F32) + rb_ref[0].astype(F32)
    z = z_ref[0].astype(F32)
    rg = rg_ref[0].astype(F32)
    for h in range(N_HEADS):
        lo = h * HEAD_DIM
        o_h = o[:, lo:lo + HEAD_DIM]
        o_n = o_h * lax.rsqrt(jnp.mean(o_h * o_h, axis=-1, keepdims=True) + NORM_EPS) * gdn_g_ref[...]
        y_ref[:, lo:lo + HEAD_DIM] = (o_n * _silu(z[:, lo:lo + HEAD_DIM])).astype(BF16)
        r_h = r[:, lo:lo + HEAD_DIM]
        mu = jnp.mean(r_h, axis=-1, keepdims=True)
        cen = r_h - mu
        var = jnp.mean(cen * cen, axis=-1, keepdims=True)
        r_n = cen * lax.rsqrt(var + NORM_EPS) * ret_g_ref[...]
        y_ref[:, GROUP_W + lo:GROUP_W + lo + HEAD_DIM] = (r_n * _silu(rg[:, lo:lo + HEAD_DIM])).astype(BF16)
    mod = mod_ref[0]
    x1 = x_ref[0] + mod[:, 2 * d_model:3 * d_model] * _dot(y_ref[...], w_ref[...])
    h2 = _rms_mod(x1, nffn_g_ref[...], mod[:, 3 * d_model:4 * d_model],
                  mod[:, 4 * d_model:5 * d_model]).astype(BF16)
    acc = None
    for lo in range(0, d_ff, FFN_CHUNK):
        hi = min(lo + FFN_CHUNK, d_ff)
        gate = _dot(h2, wi_ref[:, lo:hi])
        up = _dot(h2, wi_ref[:, d_ff + lo:d_ff + hi])
        part = _dot((_silu(gate) * up).astype(BF16), wo_ref[lo:hi, :])
        acc = part if acc is None else acc + part
    x2 = x1 + mod[:, 5 * d_model:6 * d_model] * acc
    ms = jnp.mean(x2 * x2, axis=-1, keepdims=True)
    o_ref[0] = x2 * lax.rsqrt(ms + NORM_EPS) * fg_ref[...]


def _tail(gdn_f, gdn_b, z, ret_f, ret_b, rg, x, mod3, gdn_g, ret_g, nffn_g, w_out, w_ffn_in, w_ffn_out,
          final_g):
    b, seq, d = x.shape
    d_ff = w_ffn_out.shape[0]
    tok = pl.BlockSpec((1, TAIL_TOK, GROUP_W), lambda i, t: (i, t, 0))

    def resident(shape):
        return pl.BlockSpec(shape, lambda i, t: (0,) * len(shape), pipeline_mode=pl.Buffered(1))

    return pl.pallas_call(
        functools.partial(_tail_kernel, d_model=d, d_ff=d_ff),
        grid=(b, seq // TAIL_TOK),
        in_specs=[tok, tok, tok, tok, tok, tok,
                  pl.BlockSpec((1, TAIL_TOK, d), lambda i, t: (i, t, 0)),
                  pl.BlockSpec((1, 1, mod3.shape[2]), lambda i, t: (i, 0, 0)),
                  resident((1, HEAD_DIM)), resident((1, HEAD_DIM)), resident((1, d)),
                  resident(w_out.shape), resident(w_ffn_in.shape), resident(w_ffn_out.shape),
                  resident((1, d))],
        out_specs=pl.BlockSpec((1, TAIL_TOK, d), lambda i, t: (i, t, 0)),
        out_shape=jax.ShapeDtypeStruct((b, seq, d), F32),
        scratch_shapes=[pltpu.VMEM((TAIL_TOK, 2 * GROUP_W), BF16)],
        compiler_params=_cparams(("parallel", "parallel")),
        name="tail",
    )(gdn_f, gdn_b, z, ret_f, ret_b, rg, x, mod3, gdn_g, ret_g, nffn_g, w_out, w_ffn_in, w_ffn_out, final_g)


def _rope_tables(ctx_len, n_lat):
    def angles(pos, n_pairs):
        inv = ROPE_THETA ** (-np.arange(n_pairs, dtype=np.float64) / n_pairs)
        return pos[:, None] * inv[None, :]

    rows = n_lat // GRID_W
    row = np.repeat(np.arange(rows, dtype=np.float64), GRID_W)
    col = np.tile(np.arange(GRID_W, dtype=np.float64), rows)
    zeros = np.zeros((ctx_len,), np.float64)
    p_seq = np.concatenate([np.arange(ctx_len, dtype=np.float64), np.full((n_lat,), float(ctx_len))])
    ang = np.concatenate([angles(p_seq, ROPE_PAIRS[0]),
                          angles(np.concatenate([zeros, row]), ROPE_PAIRS[1]),
                          angles(np.concatenate([zeros, col]), ROPE_PAIRS[2])], axis=-1)
    cos, sin = np.cos(ang).astype(np.float32), np.sin(ang).astype(np.float32)
    return jnp.asarray(np.concatenate([cos, cos, -sin, sin], axis=-1))


def _lane_row(values):
    flat = values.reshape(1, -1).astype(F32)
    return jnp.pad(flat, ((0, 0), (0, GATE_LANES - flat.shape[1])))


def kernel(x, c, ctx, c_ctx, ada_w, ada_b, norm_mix_g, norm_ffn_g, w_in, conv_w, gdn_a_log, gdn_dt_bias,
           gdn_norm_g, ret_decay_logit, ret_norm_g, w_out, w_ffn_in, w_ffn_out, final_g):
    assert ada_w.shape[0] == 1, "single-layer block"
    b, seq, d = x.shape
    ctx_len = ctx.shape[1]
    assert ctx_len == TILE and seq % TILE == 0 and seq % TAIL_TOK == 0 and b + 1 <= 8

    cond = jnp.concatenate([c, c_ctx[None, :], jnp.zeros((8 - b - 1, d), F32)], axis=0)
    mod = _ada(cond, ada_w, ada_b)
    mod3 = mod[:, None, :]

    n_qkvz = 4 * GROUP_W
    n_gate = 4 * N_HEADS
    qkv, z, ab, pr, rg = _inproj(x, ctx, mod3, norm_mix_g, jnp.swapaxes(w_in, 1, 2), conv_w, n_qkvz, n_gate,
                                 _rope_tables(ctx_len, seq))

    dtb_row = _lane_row(gdn_dt_bias[0])
    gdn_f, gdn_b = _gdn(qkv, ab, dtb_row, _lane_row(gdn_a_log[0]))
    ret_f, ret_b = _ret(pr, _lane_row(ret_decay_logit[0]))

    return _tail(gdn_f, gdn_b, z, ret_f, ret_b, rg, x, mod3, gdn_norm_g, ret_norm_g, norm_ffn_g,
                 w_out[0].astype(BF16), w_ffn_in[0].astype(BF16), w_ffn_out[0].astype(BF16), final_g[None, :])
```

```python
import functools
import math

import jax
import jax.numpy as jnp
import numpy as np
from jax import lax
from jax.experimental import pallas as pl
from jax.experimental.pallas import tpu as pltpu

F32 = jnp.float32
BF16 = jnp.bfloat16

HEAD_DIM = 128
N_HEADS = 4
GROUP_W = N_HEADS * HEAD_DIM
CONV_K = 5
GRID_W = 64
ROPE_THETA = 10000.0
ROPE_PAIRS = (16, 24, 24)
NORM_EPS = 1e-6

CHUNK = 64
TILE = 256
CHUNKS_PER_TILE = TILE // CHUNK
HALO = 16
GATE_LANES = 128
TAIL_TOK = 512
FFN_CHUNK = 1024
VMEM_LIMIT = 56 * 1024 * 1024


def _cparams(sem):
    return pltpu.CompilerParams(dimension_semantics=sem, vmem_limit_bytes=VMEM_LIMIT)


def _dot(a, b):
    return jnp.dot(a, b, preferred_element_type=F32)


def _dot_nt(a, b):
    return lax.dot_general(a, b, (((1,), (1,)), ((), ())), preferred_element_type=F32)


def _split(a):
    hi = a.astype(BF16)
    lo = (a - hi.astype(F32)).astype(BF16)
    return hi, lo


def _silu(x):
    return x * jax.nn.sigmoid(x)


def _lane_bcast(col, width):
    return jnp.broadcast_to(col, (col.shape[0], width))


def _ada_kernel(cond_ref, w_ref, b_ref, o_ref):
    a_hi, a_lo = _split(_silu(cond_ref[...]))
    w_hi, w_lo = _split(w_ref[0])
    o_ref[...] = _dot(a_hi, w_hi) + _dot(a_lo, w_hi) + _dot(a_hi, w_lo) + b_ref[...]


def _ada(cond, w, b):
    rows, d = cond.shape
    n = w.shape[2]
    bn = 1536
    return pl.pallas_call(
        _ada_kernel,
        grid=(n // bn,),
        in_specs=[pl.BlockSpec((rows, d), lambda j: (0, 0)),
                  pl.BlockSpec((1, d, bn), lambda j: (0, 0, j)),
                  pl.BlockSpec((1, bn), lambda j: (0, j))],
        out_specs=pl.BlockSpec((rows, bn), lambda j: (0, j)),
        out_shape=jax.ShapeDtypeStruct((rows, n), F32),
        compiler_params=_cparams(("parallel",)),
        name="ada",
    )(cond, w, b)


def _rms_mod(x, g, shift, scale):
    ms = jnp.mean(x * x, axis=-1, keepdims=True)
    return (x * lax.rsqrt(ms + NORM_EPS) * g) * (1.0 + scale) + shift


def _conv_tile(ext_ref, w_ref, o_ref):
    pad = (CONV_K - 1) // 2
    rows = TILE + 2 * HALO
    q_scale = HEAD_DIM ** -0.5
    for part in range(3):
        for hd in range(N_HEADS):
            lo = part * GROUP_W + hd * HEAD_DIM
            slab = ext_ref[:, lo:lo + HEAD_DIM]
            acc = None
            for i in range(CONV_K):
                shifted = slab if i == pad else pltpu.roll(slab, (pad - i) % rows, 0)
                term = shifted[HALO:HALO + TILE] * w_ref[0, i:i + 1, lo:lo + HEAD_DIM]
                acc = term if acc is None else acc + term
            y = _silu(acc)
            if part < 2:
                y = y * lax.rsqrt(jnp.sum(y * y, axis=-1, keepdims=True) + NORM_EPS)
            if part == 0:
                y = y * q_scale
            o_ref[0, :, lo:lo + HEAD_DIM] = y.astype(BF16)


def _inproj_kernel(x_ref, ctx_ref, mod_ref, g_ref, w_ref, cw_ref, rope_ref,
                   qkv_ref, z_ref, ab_ref, pr_ref, rg_ref, wa_ref, wab_ref, wr_ref, ext_ref,
                   *, d_model, n_half, n_gate, tiles_per_sample):
    step = pl.program_id(0)
    t = lax.rem(jnp.minimum(step, pl.num_programs(0) - 2), tiles_per_sample)
    t_conv = lax.rem(jnp.maximum(step - 1, 0), tiles_per_sample)
    width = 3 * GROUP_W

    @pl.when(step == 0)
    def _():
        wa_ref[...] = w_ref[0, :n_half, :].T.astype(BF16)
        wr_ref[...] = w_ref[0, n_half + n_gate:, :].T.astype(BF16)
        lane = lax.broadcasted_iota(jnp.int32, (d_model, GATE_LANES), 1)
        wab_ref[...] = jnp.where(lane < n_gate, w_ref[0, n_half:n_half + GATE_LANES, :].T, 0.0).astype(BF16)
        ext_ref[...] = jnp.zeros_like(ext_ref)

    xin = jnp.where(t == 0, ctx_ref[0], x_ref[0])
    mod = mod_ref[0]
    h = _rms_mod(xin, g_ref[...], mod[:, 0:d_model], mod[:, d_model:2 * d_model]).astype(BF16)
    pa = _dot(h, wa_ref[...])
    z_ref[0] = pa[:, width:].astype(BF16)

    has_next = jnp.logical_and(t_conv >= 1, t_conv < tiles_per_sample - 1)
    ext_ref[HALO + TILE:, :] = jnp.where(has_next, pa[:HALO, :width], 0.0)
    _conv_tile(ext_ref, cw_ref, qkv_ref)
    ext_ref[0:HALO, :] = jnp.where(t >= 2, ext_ref[TILE:HALO + TILE, :], 0.0)
    ext_ref[HALO:HALO + TILE, :] = pa[:, :width]

    ab_ref[0] = _dot(h, wab_ref[...])
    pr = _dot(h, wr_ref[...])
    cos2 = rope_ref[:, :HEAD_DIM]
    sin2 = rope_ref[:, HEAD_DIM:]
    k_scale = HEAD_DIM ** -0.5
    for hd in range(2 * N_HEADS):
        lo = hd * HEAD_DIM
        tt = pr[:, lo:lo + HEAD_DIM]
        rot = tt * cos2 + pltpu.roll(tt, HEAD_DIM // 2, 1) * sin2
        if hd >= N_HEADS:
            rot = rot * k_scale
        pr_ref[0, :, lo:lo + HEAD_DIM] = rot.astype(BF16)
    pr_ref[0, :, 2 * GROUP_W:] = pr[:, 2 * GROUP_W:width].astype(BF16)
    rg_ref[0] = pr[:, width:].astype(BF16)


def _inproj(x, ctx, mod3, g, w_in, conv_w, n_half, n_gate, rope):
    b, seq, d = x.shape
    n_tiles = (ctx.shape[1] + seq) // TILE
    t_total = n_tiles * TILE
    total = b * n_tiles
    width = 3 * GROUP_W
    assert w_in.shape[1] == 2 * n_half + n_gate and n_gate <= GATE_LANES and n_half == width + GROUP_W

    def now(f):
        flat = jnp.minimum(f, total - 1)
        return flat // n_tiles, flat % n_tiles

    def lagged(f):
        flat = jnp.maximum(f - 1, 0)
        return flat // n_tiles, flat % n_tiles

    def tok(cols):
        return pl.BlockSpec((1, TILE, cols), lambda f: (now(f)[0], now(f)[1], 0))

    def lat(cols):
        return pl.BlockSpec((1, TILE, cols), lambda f: (now(f)[0], jnp.maximum(now(f)[1] - 1, 0), 0))

    return pl.pallas_call(
        functools.partial(_inproj_kernel, d_model=d, n_half=n_half, n_gate=n_gate, tiles_per_sample=n_tiles),
        grid=(total + 1,),
        in_specs=[pl.BlockSpec((1, TILE, d), lambda f: (now(f)[0], jnp.maximum(now(f)[1] - 1, 0), 0)),
                  pl.BlockSpec((1, TILE, d), lambda f: (now(f)[0], 0, 0)),
                  pl.BlockSpec((1, 1, mod3.shape[2]), lambda f: (jnp.where(now(f)[1] == 0, b, now(f)[0]), 0, 0)),
                  pl.BlockSpec((1, d), lambda f: (0, 0)),
                  pl.BlockSpec(w_in.shape, lambda f: (0, 0, 0), pipeline_mode=pl.Buffered(1)),
                  pl.BlockSpec(conv_w.shape, lambda f: (0, 0, 0)),
                  pl.BlockSpec((TILE, 2 * HEAD_DIM), lambda f: (now(f)[1], 0))],
        out_specs=[pl.BlockSpec((1, TILE, width), lambda f: (lagged(f)[0], lagged(f)[1], 0)),
                   lat(GROUP_W), tok(GATE_LANES), tok(width), lat(GROUP_W)],
        out_shape=[jax.ShapeDtypeStruct((b, t_total, width), BF16),
                   jax.ShapeDtypeStruct((b, seq, GROUP_W), BF16),
                   jax.ShapeDtypeStruct((b, t_total, GATE_LANES), F32),
                   jax.ShapeDtypeStruct((b, t_total, width), BF16),
                   jax.ShapeDtypeStruct((b, seq, GROUP_W), BF16)],
        scratch_shapes=[pltpu.VMEM((d, n_half), BF16), pltpu.VMEM((d, GATE_LANES), BF16),
                        pltpu.VMEM((d, n_half), BF16), pltpu.VMEM((TILE + 2 * HALO, width), F32)],
        compiler_params=_cparams(("arbitrary",)),
        name="inproj",
    )(x, ctx, mod3, g, w_in, conv_w, rope)


def _block_diag(y, bd_mask):
    yb = y.astype(BF16)
    return jnp.where(bd_mask, jnp.concatenate([yb] * N_HEADS, axis=0), jnp.zeros((), BF16))


def _packed_matmul(x, y_bd):
    return _dot(x.astype(BF16), y_bd)


def _unit_lower_inverse_stages(a_list, eye, m16, m32, m64, bd_mask, out):
    pm = _packed_matmul
    ps = [-jnp.where(m16, a, 0.0) for a in a_list]
    xs = [eye + p for p in ps]
    ps = [p.astype(BF16) for p in ps]
    bds = [_block_diag(p, bd_mask) for p in ps]
    for _ in range(3):
        ps = [pm(p, bd).astype(BF16) for p, bd in zip(ps, bds)]
        yield
        bds = [_block_diag(p, bd_mask) for p in ps]
        xs = [x + pm(x, bd) for x, bd in zip(xs, bds)]
        yield
    for m in (m32, m64):
        zs = [pm(jnp.where(m, a, 0.0), _block_diag(x, bd_mask)).astype(BF16) for a, x in zip(a_list, xs)]
        yield
        xs = [x - pm(x, _block_diag(z, bd_mask)) for x, z in zip(xs, zs)]
        yield
    out.extend(xs)


def _gdn_front_stages(groups, ring_base, mid_base, a_scr, rhs_scr, brow_scr, wq_scr, qkd_scr, kdt_scr,
                      el_scr):
    lane = lax.broadcasted_iota(jnp.int32, (CHUNK, GATE_LANES), 1)
    low_half = lax.broadcasted_iota(jnp.int32, (CHUNK, HEAD_DIM), 1) < CHUNK
    zero_k = jnp.zeros((CHUNK, HEAD_DIM), BF16)
    for g in groups:
        tcum = g["consts"][0]
        g_hi, g_lo = _split(g["gates"])
        g["gc"] = _dot(tcum, g_hi) + _dot(tcum, g_lo)
    yield

    for gi, g in enumerate(groups):
        g_lane0 = N_HEADS if g["rev"] else 0
        b_lane0 = 2 * N_HEADS + g_lane0
        last = 0 if g["rev"] else CHUNK - 1
        gc, gates = g["gc"], g["gates"]
        ct = jnp.where(lane < 2 * N_HEADS, gc, gates).T
        egl_t = jnp.exp(ct[:, last:last + 1] - ct)
        gcb, kb, ms = [], [], []
        for h in range(N_HEADS):
            lo = h * HEAD_DIM
            gl = g_lane0 + h
            ring_unit = (ring_base + gi) * N_HEADS + h
            gcb.append(_lane_bcast(gc[:, gl:gl + 1], HEAD_DIM))
            beta = _lane_bcast(gates[:, b_lane0 + h:b_lane0 + h + 1], HEAD_DIM)
            eg_h = jnp.exp(gcb[h])
            q_h = g["q"][:, lo:lo + HEAD_DIM]
            k_h = g["k"][:, lo:lo + HEAD_DIM].astype(F32)
            kb.append((k_h * beta).astype(BF16))
            rhs_scr[(mid_base + gi) * N_HEADS + h] = jnp.concatenate(
                [g["v"][:, lo:lo + HEAD_DIM], (k_h * eg_h).astype(BF16)], axis=1)
            wq_scr[ring_unit, CHUNK:, :] = (q_h * eg_h).astype(BF16)
            kdt_scr[ring_unit] = (k_h.T * egl_t[gl:gl + 1, :]).astype(BF16)
        d_parts = []
        for p in range(N_HEADS // 2):
            h0, h1 = 2 * p, 2 * p + 1
            lo = h0 * HEAD_DIM
            lhs = jnp.concatenate([g["q"][:, lo:lo + 2 * HEAD_DIM], jnp.concatenate([kb[h0], kb[h1]], axis=1)],
                                  axis=0)
            k0 = g["k"][:, lo:lo + HEAD_DIM]
            k1 = g["k"][:, lo + HEAD_DIM:lo + 2 * HEAD_DIM]
            rhs_bd = jnp.concatenate([jnp.concatenate([k0, zero_k], axis=1),
                                      jnp.concatenate([zero_k, k1], axis=1)], axis=0)
            ms.append(_dot_nt(lhs, rhs_bd))
            g_row = jnp.concatenate([ct[g_lane0 + h0:g_lane0 + h0 + 1, :], ct[g_lane0 + h1:g_lane0 + h1 + 1, :]],
                                    axis=1)
            d_parts.append(jnp.where(low_half, gcb[h0], gcb[h1]) - g_row)
        incl4, strict4 = g["consts"][2:]
        e = jnp.exp(jnp.where(incl4, jnp.concatenate(d_parts, axis=1), 0.0))
        a_scr[mid_base + gi] = jnp.where(strict4, jnp.concatenate([m[CHUNK:] for m in ms], axis=1) * e, 0.0)
        qkd_scr[ring_base + gi] = jnp.where(incl4, jnp.concatenate([m[:CHUNK] for m in ms], axis=1) * e,
                                            0.0).astype(BF16)
        el_scr[ring_base + gi] = jnp.exp(gc[last:last + 1, :])
        brow_scr[mid_base + gi] = jnp.concatenate(
            [ct[b_lane0 + h:b_lane0 + h + 1, :] for h in range(N_HEADS)], axis=1)
        yield


def _gdn_solve_stages(n_groups, masks, ring_base, mid_base, a_scr, rhs_scr, brow_scr, u_scr, wq_scr):
    eye4, m16, m32, m64, bd_mask = masks
    t_invs = []
    yield from _unit_lower_inverse_stages([a_scr[mid_base + gi] for gi in range(n_groups)],
                                          eye4, m16, m32, m64, bd_mask, t_invs)
    for gi, t_inv in enumerate(t_invs):
        t_b = (t_inv * brow_scr[mid_base + gi]).astype(BF16)
        for h in range(N_HEADS):
            sol = _dot(t_b[:, h * CHUNK:(h + 1) * CHUNK], rhs_scr[(mid_base + gi) * N_HEADS + h])
            ring_unit = (ring_base + gi) * N_HEADS + h
            u_scr[ring_unit] = sol[:, :HEAD_DIM]
            wq_scr[ring_unit, :CHUNK, :] = sol[:, HEAD_DIM:].astype(BF16)


def _gdn_scan_stages(slot_base, keep_state, u_scr, wq_scr, qkd_scr, kdt_scr, el_scr, s_ref, of_ref, ob_ref):
    for j in range(CHUNKS_PER_TILE):
        units = [(slot_base + 2 * j + d, h, d * N_HEADS + h) for d in range(2) for h in range(N_HEADS)]
        states = [jnp.where(keep_state, s_ref[slot], 0.0) if j == 0 else s_ref[slot] for _, _, slot in units]
        wss = [_dot(wq_scr[gi * N_HEADS + h], s.astype(BF16)) for (gi, h, _), s in zip(units, states)]
        yield
        v_news = [(u_scr[gi * N_HEADS + h] - ws[:CHUNK]).astype(BF16) for (gi, h, _), ws in zip(units, wss)]
        outs = [ws[CHUNK:] + _dot(qkd_scr[gi][:, h * CHUNK:(h + 1) * CHUNK], vn)
                for (gi, h, _), ws, vn in zip(units, wss, v_news)]
        for (gi, h, slot), s, vn in zip(units, states, v_news):
            s_ref[slot] = el_scr[gi][:, slot:slot + 1] * s + _dot(kdt_scr[gi * N_HEADS + h], vn)
        rf = j * CHUNK
        rb = (CHUNKS_PER_TILE - 1 - j) * CHUNK
        of_ref[0, rf:rf + CHUNK, :] = jnp.concatenate(outs[:N_HEADS], axis=1).astype(BF16)
        ob_ref[0, rb:rb + CHUNK, :] = jnp.concatenate(outs[N_HEADS:], axis=1).astype(BF16)
        yield


def _interleave(*stage_generators):
    live = list(stage_generators)
    while live:
        for gen in list(live):
            try:
                next(gen)
            except StopIteration:
                live.remove(gen)


def _gate_tile(ab, dtb, a_log):
    z = ab + dtb
    softplus = jnp.maximum(z, 0.0) + jnp.log(1.0 + jnp.exp(-jnp.abs(z)))
    lane = lax.broadcasted_iota(jnp.int32, ab.shape, 1)
    return jnp.where(lane < 2 * N_HEADS, -jnp.exp(a_log) * softplus, jax.nn.sigmoid(ab))


def _gdn_kernel(qkvf_ref, abf_ref, qkvb_ref, abb_ref, dtb_ref, alog_ref, rqkvf_ref, rqkvb_ref, logit_ref,
                of_ref, ob_ref, rof_ref, rob_ref,
                s_ref, a_scr, rhs_scr, brow_scr, u_scr, wq_scr, qkd_scr, kdt_scr, el_scr,
                r_ref, decay_ref, xi_ref, zeta_ref, *, tiles_per_sample):
    step = pl.program_id(0)
    lg_row = _log_sigmoid(logit_ref[...])
    keep_state = lax.rem(step + tiles_per_sample - 2, tiles_per_sample) != 0
    groups_per_tile = 2 * CHUNKS_PER_TILE
    front_ring = lax.rem(step, 3) * groups_per_tile
    solve_ring = lax.rem(step + 2, 3) * groups_per_tile
    scan_ring = lax.rem(step + 1, 3) * groups_per_tile
    front_mid = lax.rem(step, 2) * groups_per_tile
    solve_mid = groups_per_tile - front_mid

    @pl.when(step == 0)
    def _():
        for scr in (s_ref, a_scr, rhs_scr, brow_scr, u_scr, wq_scr, qkd_scr, kdt_scr, el_scr, r_ref):
            scr[...] = jnp.zeros_like(scr)
        _ret_tables(lg_row, decay_ref, xi_ref, zeta_ref)

    ri = lax.broadcasted_iota(jnp.int32, (CHUNK, CHUNK), 0)
    ci = lax.broadcasted_iota(jnp.int32, (CHUNK, CHUNK), 1)
    ri4 = lax.broadcasted_iota(jnp.int32, (CHUNK, N_HEADS * CHUNK), 0)
    ci4 = lax.broadcasted_iota(jnp.int32, (CHUNK, N_HEADS * CHUNK), 1) % CHUNK
    rbd = lax.broadcasted_iota(jnp.int32, (N_HEADS * CHUNK, N_HEADS * CHUNK), 0) // CHUNK
    cbd = lax.broadcasted_iota(jnp.int32, (N_HEADS * CHUNK, N_HEADS * CHUNK), 1) // CHUNK
    bd_mask = rbd == cbd
    eye4 = (ri4 == ci4).astype(F32)
    m16 = (ri4 // 16) == (ci4 // 16)
    m32 = jnp.logical_and((ri4 // 32) == (ci4 // 32), jnp.logical_not(m16))
    m64 = (ri4 // 32) != (ci4 // 32)

    cf = ((ci <= ri).astype(BF16), None, ri4 >= ci4, ri4 > ci4)
    cb = ((ci >= ri).astype(BF16), None, ri4 <= ci4, ri4 < ci4)
    dtb = dtb_ref[...]
    alog = alog_ref[...]

    groups = []
    for j in range(CHUNKS_PER_TILE):
        rf = j * CHUNK
        rb = (CHUNKS_PER_TILE - 1 - j) * CHUNK
        for ref, ab_ref, r0, rev, consts in ((qkvf_ref, abf_ref, rf, False, cf), (qkvb_ref, abb_ref, rb, True, cb)):
            groups.append(dict(q=ref[0, r0:r0 + CHUNK, 0:GROUP_W], k=ref[0, r0:r0 + CHUNK, GROUP_W:2 * GROUP_W],
                               v=ref[0, r0:r0 + CHUNK, 2 * GROUP_W:],
                               gates=_gate_tile(ab_ref[0, r0:r0 + CHUNK, :], dtb, alog), rev=rev, consts=consts))
    _interleave(
        _gdn_solve_stages(len(groups), (eye4, m16, m32, m64, bd_mask), solve_ring, solve_mid,
                          a_scr, rhs_scr, brow_scr, u_scr, wq_scr),
        _gdn_scan_stages(scan_ring, keep_state, u_scr, wq_scr, qkd_scr, kdt_scr, el_scr, s_ref, of_ref, ob_ref),
        _ret_stages(rqkvf_ref, rqkvb_ref, rof_ref, rob_ref, lg_row, keep_state, r_ref, decay_ref, xi_ref, zeta_ref),
        _gdn_front_stages(groups, front_ring, front_mid, a_scr, rhs_scr, brow_scr, wq_scr, qkd_scr, kdt_scr,
                          el_scr))


def _bwd_tile(s, n_tiles):
    return jnp.where(s == 0, 0, n_tiles - s)


def _latent_block(tile, bwd, n_tiles):
    return jnp.where(tile == 0, n_tiles - 2 if bwd else 0, tile - 1)


def _mixers(qkv, ab, dtb_row, alog_row, pr, logit_row):
    b, t_total, _ = qkv.shape
    n_tiles = t_total // TILE

    total = b * n_tiles

    def block_of(f, bwd, lag, latent):
        flat = jnp.maximum(f - 2, 0) if lag else jnp.minimum(f, total - 1)
        pos = flat % n_tiles
        tile = _bwd_tile(pos, n_tiles) if bwd else pos
        return (flat // n_tiles, _latent_block(tile, bwd, n_tiles) if latent else tile, 0)

    def tok(width, bwd, lag=False, latent=False):
        return pl.BlockSpec((1, TILE, width), lambda f: block_of(f, bwd, lag, latent))

    row = pl.BlockSpec((1, GATE_LANES), lambda f: (0, 0))
    per_tile = 2 * CHUNKS_PER_TILE
    n_groups = 3 * per_tile
    n_units = n_groups * N_HEADS
    n_mid = 2 * per_tile
    return pl.pallas_call(
        functools.partial(_gdn_kernel, tiles_per_sample=n_tiles),
        grid=(total + 2,),
        in_specs=[tok(3 * GROUP_W, False), tok(GATE_LANES, False),
                  tok(3 * GROUP_W, True), tok(GATE_LANES, True), row, row,
                  tok(3 * GROUP_W, False, lag=True), tok(3 * GROUP_W, True, lag=True), row],
        out_specs=[tok(GROUP_W, bwd, lag=True, latent=True) for bwd in (False, True, False, True)],
        out_shape=[jax.ShapeDtypeStruct((b, t_total - TILE, GROUP_W), BF16)] * 4,
        scratch_shapes=[pltpu.VMEM((2 * N_HEADS, HEAD_DIM, HEAD_DIM), F32),
                        pltpu.VMEM((n_mid, CHUNK, N_HEADS * CHUNK), F32),
                        pltpu.VMEM((n_mid * N_HEADS, CHUNK, 2 * HEAD_DIM), BF16),
                        pltpu.VMEM((n_mid, 1, N_HEADS * CHUNK), F32),
                        pltpu.VMEM((n_units, CHUNK, HEAD_DIM), F32),
                        pltpu.VMEM((n_units, 2 * CHUNK, HEAD_DIM), BF16),
                        pltpu.VMEM((n_groups, CHUNK, N_HEADS * CHUNK), BF16),
                        pltpu.VMEM((n_units, HEAD_DIM, CHUNK), BF16),
                        pltpu.VMEM((n_groups, 1, GATE_LANES), F32),
                        pltpu.VMEM((2 * N_HEADS, HEAD_DIM, HEAD_DIM), F32),
                        pltpu.VMEM((2 * N_HEADS, TILE, TILE), F32),
                        pltpu.VMEM((2 * N_HEADS, TILE, HEAD_DIM), F32),
                        pltpu.VMEM((2 * N_HEADS, TILE, HEAD_DIM), F32)],
        compiler_params=_cparams(("arbitrary",)),
        name="mixers",
    )(qkv, ab, qkv, ab, dtb_row, alog_row, pr, pr, logit_row)


def _log_sigmoid(x):
    return jnp.minimum(x, 0.0) - jnp.log(1.0 + jnp.exp(-jnp.abs(x)))


def _ret_tables(lg_row, decay_ref, xi_ref, zeta_ref):
    ri = lax.broadcasted_iota(jnp.int32, (TILE, TILE), 0)
    ci = lax.broadcasted_iota(jnp.int32, (TILE, TILE), 1)
    row = lax.broadcasted_iota(jnp.int32, (TILE, HEAD_DIM), 0).astype(F32)
    for slot in range(2 * N_HEADS):
        rev = slot >= N_HEADS
        lg = lg_row[:, slot:slot + 1]
        diff = ((ci - ri) if rev else (ri - ci)).astype(F32)
        pos = ((TILE - 1.0) - row) if rev else row
        decay_ref[slot] = jnp.where(diff >= 0, jnp.exp(lg * jnp.maximum(diff, 0.0)), 0.0)
        xi_ref[slot] = jnp.exp(lg * (pos + 1.0))
        zeta_ref[slot] = jnp.exp(lg * ((TILE - 1.0) - pos))


def _ret_stages(qkvf_ref, qkvb_ref, of_ref, ob_ref, lg_row, keep_state, r_ref, decay_ref, xi_ref, zeta_ref):
    units = [(refs, (N_HEADS if rev else 0) + h, h * HEAD_DIM)
             for refs, rev in (((qkvf_ref, of_ref), False), ((qkvb_ref, ob_ref), True))
             for h in range(N_HEADS)]
    qs = [refs[0][0, :, lo:lo + HEAD_DIM] for refs, _, lo in units]
    ks = [refs[0][0, :, GROUP_W + lo:GROUP_W + lo + HEAD_DIM] for refs, _, lo in units]
    vs = [refs[0][0, :, 2 * GROUP_W + lo:2 * GROUP_W + lo + HEAD_DIM] for refs, _, lo in units]
    per_yield = 2
    qks = []
    for i, (q, k) in enumerate(zip(qs, ks)):
        qks.append(_dot_nt(q, k))
        if i % per_yield == per_yield - 1:
            yield
    rs = [jnp.where(keep_state, r_ref[slot], 0.0) for _, slot, _ in units]
    for i, ((refs, slot, lo), q, v, qk, r) in enumerate(zip(units, qs, vs, qks, rs)):
        qkd = (qk * decay_ref[slot]).astype(BF16)
        refs[1][0, :, lo:lo + HEAD_DIM] = (
            _dot(qkd, v) + _dot((q * xi_ref[slot]).astype(BF16), r.astype(BF16))).astype(BF16)
        if i % per_yield == per_yield - 1:
            yield
    for i, ((_, slot, _), k, v, r) in enumerate(zip(units, ks, vs, rs)):
        g_chunk = jnp.exp(lg_row[:, slot:slot + 1] * float(TILE))
        r_ref[slot] = g_chunk * r + _dot((k * zeta_ref[slot]).T.astype(BF16), v)
        if i % per_yield == per_yield - 1:
            yield


def _tail_kernel(gf_ref, gb_ref, z_ref, rf_ref, rb_ref, rg_ref, x_ref, mod_ref, gdn_g_ref, ret_g_ref,
                 nffn_g_ref, w_ref, wi_ref, wo_ref, fg_ref, o_ref, y_ref, *, d_model, d_ff):
    o = gf_ref[0].astype(F32) + gb_ref[0].astype(F32)
    r = rf_ref[0].astype(F32) + rb_ref[0].astype(F32)
    z = z_ref[0].astype(F32)
    rg = rg_ref[0].astype(F32)
    for h in range(N_HEADS):
        lo = h * HEAD_DIM
        o_h = o[:, lo:lo + HEAD_DIM]
        o_n = o_h * lax.rsqrt(jnp.mean(o_h * o_h, axis=-1, keepdims=True) + NORM_EPS) * gdn_g_ref[...]
        y_ref[:, lo:lo + HEAD_DIM] = (o_n * _silu(z[:, lo:lo + HEAD_DIM])).astype(BF16)
        r_h = r[:, lo:lo + HEAD_DIM]
        mu = jnp.mean(r_h, axis=-1, keepdims=True)
        cen = r_h - mu
        var = jnp.mean(cen * cen, axis=-1, keepdims=True)
        r_n = cen * lax.rsqrt(var + NORM_EPS) * ret_g_ref[...]
        y_ref[:, GROUP_W + lo:GROUP_W + lo + HEAD_DIM] = (r_n * _silu(rg[:, lo:lo + HEAD_DIM])).astype(BF16)
    mod = mod_ref[0]
    x1 = x_ref[0] + mod[:, 2 * d_model:3 * d_model] * _dot(y_ref[...], w_ref[...])
    h2 = _rms_mod(x1, nffn_g_ref[...], mod[:, 3 * d_model:4 * d_model],
                  mod[:, 4 * d_model:5 * d_model]).astype(BF16)
    acc = None
    for lo in range(0, d_ff, FFN_CHUNK):
        hi = min(lo + FFN_CHUNK, d_ff)
        gate = _dot(h2, wi_ref[:, lo:hi])
        up = _dot(h2, wi_ref[:, d_ff + lo:d_ff + hi])
        part = _dot((_silu(gate) * up).astype(BF16), wo_ref[lo:hi, :])
        acc = part if acc is None else acc + part
    x2 = x1 + mod[:, 5 * d_model:6 * d_model] * acc
    ms = jnp.mean(x2 * x2, axis=-1, keepdims=True)
    o_ref[0] = x2 * lax.rsqrt(ms + NORM_EPS) * fg_ref[...]


def _tail(gdn_f, gdn_b, z, ret_f, ret_b, rg, x, mod3, gdn_g, ret_g, nffn_g, w_out, w_ffn_in, w_ffn_out,
          final_g):
    b, seq, d = x.shape
    d_ff = w_ffn_out.shape[0]
    tok = pl.BlockSpec((1, TAIL_TOK, GROUP_W), lambda i, t: (i, t, 0))

    def resident(shape):
        return pl.BlockSpec(shape, lambda i, t: (0,) * len(shape), pipeline_mode=pl.Buffered(1))

    return pl.pallas_call(
        functools.partial(_tail_kernel, d_model=d, d_ff=d_ff),
        grid=(b, seq // TAIL_TOK),
        in_specs=[tok, tok, tok, tok, tok, tok,
                  pl.BlockSpec((1, TAIL_TOK, d), lambda i, t: (i, t, 0)),
                  pl.BlockSpec((1, 1, mod3.shape[2]), lambda i, t: (i, 0, 0)),
                  resident((1, HEAD_DIM)), resident((1, HEAD_DIM)), resident((1, d)),
                  resident(w_out.shape), resident(w_ffn_in.shape), resident(w_ffn_out.shape),
                  resident((1, d))],
        out_specs=pl.BlockSpec((1, TAIL_TOK, d), lambda i, t: (i, t, 0)),
        out_shape=jax.ShapeDtypeStruct((b, seq, d), F32),
        scratch_shapes=[pltpu.VMEM((TAIL_TOK, 2 * GROUP_W), BF16)],
        compiler_params=_cparams(("parallel", "parallel")),
        name="tail",
    )(gdn_f, gdn_b, z, ret_f, ret_b, rg, x, mod3, gdn_g, ret_g, nffn_g, w_out, w_ffn_in, w_ffn_out, final_g)


def _rope_tables(ctx_len, n_lat):
    def angles(pos, n_pairs):
        inv = ROPE_THETA ** (-np.arange(n_pairs, dtype=np.float64) / n_pairs)
        return pos[:, None] * inv[None, :]

    rows = n_lat // GRID_W
    row = np.repeat(np.arange(rows, dtype=np.float64), GRID_W)
    col = np.tile(np.arange(GRID_W, dtype=np.float64), rows)
    zeros = np.zeros((ctx_len,), np.float64)
    p_seq = np.concatenate([np.arange(ctx_len, dtype=np.float64), np.full((n_lat,), float(ctx_len))])
    ang = np.concatenate([angles(p_seq, ROPE_PAIRS[0]),
                          angles(np.concatenate([zeros, row]), ROPE_PAIRS[1]),
                          angles(np.concatenate([zeros, col]), ROPE_PAIRS[2])], axis=-1)
    cos, sin = np.cos(ang).astype(np.float32), np.sin(ang).astype(np.float32)
    return jnp.asarray(np.concatenate([cos, cos, -sin, sin], axis=-1))


def _lane_row(values):
    flat = values.reshape(1, -1).astype(F32)
    return jnp.pad(flat, ((0, 0), (0, GATE_LANES - flat.shape[1])))


def kernel(x, c, ctx, c_ctx, ada_w, ada_b, norm_mix_g, norm_ffn_g, w_in, conv_w, gdn_a_log, gdn_dt_bias,
           gdn_norm_g, ret_decay_logit, ret_norm_g, w_out, w_ffn_in, w_ffn_out, final_g):
    assert ada_w.shape[0] == 1, "single-layer block"
    b, seq, d = x.shape
    ctx_len = ctx.shape[1]
    assert ctx_len == TILE and seq % TILE == 0 and seq % TAIL_TOK == 0 and b + 1 <= 8

    cond = jnp.concatenate([c, c_ctx[None, :], jnp.zeros((8 - b - 1, d), F32)], axis=0)
    mod = _ada(cond, ada_w, ada_b)
    mod3 = mod[:, None, :]

    n_qkvz = 4 * GROUP_W
    n_gate = 4 * N_HEADS
    qkv, z, ab, pr, rg = _inproj(x, ctx, mod3, norm_mix_g, jnp.swapaxes(w_in, 1, 2), conv_w, n_qkvz, n_gate,
                                 _rope_tables(ctx_len, seq))

    dtb_row = _lane_row(gdn_dt_bias[0])
    gdn_f, gdn_b, ret_f, ret_b = _mixers(qkv, ab, dtb_row, _lane_row(gdn_a_log[0]), pr,
                                         _lane_row(ret_decay_logit[0]))

    return _tail(gdn_f, gdn_b, z, ret_f, ret_b, rg, x, mod3, gdn_norm_g, ret_norm_g, norm_ffn_g,
                 w_out[0].astype(BF16), w_ffn_in[0].astype(BF16), w_ffn_out[0].astype(BF16), final_g[None, :])
```

```python
import functools
import math

import jax
import jax.numpy as jnp
import numpy as np
from jax import lax
from jax.experimental import pallas as pl
from jax.experimental.pallas import tpu as pltpu

F32 = jnp.float32
BF16 = jnp.bfloat16

HEAD_DIM = 128
N_HEADS = 4
GROUP_W = N_HEADS * HEAD_DIM
CONV_K = 5
GRID_W = 64
ROPE_THETA = 10000.0
ROPE_PAIRS = (16, 24, 24)
NORM_EPS = 1e-6

CHUNK = 64
TILE = 256
CHUNKS_PER_TILE = TILE // CHUNK
HALO = 16
GATE_LANES = 128
TAIL_TOK = 512
FFN_CHUNK = 1024
VMEM_LIMIT = 56 * 1024 * 1024


def _cparams(sem):
    return pltpu.CompilerParams(dimension_semantics=sem, vmem_limit_bytes=VMEM_LIMIT)


def _dot(a, b):
    return jnp.dot(a, b, preferred_element_type=F32)


def _dot_nt(a, b):
    return lax.dot_general(a, b, (((1,), (1,)), ((), ())), preferred_element_type=F32)


def _split(a):
    hi = a.astype(BF16)
    lo = (a - hi.astype(F32)).astype(BF16)
    return hi, lo


def _silu(x):
    return x * jax.nn.sigmoid(x)


def _lane_bcast(col, width):
    return jnp.broadcast_to(col, (col.shape[0], width))


def _ada_kernel(cond_ref, w_ref, b_ref, o_ref):
    a_hi, a_lo = _split(_silu(cond_ref[...]))
    w_hi, w_lo = _split(w_ref[0])
    o_ref[...] = _dot(a_hi, w_hi) + _dot(a_lo, w_hi) + _dot(a_hi, w_lo) + b_ref[...]


def _ada(cond, w, b):
    rows, d = cond.shape
    n = w.shape[2]
    bn = 1536
    return pl.pallas_call(
        _ada_kernel,
        grid=(n // bn,),
        in_specs=[pl.BlockSpec((rows, d), lambda j: (0, 0)),
                  pl.BlockSpec((1, d, bn), lambda j: (0, 0, j)),
                  pl.BlockSpec((1, bn), lambda j: (0, j))],
        out_specs=pl.BlockSpec((rows, bn), lambda j: (0, j)),
        out_shape=jax.ShapeDtypeStruct((rows, n), F32),
        compiler_params=_cparams(("parallel",)),
        name="ada",
    )(cond, w, b)


def _rms_mod(x, g, shift, scale):
    ms = jnp.mean(x * x, axis=-1, keepdims=True)
    return (x * lax.rsqrt(ms + NORM_EPS) * g) * (1.0 + scale) + shift


def _conv_tile(ext_ref, w_ref, o_ref):
    pad = (CONV_K - 1) // 2
    rows = TILE + 2 * HALO
    q_scale = HEAD_DIM ** -0.5
    for part in range(3):
        for hd in range(N_HEADS):
            lo = part * GROUP_W + hd * HEAD_DIM
            slab = ext_ref[:, lo:lo + HEAD_DIM]
            acc = None
            for i in range(CONV_K):
                shifted = slab if i == pad else pltpu.roll(slab, (pad - i) % rows, 0)
                term = shifted[HALO:HALO + TILE] * w_ref[0, i:i + 1, lo:lo + HEAD_DIM]
                acc = term if acc is None else acc + term
            y = _silu(acc)
            if part < 2:
                y = y * lax.rsqrt(jnp.sum(y * y, axis=-1, keepdims=True) + NORM_EPS)
            if part == 0:
                y = y * q_scale
            o_ref[0, :, lo:lo + HEAD_DIM] = y.astype(BF16)


def _inproj_kernel(x_ref, ctx_ref, mod_ref, g_ref, w_ref, cw_ref, rope_ref,
                   qkv_ref, z_ref, ab_ref, pr_ref, rg_ref, wa_ref, wab_ref, wr_ref, ext_ref,
                   *, d_model, n_half, n_gate, tiles_per_sample):
    step = pl.program_id(0)
    t = lax.rem(jnp.minimum(step, pl.num_programs(0) - 2), tiles_per_sample)
    t_conv = lax.rem(jnp.maximum(step - 1, 0), tiles_per_sample)
    width = 3 * GROUP_W

    @pl.when(step == 0)
    def _():
        wa_ref[...] = w_ref[0, :n_half, :].T.astype(BF16)
        wr_ref[...] = w_ref[0, n_half + n_gate:, :].T.astype(BF16)
        lane = lax.broadcasted_iota(jnp.int32, (d_model, GATE_LANES), 1)
        wab_ref[...] = jnp.where(lane < n_gate, w_ref[0, n_half:n_half + GATE_LANES, :].T, 0.0).astype(BF16)
        ext_ref[...] = jnp.zeros_like(ext_ref)

    xin = jnp.where(t == 0, ctx_ref[0], x_ref[0])
    mod = mod_ref[0]
    h = _rms_mod(xin, g_ref[...], mod[:, 0:d_model], mod[:, d_model:2 * d_model]).astype(BF16)
    pa = _dot(h, wa_ref[...])
    z_ref[0] = pa[:, width:].astype(BF16)

    has_next = jnp.logical_and(t_conv >= 1, t_conv < tiles_per_sample - 1)
    ext_ref[HALO + TILE:, :] = jnp.where(has_next, pa[:HALO, :width], 0.0)
    _conv_tile(ext_ref, cw_ref, qkv_ref)
    ext_ref[0:HALO, :] = jnp.where(t >= 2, ext_ref[TILE:HALO + TILE, :], 0.0)
    ext_ref[HALO:HALO + TILE, :] = pa[:, :width]

    ab_ref[0] = _dot(h, wab_ref[...])
    pr = _dot(h, wr_ref[...])
    cos2 = rope_ref[:, :HEAD_DIM]
    sin2 = rope_ref[:, HEAD_DIM:]
    k_scale = HEAD_DIM ** -0.5
    for hd in range(2 * N_HEADS):
        lo = hd * HEAD_DIM
        tt = pr[:, lo:lo + HEAD_DIM]
        rot = tt * cos2 + pltpu.roll(tt, HEAD_DIM // 2, 1) * sin2
        if hd >= N_HEADS:
            rot = rot * k_scale
        pr_ref[0, :, lo:lo + HEAD_DIM] = rot.astype(BF16)
    pr_ref[0, :, 2 * GROUP_W:] = pr[:, 2 * GROUP_W:width].astype(BF16)
    rg_ref[0] = pr[:, width:].astype(BF16)


def _inproj(x, ctx, mod3, g, w_in, conv_w, n_half, n_gate, rope):
    b, seq, d = x.shape
    n_tiles = (ctx.shape[1] + seq) // TILE
    t_total = n_tiles * TILE
    total = b * n_tiles
    width = 3 * GROUP_W
    assert w_in.shape[1] == 2 * n_half + n_gate and n_gate <= GATE_LANES and n_half == width + GROUP_W

    def now(f):
        flat = jnp.minimum(f, total - 1)
        return flat // n_tiles, flat % n_tiles

    def lagged(f):
        flat = jnp.maximum(f - 1, 0)
        return flat // n_tiles, flat % n_tiles

    def tok(cols):
        return pl.BlockSpec((1, TILE, cols), lambda f: (now(f)[0], now(f)[1], 0))

    def lat(cols):
        return pl.BlockSpec((1, TILE, cols), lambda f: (now(f)[0], jnp.maximum(now(f)[1] - 1, 0), 0))

    return pl.pallas_call(
        functools.partial(_inproj_kernel, d_model=d, n_half=n_half, n_gate=n_gate, tiles_per_sample=n_tiles),
        grid=(total + 1,),
        in_specs=[pl.BlockSpec((1, TILE, d), lambda f: (now(f)[0], jnp.maximum(now(f)[1] - 1, 0), 0)),
                  pl.BlockSpec((1, TILE, d), lambda f: (now(f)[0], 0, 0)),
                  pl.BlockSpec((1, 1, mod3.shape[2]), lambda f: (jnp.where(now(f)[1] == 0, b, now(f)[0]), 0, 0)),
                  pl.BlockSpec((1, d), lambda f: (0, 0)),
                  pl.BlockSpec(w_in.shape, lambda f: (0, 0, 0), pipeline_mode=pl.Buffered(1)),
                  pl.BlockSpec(conv_w.shape, lambda f: (0, 0, 0)),
                  pl.BlockSpec((TILE, 2 * HEAD_DIM), lambda f: (now(f)[1], 0))],
        out_specs=[pl.BlockSpec((1, TILE, width), lambda f: (lagged(f)[0], lagged(f)[1], 0)),
                   lat(GROUP_W), tok(GATE_LANES), tok(width), lat(GROUP_W)],
        out_shape=[jax.ShapeDtypeStruct((b, t_total, width), BF16),
                   jax.ShapeDtypeStruct((b, seq, GROUP_W), BF16),
                   jax.ShapeDtypeStruct((b, t_total, GATE_LANES), F32),
                   jax.ShapeDtypeStruct((b, t_total, width), BF16),
                   jax.ShapeDtypeStruct((b, seq, GROUP_W), BF16)],
        scratch_shapes=[pltpu.VMEM((d, n_half), BF16), pltpu.VMEM((d, GATE_LANES), BF16),
                        pltpu.VMEM((d, n_half), BF16), pltpu.VMEM((TILE + 2 * HALO, width), F32)],
        compiler_params=_cparams(("arbitrary",)),
        name="inproj",
    )(x, ctx, mod3, g, w_in, conv_w, rope)


def _block_diag(y, bd_mask):
    yb = y.astype(BF16)
    return jnp.where(bd_mask, jnp.concatenate([yb] * N_HEADS, axis=0), jnp.zeros((), BF16))


def _packed_matmul(x, y_bd):
    return _dot(x.astype(BF16), y_bd)


def _unit_lower_inverse_stages(a_list, eye, m16, m32, m64, bd_mask, out):
    pm = _packed_matmul
    ps = [-jnp.where(m16, a, 0.0) for a in a_list]
    xs = [eye + p for p in ps]
    ps = [p.astype(BF16) for p in ps]
    bds = [_block_diag(p, bd_mask) for p in ps]
    for _ in range(3):
        ps = [pm(p, bd).astype(BF16) for p, bd in zip(ps, bds)]
        yield
        bds = [_block_diag(p, bd_mask) for p in ps]
        xs = [x + pm(x, bd) for x, bd in zip(xs, bds)]
        yield
    for m in (m32, m64):
        zs = [pm(jnp.where(m, a, 0.0), _block_diag(x, bd_mask)).astype(BF16) for a, x in zip(a_list, xs)]
        yield
        xs = [x - pm(x, _block_diag(z, bd_mask)) for x, z in zip(xs, zs)]
        yield
    out.extend(xs)


def _gdn_front_stages(groups, ring_base, mid_base, a_scr, rhs_scr, brow_scr, wq_scr, qkd_scr, kdt_scr,
                      el_scr):
    lane = lax.broadcasted_iota(jnp.int32, (CHUNK, GATE_LANES), 1)
    low_half = lax.broadcasted_iota(jnp.int32, (CHUNK, HEAD_DIM), 1) < CHUNK
    zero_k = jnp.zeros((CHUNK, HEAD_DIM), BF16)
    for g in groups:
        tcum = g["consts"][0]
        g_hi, g_lo = _split(g["gates"])
        g["gc"] = _dot(tcum, g_hi) + _dot(tcum, g_lo)
    yield

    for gi, g in enumerate(groups):
        g_lane0 = N_HEADS if g["rev"] else 0
        b_lane0 = 2 * N_HEADS + g_lane0
        last = 0 if g["rev"] else CHUNK - 1
        gc, gates = g["gc"], g["gates"]
        ct = jnp.where(lane < 2 * N_HEADS, gc, gates).T
        egl_t = jnp.exp(ct[:, last:last + 1] - ct)
        gcb, kb, ms = [], [], []
        for h in range(N_HEADS):
            lo = h * HEAD_DIM
            gl = g_lane0 + h
            ring_unit = (ring_base + gi) * N_HEADS + h
            gcb.append(_lane_bcast(gc[:, gl:gl + 1], HEAD_DIM))
            beta = _lane_bcast(gates[:, b_lane0 + h:b_lane0 + h + 1], HEAD_DIM)
            eg_h = jnp.exp(gcb[h])
            q_h = g["q"][:, lo:lo + HEAD_DIM]
            k_h = g["k"][:, lo:lo + HEAD_DIM].astype(F32)
            kb.append((k_h * beta).astype(BF16))
            rhs_scr[(mid_base + gi) * N_HEADS + h] = jnp.concatenate(
                [g["v"][:, lo:lo + HEAD_DIM], (k_h * eg_h).astype(BF16)], axis=1)
            wq_scr[ring_unit, CHUNK:, :] = (q_h * eg_h).astype(BF16)
            kdt_scr[ring_unit] = (k_h.T * egl_t[gl:gl + 1, :]).astype(BF16)
        d_parts = []
        for p in range(N_HEADS // 2):
            h0, h1 = 2 * p, 2 * p + 1
            lo = h0 * HEAD_DIM
            lhs = jnp.concatenate([g["q"][:, lo:lo + 2 * HEAD_DIM], jnp.concatenate([kb[h0], kb[h1]], axis=1)],
                                  axis=0)
            k0 = g["k"][:, lo:lo + HEAD_DIM]
            k1 = g["k"][:, lo + HEAD_DIM:lo + 2 * HEAD_DIM]
            rhs_bd = jnp.concatenate([jnp.concatenate([k0, zero_k], axis=1),
                                      jnp.concatenate([zero_k, k1], axis=1)], axis=0)
            ms.append(_dot_nt(lhs, rhs_bd))
            g_row = jnp.concatenate([ct[g_lane0 + h0:g_lane0 + h0 + 1, :], ct[g_lane0 + h1:g_lane0 + h1 + 1, :]],
                                    axis=1)
            d_parts.append(jnp.where(low_half, gcb[h0], gcb[h1]) - g_row)
        incl4, strict4 = g["consts"][2:]
        e = jnp.exp(jnp.where(incl4, jnp.concatenate(d_parts, axis=1), 0.0))
        a_scr[mid_base + gi] = jnp.where(strict4, jnp.concatenate([m[CHUNK:] for m in ms], axis=1) * e, 0.0)
        qkd_scr[ring_base + gi] = jnp.where(incl4, jnp.concatenate([m[:CHUNK] for m in ms], axis=1) * e,
                                            0.0).astype(BF16)
        el_scr[ring_base + gi] = jnp.exp(gc[last:last + 1, :])
        brow_scr[mid_base + gi] = jnp.concatenate(
            [ct[b_lane0 + h:b_lane0 + h + 1, :] for h in range(N_HEADS)], axis=1)
        yield


def _gdn_solve_stages(n_groups, masks, ring_base, mid_base, a_scr, rhs_scr, brow_scr, u_scr, wq_scr):
    eye4, m16, m32, m64, bd_mask = masks
    t_invs = []
    yield from _unit_lower_inverse_stages([a_scr[mid_base + gi] for gi in range(n_groups)],
                                          eye4, m16, m32, m64, bd_mask, t_invs)
    for gi, t_inv in enumerate(t_invs):
        t_b = (t_inv * brow_scr[mid_base + gi]).astype(BF16)
        for h in range(N_HEADS):
            sol = _dot(t_b[:, h * CHUNK:(h + 1) * CHUNK], rhs_scr[(mid_base + gi) * N_HEADS + h])
            ring_unit = (ring_base + gi) * N_HEADS + h
            u_scr[ring_unit] = sol[:, :HEAD_DIM]
            wq_scr[ring_unit, :CHUNK, :] = sol[:, HEAD_DIM:].astype(BF16)


def _gdn_scan_stages(slot_base, keep_state, u_scr, wq_scr, qkd_scr, kdt_scr, el_scr, s_ref, of_ref, ob_ref):
    for j in range(CHUNKS_PER_TILE):
        units = [(slot_base + 2 * j + d, h, d * N_HEADS + h) for d in range(2) for h in range(N_HEADS)]
        states = [jnp.where(keep_state, s_ref[slot], 0.0) if j == 0 else s_ref[slot] for _, _, slot in units]
        wss = [_dot(wq_scr[gi * N_HEADS + h], s.astype(BF16)) for (gi, h, _), s in zip(units, states)]
        yield
        v_news = [(u_scr[gi * N_HEADS + h] - ws[:CHUNK]).astype(BF16) for (gi, h, _), ws in zip(units, wss)]
        outs = [ws[CHUNK:] + _dot(qkd_scr[gi][:, h * CHUNK:(h + 1) * CHUNK], vn)
                for (gi, h, _), ws, vn in zip(units, wss, v_news)]
        for (gi, h, slot), s, vn in zip(units, states, v_news):
            s_ref[slot] = el_scr[gi][:, slot:slot + 1] * s + _dot(kdt_scr[gi * N_HEADS + h], vn)
        rf = j * CHUNK
        rb = (CHUNKS_PER_TILE - 1 - j) * CHUNK
        of_ref[0, rf:rf + CHUNK, :] = jnp.concatenate(outs[:N_HEADS], axis=1).astype(BF16)
        ob_ref[0, rb:rb + CHUNK, :] = jnp.concatenate(outs[N_HEADS:], axis=1).astype(BF16)
        yield


def _interleave(*stage_generators):
    live = list(stage_generators)
    while live:
        for gen in list(live):
            try:
                next(gen)
            except StopIteration:
                live.remove(gen)


def _gate_tile(ab, dtb, a_log):
    z = ab + dtb
    softplus = jnp.maximum(z, 0.0) + jnp.log(1.0 + jnp.exp(-jnp.abs(z)))
    lane = lax.broadcasted_iota(jnp.int32, ab.shape, 1)
    return jnp.where(lane < 2 * N_HEADS, -jnp.exp(a_log) * softplus, jax.nn.sigmoid(ab))


def _gdn_kernel(qkvf_ref, abf_ref, qkvb_ref, abb_ref, dtb_ref, alog_ref, rqkvf_ref, rqkvb_ref, logit_ref,
                of_ref, ob_ref, rof_ref, rob_ref,
                s_ref, a_scr, rhs_scr, brow_scr, u_scr, wq_scr, qkd_scr, kdt_scr, el_scr,
                r_ref, decay_ref, xi_ref, zeta_ref, *, tiles_per_sample):
    step = pl.program_id(0)
    lg_row = _log_sigmoid(logit_ref[...])
    keep_state = lax.rem(step + tiles_per_sample - 2, tiles_per_sample) != 0
    groups_per_tile = 2 * CHUNKS_PER_TILE
    front_ring = lax.rem(step, 3) * groups_per_tile
    solve_ring = lax.rem(step + 2, 3) * groups_per_tile
    scan_ring = lax.rem(step + 1, 3) * groups_per_tile
    front_mid = lax.rem(step, 2) * groups_per_tile
    solve_mid = groups_per_tile - front_mid

    @pl.when(step == 0)
    def _():
        for scr in (s_ref, a_scr, rhs_scr, brow_scr, u_scr, wq_scr, qkd_scr, kdt_scr, el_scr, r_ref):
            scr[...] = jnp.zeros_like(scr)
        _ret_tables(lg_row, decay_ref, xi_ref, zeta_ref)

    ri = lax.broadcasted_iota(jnp.int32, (CHUNK, CHUNK), 0)
    ci = lax.broadcasted_iota(jnp.int32, (CHUNK, CHUNK), 1)
    ri4 = lax.broadcasted_iota(jnp.int32, (CHUNK, N_HEADS * CHUNK), 0)
    ci4 = lax.broadcasted_iota(jnp.int32, (CHUNK, N_HEADS * CHUNK), 1) % CHUNK
    rbd = lax.broadcasted_iota(jnp.int32, (N_HEADS * CHUNK, N_HEADS * CHUNK), 0) // CHUNK
    cbd = lax.broadcasted_iota(jnp.int32, (N_HEADS * CHUNK, N_HEADS * CHUNK), 1) // CHUNK
    bd_mask = rbd == cbd
    eye4 = (ri4 == ci4).astype(F32)
    m16 = (ri4 // 16) == (ci4 // 16)
    m32 = jnp.logical_and((ri4 // 32) == (ci4 // 32), jnp.logical_not(m16))
    m64 = (ri4 // 32) != (ci4 // 32)

    cf = ((ci <= ri).astype(BF16), None, ri4 >= ci4, ri4 > ci4)
    cb = ((ci >= ri).astype(BF16), None, ri4 <= ci4, ri4 < ci4)
    dtb = dtb_ref[...]
    alog = alog_ref[...]

    groups = []
    for j in range(CHUNKS_PER_TILE):
        rf = j * CHUNK
        rb = (CHUNKS_PER_TILE - 1 - j) * CHUNK
        for ref, ab_ref, r0, rev, consts in ((qkvf_ref, abf_ref, rf, False, cf), (qkvb_ref, abb_ref, rb, True, cb)):
            groups.append(dict(q=ref[0, r0:r0 + CHUNK, 0:GROUP_W], k=ref[0, r0:r0 + CHUNK, GROUP_W:2 * GROUP_W],
                               v=ref[0, r0:r0 + CHUNK, 2 * GROUP_W:],
                               gates=_gate_tile(ab_ref[0, r0:r0 + CHUNK, :], dtb, alog), rev=rev, consts=consts))
    _interleave(
        _ret_stages(rqkvf_ref, rqkvb_ref, rof_ref, rob_ref, lg_row, keep_state, r_ref, decay_ref, xi_ref, zeta_ref),
        _gdn_solve_stages(len(groups), (eye4, m16, m32, m64, bd_mask), solve_ring, solve_mid,
                          a_scr, rhs_scr, brow_scr, u_scr, wq_scr),
        _gdn_scan_stages(scan_ring, keep_state, u_scr, wq_scr, qkd_scr, kdt_scr, el_scr, s_ref, of_ref, ob_ref),
        _gdn_front_stages(groups, front_ring, front_mid, a_scr, rhs_scr, brow_scr, wq_scr, qkd_scr, kdt_scr,
                          el_scr))


def _bwd_tile(s, n_tiles):
    return jnp.where(s == 0, 0, n_tiles - s)


def _latent_block(tile, bwd, n_tiles):
    return jnp.where(tile == 0, n_tiles - 2 if bwd else 0, tile - 1)


def _mixers(qkv, ab, dtb_row, alog_row, pr, logit_row):
    b, t_total, _ = qkv.shape
    n_tiles = t_total // TILE

    total = b * n_tiles

    def block_of(f, bwd, lag, latent):
        flat = jnp.maximum(f - 2, 0) if lag else jnp.minimum(f, total - 1)
        pos = flat % n_tiles
        tile = _bwd_tile(pos, n_tiles) if bwd else pos
        return (flat // n_tiles, _latent_block(tile, bwd, n_tiles) if latent else tile, 0)

    def tok(width, bwd, lag=False, latent=False):
        return pl.BlockSpec((1, TILE, width), lambda f: block_of(f, bwd, lag, latent))

    row = pl.BlockSpec((1, GATE_LANES), lambda f: (0, 0))
    per_tile = 2 * CHUNKS_PER_TILE
    n_groups = 3 * per_tile
    n_units = n_groups * N_HEADS
    n_mid = 2 * per_tile
    return pl.pallas_call(
        functools.partial(_gdn_kernel, tiles_per_sample=n_tiles),
        grid=(total + 2,),
        in_specs=[tok(3 * GROUP_W, False), tok(GATE_LANES, False),
                  tok(3 * GROUP_W, True), tok(GATE_LANES, True), row, row,
                  tok(3 * GROUP_W, False, lag=True), tok(3 * GROUP_W, True, lag=True), row],
        out_specs=[tok(GROUP_W, bwd, lag=True, latent=True) for bwd in (False, True, False, True)],
        out_shape=[jax.ShapeDtypeStruct((b, t_total - TILE, GROUP_W), BF16)] * 4,
        scratch_shapes=[pltpu.VMEM((2 * N_HEADS, HEAD_DIM, HEAD_DIM), F32),
                        pltpu.VMEM((n_mid, CHUNK, N_HEADS * CHUNK), F32),
                        pltpu.VMEM((n_mid * N_HEADS, CHUNK, 2 * HEAD_DIM), BF16),
                        pltpu.VMEM((n_mid, 1, N_HEADS * CHUNK), F32),
                        pltpu.VMEM((n_units, CHUNK, HEAD_DIM), F32),
                        pltpu.VMEM((n_units, 2 * CHUNK, HEAD_DIM), BF16),
                        pltpu.VMEM((n_groups, CHUNK, N_HEADS * CHUNK), BF16),
                        pltpu.VMEM((n_units, HEAD_DIM, CHUNK), BF16),
                        pltpu.VMEM((n_groups, 1, GATE_LANES), F32),
                        pltpu.VMEM((2 * N_HEADS, HEAD_DIM, HEAD_DIM), F32),
                        pltpu.VMEM((2 * N_HEADS, TILE, TILE), F32),
                        pltpu.VMEM((2 * N_HEADS, TILE, HEAD_DIM), F32),
                        pltpu.VMEM((2 * N_HEADS, TILE, HEAD_DIM), F32)],
        compiler_params=_cparams(("arbitrary",)),
        name="mixers",
    )(qkv, ab, qkv, ab, dtb_row, alog_row, pr, pr, logit_row)


def _log_sigmoid(x):
    return jnp.minimum(x, 0.0) - jnp.log(1.0 + jnp.exp(-jnp.abs(x)))


def _ret_tables(lg_row, decay_ref, xi_ref, zeta_ref):
    ri = lax.broadcasted_iota(jnp.int32, (TILE, TILE), 0)
    ci = lax.broadcasted_iota(jnp.int32, (TILE, TILE), 1)
    row = lax.broadcasted_iota(jnp.int32, (TILE, HEAD_DIM), 0).astype(F32)
    for slot in range(2 * N_HEADS):
        rev = slot >= N_HEADS
        lg = lg_row[:, slot:slot + 1]
        diff = ((ci - ri) if rev else (ri - ci)).astype(F32)
        pos = ((TILE - 1.0) - row) if rev else row
        decay_ref[slot] = jnp.where(diff >= 0, jnp.exp(lg * jnp.maximum(diff, 0.0)), 0.0)
        xi_ref[slot] = jnp.exp(lg * (pos + 1.0))
        zeta_ref[slot] = jnp.exp(lg * ((TILE - 1.0) - pos))


def _ret_stages(qkvf_ref, qkvb_ref, of_ref, ob_ref, lg_row, keep_state, r_ref, decay_ref, xi_ref, zeta_ref):
    units = [(refs, (N_HEADS if rev else 0) + h, h * HEAD_DIM)
             for refs, rev in (((qkvf_ref, of_ref), False), ((qkvb_ref, ob_ref), True))
             for h in range(N_HEADS)]
    qs = [refs[0][0, :, lo:lo + HEAD_DIM] for refs, _, lo in units]
    ks = [refs[0][0, :, GROUP_W + lo:GROUP_W + lo + HEAD_DIM] for refs, _, lo in units]
    vs = [refs[0][0, :, 2 * GROUP_W + lo:2 * GROUP_W + lo + HEAD_DIM] for refs, _, lo in units]
    per_yield = 1
    qks = []
    for i, (q, k) in enumerate(zip(qs, ks)):
        qks.append(_dot_nt(q, k))
        if i % per_yield == per_yield - 1:
            yield
    rs = [jnp.where(keep_state, r_ref[slot], 0.0) for _, slot, _ in units]
    for i, ((refs, slot, lo), q, v, qk, r) in enumerate(zip(units, qs, vs, qks, rs)):
        qkd = (qk * decay_ref[slot]).astype(BF16)
        refs[1][0, :, lo:lo + HEAD_DIM] = (
            _dot(qkd, v) + _dot((q * xi_ref[slot]).astype(BF16), r.astype(BF16))).astype(BF16)
        if i % per_yield == per_yield - 1:
            yield
    for i, ((_, slot, _), k, v, r) in enumerate(zip(units, ks, vs, rs)):
        g_chunk = jnp.exp(lg_row[:, slot:slot + 1] * float(TILE))
        r_ref[slot] = g_chunk * r + _dot((k * zeta_ref[slot]).T.astype(BF16), v)
        if i % per_yield == per_yield - 1:
            yield


def _tail_kernel(gf_ref, gb_ref, z_ref, rf_ref, rb_ref, rg_ref, x_ref, mod_ref, gdn_g_ref, ret_g_ref,
                 nffn_g_ref, w_ref, wi_ref, wo_ref, fg_ref, o_ref, y_ref, *, d_model, d_ff):
    o = gf_ref[0].astype(F32) + gb_ref[0].astype(F32)
    r = rf_ref[0].astype(F32) + rb_ref[0].astype(F32)
    z = z_ref[0].astype(F32)
    rg = rg_ref[0].astype(F32)
    for h in range(N_HEADS):
        lo = h * HEAD_DIM
        o_h = o[:, lo:lo + HEAD_DIM]
        o_n = o_h * lax.rsqrt(jnp.mean(o_h * o_h, axis=-1, keepdims=True) + NORM_EPS) * gdn_g_ref[...]
        y_ref[:, lo:lo + HEAD_DIM] = (o_n * _silu(z[:, lo:lo + HEAD_DIM])).astype(BF16)
        r_h = r[:, lo:lo + HEAD_DIM]
        mu = jnp.mean(r_h, axis=-1, keepdims=True)
        cen = r_h - mu
        var = jnp.mean(cen * cen, axis=-1, keepdims=True)
        r_n = cen * lax.rsqrt(var + NORM_EPS) * ret_g_ref[...]
        y_ref[:, GROUP_W + lo:GROUP_W + lo + HEAD_DIM] = (r_n * _silu(rg[:, lo:lo + HEAD_DIM])).astype(BF16)
    mod = mod_ref[0]
    x1 = x_ref[0] + mod[:, 2 * d_model:3 * d_model] * _dot(y_ref[...], w_ref[...])
    h2 = _rms_mod(x1, nffn_g_ref[...], mod[:, 3 * d_model:4 * d_model],
                  mod[:, 4 * d_model:5 * d_model]).astype(BF16)
    acc = None
    for lo in range(0, d_ff, FFN_CHUNK):
        hi = min(lo + FFN_CHUNK, d_ff)
        gate = _dot(h2, wi_ref[:, lo:hi])
        up = _dot(h2, wi_ref[:, d_ff + lo:d_ff + hi])
        part = _dot((_silu(gate) * up).astype(BF16), wo_ref[lo:hi, :])
        acc = part if acc is None else acc + part
    x2 = x1 + mod[:, 5 * d_model:6 * d_model] * acc
    ms = jnp.mean(x2 * x2, axis=-1, keepdims=True)
    o_ref[0] = x2 * lax.rsqrt(ms + NORM_EPS) * fg_ref[...]


def _tail(gdn_f, gdn_b, z, ret_f, ret_b, rg, x, mod3, gdn_g, ret_g, nffn_g, w_out, w_ffn_in, w_ffn_out,
          final_g):
    b, seq, d = x.shape
    d_ff = w_ffn_out.shape[0]
    tok = pl.BlockSpec((1, TAIL_TOK, GROUP_W), lambda i, t: (i, t, 0))

    def resident(shape):
        return pl.BlockSpec(shape, lambda i, t: (0,) * len(shape), pipeline_mode=pl.Buffered(1))

    return pl.pallas_call(
        functools.partial(_tail_kernel, d_model=d, d_ff=d_ff),
        grid=(b, seq // TAIL_TOK),
        in_specs=[tok, tok, tok, tok, tok, tok,
                  pl.BlockSpec((1, TAIL_TOK, d), lambda i, t: (i, t, 0)),
                  pl.BlockSpec((1, 1, mod3.shape[2]), lambda i, t: (i, 0, 0)),
                  resident((1, HEAD_DIM)), resident((1, HEAD_DIM)), resident((1, d)),
                  resident(w_out.shape), resident(w_ffn_in.shape), resident(w_ffn_out.shape),
                  resident((1, d))],
        out_specs=pl.BlockSpec((1, TAIL_TOK, d), lambda i, t: (i, t, 0)),
        out_shape=jax.ShapeDtypeStruct((b, seq, d), F32),
        scratch_shapes=[pltpu.VMEM((TAIL_TOK, 2 * GROUP_W), BF16)],
        compiler_params=_cparams(("parallel", "parallel")),
        name="tail",
    )(gdn_f, gdn_b, z, ret_f, ret_b, rg, x, mod3, gdn_g, ret_g, nffn_g, w_out, w_ffn_in, w_ffn_out, final_g)


def _rope_tables(ctx_len, n_lat):
    def angles(pos, n_pairs):
        inv = ROPE_THETA ** (-np.arange(n_pairs, dtype=np.float64) / n_pairs)
        return pos[:, None] * inv[None, :]

    rows = n_lat // GRID_W
    row = np.repeat(np.arange(rows, dtype=np.float64), GRID_W)
    col = np.tile(np.arange(GRID_W, dtype=np.float64), rows)
    zeros = np.zeros((ctx_len,), np.float64)
    p_seq = np.concatenate([np.arange(ctx_len, dtype=np.float64), np.full((n_lat,), float(ctx_len))])
    ang = np.concatenate([angles(p_seq, ROPE_PAIRS[0]),
                          angles(np.concatenate([zeros, row]), ROPE_PAIRS[1]),
                          angles(np.concatenate([zeros, col]), ROPE_PAIRS[2])], axis=-1)
    cos, sin = np.cos(ang).astype(np.float32), np.sin(ang).astype(np.float32)
    return jnp.asarray(np.concatenate([cos, cos, -sin, sin], axis=-1))


def _lane_row(values):
    flat = values.reshape(1, -1).astype(F32)
    return jnp.pad(flat, ((0, 0), (0, GATE_LANES - flat.shape[1])))


def kernel(x, c, ctx, c_ctx, ada_w, ada_b, norm_mix_g, norm_ffn_g, w_in, conv_w, gdn_a_log, gdn_dt_bias,
           gdn_norm_g, ret_decay_logit, ret_norm_g, w_out, w_ffn_in, w_ffn_out, final_g):
    assert ada_w.shape[0] == 1, "single-layer block"
    b, seq, d = x.shape
    ctx_len = ctx.shape[1]
    assert ctx_len == TILE and seq % TILE == 0 and seq % TAIL_TOK == 0 and b + 1 <= 8

    cond = jnp.concatenate([c, c_ctx[None, :], jnp.zeros((8 - b - 1, d), F32)], axis=0)
    mod = _ada(cond, ada_w, ada_b)
    mod3 = mod[:, None, :]

    n_qkvz = 4 * GROUP_W
    n_gate = 4 * N_HEADS
    qkv, z, ab, pr, rg = _inproj(x, ctx, mod3, norm_mix_g, jnp.swapaxes(w_in, 1, 2), conv_w, n_qkvz, n_gate,
                                 _rope_tables(ctx_len, seq))

    dtb_row = _lane_row(gdn_dt_bias[0])
    gdn_f, gdn_b, ret_f, ret_b = _mixers(qkv, ab, dtb_row, _lane_row(gdn_a_log[0]), pr,
                                         _lane_row(ret_decay_logit[0]))

    return _tail(gdn_f, gdn_b, z, ret_f, ret_b, rg, x, mod3, gdn_norm_g, ret_norm_g, norm_ffn_g,
                 w_out[0].astype(BF16), w_ffn_in[0].astype(BF16), w_ffn_out[0].astype(BF16), final_g[None, :])
```

```python
import functools
import math

import jax
import jax.numpy as jnp
import numpy as np
from jax import lax
from jax.experimental import pallas as pl
from jax.experimental.pallas import tpu as pltpu

F32 = jnp.float32
BF16 = jnp.bfloat16

HEAD_DIM = 128
N_HEADS = 4
GROUP_W = N_HEADS * HEAD_DIM
CONV_K = 5
GRID_W = 64
ROPE_THETA = 10000.0
ROPE_PAIRS = (16, 24, 24)
NORM_EPS = 1e-6

CHUNK = 64
TILE = 256
CHUNKS_PER_TILE = TILE // CHUNK
HALO = 16
GATE_LANES = 128
TAIL_TOK = 512
FFN_CHUNK = 1024
VMEM_LIMIT = 56 * 1024 * 1024


def _cparams(sem):
    return pltpu.CompilerParams(dimension_semantics=sem, vmem_limit_bytes=VMEM_LIMIT)


def _dot(a, b):
    return jnp.dot(a, b, preferred_element_type=F32)


def _dot_nt(a, b):
    return lax.dot_general(a, b, (((1,), (1,)), ((), ())), preferred_element_type=F32)


def _split(a):
    hi = a.astype(BF16)
    lo = (a - hi.astype(F32)).astype(BF16)
    return hi, lo


def _silu(x):
    return x * jax.nn.sigmoid(x)


def _lane_bcast(col, width):
    return jnp.broadcast_to(col, (col.shape[0], width))


def _ada_kernel(cond_ref, w_ref, b_ref, o_ref):
    a_hi, a_lo = _split(_silu(cond_ref[...]))
    w_hi, w_lo = _split(w_ref[0])
    o_ref[...] = _dot(a_hi, w_hi) + _dot(a_lo, w_hi) + _dot(a_hi, w_lo) + b_ref[...]


def _ada(cond, w, b):
    rows, d = cond.shape
    n = w.shape[2]
    bn = 1536
    return pl.pallas_call(
        _ada_kernel,
        grid=(n // bn,),
        in_specs=[pl.BlockSpec((rows, d), lambda j: (0, 0)),
                  pl.BlockSpec((1, d, bn), lambda j: (0, 0, j)),
                  pl.BlockSpec((1, bn), lambda j: (0, j))],
        out_specs=pl.BlockSpec((rows, bn), lambda j: (0, j)),
        out_shape=jax.ShapeDtypeStruct((rows, n), F32),
        compiler_params=_cparams(("parallel",)),
        name="ada",
    )(cond, w, b)


def _rms_mod(x, g, shift, scale):
    ms = jnp.mean(x * x, axis=-1, keepdims=True)
    return (x * lax.rsqrt(ms + NORM_EPS) * g) * (1.0 + scale) + shift


def _conv_tile(ext_ref, w_ref, o_ref):
    pad = (CONV_K - 1) // 2
    rows = TILE + 2 * HALO
    q_scale = HEAD_DIM ** -0.5
    for part in range(3):
        for hd in range(N_HEADS):
            lo = part * GROUP_W + hd * HEAD_DIM
            slab = ext_ref[:, lo:lo + HEAD_DIM]
            acc = None
            for i in range(CONV_K):
                shifted = slab if i == pad else pltpu.roll(slab, (pad - i) % rows, 0)
                term = shifted[HALO:HALO + TILE] * w_ref[0, i:i + 1, lo:lo + HEAD_DIM]
                acc = term if acc is None else acc + term
            y = _silu(acc)
            if part < 2:
                y = y * lax.rsqrt(jnp.sum(y * y, axis=-1, keepdims=True) + NORM_EPS)
            if part == 0:
                y = y * q_scale
            o_ref[0, :, lo:lo + HEAD_DIM] = y.astype(BF16)


def _inproj_kernel(x_ref, ctx_ref, mod_ref, g_ref, w_ref, cw_ref, rope_ref,
                   qkv_ref, z_ref, ab_ref, pr_ref, rg_ref, wa_ref, wab_ref, wr_ref, ext_ref,
                   *, d_model, n_half, n_gate, tiles_per_sample):
    step = pl.program_id(0)
    t = lax.rem(jnp.minimum(step, pl.num_programs(0) - 2), tiles_per_sample)
    t_conv = lax.rem(jnp.maximum(step - 1, 0), tiles_per_sample)
    width = 3 * GROUP_W

    @pl.when(step == 0)
    def _():
        wa_ref[...] = w_ref[0, :n_half, :].T.astype(BF16)
        wr_ref[...] = w_ref[0, n_half + n_gate:, :].T.astype(BF16)
        lane = lax.broadcasted_iota(jnp.int32, (d_model, GATE_LANES), 1)
        wab_ref[...] = jnp.where(lane < n_gate, w_ref[0, n_half:n_half + GATE_LANES, :].T, 0.0).astype(BF16)
        ext_ref[...] = jnp.zeros_like(ext_ref)

    xin = jnp.where(t == 0, ctx_ref[0], x_ref[0])
    mod = mod_ref[0]
    h = _rms_mod(xin, g_ref[...], mod[:, 0:d_model], mod[:, d_model:2 * d_model]).astype(BF16)
    pa = _dot(h, wa_ref[...])
    z_ref[0] = pa[:, width:].astype(BF16)

    has_next = jnp.logical_and(t_conv >= 1, t_conv < tiles_per_sample - 1)
    ext_ref[HALO + TILE:, :] = jnp.where(has_next, pa[:HALO, :width], 0.0)
    _conv_tile(ext_ref, cw_ref, qkv_ref)
    ext_ref[0:HALO, :] = jnp.where(t >= 2, ext_ref[TILE:HALO + TILE, :], 0.0)
    ext_ref[HALO:HALO + TILE, :] = pa[:, :width]

    ab_ref[0] = _dot(h, wab_ref[...])
    pr = _dot(h, wr_ref[...])
    cos2 = rope_ref[:, :HEAD_DIM]
    sin2 = rope_ref[:, HEAD_DIM:]
    k_scale = HEAD_DIM ** -0.5
    for hd in range(2 * N_HEADS):
        lo = hd * HEAD_DIM
        tt = pr[:, lo:lo + HEAD_DIM]
        rot = tt * cos2 + pltpu.roll(tt, HEAD_DIM // 2, 1) * sin2
        if hd >= N_HEADS:
            rot = rot * k_scale
        pr_ref[0, :, lo:lo + HEAD_DIM] = rot.astype(BF16)
    pr_ref[0, :, 2 * GROUP_W:] = pr[:, 2 * GROUP_W:width].astype(BF16)
    rg_ref[0] = pr[:, width:].astype(BF16)


def _inproj(x, ctx, mod3, g, w_in, conv_w, n_half, n_gate, rope):
    b, seq, d = x.shape
    n_tiles = (ctx.shape[1] + seq) // TILE
    t_total = n_tiles * TILE
    total = b * n_tiles
    width = 3 * GROUP_W
    assert w_in.shape[1] == 2 * n_half + n_gate and n_gate <= GATE_LANES and n_half == width + GROUP_W

    def now(f):
        flat = jnp.minimum(f, total - 1)
        return flat // n_tiles, flat % n_tiles

    def lagged(f):
        flat = jnp.maximum(f - 1, 0)
        return flat // n_tiles, flat % n_tiles

    def tok(cols):
        return pl.BlockSpec((1, TILE, cols), lambda f: (now(f)[0], now(f)[1], 0))

    def lat(cols):
        return pl.BlockSpec((1, TILE, cols), lambda f: (now(f)[0], jnp.maximum(now(f)[1] - 1, 0), 0))

    return pl.pallas_call(
        functools.partial(_inproj_kernel, d_model=d, n_half=n_half, n_gate=n_gate, tiles_per_sample=n_tiles),
        grid=(total + 1,),
        in_specs=[pl.BlockSpec((1, TILE, d), lambda f: (now(f)[0], jnp.maximum(now(f)[1] - 1, 0), 0)),
                  pl.BlockSpec((1, TILE, d), lambda f: (now(f)[0], 0, 0)),
                  pl.BlockSpec((1, 1, mod3.shape[2]), lambda f: (jnp.where(now(f)[1] == 0, b, now(f)[0]), 0, 0)),
                  pl.BlockSpec((1, d), lambda f: (0, 0)),
                  pl.BlockSpec(w_in.shape, lambda f: (0, 0, 0), pipeline_mode=pl.Buffered(1)),
                  pl.BlockSpec(conv_w.shape, lambda f: (0, 0, 0)),
                  pl.BlockSpec((TILE, 2 * HEAD_DIM), lambda f: (now(f)[1], 0))],
        out_specs=[pl.BlockSpec((1, TILE, width), lambda f: (lagged(f)[0], lagged(f)[1], 0)),
                   lat(GROUP_W), tok(GATE_LANES), tok(width), lat(GROUP_W)],
        out_shape=[jax.ShapeDtypeStruct((b, t_total, width), BF16),
                   jax.ShapeDtypeStruct((b, seq, GROUP_W), BF16),
                   jax.ShapeDtypeStruct((b, t_total, GATE_LANES), F32),
                   jax.ShapeDtypeStruct((b, t_total, width), BF16),
                   jax.ShapeDtypeStruct((b, seq, GROUP_W), BF16)],
        scratch_shapes=[pltpu.VMEM((d, n_half), BF16), pltpu.VMEM((d, GATE_LANES), BF16),
                        pltpu.VMEM((d, n_half), BF16), pltpu.VMEM((TILE + 2 * HALO, width), F32)],
        compiler_params=_cparams(("arbitrary",)),
        name="inproj",
    )(x, ctx, mod3, g, w_in, conv_w, rope)


def _block_diag(y, bd_mask):
    yb = y.astype(BF16)
    return jnp.where(bd_mask, jnp.concatenate([yb] * N_HEADS, axis=0), jnp.zeros((), BF16))


def _packed_matmul(x, y_bd):
    return _dot(x.astype(BF16), y_bd)


def _unit_lower_inverse_stages(a_list, eye, m16, m32, m64, bd_mask, out):
    pm = _packed_matmul
    ps = [-jnp.where(m16, a, 0.0) for a in a_list]
    xs = [eye + p for p in ps]
    ps = [p.astype(BF16) for p in ps]
    ps = [pm(p, _block_diag(p, bd_mask)).astype(BF16) for p in ps]
    yield
    for _ in range(2):
        prods = [_dot(jnp.concatenate([x.astype(BF16), p], axis=0), _block_diag(p, bd_mask))
                 for x, p in zip(xs, ps)]
        xs = [x + pr[:CHUNK] for x, pr in zip(xs, prods)]
        ps = [pr[CHUNK:].astype(BF16) for pr in prods]
        yield
    xs = [x + pm(x, _block_diag(p, bd_mask)) for x, p in zip(xs, ps)]
    yield
    for m in (m32, m64):
        zs = [pm(jnp.where(m, a, 0.0), _block_diag(x, bd_mask)).astype(BF16) for a, x in zip(a_list, xs)]
        yield
        xs = [x - pm(x, _block_diag(z, bd_mask)) for x, z in zip(xs, zs)]
        yield
    out.extend(xs)


def _gdn_front_stages(groups, ring_base, mid_base, a_scr, rhs_scr, brow_scr, wq_scr, qkd_scr, kdt_scr,
                      el_scr):
    lane = lax.broadcasted_iota(jnp.int32, (CHUNK, GATE_LANES), 1)
    low_half = lax.broadcasted_iota(jnp.int32, (CHUNK, HEAD_DIM), 1) < CHUNK
    zero_k = jnp.zeros((CHUNK, HEAD_DIM), BF16)
    for g in groups:
        tcum = g["consts"][0]
        g_hi, g_lo = _split(g["gates"])
        g["gc"] = _dot(tcum, g_hi) + _dot(tcum, g_lo)
    yield

    for gi, g in enumerate(groups):
        g_lane0 = N_HEADS if g["rev"] else 0
        b_lane0 = 2 * N_HEADS + g_lane0
        last = 0 if g["rev"] else CHUNK - 1
        gc, gates = g["gc"], g["gates"]
        ct = jnp.where(lane < 2 * N_HEADS, gc, gates).T
        egl_t = jnp.exp(ct[:, last:last + 1] - ct)
        gcb, kb, ms = [], [], []
        for h in range(N_HEADS):
            lo = h * HEAD_DIM
            gl = g_lane0 + h
            ring_unit = (ring_base + gi) * N_HEADS + h
            gcb.append(_lane_bcast(gc[:, gl:gl + 1], HEAD_DIM))
            beta = _lane_bcast(gates[:, b_lane0 + h:b_lane0 + h + 1], HEAD_DIM)
            eg_h = jnp.exp(gcb[h])
            q_h = g["q"][:, lo:lo + HEAD_DIM]
            k_h = g["k"][:, lo:lo + HEAD_DIM].astype(F32)
            kb.append((k_h * beta).astype(BF16))
            rhs_scr[(mid_base + gi) * N_HEADS + h] = jnp.concatenate(
                [g["v"][:, lo:lo + HEAD_DIM], (k_h * eg_h).astype(BF16)], axis=1)
            wq_scr[ring_unit, CHUNK:, :] = (q_h * eg_h).astype(BF16)
            kdt_scr[ring_unit] = (k_h.T * egl_t[gl:gl + 1, :]).astype(BF16)
        d_parts = []
        for p in range(N_HEADS // 2):
            h0, h1 = 2 * p, 2 * p + 1
            lo = h0 * HEAD_DIM
            lhs = jnp.concatenate([g["q"][:, lo:lo + 2 * HEAD_DIM], jnp.concatenate([kb[h0], kb[h1]], axis=1)],
                                  axis=0)
            k0 = g["k"][:, lo:lo + HEAD_DIM]
            k1 = g["k"][:, lo + HEAD_DIM:lo + 2 * HEAD_DIM]
            rhs_bd = jnp.concatenate([jnp.concatenate([k0, zero_k], axis=1),
                                      jnp.concatenate([zero_k, k1], axis=1)], axis=0)
            ms.append(_dot_nt(lhs, rhs_bd))
            g_row = jnp.concatenate([ct[g_lane0 + h0:g_lane0 + h0 + 1, :], ct[g_lane0 + h1:g_lane0 + h1 + 1, :]],
                                    axis=1)
            d_parts.append(jnp.where(low_half, gcb[h0], gcb[h1]) - g_row)
        incl4, strict4 = g["consts"][2:]
        e = jnp.exp(jnp.where(incl4, jnp.concatenate(d_parts, axis=1), 0.0))
        a_scr[mid_base + gi] = jnp.where(strict4, jnp.concatenate([m[CHUNK:] for m in ms], axis=1) * e, 0.0)
        qkd_scr[ring_base + gi] = jnp.where(incl4, jnp.concatenate([m[:CHUNK] for m in ms], axis=1) * e,
                                            0.0).astype(BF16)
        el_scr[ring_base + gi] = jnp.exp(gc[last:last + 1, :])
        brow_scr[mid_base + gi] = jnp.concatenate(
            [ct[b_lane0 + h:b_lane0 + h + 1, :] for h in range(N_HEADS)], axis=1)
        yield


def _gdn_solve_stages(n_groups, masks, ring_base, mid_base, a_scr, rhs_scr, brow_scr, u_scr, wq_scr):
    eye4, m16, m32, m64, bd_mask = masks
    t_invs = []
    yield from _unit_lower_inverse_stages([a_scr[mid_base + gi] for gi in range(n_groups)],
                                          eye4, m16, m32, m64, bd_mask, t_invs)
    for gi, t_inv in enumerate(t_invs):
        t_b = (t_inv * brow_scr[mid_base + gi]).astype(BF16)
        for h in range(N_HEADS):
            sol = _dot(t_b[:, h * CHUNK:(h + 1) * CHUNK], rhs_scr[(mid_base + gi) * N_HEADS + h])
            ring_unit = (ring_base + gi) * N_HEADS + h
            u_scr[ring_unit] = sol[:, :HEAD_DIM]
            wq_scr[ring_unit, :CHUNK, :] = sol[:, HEAD_DIM:].astype(BF16)


def _gdn_scan_stages(slot_base, keep_state, u_scr, wq_scr, qkd_scr, kdt_scr, el_scr, s_ref, of_ref, ob_ref):
    for j in range(CHUNKS_PER_TILE):
        units = [(slot_base + 2 * j + d, h, d * N_HEADS + h) for d in range(2) for h in range(N_HEADS)]
        states = [jnp.where(keep_state, s_ref[slot], 0.0) if j == 0 else s_ref[slot] for _, _, slot in units]
        wss = [_dot(wq_scr[gi * N_HEADS + h], s.astype(BF16)) for (gi, h, _), s in zip(units, states)]
        yield
        v_news = [(u_scr[gi * N_HEADS + h] - ws[:CHUNK]).astype(BF16) for (gi, h, _), ws in zip(units, wss)]
        outs = [ws[CHUNK:] + _dot(qkd_scr[gi][:, h * CHUNK:(h + 1) * CHUNK], vn)
                for (gi, h, _), ws, vn in zip(units, wss, v_news)]
        for (gi, h, slot), s, vn in zip(units, states, v_news):
            s_ref[slot] = el_scr[gi][:, slot:slot + 1] * s + _dot(kdt_scr[gi * N_HEADS + h], vn)
        rf = j * CHUNK
        rb = (CHUNKS_PER_TILE - 1 - j) * CHUNK
        of_ref[0, rf:rf + CHUNK, :] = jnp.concatenate(outs[:N_HEADS], axis=1).astype(BF16)
        ob_ref[0, rb:rb + CHUNK, :] = jnp.concatenate(outs[N_HEADS:], axis=1).astype(BF16)
        yield


def _interleave(*stage_generators):
    live = list(stage_generators)
    while live:
        for gen in list(live):
            try:
                next(gen)
            except StopIteration:
                live.remove(gen)


def _gate_tile(ab, dtb, a_log):
    z = ab + dtb
    softplus = jnp.maximum(z, 0.0) + jnp.log(1.0 + jnp.exp(-jnp.abs(z)))
    lane = lax.broadcasted_iota(jnp.int32, ab.shape, 1)
    return jnp.where(lane < 2 * N_HEADS, -jnp.exp(a_log) * softplus, jax.nn.sigmoid(ab))


def _gdn_kernel(qkvf_ref, abf_ref, qkvb_ref, abb_ref, dtb_ref, alog_ref, rqkvf_ref, rqkvb_ref, logit_ref,
                of_ref, ob_ref, rof_ref, rob_ref,
                s_ref, a_scr, rhs_scr, brow_scr, u_scr, wq_scr, qkd_scr, kdt_scr, el_scr,
                r_ref, decay_ref, xi_ref, zeta_ref, *, tiles_per_sample):
    step = pl.program_id(0)
    lg_row = _log_sigmoid(logit_ref[...])
    keep_state = lax.rem(step + tiles_per_sample - 2, tiles_per_sample) != 0
    groups_per_tile = 2 * CHUNKS_PER_TILE
    front_ring = lax.rem(step, 3) * groups_per_tile
    solve_ring = lax.rem(step + 2, 3) * groups_per_tile
    scan_ring = lax.rem(step + 1, 3) * groups_per_tile
    front_mid = lax.rem(step, 2) * groups_per_tile
    solve_mid = groups_per_tile - front_mid

    @pl.when(step == 0)
    def _():
        for scr in (s_ref, a_scr, rhs_scr, brow_scr, u_scr, wq_scr, qkd_scr, kdt_scr, el_scr, r_ref):
            scr[...] = jnp.zeros_like(scr)
        _ret_tables(lg_row, decay_ref, xi_ref, zeta_ref)

    ri = lax.broadcasted_iota(jnp.int32, (CHUNK, CHUNK), 0)
    ci = lax.broadcasted_iota(jnp.int32, (CHUNK, CHUNK), 1)
    ri4 = lax.broadcasted_iota(jnp.int32, (CHUNK, N_HEADS * CHUNK), 0)
    ci4 = lax.broadcasted_iota(jnp.int32, (CHUNK, N_HEADS * CHUNK), 1) % CHUNK
    rbd = lax.broadcasted_iota(jnp.int32, (N_HEADS * CHUNK, N_HEADS * CHUNK), 0) // CHUNK
    cbd = lax.broadcasted_iota(jnp.int32, (N_HEADS * CHUNK, N_HEADS * CHUNK), 1) // CHUNK
    bd_mask = rbd == cbd
    eye4 = (ri4 == ci4).astype(F32)
    m16 = (ri4 // 16) == (ci4 // 16)
    m32 = jnp.logical_and((ri4 // 32) == (ci4 // 32), jnp.logical_not(m16))
    m64 = (ri4 // 32) != (ci4 // 32)

    cf = ((ci <= ri).astype(BF16), None, ri4 >= ci4, ri4 > ci4)
    cb = ((ci >= ri).astype(BF16), None, ri4 <= ci4, ri4 < ci4)
    dtb = dtb_ref[...]
    alog = alog_ref[...]

    groups = []
    for j in range(CHUNKS_PER_TILE):
        rf = j * CHUNK
        rb = (CHUNKS_PER_TILE - 1 - j) * CHUNK
        for ref, ab_ref, r0, rev, consts in ((qkvf_ref, abf_ref, rf, False, cf), (qkvb_ref, abb_ref, rb, True, cb)):
            groups.append(dict(q=ref[0, r0:r0 + CHUNK, 0:GROUP_W], k=ref[0, r0:r0 + CHUNK, GROUP_W:2 * GROUP_W],
                               v=ref[0, r0:r0 + CHUNK, 2 * GROUP_W:],
                               gates=_gate_tile(ab_ref[0, r0:r0 + CHUNK, :], dtb, alog), rev=rev, consts=consts))
    _interleave(
        _ret_stages(rqkvf_ref, rqkvb_ref, rof_ref, rob_ref, lg_row, keep_state, r_ref, decay_ref, xi_ref, zeta_ref),
        _gdn_solve_stages(len(groups), (eye4, m16, m32, m64, bd_mask), solve_ring, solve_mid,
                          a_scr, rhs_scr, brow_scr, u_scr, wq_scr),
        _gdn_scan_stages(scan_ring, keep_state, u_scr, wq_scr, qkd_scr, kdt_scr, el_scr, s_ref, of_ref, ob_ref),
        _gdn_front_stages(groups, front_ring, front_mid, a_scr, rhs_scr, brow_scr, wq_scr, qkd_scr, kdt_scr,
                          el_scr))


def _bwd_tile(s, n_tiles):
    return jnp.where(s == 0, 0, n_tiles - s)


def _latent_block(tile, bwd, n_tiles):
    return jnp.where(tile == 0, n_tiles - 2 if bwd else 0, tile - 1)


def _mixers(qkv, ab, dtb_row, alog_row, pr, logit_row):
    b, t_total, _ = qkv.shape
    n_tiles = t_total // TILE

    total = b * n_tiles

    def block_of(f, bwd, lag, latent):
        flat = jnp.maximum(f - 2, 0) if lag else jnp.minimum(f, total - 1)
        pos = flat % n_tiles
        tile = _bwd_tile(pos, n_tiles) if bwd else pos
        return (flat // n_tiles, _latent_block(tile, bwd, n_tiles) if latent else tile, 0)

    def tok(width, bwd, lag=False, latent=False):
        return pl.BlockSpec((1, TILE, width), lambda f: block_of(f, bwd, lag, latent))

    row = pl.BlockSpec((1, GATE_LANES), lambda f: (0, 0))
    per_tile = 2 * CHUNKS_PER_TILE
    n_groups = 3 * per_tile
    n_units = n_groups * N_HEADS
    n_mid = 2 * per_tile
    return pl.pallas_call(
        functools.partial(_gdn_kernel, tiles_per_sample=n_tiles),
        grid=(total + 2,),
        in_specs=[tok(3 * GROUP_W, False), tok(GATE_LANES, False),
                  tok(3 * GROUP_W, True), tok(GATE_LANES, True), row, row,
                  tok(3 * GROUP_W, False, lag=True), tok(3 * GROUP_W, True, lag=True), row],
        out_specs=[tok(GROUP_W, bwd, lag=True, latent=True) for bwd in (False, True, False, True)],
        out_shape=[jax.ShapeDtypeStruct((b, t_total - TILE, GROUP_W), BF16)] * 4,
        scratch_shapes=[pltpu.VMEM((2 * N_HEADS, HEAD_DIM, HEAD_DIM), F32),
                        pltpu.VMEM((n_mid, CHUNK, N_HEADS * CHUNK), F32),
                        pltpu.VMEM((n_mid * N_HEADS, CHUNK, 2 * HEAD_DIM), BF16),
                        pltpu.VMEM((n_mid, 1, N_HEADS * CHUNK), F32),
                        pltpu.VMEM((n_units, CHUNK, HEAD_DIM), F32),
                        pltpu.VMEM((n_units, 2 * CHUNK, HEAD_DIM), BF16),
                        pltpu.VMEM((n_groups, CHUNK, N_HEADS * CHUNK), BF16),
                        pltpu.VMEM((n_units, HEAD_DIM, CHUNK), BF16),
                        pltpu.VMEM((n_groups, 1, GATE_LANES), F32),
                        pltpu.VMEM((2 * N_HEADS, HEAD_DIM, HEAD_DIM), F32),
                        pltpu.VMEM((2 * N_HEADS, TILE, TILE), F32),
                        pltpu.VMEM((2 * N_HEADS, TILE, HEAD_DIM), F32),
                        pltpu.VMEM((2 * N_HEADS, TILE, HEAD_DIM), F32)],
        compiler_params=_cparams(("arbitrary",)),
        name="mixers",
    )(qkv, ab, qkv, ab, dtb_row, alog_row, pr, pr, logit_row)


def _log_sigmoid(x):
    return jnp.minimum(x, 0.0) - jnp.log(1.0 + jnp.exp(-jnp.abs(x)))


def _ret_tables(lg_row, decay_ref, xi_ref, zeta_ref):
    ri = lax.broadcasted_iota(jnp.int32, (TILE, TILE), 0)
    ci = lax.broadcasted_iota(jnp.int32, (TILE, TILE), 1)
    row = lax.broadcasted_iota(jnp.int32, (TILE, HEAD_DIM), 0).astype(F32)
    for slot in range(2 * N_HEADS):
        rev = slot >= N_HEADS
        lg = lg_row[:, slot:slot + 1]
        diff = ((ci - ri) if rev else (ri - ci)).astype(F32)
        pos = ((TILE - 1.0) - row) if rev else row
        decay_ref[slot] = jnp.where(diff >= 0, jnp.exp(lg * jnp.maximum(diff, 0.0)), 0.0)
        xi_ref[slot] = jnp.exp(lg * (pos + 1.0))
        zeta_ref[slot] = jnp.exp(lg * ((TILE - 1.0) - pos))


def _ret_stages(qkvf_ref, qkvb_ref, of_ref, ob_ref, lg_row, keep_state, r_ref, decay_ref, xi_ref, zeta_ref):
    units = [(refs, (N_HEADS if rev else 0) + h, h * HEAD_DIM)
             for refs, rev in (((qkvf_ref, of_ref), False), ((qkvb_ref, ob_ref), True))
             for h in range(N_HEADS)]
    qs = [refs[0][0, :, lo:lo + HEAD_DIM] for refs, _, lo in units]
    ks = [refs[0][0, :, GROUP_W + lo:GROUP_W + lo + HEAD_DIM] for refs, _, lo in units]
    vs = [refs[0][0, :, 2 * GROUP_W + lo:2 * GROUP_W + lo + HEAD_DIM] for refs, _, lo in units]
    per_yield = 2
    qks = []
    for i, (q, k) in enumerate(zip(qs, ks)):
        qks.append(_dot_nt(q, k))
        if i % per_yield == per_yield - 1:
            yield
    rs = [jnp.where(keep_state, r_ref[slot], 0.0) for _, slot, _ in units]
    for i, ((refs, slot, lo), q, k, v, qk, r) in enumerate(zip(units, qs, ks, vs, qks, rs)):
        lhs = jnp.concatenate([(qk * decay_ref[slot]).astype(BF16), (k * zeta_ref[slot]).T.astype(BF16)], axis=0)
        both = _dot(lhs, v)
        refs[1][0, :, lo:lo + HEAD_DIM] = (
            both[:TILE] + _dot((q * xi_ref[slot]).astype(BF16), r.astype(BF16))).astype(BF16)
        g_chunk = jnp.exp(lg_row[:, slot:slot + 1] * float(TILE))
        r_ref[slot] = g_chunk * r + both[TILE:]
        if i % per_yield == per_yield - 1:
            yield


def _tail_kernel(gf_ref, gb_ref, z_ref, rf_ref, rb_ref, rg_ref, x_ref, mod_ref, gdn_g_ref, ret_g_ref,
                 nffn_g_ref, w_ref, wi_ref, wo_ref, fg_ref, o_ref, y_ref, *, d_model, d_ff):
    o = gf_ref[0].astype(F32) + gb_ref[0].astype(F32)
    r = rf_ref[0].astype(F32) + rb_ref[0].astype(F32)
    z = z_ref[0].astype(F32)
    rg = rg_ref[0].astype(F32)
    for h in range(N_HEADS):
        lo = h * HEAD_DIM
        o_h = o[:, lo:lo + HEAD_DIM]
        o_n = o_h * lax.rsqrt(jnp.mean(o_h * o_h, axis=-1, keepdims=True) + NORM_EPS) * gdn_g_ref[...]
        y_ref[:, lo:lo + HEAD_DIM] = (o_n * _silu(z[:, lo:lo + HEAD_DIM])).astype(BF16)
        r_h = r[:, lo:lo + HEAD_DIM]
        mu = jnp.mean(r_h, axis=-1, keepdims=True)
        cen = r_h - mu
        var = jnp.mean(cen * cen, axis=-1, keepdims=True)
        r_n = cen * lax.rsqrt(var + NORM_EPS) * ret_g_ref[...]
        y_ref[:, GROUP_W + lo:GROUP_W + lo + HEAD_DIM] = (r_n * _silu(rg[:, lo:lo + HEAD_DIM])).astype(BF16)
    mod = mod_ref[0]
    x1 = x_ref[0] + mod[:, 2 * d_model:3 * d_model] * _dot(y_ref[...], w_ref[...])
    h2 = _rms_mod(x1, nffn_g_ref[...], mod[:, 3 * d_model:4 * d_model],
                  mod[:, 4 * d_model:5 * d_model]).astype(BF16)
    acc = None
    for lo in range(0, d_ff, FFN_CHUNK):
        hi = min(lo + FFN_CHUNK, d_ff)
        gate = _dot(h2, wi_ref[:, lo:hi])
        up = _dot(h2, wi_ref[:, d_ff + lo:d_ff + hi])
        part = _dot((_silu(gate) * up).astype(BF16), wo_ref[lo:hi, :])
        acc = part if acc is None else acc + part
    x2 = x1 + mod[:, 5 * d_model:6 * d_model] * acc
    ms = jnp.mean(x2 * x2, axis=-1, keepdims=True)
    o_ref[0] = x2 * lax.rsqrt(ms + NORM_EPS) * fg_ref[...]


def _tail(gdn_f, gdn_b, z, ret_f, ret_b, rg, x, mod3, gdn_g, ret_g, nffn_g, w_out, w_ffn_in, w_ffn_out,
          final_g):
    b, seq, d = x.shape
    d_ff = w_ffn_out.shape[0]
    tok = pl.BlockSpec((1, TAIL_TOK, GROUP_W), lambda i, t: (i, t, 0))

    def resident(shape):
        return pl.BlockSpec(shape, lambda i, t: (0,) * len(shape), pipeline_mode=pl.Buffered(1))

    return pl.pallas_call(
        functools.partial(_tail_kernel, d_model=d, d_ff=d_ff),
        grid=(b, seq // TAIL_TOK),
        in_specs=[tok, tok, tok, tok, tok, tok,
                  pl.BlockSpec((1, TAIL_TOK, d), lambda i, t: (i, t, 0)),
                  pl.BlockSpec((1, 1, mod3.shape[2]), lambda i, t: (i, 0, 0)),
                  resident((1, HEAD_DIM)), resident((1, HEAD_DIM)), resident((1, d)),
                  resident(w_out.shape), resident(w_ffn_in.shape), resident(w_ffn_out.shape),
                  resident((1, d))],
        out_specs=pl.BlockSpec((1, TAIL_TOK, d), lambda i, t: (i, t, 0)),
        out_shape=jax.ShapeDtypeStruct((b, seq, d), F32),
        scratch_shapes=[pltpu.VMEM((TAIL_TOK, 2 * GROUP_W), BF16)],
        compiler_params=_cparams(("parallel", "parallel")),
        name="tail",
    )(gdn_f, gdn_b, z, ret_f, ret_b, rg, x, mod3, gdn_g, ret_g, nffn_g, w_out, w_ffn_in, w_ffn_out, final_g)


def _rope_tables(ctx_len, n_lat):
    def angles(pos, n_pairs):
        inv = ROPE_THETA ** (-np.arange(n_pairs, dtype=np.float64) / n_pairs)
        return pos[:, None] * inv[None, :]

    rows = n_lat // GRID_W
    row = np.repeat(np.arange(rows, dtype=np.float64), GRID_W)
    col = np.tile(np.arange(GRID_W, dtype=np.float64), rows)
    zeros = np.zeros((ctx_len,), np.float64)
    p_seq = np.concatenate([np.arange(ctx_len, dtype=np.float64), np.full((n_lat,), float(ctx_len))])
    ang = np.concatenate([angles(p_seq, ROPE_PAIRS[0]),
                          angles(np.concatenate([zeros, row]), ROPE_PAIRS[1]),
                          angles(np.concatenate([zeros, col]), ROPE_PAIRS[2])], axis=-1)
    cos, sin = np.cos(ang).astype(np.float32), np.sin(ang).astype(np.float32)
    return jnp.asarray(np.concatenate([cos, cos, -sin, sin], axis=-1))


def _lane_row(values):
    flat = values.reshape(1, -1).astype(F32)
    return jnp.pad(flat, ((0, 0), (0, GATE_LANES - flat.shape[1])))


def kernel(x, c, ctx, c_ctx, ada_w, ada_b, norm_mix_g, norm_ffn_g, w_in, conv_w, gdn_a_log, gdn_dt_bias,
           gdn_norm_g, ret_decay_logit, ret_norm_g, w_out, w_ffn_in, w_ffn_out, final_g):
    assert ada_w.shape[0] == 1, "single-layer block"
    b, seq, d = x.shape
    ctx_len = ctx.shape[1]
    assert ctx_len == TILE and seq % TILE == 0 and seq % TAIL_TOK == 0 and b + 1 <= 8

    cond = jnp.concatenate([c, c_ctx[None, :], jnp.zeros((8 - b - 1, d), F32)], axis=0)
    mod = _ada(cond, ada_w, ada_b)
    mod3 = mod[:, None, :]

    n_qkvz = 4 * GROUP_W
    n_gate = 4 * N_HEADS
    qkv, z, ab, pr, rg = _inproj(x, ctx, mod3, norm_mix_g, jnp.swapaxes(w_in, 1, 2), conv_w, n_qkvz, n_gate,
                                 _rope_tables(ctx_len, seq))

    dtb_row = _lane_row(gdn_dt_bias[0])
    gdn_f, gdn_b, ret_f, ret_b = _mixers(qkv, ab, dtb_row, _lane_row(gdn_a_log[0]), pr,
                                         _lane_row(ret_decay_logit[0]))

    return _tail(gdn_f, gdn_b, z, ret_f, ret_b, rg, x, mod3, gdn_norm_g, ret_norm_g, norm_ffn_g,
                 w_out[0].astype(BF16), w_ffn_in[0].astype(BF16), w_ffn_out[0].astype(BF16), final_g[None, :])
```

```python
import functools
import math

import jax
import jax.numpy as jnp
import numpy as np
from jax import lax
from jax.experimental import pallas as pl
from jax.experimental.pallas import tpu as pltpu

F32 = jnp.float32
BF16 = jnp.bfloat16

HEAD_DIM = 128
N_HEADS = 4
GROUP_W = N_HEADS * HEAD_DIM
CONV_K = 5
GRID_W = 64
ROPE_THETA = 10000.0
ROPE_PAIRS = (16, 24, 24)
NORM_EPS = 1e-6

CHUNK = 64
TILE = 256
CHUNKS_PER_TILE = TILE // CHUNK
HALO = 16
GATE_LANES = 128
TAIL_TOK = 512
FFN_CHUNK = 1024
VMEM_LIMIT = 56 * 1024 * 1024


def _cparams(sem):
    return pltpu.CompilerParams(dimension_semantics=sem, vmem_limit_bytes=VMEM_LIMIT)


def _dot(a, b):
    return jnp.dot(a, b, preferred_element_type=F32)


def _dot_nt(a, b):
    return lax.dot_general(a, b, (((1,), (1,)), ((), ())), preferred_element_type=F32)


def _split(a):
    hi = a.astype(BF16)
    lo = (a - hi.astype(F32)).astype(BF16)
    return hi, lo


def _silu(x):
    return x * jax.nn.sigmoid(x)


def _lane_bcast(col, width):
    return jnp.broadcast_to(col, (col.shape[0], width))


def _ada_kernel(cond_ref, w_ref, b_ref, o_ref):
    a_hi, a_lo = _split(_silu(cond_ref[...]))
    w_hi, w_lo = _split(w_ref[0])
    o_ref[...] = _dot(a_hi, w_hi) + _dot(a_lo, w_hi) + _dot(a_hi, w_lo) + b_ref[...]


def _ada(cond, w, b):
    rows, d = cond.shape
    n = w.shape[2]
    bn = 1536
    return pl.pallas_call(
        _ada_kernel,
        grid=(n // bn,),
        in_specs=[pl.BlockSpec((rows, d), lambda j: (0, 0)),
                  pl.BlockSpec((1, d, bn), lambda j: (0, 0, j)),
                  pl.BlockSpec((1, bn), lambda j: (0, j))],
        out_specs=pl.BlockSpec((rows, bn), lambda j: (0, j)),
        out_shape=jax.ShapeDtypeStruct((rows, n), F32),
        compiler_params=_cparams(("parallel",)),
        name="ada",
    )(cond, w, b)


def _rms_mod(x, g, shift, scale):
    ms = jnp.mean(x * x, axis=-1, keepdims=True)
    return (x * lax.rsqrt(ms + NORM_EPS) * g) * (1.0 + scale) + shift


def _conv_tile(ext_ref, w_ref, o_ref):
    pad = (CONV_K - 1) // 2
    rows = TILE + 2 * HALO
    q_scale = HEAD_DIM ** -0.5
    for part in range(3):
        for hd in range(N_HEADS):
            lo = part * GROUP_W + hd * HEAD_DIM
            slab = ext_ref[:, lo:lo + HEAD_DIM]
            acc = None
            for i in range(CONV_K):
                shifted = slab if i == pad else pltpu.roll(slab, (pad - i) % rows, 0)
                term = shifted[HALO:HALO + TILE] * w_ref[0, i:i + 1, lo:lo + HEAD_DIM]
                acc = term if acc is None else acc + term
            y = _silu(acc)
            if part < 2:
                y = y * lax.rsqrt(jnp.sum(y * y, axis=-1, keepdims=True) + NORM_EPS)
            if part == 0:
                y = y * q_scale
            o_ref[0, :, lo:lo + HEAD_DIM] = y.astype(BF16)


def _inproj_kernel(x_ref, ctx_ref, mod_ref, g_ref, w_ref, cw_ref, rope_ref,
                   qkv_ref, z_ref, ab_ref, pr_ref, rg_ref, wa_ref, wab_ref, wr_ref, ext_ref,
                   *, d_model, n_half, n_gate, tiles_per_sample):
    step = pl.program_id(0)
    t = lax.rem(jnp.minimum(step, pl.num_programs(0) - 2), tiles_per_sample)
    t_conv = lax.rem(jnp.maximum(step - 1, 0), tiles_per_sample)
    width = 3 * GROUP_W

    @pl.when(step == 0)
    def _():
        wa_ref[...] = w_ref[0, :n_half, :].T.astype(BF16)
        wr_ref[...] = w_ref[0, n_half + n_gate:, :].T.astype(BF16)
        lane = lax.broadcasted_iota(jnp.int32, (d_model, GATE_LANES), 1)
        wab_ref[...] = jnp.where(lane < n_gate, w_ref[0, n_half:n_half + GATE_LANES, :].T, 0.0).astype(BF16)
        ext_ref[...] = jnp.zeros_like(ext_ref)

    xin = jnp.where(t == 0, ctx_ref[0], x_ref[0])
    mod = mod_ref[0]
    h = _rms_mod(xin, g_ref[...], mod[:, 0:d_model], mod[:, d_model:2 * d_model]).astype(BF16)
    pa = _dot(h, wa_ref[...])
    z_ref[0] = pa[:, width:].astype(BF16)

    has_next = jnp.logical_and(t_conv >= 1, t_conv < tiles_per_sample - 1)
    ext_ref[HALO + TILE:, :] = jnp.where(has_next, pa[:HALO, :width], 0.0)
    _conv_tile(ext_ref, cw_ref, qkv_ref)
    ext_ref[0:HALO, :] = jnp.where(t >= 2, ext_ref[TILE:HALO + TILE, :], 0.0)
    ext_ref[HALO:HALO + TILE, :] = pa[:, :width]

    ab_ref[0] = _dot(h, wab_ref[...])
    pr = _dot(h, wr_ref[...])
    cos2 = rope_ref[:, :HEAD_DIM]
    sin2 = rope_ref[:, HEAD_DIM:]
    k_scale = HEAD_DIM ** -0.5
    for hd in range(2 * N_HEADS):
        lo = hd * HEAD_DIM
        tt = pr[:, lo:lo + HEAD_DIM]
        rot = tt * cos2 + pltpu.roll(tt, HEAD_DIM // 2, 1) * sin2
        if hd >= N_HEADS:
            rot = rot * k_scale
        pr_ref[0, :, lo:lo + HEAD_DIM] = rot.astype(BF16)
    pr_ref[0, :, 2 * GROUP_W:] = pr[:, 2 * GROUP_W:width].astype(BF16)
    rg_ref[0] = pr[:, width:].astype(BF16)


def _inproj(x, ctx, mod3, g, w_in, conv_w, n_half, n_gate, rope):
    b, seq, d = x.shape
    n_tiles = (ctx.shape[1] + seq) // TILE
    t_total = n_tiles * TILE
    total = b * n_tiles
    width = 3 * GROUP_W
    assert w_in.shape[1] == 2 * n_half + n_gate and n_gate <= GATE_LANES and n_half == width + GROUP_W

    def now(f):
        flat = jnp.minimum(f, total - 1)
        return flat // n_tiles, flat % n_tiles

    def lagged(f):
        flat = jnp.maximum(f - 1, 0)
        return flat // n_tiles, flat % n_tiles

    def tok(cols):
        return pl.BlockSpec((1, TILE, cols), lambda f: (now(f)[0], now(f)[1], 0))

    def lat(cols):
        return pl.BlockSpec((1, TILE, cols), lambda f: (now(f)[0], jnp.maximum(now(f)[1] - 1, 0), 0))

    return pl.pallas_call(
        functools.partial(_inproj_kernel, d_model=d, n_half=n_half, n_gate=n_gate, tiles_per_sample=n_tiles),
        grid=(total + 1,),
        in_specs=[pl.BlockSpec((1, TILE, d), lambda f: (now(f)[0], jnp.maximum(now(f)[1] - 1, 0), 0)),
                  pl.BlockSpec((1, TILE, d), lambda f: (now(f)[0], 0, 0)),
                  pl.BlockSpec((1, 1, mod3.shape[2]), lambda f: (jnp.where(now(f)[1] == 0, b, now(f)[0]), 0, 0)),
                  pl.BlockSpec((1, d), lambda f: (0, 0)),
                  pl.BlockSpec(w_in.shape, lambda f: (0, 0, 0), pipeline_mode=pl.Buffered(1)),
                  pl.BlockSpec(conv_w.shape, lambda f: (0, 0, 0)),
                  pl.BlockSpec((TILE, 2 * HEAD_DIM), lambda f: (now(f)[1], 0))],
        out_specs=[pl.BlockSpec((1, TILE, width), lambda f: (lagged(f)[0], lagged(f)[1], 0)),
                   lat(GROUP_W), tok(GATE_LANES), tok(width), lat(GROUP_W)],
        out_shape=[jax.ShapeDtypeStruct((b, t_total, width), BF16),
                   jax.ShapeDtypeStruct((b, seq, GROUP_W), BF16),
                   jax.ShapeDtypeStruct((b, t_total, GATE_LANES), F32),
                   jax.ShapeDtypeStruct((b, t_total, width), BF16),
                   jax.ShapeDtypeStruct((b, seq, GROUP_W), BF16)],
        scratch_shapes=[pltpu.VMEM((d, n_half), BF16), pltpu.VMEM((d, GATE_LANES), BF16),
                        pltpu.VMEM((d, n_half), BF16), pltpu.VMEM((TILE + 2 * HALO, width), F32)],
        compiler_params=_cparams(("arbitrary",)),
        name="inproj",
    )(x, ctx, mod3, g, w_in, conv_w, rope)


def _block_diag(y, bd_mask):
    yb = y.astype(BF16)
    return jnp.where(bd_mask, jnp.concatenate([yb] * N_HEADS, axis=0), jnp.zeros((), BF16))


def _packed_matmul(x, y_bd):
    return _dot(x.astype(BF16), y_bd)


def _unit_lower_inverse_stages(a_list, eye, m16, m32, m64, bd_mask, out):
    pm = _packed_matmul
    ps = [-jnp.where(m16, a, 0.0) for a in a_list]
    xs = [eye + p for p in ps]
    ps = [p.astype(BF16) for p in ps]
    ps = [pm(p, _block_diag(p, bd_mask)).astype(BF16) for p in ps]
    yield
    for _ in range(2):
        prods = [_dot(jnp.concatenate([x.astype(BF16), p], axis=0), _block_diag(p, bd_mask))
                 for x, p in zip(xs, ps)]
        xs = [x + pr[:CHUNK] for x, pr in zip(xs, prods)]
        ps = [pr[CHUNK:].astype(BF16) for pr in prods]
        yield
    xs = [x + pm(x, _block_diag(p, bd_mask)) for x, p in zip(xs, ps)]
    yield
    for m in (m32, m64):
        zs = [pm(jnp.where(m, a, 0.0), _block_diag(x, bd_mask)).astype(BF16) for a, x in zip(a_list, xs)]
        yield
        xs = [x - pm(x, _block_diag(z, bd_mask)) for x, z in zip(xs, zs)]
        yield
    out.extend(xs)


def _gdn_front_stages(groups, ring_base, mid_base, a_scr, rhs_scr, brow_scr, wq_scr, qkd_scr, kdt_scr,
                      el_scr):
    lane = lax.broadcasted_iota(jnp.int32, (CHUNK, GATE_LANES), 1)
    low_half = lax.broadcasted_iota(jnp.int32, (CHUNK, HEAD_DIM), 1) < CHUNK
    zero_k = jnp.zeros((CHUNK, HEAD_DIM), BF16)
    for g in groups:
        tcum = g["consts"][0]
        g_hi, g_lo = _split(g["gates"])
        g["gc"] = _dot(tcum, g_hi) + _dot(tcum, g_lo)
    yield

    for gi, g in enumerate(groups):
        g_lane0 = N_HEADS if g["rev"] else 0
        b_lane0 = 2 * N_HEADS + g_lane0
        last = 0 if g["rev"] else CHUNK - 1
        gc, gates = g["gc"], g["gates"]
        ct = jnp.where(lane < 2 * N_HEADS, gc, gates).T
        egl_t = jnp.exp(ct[:, last:last + 1] - ct)
        gcb, kb, ms = [], [], []
        for h in range(N_HEADS):
            lo = h * HEAD_DIM
            gl = g_lane0 + h
            ring_unit = (ring_base + gi) * N_HEADS + h
            gcb.append(_lane_bcast(gc[:, gl:gl + 1], HEAD_DIM))
            beta = _lane_bcast(gates[:, b_lane0 + h:b_lane0 + h + 1], HEAD_DIM)
            eg_h = jnp.exp(gcb[h])
            q_h = g["q"][:, lo:lo + HEAD_DIM]
            k_h = g["k"][:, lo:lo + HEAD_DIM].astype(F32)
            kb.append((k_h * beta).astype(BF16))
            rhs_scr[(mid_base + gi) * N_HEADS + h] = jnp.concatenate(
                [g["v"][:, lo:lo + HEAD_DIM], (k_h * eg_h).astype(BF16)], axis=1)
            wq_scr[ring_unit, CHUNK:, :] = (q_h * eg_h).astype(BF16)
            kdt_scr[ring_unit] = (k_h.T * egl_t[gl:gl + 1, :]).astype(BF16)
        d_parts = []
        for p in range(N_HEADS // 2):
            h0, h1 = 2 * p, 2 * p + 1
            lo = h0 * HEAD_DIM
            lhs = jnp.concatenate([g["q"][:, lo:lo + 2 * HEAD_DIM], jnp.concatenate([kb[h0], kb[h1]], axis=1)],
                                  axis=0)
            k0 = g["k"][:, lo:lo + HEAD_DIM]
            k1 = g["k"][:, lo + HEAD_DIM:lo + 2 * HEAD_DIM]
            rhs_bd = jnp.concatenate([jnp.concatenate([k0, zero_k], axis=1),
                                      jnp.concatenate([zero_k, k1], axis=1)], axis=0)
            ms.append(_dot_nt(lhs, rhs_bd))
            g_row = jnp.concatenate([ct[g_lane0 + h0:g_lane0 + h0 + 1, :], ct[g_lane0 + h1:g_lane0 + h1 + 1, :]],
                                    axis=1)
            d_parts.append(jnp.where(low_half, gcb[h0], gcb[h1]) - g_row)
        incl4, strict4 = g["consts"][2:]
        e = jnp.exp(jnp.where(incl4, jnp.concatenate(d_parts, axis=1), 0.0))
        a_scr[mid_base + gi] = jnp.where(strict4, jnp.concatenate([m[CHUNK:] for m in ms], axis=1) * e, 0.0)
        qkd_scr[ring_base + gi] = jnp.where(incl4, jnp.concatenate([m[:CHUNK] for m in ms], axis=1) * e,
                                            0.0).astype(BF16)
        el_scr[ring_base + gi] = jnp.exp(gc[last:last + 1, :])
        brow_scr[mid_base + gi] = jnp.concatenate(
            [ct[b_lane0 + h:b_lane0 + h + 1, :] for h in range(N_HEADS)], axis=1)
        yield


def _gdn_solve_stages(n_groups, masks, ring_base, mid_base, a_scr, rhs_scr, brow_scr, u_scr, wq_scr):
    eye4, m16, m32, m64, bd_mask = masks
    t_invs = []
    yield from _unit_lower_inverse_stages([a_scr[mid_base + gi] for gi in range(n_groups)],
                                          eye4, m16, m32, m64, bd_mask, t_invs)
    for gi, t_inv in enumerate(t_invs):
        t_b = (t_inv * brow_scr[mid_base + gi]).astype(BF16)
        for h in range(N_HEADS):
            sol = _dot(t_b[:, h * CHUNK:(h + 1) * CHUNK], rhs_scr[(mid_base + gi) * N_HEADS + h])
            ring_unit = (ring_base + gi) * N_HEADS + h
            u_scr[ring_unit] = sol[:, :HEAD_DIM]
            wq_scr[ring_unit, :CHUNK, :] = sol[:, HEAD_DIM:].astype(BF16)


def _gdn_scan_stages(slot_base, keep_state, u_scr, wq_scr, qkd_scr, kdt_scr, el_scr, s_ref, of_ref, ob_ref):
    for j in range(CHUNKS_PER_TILE):
        units = [(slot_base + 2 * j + d, h, d * N_HEADS + h) for d in range(2) for h in range(N_HEADS)]
        states = [jnp.where(keep_state, s_ref[slot], 0.0) if j == 0 else s_ref[slot] for _, _, slot in units]
        wss = [_dot(wq_scr[gi * N_HEADS + h], s.astype(BF16)) for (gi, h, _), s in zip(units, states)]
        yield
        v_news = [(u_scr[gi * N_HEADS + h] - ws[:CHUNK]).astype(BF16) for (gi, h, _), ws in zip(units, wss)]
        outs = [ws[CHUNK:] + _dot(qkd_scr[gi][:, h * CHUNK:(h + 1) * CHUNK], vn)
                for (gi, h, _), ws, vn in zip(units, wss, v_news)]
        for (gi, h, slot), s, vn in zip(units, states, v_news):
            s_ref[slot] = el_scr[gi][:, slot:slot + 1] * s + _dot(kdt_scr[gi * N_HEADS + h], vn)
        rf = j * CHUNK
        rb = (CHUNKS_PER_TILE - 1 - j) * CHUNK
        of_ref[0, rf:rf + CHUNK, :] = jnp.concatenate(outs[:N_HEADS], axis=1).astype(BF16)
        ob_ref[0, rb:rb + CHUNK, :] = jnp.concatenate(outs[N_HEADS:], axis=1).astype(BF16)
        yield


def _interleave(*stage_generators):
    live = list(stage_generators)
    while live:
        for gen in list(live):
            try:
                next(gen)
            except StopIteration:
                live.remove(gen)


def _gate_tile(ab, dtb, a_log):
    z = ab + dtb
    softplus = jnp.maximum(z, 0.0) + jnp.log(1.0 + jnp.exp(-jnp.abs(z)))
    lane = lax.broadcasted_iota(jnp.int32, ab.shape, 1)
    return jnp.where(lane < 2 * N_HEADS, -jnp.exp(a_log) * softplus, jax.nn.sigmoid(ab))


def _gdn_kernel(qkvf_ref, abf_ref, qkvb_ref, abb_ref, dtb_ref, alog_ref, rqkvf_ref, rqkvb_ref, logit_ref,
                w1_ref, w2_ref, w3_ref, of_ref, ob_ref, rof_ref, rob_ref, w1b_ref, w2b_ref, w3b_ref,
                s_ref, a_scr, rhs_scr, brow_scr, u_scr, wq_scr, qkd_scr, kdt_scr, el_scr,
                r_ref, decay_ref, xi_ref, zeta_ref, *, tiles_per_sample):
    step = pl.program_id(0)
    lg_row = _log_sigmoid(logit_ref[...])
    for src, dst in ((w1_ref, w1b_ref), (w2_ref, w2b_ref), (w3_ref, w3b_ref)):
        dst[...] = src[0].astype(BF16)
    keep_state = lax.rem(step + tiles_per_sample - 2, tiles_per_sample) != 0
    groups_per_tile = 2 * CHUNKS_PER_TILE
    front_ring = lax.rem(step, 3) * groups_per_tile
    solve_ring = lax.rem(step + 2, 3) * groups_per_tile
    scan_ring = lax.rem(step + 1, 3) * groups_per_tile
    front_mid = lax.rem(step, 2) * groups_per_tile
    solve_mid = groups_per_tile - front_mid

    @pl.when(step == 0)
    def _():
        for scr in (s_ref, a_scr, rhs_scr, brow_scr, u_scr, wq_scr, qkd_scr, kdt_scr, el_scr, r_ref):
            scr[...] = jnp.zeros_like(scr)
        _ret_tables(lg_row, decay_ref, xi_ref, zeta_ref)

    ri = lax.broadcasted_iota(jnp.int32, (CHUNK, CHUNK), 0)
    ci = lax.broadcasted_iota(jnp.int32, (CHUNK, CHUNK), 1)
    ri4 = lax.broadcasted_iota(jnp.int32, (CHUNK, N_HEADS * CHUNK), 0)
    ci4 = lax.broadcasted_iota(jnp.int32, (CHUNK, N_HEADS * CHUNK), 1) % CHUNK
    rbd = lax.broadcasted_iota(jnp.int32, (N_HEADS * CHUNK, N_HEADS * CHUNK), 0) // CHUNK
    cbd = lax.broadcasted_iota(jnp.int32, (N_HEADS * CHUNK, N_HEADS * CHUNK), 1) // CHUNK
    bd_mask = rbd == cbd
    eye4 = (ri4 == ci4).astype(F32)
    m16 = (ri4 // 16) == (ci4 // 16)
    m32 = jnp.logical_and((ri4 // 32) == (ci4 // 32), jnp.logical_not(m16))
    m64 = (ri4 // 32) != (ci4 // 32)

    cf = ((ci <= ri).astype(BF16), None, ri4 >= ci4, ri4 > ci4)
    cb = ((ci >= ri).astype(BF16), None, ri4 <= ci4, ri4 < ci4)
    dtb = dtb_ref[...]
    alog = alog_ref[...]

    groups = []
    for j in range(CHUNKS_PER_TILE):
        rf = j * CHUNK
        rb = (CHUNKS_PER_TILE - 1 - j) * CHUNK
        for ref, ab_ref, r0, rev, consts in ((qkvf_ref, abf_ref, rf, False, cf), (qkvb_ref, abb_ref, rb, True, cb)):
            groups.append(dict(q=ref[0, r0:r0 + CHUNK, 0:GROUP_W], k=ref[0, r0:r0 + CHUNK, GROUP_W:2 * GROUP_W],
                               v=ref[0, r0:r0 + CHUNK, 2 * GROUP_W:],
                               gates=_gate_tile(ab_ref[0, r0:r0 + CHUNK, :], dtb, alog), rev=rev, consts=consts))
    _interleave(
        _ret_stages(rqkvf_ref, rqkvb_ref, rof_ref, rob_ref, lg_row, keep_state, r_ref, decay_ref, xi_ref, zeta_ref),
        _gdn_solve_stages(len(groups), (eye4, m16, m32, m64, bd_mask), solve_ring, solve_mid,
                          a_scr, rhs_scr, brow_scr, u_scr, wq_scr),
        _gdn_scan_stages(scan_ring, keep_state, u_scr, wq_scr, qkd_scr, kdt_scr, el_scr, s_ref, of_ref, ob_ref),
        _gdn_front_stages(groups, front_ring, front_mid, a_scr, rhs_scr, brow_scr, wq_scr, qkd_scr, kdt_scr,
                          el_scr))


def _bwd_tile(s, n_tiles):
    return jnp.where(s == 0, 0, n_tiles - s)


def _latent_block(tile, bwd, n_tiles):
    return jnp.where(tile == 0, n_tiles - 2 if bwd else 0, tile - 1)


def _rider(w, n_steps):
    _, rows, cols = w.shape
    block = next(r for r in range(16, rows + 1, 16) if rows % r == 0 and rows // r <= n_steps)
    last = rows // block - 1
    return (pl.BlockSpec((1, block, cols), lambda f: (0, jnp.minimum(f, last), 0)),
            pl.BlockSpec((block, cols), lambda f: (jnp.minimum(f, last), 0)),
            jax.ShapeDtypeStruct((rows, cols), BF16))


def _mixers(qkv, ab, dtb_row, alog_row, pr, logit_row, cast_weights):
    b, t_total, _ = qkv.shape
    n_tiles = t_total // TILE

    total = b * n_tiles
    riders = [_rider(w, total + 2) for w in cast_weights]

    def block_of(f, bwd, lag, latent):
        flat = jnp.maximum(f - 2, 0) if lag else jnp.minimum(f, total - 1)
        pos = flat % n_tiles
        tile = _bwd_tile(pos, n_tiles) if bwd else pos
        return (flat // n_tiles, _latent_block(tile, bwd, n_tiles) if latent else tile, 0)

    def tok(width, bwd, lag=False, latent=False):
        return pl.BlockSpec((1, TILE, width), lambda f: block_of(f, bwd, lag, latent))

    row = pl.BlockSpec((1, GATE_LANES), lambda f: (0, 0))
    per_tile = 2 * CHUNKS_PER_TILE
    n_groups = 3 * per_tile
    n_units = n_groups * N_HEADS
    n_mid = 2 * per_tile
    return pl.pallas_call(
        functools.partial(_gdn_kernel, tiles_per_sample=n_tiles),
        grid=(total + 2,),
        in_specs=[tok(3 * GROUP_W, False), tok(GATE_LANES, False),
                  tok(3 * GROUP_W, True), tok(GATE_LANES, True), row, row,
                  tok(3 * GROUP_W, False, lag=True), tok(3 * GROUP_W, True, lag=True), row]
        + [r[0] for r in riders],
        out_specs=[tok(GROUP_W, bwd, lag=True, latent=True) for bwd in (False, True, False, True)]
        + [r[1] for r in riders],
        out_shape=[jax.ShapeDtypeStruct((b, t_total - TILE, GROUP_W), BF16)] * 4 + [r[2] for r in riders],
        scratch_shapes=[pltpu.VMEM((2 * N_HEADS, HEAD_DIM, HEAD_DIM), F32),
                        pltpu.VMEM((n_mid, CHUNK, N_HEADS * CHUNK), F32),
                        pltpu.VMEM((n_mid * N_HEADS, CHUNK, 2 * HEAD_DIM), BF16),
                        pltpu.VMEM((n_mid, 1, N_HEADS * CHUNK), F32),
                        pltpu.VMEM((n_units, CHUNK, HEAD_DIM), F32),
                        pltpu.VMEM((n_units, 2 * CHUNK, HEAD_DIM), BF16),
                        pltpu.VMEM((n_groups, CHUNK, N_HEADS * CHUNK), BF16),
                        pltpu.VMEM((n_units, HEAD_DIM, CHUNK), BF16),
                        pltpu.VMEM((n_groups, 1, GATE_LANES), F32),
                        pltpu.VMEM((2 * N_HEADS, HEAD_DIM, HEAD_DIM), F32),
                        pltpu.VMEM((2 * N_HEADS, TILE, TILE), F32),
                        pltpu.VMEM((2 * N_HEADS, TILE, HEAD_DIM), F32),
                        pltpu.VMEM((2 * N_HEADS, TILE, HEAD_DIM), F32)],
        compiler_params=_cparams(("arbitrary",)),
        name="mixers",
    )(qkv, ab, qkv, ab, dtb_row, alog_row, pr, pr, logit_row, *cast_weights)


def _log_sigmoid(x):
    return jnp.minimum(x, 0.0) - jnp.log(1.0 + jnp.exp(-jnp.abs(x)))


def _ret_tables(lg_row, decay_ref, xi_ref, zeta_ref):
    ri = lax.broadcasted_iota(jnp.int32, (TILE, TILE), 0)
    ci = lax.broadcasted_iota(jnp.int32, (TILE, TILE), 1)
    row = lax.broadcasted_iota(jnp.int32, (TILE, HEAD_DIM), 0).astype(F32)
    for slot in range(2 * N_HEADS):
        rev = slot >= N_HEADS
        lg = lg_row[:, slot:slot + 1]
        diff = ((ci - ri) if rev else (ri - ci)).astype(F32)
        pos = ((TILE - 1.0) - row) if rev else row
        decay_ref[slot] = jnp.where(diff >= 0, jnp.exp(lg * jnp.maximum(diff, 0.0)), 0.0)
        xi_ref[slot] = jnp.exp(lg * (pos + 1.0))
        zeta_ref[slot] = jnp.exp(lg * ((TILE - 1.0) - pos))


def _ret_stages(qkvf_ref, qkvb_ref, of_ref, ob_ref, lg_row, keep_state, r_ref, decay_ref, xi_ref, zeta_ref):
    units = [(refs, (N_HEADS if rev else 0) + h, h * HEAD_DIM)
             for refs, rev in (((qkvf_ref, of_ref), False), ((qkvb_ref, ob_ref), True))
             for h in range(N_HEADS)]
    qs = [refs[0][0, :, lo:lo + HEAD_DIM] for refs, _, lo in units]
    ks = [refs[0][0, :, GROUP_W + lo:GROUP_W + lo + HEAD_DIM] for refs, _, lo in units]
    vs = [refs[0][0, :, 2 * GROUP_W + lo:2 * GROUP_W + lo + HEAD_DIM] for refs, _, lo in units]
    per_yield = 2
    qks = []
    for i, (q, k) in enumerate(zip(qs, ks)):
        qks.append(_dot_nt(q, k))
        if i % per_yield == per_yield - 1:
            yield
    rs = [jnp.where(keep_state, r_ref[slot], 0.0) for _, slot, _ in units]
    for i, ((refs, slot, lo), q, k, v, qk, r) in enumerate(zip(units, qs, ks, vs, qks, rs)):
        lhs = jnp.concatenate([(qk * decay_ref[slot]).astype(BF16), (k * zeta_ref[slot]).T.astype(BF16)], axis=0)
        both = _dot(lhs, v)
        refs[1][0, :, lo:lo + HEAD_DIM] = (
            both[:TILE] + _dot((q * xi_ref[slot]).astype(BF16), r.astype(BF16))).astype(BF16)
        g_chunk = jnp.exp(lg_row[:, slot:slot + 1] * float(TILE))
        r_ref[slot] = g_chunk * r + both[TILE:]
        if i % per_yield == per_yield - 1:
            yield


def _tail_kernel(gf_ref, gb_ref, z_ref, rf_ref, rb_ref, rg_ref, x_ref, mod_ref, gdn_g_ref, ret_g_ref,
                 nffn_g_ref, w_ref, wi_ref, wo_ref, fg_ref, o_ref, y_ref, *, d_model, d_ff):
    o = gf_ref[0].astype(F32) + gb_ref[0].astype(F32)
    r = rf_ref[0].astype(F32) + rb_ref[0].astype(F32)
    z = z_ref[0].astype(F32)
    rg = rg_ref[0].astype(F32)
    for h in range(N_HEADS):
        lo = h * HEAD_DIM
        o_h = o[:, lo:lo + HEAD_DIM]
        o_n = o_h * lax.rsqrt(jnp.mean(o_h * o_h, axis=-1, keepdims=True) + NORM_EPS) * gdn_g_ref[...]
        y_ref[:, lo:lo + HEAD_DIM] = (o_n * _silu(z[:, lo:lo + HEAD_DIM])).astype(BF16)
        r_h = r[:, lo:lo + HEAD_DIM]
        mu = jnp.mean(r_h, axis=-1, keepdims=True)
        cen = r_h - mu
        var = jnp.mean(cen * cen, axis=-1, keepdims=True)
        r_n = cen * lax.rsqrt(var + NORM_EPS) * ret_g_ref[...]
        y_ref[:, GROUP_W + lo:GROUP_W + lo + HEAD_DIM] = (r_n * _silu(rg[:, lo:lo + HEAD_DIM])).astype(BF16)
    mod = mod_ref[0]
    x1 = x_ref[0] + mod[:, 2 * d_model:3 * d_model] * _dot(y_ref[...], w_ref[...])
    h2 = _rms_mod(x1, nffn_g_ref[...], mod[:, 3 * d_model:4 * d_model],
                  mod[:, 4 * d_model:5 * d_model]).astype(BF16)
    acc = None
    for lo in range(0, d_ff, FFN_CHUNK):
        hi = min(lo + FFN_CHUNK, d_ff)
        gate = _dot(h2, wi_ref[:, lo:hi])
        up = _dot(h2, wi_ref[:, d_ff + lo:d_ff + hi])
        part = _dot((_silu(gate) * up).astype(BF16), wo_ref[lo:hi, :])
        acc = part if acc is None else acc + part
    x2 = x1 + mod[:, 5 * d_model:6 * d_model] * acc
    ms = jnp.mean(x2 * x2, axis=-1, keepdims=True)
    o_ref[0] = x2 * lax.rsqrt(ms + NORM_EPS) * fg_ref[...]


def _tail(gdn_f, gdn_b, z, ret_f, ret_b, rg, x, mod3, gdn_g, ret_g, nffn_g, w_out, w_ffn_in, w_ffn_out,
          final_g):
    b, seq, d = x.shape
    d_ff = w_ffn_out.shape[0]
    tok = pl.BlockSpec((1, TAIL_TOK, GROUP_W), lambda i, t: (i, t, 0))

    def resident(shape):
        return pl.BlockSpec(shape, lambda i, t: (0,) * len(shape), pipeline_mode=pl.Buffered(1))

    return pl.pallas_call(
        functools.partial(_tail_kernel, d_model=d, d_ff=d_ff),
        grid=(b, seq // TAIL_TOK),
        in_specs=[tok, tok, tok, tok, tok, tok,
                  pl.BlockSpec((1, TAIL_TOK, d), lambda i, t: (i, t, 0)),
                  pl.BlockSpec((1, 1, mod3.shape[2]), lambda i, t: (i, 0, 0)),
                  resident((1, HEAD_DIM)), resident((1, HEAD_DIM)), resident((1, d)),
                  resident(w_out.shape), resident(w_ffn_in.shape), resident(w_ffn_out.shape),
                  resident((1, d))],
        out_specs=pl.BlockSpec((1, TAIL_TOK, d), lambda i, t: (i, t, 0)),
        out_shape=jax.ShapeDtypeStruct((b, seq, d), F32),
        scratch_shapes=[pltpu.VMEM((TAIL_TOK, 2 * GROUP_W), BF16)],
        compiler_params=_cparams(("parallel", "parallel")),
        name="tail",
    )(gdn_f, gdn_b, z, ret_f, ret_b, rg, x, mod3, gdn_g, ret_g, nffn_g, w_out, w_ffn_in, w_ffn_out, final_g)


def _rope_tables(ctx_len, n_lat):
    def angles(pos, n_pairs):
        inv = ROPE_THETA ** (-np.arange(n_pairs, dtype=np.float64) / n_pairs)
        return pos[:, None] * inv[None, :]

    rows = n_lat // GRID_W
    row = np.repeat(np.arange(rows, dtype=np.float64), GRID_W)
    col = np.tile(np.arange(GRID_W, dtype=np.float64), rows)
    zeros = np.zeros((ctx_len,), np.float64)
    p_seq = np.concatenate([np.arange(ctx_len, dtype=np.float64), np.full((n_lat,), float(ctx_len))])
    ang = np.concatenate([angles(p_seq, ROPE_PAIRS[0]),
                          angles(np.concatenate([zeros, row]), ROPE_PAIRS[1]),
                          angles(np.concatenate([zeros, col]), ROPE_PAIRS[2])], axis=-1)
    cos, sin = np.cos(ang).astype(np.float32), np.sin(ang).astype(np.float32)
    return jnp.asarray(np.concatenate([cos, cos, -sin, sin], axis=-1))


def _lane_row(values):
    flat = values.reshape(1, -1).astype(F32)
    return jnp.pad(flat, ((0, 0), (0, GATE_LANES - flat.shape[1])))


def kernel(x, c, ctx, c_ctx, ada_w, ada_b, norm_mix_g, norm_ffn_g, w_in, conv_w, gdn_a_log, gdn_dt_bias,
           gdn_norm_g, ret_decay_logit, ret_norm_g, w_out, w_ffn_in, w_ffn_out, final_g):
    assert ada_w.shape[0] == 1, "single-layer block"
    b, seq, d = x.shape
    ctx_len = ctx.shape[1]
    assert ctx_len == TILE and seq % TILE == 0 and seq % TAIL_TOK == 0 and b + 1 <= 8

    cond = jnp.concatenate([c, c_ctx[None, :], jnp.zeros((8 - b - 1, d), F32)], axis=0)
    mod = _ada(cond, ada_w, ada_b)
    mod3 = mod[:, None, :]

    n_qkvz = 4 * GROUP_W
    n_gate = 4 * N_HEADS
    qkv, z, ab, pr, rg = _inproj(x, ctx, mod3, norm_mix_g, jnp.swapaxes(w_in, 1, 2), conv_w, n_qkvz, n_gate,
                                 _rope_tables(ctx_len, seq))

    dtb_row = _lane_row(gdn_dt_bias[0])
    gdn_f, gdn_b, ret_f, ret_b, w_out_b, w_ffn_in_b, w_ffn_out_b = _mixers(
        qkv, ab, dtb_row, _lane_row(gdn_a_log[0]), pr, _lane_row(ret_decay_logit[0]),
        (w_out, w_ffn_in, w_ffn_out))

    return _tail(gdn_f, gdn_b, z, ret_f, ret_b, rg, x, mod3, gdn_norm_g, ret_norm_g, norm_ffn_g,
                 w_out_b, w_ffn_in_b, w_ffn_out_b, final_g[None, :])
```

```python
import functools
import math

import jax
import jax.numpy as jnp
import numpy as np
from jax import lax
from jax.experimental import pallas as pl
from jax.experimental.pallas import tpu as pltpu

F32 = jnp.float32
BF16 = jnp.bfloat16

HEAD_DIM = 128
N_HEADS = 4
GROUP_W = N_HEADS * HEAD_DIM
CONV_K = 5
GRID_W = 64
ROPE_THETA = 10000.0
ROPE_PAIRS = (16, 24, 24)
NORM_EPS = 1e-6

CHUNK = 64
TILE = 256
CHUNKS_PER_TILE = TILE // CHUNK
HALO = 16
GATE_LANES = 128
TAIL_TOK = 512
FFN_CHUNK = 1024
VMEM_LIMIT = 56 * 1024 * 1024


def _cparams(sem):
    return pltpu.CompilerParams(dimension_semantics=sem, vmem_limit_bytes=VMEM_LIMIT)


def _dot(a, b):
    return jnp.dot(a, b, preferred_element_type=F32)


def _dot_nt(a, b):
    return lax.dot_general(a, b, (((1,), (1,)), ((), ())), preferred_element_type=F32)


def _split(a):
    hi = a.astype(BF16)
    lo = (a - hi.astype(F32)).astype(BF16)
    return hi, lo


def _silu(x):
    return x * jax.nn.sigmoid(x)


def _lane_bcast(col, width):
    return jnp.broadcast_to(col, (col.shape[0], width))


def _ada_kernel(cond_ref, w_ref, b_ref, o_ref):
    a_hi, a_lo = _split(_silu(cond_ref[...]))
    w_hi, w_lo = _split(w_ref[0])
    o_ref[...] = _dot(a_hi, w_hi) + _dot(a_lo, w_hi) + _dot(a_hi, w_lo) + b_ref[...]


def _ada(cond, w, b):
    rows, d = cond.shape
    n = w.shape[2]
    bn = 1536
    return pl.pallas_call(
        _ada_kernel,
        grid=(n // bn,),
        in_specs=[pl.BlockSpec((rows, d), lambda j: (0, 0)),
                  pl.BlockSpec((1, d, bn), lambda j: (0, 0, j)),
                  pl.BlockSpec((1, bn), lambda j: (0, j))],
        out_specs=pl.BlockSpec((rows, bn), lambda j: (0, j)),
        out_shape=jax.ShapeDtypeStruct((rows, n), F32),
        compiler_params=_cparams(("parallel",)),
        name="ada",
    )(cond, w, b)


def _rms_mod(x, g, shift, scale):
    ms = jnp.mean(x * x, axis=-1, keepdims=True)
    return (x * lax.rsqrt(ms + NORM_EPS) * g) * (1.0 + scale) + shift


def _conv_tile(ext_ref, w_ref, o_ref):
    pad = (CONV_K - 1) // 2
    rows = TILE + 2 * HALO
    q_scale = HEAD_DIM ** -0.5
    for part in range(3):
        for hd in range(N_HEADS):
            lo = part * GROUP_W + hd * HEAD_DIM
            slab = ext_ref[:, lo:lo + HEAD_DIM]
            acc = None
            for i in range(CONV_K):
                shifted = slab if i == pad else pltpu.roll(slab, (pad - i) % rows, 0)
                term = shifted[HALO:HALO + TILE] * w_ref[0, i:i + 1, lo:lo + HEAD_DIM]
                acc = term if acc is None else acc + term
            y = _silu(acc)
            if part < 2:
                y = y * lax.rsqrt(jnp.sum(y * y, axis=-1, keepdims=True) + NORM_EPS)
            if part == 0:
                y = y * q_scale
            o_ref[0, :, lo:lo + HEAD_DIM] = y.astype(BF16)


def _inproj_kernel(x_ref, ctx_ref, mod_ref, g_ref, w_ref, cw_ref, rope_ref,
                   qkv_ref, z_ref, ab_ref, pr_ref, rg_ref, wa_ref, wab_ref, wr_ref, ext_ref,
                   *, d_model, n_half, n_gate, tiles_per_sample):
    step = pl.program_id(0)
    t = lax.rem(jnp.minimum(step, pl.num_programs(0) - 2), tiles_per_sample)
    t_conv = lax.rem(jnp.maximum(step - 1, 0), tiles_per_sample)
    width = 3 * GROUP_W

    @pl.when(step == 0)
    def _():
        wa_ref[...] = w_ref[0, :n_half, :].T.astype(BF16)
        wr_ref[...] = w_ref[0, n_half + n_gate:, :].T.astype(BF16)
        lane = lax.broadcasted_iota(jnp.int32, (d_model, GATE_LANES), 1)
        wab_ref[...] = jnp.where(lane < n_gate, w_ref[0, n_half:n_half + GATE_LANES, :].T, 0.0).astype(BF16)
        ext_ref[...] = jnp.zeros_like(ext_ref)

    xin = jnp.where(t == 0, ctx_ref[0], x_ref[0])
    mod = mod_ref[0]
    h = _rms_mod(xin, g_ref[...], mod[:, 0:d_model], mod[:, d_model:2 * d_model]).astype(BF16)
    pa = _dot(h, wa_ref[...])
    z_ref[0] = pa[:, width:].astype(BF16)

    has_next = jnp.logical_and(t_conv >= 1, t_conv < tiles_per_sample - 1)
    ext_ref[HALO + TILE:, :] = jnp.where(has_next, pa[:HALO, :width], 0.0)
    _conv_tile(ext_ref, cw_ref, qkv_ref)
    ext_ref[0:HALO, :] = jnp.where(t >= 2, ext_ref[TILE:HALO + TILE, :], 0.0)
    ext_ref[HALO:HALO + TILE, :] = pa[:, :width]

    ab_ref[0] = _dot(h, wab_ref[...])
    pr = _dot(h, wr_ref[...])
    cos2 = rope_ref[:, :HEAD_DIM]
    sin2 = rope_ref[:, HEAD_DIM:]
    k_scale = HEAD_DIM ** -0.5
    for hd in range(2 * N_HEADS):
        lo = hd * HEAD_DIM
        tt = pr[:, lo:lo + HEAD_DIM]
        rot = tt * cos2 + pltpu.roll(tt, HEAD_DIM // 2, 1) * sin2
        if hd >= N_HEADS:
            rot = rot * k_scale
        pr_ref[0, :, lo:lo + HEAD_DIM] = rot.astype(BF16)
    pr_ref[0, :, 2 * GROUP_W:] = pr[:, 2 * GROUP_W:width].astype(BF16)
    rg_ref[0] = pr[:, width:].astype(BF16)


def _inproj(x, ctx, mod3, g, w_in, conv_w, n_half, n_gate, rope):
    b, seq, d = x.shape
    n_tiles = (ctx.shape[1] + seq) // TILE
    t_total = n_tiles * TILE
    total = b * n_tiles
    width = 3 * GROUP_W
    assert w_in.shape[1] == 2 * n_half + n_gate and n_gate <= GATE_LANES and n_half == width + GROUP_W

    def now(f):
        flat = jnp.minimum(f, total - 1)
        return flat // n_tiles, flat % n_tiles

    def lagged(f):
        flat = jnp.maximum(f - 1, 0)
        return flat // n_tiles, flat % n_tiles

    def tok(cols):
        return pl.BlockSpec((1, TILE, cols), lambda f: (now(f)[0], now(f)[1], 0))

    def lat(cols):
        return pl.BlockSpec((1, TILE, cols), lambda f: (now(f)[0], jnp.maximum(now(f)[1] - 1, 0), 0))

    return pl.pallas_call(
        functools.partial(_inproj_kernel, d_model=d, n_half=n_half, n_gate=n_gate, tiles_per_sample=n_tiles),
        grid=(total + 1,),
        in_specs=[pl.BlockSpec((1, TILE, d), lambda f: (now(f)[0], jnp.maximum(now(f)[1] - 1, 0), 0)),
                  pl.BlockSpec((1, TILE, d), lambda f: (now(f)[0], 0, 0)),
                  pl.BlockSpec((1, 1, mod3.shape[2]), lambda f: (jnp.where(now(f)[1] == 0, b, now(f)[0]), 0, 0)),
                  pl.BlockSpec((1, d), lambda f: (0, 0)),
                  pl.BlockSpec(w_in.shape, lambda f: (0, 0, 0), pipeline_mode=pl.Buffered(1)),
                  pl.BlockSpec(conv_w.shape, lambda f: (0, 0, 0)),
                  pl.BlockSpec((TILE, 2 * HEAD_DIM), lambda f: (now(f)[1], 0))],
        out_specs=[pl.BlockSpec((1, TILE, width), lambda f: (lagged(f)[0], lagged(f)[1], 0)),
                   lat(GROUP_W), tok(GATE_LANES), tok(width), lat(GROUP_W)],
        out_shape=[jax.ShapeDtypeStruct((b, t_total, width), BF16),
                   jax.ShapeDtypeStruct((b, seq, GROUP_W), BF16),
                   jax.ShapeDtypeStruct((b, t_total, GATE_LANES), F32),
                   jax.ShapeDtypeStruct((b, t_total, width), BF16),
                   jax.ShapeDtypeStruct((b, seq, GROUP_W), BF16)],
        scratch_shapes=[pltpu.VMEM((d, n_half), BF16), pltpu.VMEM((d, GATE_LANES), BF16),
                        pltpu.VMEM((d, n_half), BF16), pltpu.VMEM((TILE + 2 * HALO, width), F32)],
        compiler_params=_cparams(("arbitrary",)),
        name="inproj",
    )(x, ctx, mod3, g, w_in, conv_w, rope)


def _block_diag(y, bd_mask):
    yb = y.astype(BF16)
    return jnp.where(bd_mask, jnp.concatenate([yb] * N_HEADS, axis=0), jnp.zeros((), BF16))


def _packed_matmul(x, y_bd):
    return _dot(x.astype(BF16), y_bd)


def _unit_lower_inverse_stages(a_list, eye, m16, m32, m64, bd_mask, out):
    pm = _packed_matmul
    ps = [-jnp.where(m16, a, 0.0) for a in a_list]
    xs = [eye + p for p in ps]
    ps = [p.astype(BF16) for p in ps]
    ps = [pm(p, _block_diag(p, bd_mask)).astype(BF16) for p in ps]
    yield
    for _ in range(2):
        prods = [_dot(jnp.concatenate([x.astype(BF16), p], axis=0), _block_diag(p, bd_mask))
                 for x, p in zip(xs, ps)]
        xs = [x + pr[:CHUNK] for x, pr in zip(xs, prods)]
        ps = [pr[CHUNK:].astype(BF16) for pr in prods]
        yield
    xs = [x + pm(x, _block_diag(p, bd_mask)) for x, p in zip(xs, ps)]
    yield
    for m in (m32, m64):
        zs = [pm(jnp.where(m, a, 0.0), _block_diag(x, bd_mask)).astype(BF16) for a, x in zip(a_list, xs)]
        yield
        xs = [x - pm(x, _block_diag(z, bd_mask)) for x, z in zip(xs, zs)]
        yield
    out.extend(xs)


def _gdn_front_stages(groups, ring_base, mid_base, a_scr, rhs_scr, brow_scr, wq_scr, qkd_scr, kdt_scr,
                      el_scr):
    lane = lax.broadcasted_iota(jnp.int32, (CHUNK, GATE_LANES), 1)
    low_half = lax.broadcasted_iota(jnp.int32, (CHUNK, HEAD_DIM), 1) < CHUNK
    zero_k = jnp.zeros((CHUNK, HEAD_DIM), BF16)
    for g in groups:
        tcum = g["consts"][0]
        g_hi, g_lo = _split(g["gates"])
        g["gc"] = _dot(tcum, g_hi) + _dot(tcum, g_lo)
    yield

    for gi, g in enumerate(groups):
        g_lane0 = N_HEADS if g["rev"] else 0
        b_lane0 = 2 * N_HEADS + g_lane0
        last = 0 if g["rev"] else CHUNK - 1
        gc, gates = g["gc"], g["gates"]
        ct = jnp.where(lane < 2 * N_HEADS, gc, gates).T
        egl_t = jnp.exp(ct[:, last:last + 1] - ct)
        gcb, kb, ms = [], [], []
        for h in range(N_HEADS):
            lo = h * HEAD_DIM
            gl = g_lane0 + h
            ring_unit = (ring_base + gi) * N_HEADS + h
            gcb.append(_lane_bcast(gc[:, gl:gl + 1], HEAD_DIM))
            beta = _lane_bcast(gates[:, b_lane0 + h:b_lane0 + h + 1], HEAD_DIM)
            eg_h = jnp.exp(gcb[h])
            q_h = g["q"][:, lo:lo + HEAD_DIM]
            k_h = g["k"][:, lo:lo + HEAD_DIM].astype(F32)
            kb.append((k_h * beta).astype(BF16))
            rhs_scr[(mid_base + gi) * N_HEADS + h] = jnp.concatenate(
                [g["v"][:, lo:lo + HEAD_DIM], (k_h * eg_h).astype(BF16)], axis=1)
            wq_scr[ring_unit, CHUNK:, :] = (q_h * eg_h).astype(BF16)
            kdt_scr[ring_unit] = (k_h.T * egl_t[gl:gl + 1, :]).astype(BF16)
        d_parts = []
        for p in range(N_HEADS // 2):
            h0, h1 = 2 * p, 2 * p + 1
            lo = h0 * HEAD_DIM
            lhs = jnp.concatenate([g["q"][:, lo:lo + 2 * HEAD_DIM], jnp.concatenate([kb[h0], kb[h1]], axis=1)],
                                  axis=0)
            k0 = g["k"][:, lo:lo + HEAD_DIM]
            k1 = g["k"][:, lo + HEAD_DIM:lo + 2 * HEAD_DIM]
            rhs_bd = jnp.concatenate([jnp.concatenate([k0, zero_k], axis=1),
                                      jnp.concatenate([zero_k, k1], axis=1)], axis=0)
            ms.append(_dot_nt(lhs, rhs_bd))
            g_row = jnp.concatenate([ct[g_lane0 + h0:g_lane0 + h0 + 1, :], ct[g_lane0 + h1:g_lane0 + h1 + 1, :]],
                                    axis=1)
            d_parts.append(jnp.where(low_half, gcb[h0], gcb[h1]) - g_row)
        incl4, strict4 = g["consts"][2:]
        e = jnp.exp(jnp.where(incl4, jnp.concatenate(d_parts, axis=1), 0.0))
        a_scr[mid_base + gi] = jnp.where(strict4, jnp.concatenate([m[CHUNK:] for m in ms], axis=1) * e, 0.0)
        qkd_scr[ring_base + gi] = jnp.where(incl4, jnp.concatenate([m[:CHUNK] for m in ms], axis=1) * e,
                                            0.0).astype(BF16)
        el_scr[ring_base + gi] = jnp.exp(gc[last:last + 1, :])
        brow_scr[mid_base + gi] = jnp.concatenate(
            [ct[b_lane0 + h:b_lane0 + h + 1, :] for h in range(N_HEADS)], axis=1)
        yield


def _gdn_solve_stages(n_groups, masks, ring_base, mid_base, a_scr, rhs_scr, brow_scr, u_scr, wq_scr):
    eye4, m16, m32, m64, bd_mask = masks
    t_invs = []
    yield from _unit_lower_inverse_stages([a_scr[mid_base + gi] for gi in range(n_groups)],
                                          eye4, m16, m32, m64, bd_mask, t_invs)
    for gi, t_inv in enumerate(t_invs):
        t_b = (t_inv * brow_scr[mid_base + gi]).astype(BF16)
        for h in range(N_HEADS):
            sol = _dot(t_b[:, h * CHUNK:(h + 1) * CHUNK], rhs_scr[(mid_base + gi) * N_HEADS + h])
            ring_unit = (ring_base + gi) * N_HEADS + h
            u_scr[ring_unit] = sol[:, :HEAD_DIM]
            wq_scr[ring_unit, :CHUNK, :] = sol[:, HEAD_DIM:].astype(BF16)


def _gdn_scan_stages(slot_base, keep_state, u_scr, wq_scr, qkd_scr, kdt_scr, el_scr, s_ref, of_ref, ob_ref):
    states = [jnp.where(keep_state, s_ref[slot], 0.0) for slot in range(2 * N_HEADS)]
    for j in range(CHUNKS_PER_TILE):
        units = [(slot_base + 2 * j + d, h, d * N_HEADS + h) for d in range(2) for h in range(N_HEADS)]
        wss = [_dot(wq_scr[gi * N_HEADS + h], s.astype(BF16)) for (gi, h, _), s in zip(units, states)]
        yield
        v_news = [(u_scr[gi * N_HEADS + h] - ws[:CHUNK]).astype(BF16) for (gi, h, _), ws in zip(units, wss)]
        outs = [ws[CHUNK:] + _dot(qkd_scr[gi][:, h * CHUNK:(h + 1) * CHUNK], vn)
                for (gi, h, _), ws, vn in zip(units, wss, v_news)]
        states = [el_scr[gi][:, slot:slot + 1] * s + _dot(kdt_scr[gi * N_HEADS + h], vn)
                  for (gi, h, slot), s, vn in zip(units, states, v_news)]
        if j == CHUNKS_PER_TILE - 1:
            for slot, s in enumerate(states):
                s_ref[slot] = s
        rf = j * CHUNK
        rb = (CHUNKS_PER_TILE - 1 - j) * CHUNK
        of_ref[0, rf:rf + CHUNK, :] = jnp.concatenate(outs[:N_HEADS], axis=1).astype(BF16)
        ob_ref[0, rb:rb + CHUNK, :] = jnp.concatenate(outs[N_HEADS:], axis=1).astype(BF16)
        yield


def _interleave(*stage_generators):
    live = list(stage_generators)
    while live:
        for gen in list(live):
            try:
                next(gen)
            except StopIteration:
                live.remove(gen)


def _gate_tile(ab, dtb, a_log):
    z = ab + dtb
    softplus = jnp.maximum(z, 0.0) + jnp.log(1.0 + jnp.exp(-jnp.abs(z)))
    lane = lax.broadcasted_iota(jnp.int32, ab.shape, 1)
    return jnp.where(lane < 2 * N_HEADS, -jnp.exp(a_log) * softplus, jax.nn.sigmoid(ab))


def _gdn_kernel(qkvf_ref, abf_ref, qkvb_ref, abb_ref, dtb_ref, alog_ref, rqkvf_ref, rqkvb_ref, logit_ref,
                w1_ref, w2_ref, w3_ref, of_ref, ob_ref, rof_ref, rob_ref, w1b_ref, w2b_ref, w3b_ref,
                s_ref, a_scr, rhs_scr, brow_scr, u_scr, wq_scr, qkd_scr, kdt_scr, el_scr,
                r_ref, decay_ref, xi_ref, zeta_ref, *, tiles_per_sample):
    step = pl.program_id(0)
    lg_row = _log_sigmoid(logit_ref[...])
    for src, dst in ((w1_ref, w1b_ref), (w2_ref, w2b_ref), (w3_ref, w3b_ref)):
        dst[...] = src[0].astype(BF16)
    keep_state = lax.rem(step + tiles_per_sample - 2, tiles_per_sample) != 0
    groups_per_tile = 2 * CHUNKS_PER_TILE
    front_ring = lax.rem(step, 3) * groups_per_tile
    solve_ring = lax.rem(step + 2, 3) * groups_per_tile
    scan_ring = lax.rem(step + 1, 3) * groups_per_tile
    front_mid = lax.rem(step, 2) * groups_per_tile
    solve_mid = groups_per_tile - front_mid

    @pl.when(step == 0)
    def _():
        for scr in (s_ref, a_scr, rhs_scr, brow_scr, u_scr, wq_scr, qkd_scr, kdt_scr, el_scr, r_ref):
            scr[...] = jnp.zeros_like(scr)
        _ret_tables(lg_row, decay_ref, xi_ref, zeta_ref)

    ri = lax.broadcasted_iota(jnp.int32, (CHUNK, CHUNK), 0)
    ci = lax.broadcasted_iota(jnp.int32, (CHUNK, CHUNK), 1)
    ri4 = lax.broadcasted_iota(jnp.int32, (CHUNK, N_HEADS * CHUNK), 0)
    ci4 = lax.broadcasted_iota(jnp.int32, (CHUNK, N_HEADS * CHUNK), 1) % CHUNK
    rbd = lax.broadcasted_iota(jnp.int32, (N_HEADS * CHUNK, N_HEADS * CHUNK), 0) // CHUNK
    cbd = lax.broadcasted_iota(jnp.int32, (N_HEADS * CHUNK, N_HEADS * CHUNK), 1) // CHUNK
    bd_mask = rbd == cbd
    eye4 = (ri4 == ci4).astype(F32)
    m16 = (ri4 // 16) == (ci4 // 16)
    m32 = jnp.logical_and((ri4 // 32) == (ci4 // 32), jnp.logical_not(m16))
    m64 = (ri4 // 32) != (ci4 // 32)

    cf = ((ci <= ri).astype(BF16), None, ri4 >= ci4, ri4 > ci4)
    cb = ((ci >= ri).astype(BF16), None, ri4 <= ci4, ri4 < ci4)
    dtb = dtb_ref[...]
    alog = alog_ref[...]

    groups = []
    for j in range(CHUNKS_PER_TILE):
        rf = j * CHUNK
        rb = (CHUNKS_PER_TILE - 1 - j) * CHUNK
        for ref, ab_ref, r0, rev, consts in ((qkvf_ref, abf_ref, rf, False, cf), (qkvb_ref, abb_ref, rb, True, cb)):
            groups.append(dict(q=ref[0, r0:r0 + CHUNK, 0:GROUP_W], k=ref[0, r0:r0 + CHUNK, GROUP_W:2 * GROUP_W],
                               v=ref[0, r0:r0 + CHUNK, 2 * GROUP_W:],
                               gates=_gate_tile(ab_ref[0, r0:r0 + CHUNK, :], dtb, alog), rev=rev, consts=consts))
    _interleave(
        _ret_stages(rqkvf_ref, rqkvb_ref, rof_ref, rob_ref, lg_row, keep_state, r_ref, decay_ref, xi_ref, zeta_ref),
        _gdn_solve_stages(len(groups), (eye4, m16, m32, m64, bd_mask), solve_ring, solve_mid,
                          a_scr, rhs_scr, brow_scr, u_scr, wq_scr),
        _gdn_scan_stages(scan_ring, keep_state, u_scr, wq_scr, qkd_scr, kdt_scr, el_scr, s_ref, of_ref, ob_ref),
        _gdn_front_stages(groups, front_ring, front_mid, a_scr, rhs_scr, brow_scr, wq_scr, qkd_scr, kdt_scr,
                          el_scr))


def _bwd_tile(s, n_tiles):
    return jnp.where(s == 0, 0, n_tiles - s)


def _latent_block(tile, bwd, n_tiles):
    return jnp.where(tile == 0, n_tiles - 2 if bwd else 0, tile - 1)


def _rider(w, n_steps):
    _, rows, cols = w.shape
    block = next(r for r in range(16, rows + 1, 16) if rows % r == 0 and rows // r <= n_steps)
    last = rows // block - 1
    return (pl.BlockSpec((1, block, cols), lambda f: (0, jnp.minimum(f, last), 0)),
            pl.BlockSpec((block, cols), lambda f: (jnp.minimum(f, last), 0)),
            jax.ShapeDtypeStruct((rows, cols), BF16))


def _mixers(qkv, ab, dtb_row, alog_row, pr, logit_row, cast_weights):
    b, t_total, _ = qkv.shape
    n_tiles = t_total // TILE

    total = b * n_tiles
    riders = [_rider(w, total + 2) for w in cast_weights]

    def block_of(f, bwd, lag, latent):
        flat = jnp.maximum(f - 2, 0) if lag else jnp.minimum(f, total - 1)
        pos = flat % n_tiles
        tile = _bwd_tile(pos, n_tiles) if bwd else pos
        return (flat // n_tiles, _latent_block(tile, bwd, n_tiles) if latent else tile, 0)

    def tok(width, bwd, lag=False, latent=False):
        return pl.BlockSpec((1, TILE, width), lambda f: block_of(f, bwd, lag, latent))

    row = pl.BlockSpec((1, GATE_LANES), lambda f: (0, 0))
    per_tile = 2 * CHUNKS_PER_TILE
    n_groups = 3 * per_tile
    n_units = n_groups * N_HEADS
    n_mid = 2 * per_tile
    return pl.pallas_call(
        functools.partial(_gdn_kernel, tiles_per_sample=n_tiles),
        grid=(total + 2,),
        in_specs=[tok(3 * GROUP_W, False), tok(GATE_LANES, False),
                  tok(3 * GROUP_W, True), tok(GATE_LANES, True), row, row,
                  tok(3 * GROUP_W, False, lag=True), tok(3 * GROUP_W, True, lag=True), row]
        + [r[0] for r in riders],
        out_specs=[tok(GROUP_W, bwd, lag=True, latent=True) for bwd in (False, True, False, True)]
        + [r[1] for r in riders],
        out_shape=[jax.ShapeDtypeStruct((b, t_total - TILE, GROUP_W), BF16)] * 4 + [r[2] for r in riders],
        scratch_shapes=[pltpu.VMEM((2 * N_HEADS, HEAD_DIM, HEAD_DIM), F32),
                        pltpu.VMEM((n_mid, CHUNK, N_HEADS * CHUNK), F32),
                        pltpu.VMEM((n_mid * N_HEADS, CHUNK, 2 * HEAD_DIM), BF16),
                        pltpu.VMEM((n_mid, 1, N_HEADS * CHUNK), F32),
                        pltpu.VMEM((n_units, CHUNK, HEAD_DIM), F32),
                        pltpu.VMEM((n_units, 2 * CHUNK, HEAD_DIM), BF16),
                        pltpu.VMEM((n_groups, CHUNK, N_HEADS * CHUNK), BF16),
                        pltpu.VMEM((n_units, HEAD_DIM, CHUNK), BF16),
                        pltpu.VMEM((n_groups, 1, GATE_LANES), F32),
                        pltpu.VMEM((2 * N_HEADS, HEAD_DIM, HEAD_DIM), F32),
                        pltpu.VMEM((2 * N_HEADS, TILE, TILE), F32),
                        pltpu.VMEM((2 * N_HEADS, TILE, HEAD_DIM), F32),
                        pltpu.VMEM((2 * N_HEADS, TILE, HEAD_DIM), F32)],
        compiler_params=_cparams(("arbitrary",)),
        name="mixers",
    )(qkv, ab, qkv, ab, dtb_row, alog_row, pr, pr, logit_row, *cast_weights)


def _log_sigmoid(x):
    return jnp.minimum(x, 0.0) - jnp.log(1.0 + jnp.exp(-jnp.abs(x)))


def _ret_tables(lg_row, decay_ref, xi_ref, zeta_ref):
    ri = lax.broadcasted_iota(jnp.int32, (TILE, TILE), 0)
    ci = lax.broadcasted_iota(jnp.int32, (TILE, TILE), 1)
    row = lax.broadcasted_iota(jnp.int32, (TILE, HEAD_DIM), 0).astype(F32)
    for slot in range(2 * N_HEADS):
        rev = slot >= N_HEADS
        lg = lg_row[:, slot:slot + 1]
        diff = ((ci - ri) if rev else (ri - ci)).astype(F32)
        pos = ((TILE - 1.0) - row) if rev else row
        decay_ref[slot] = jnp.where(diff >= 0, jnp.exp(lg * jnp.maximum(diff, 0.0)), 0.0)
        xi_ref[slot] = jnp.exp(lg * (pos + 1.0))
        zeta_ref[slot] = jnp.exp(lg * ((TILE - 1.0) - pos))


def _ret_stages(qkvf_ref, qkvb_ref, of_ref, ob_ref, lg_row, keep_state, r_ref, decay_ref, xi_ref, zeta_ref):
    units = [(refs, (N_HEADS if rev else 0) + h, h * HEAD_DIM)
             for refs, rev in (((qkvf_ref, of_ref), False), ((qkvb_ref, ob_ref), True))
             for h in range(N_HEADS)]

    def finish(unit, qk):
        refs, slot, lo = unit
        q = refs[0][0, :, lo:lo + HEAD_DIM]
        k = refs[0][0, :, GROUP_W + lo:GROUP_W + lo + HEAD_DIM]
        v = refs[0][0, :, 2 * GROUP_W + lo:2 * GROUP_W + lo + HEAD_DIM]
        r = jnp.where(keep_state, r_ref[slot], 0.0)
        lhs = jnp.concatenate([(qk * decay_ref[slot]).astype(BF16), (k * zeta_ref[slot]).T.astype(BF16)], axis=0)
        both = _dot(lhs, v)
        refs[1][0, :, lo:lo + HEAD_DIM] = (
            both[:TILE] + _dot((q * xi_ref[slot]).astype(BF16), r.astype(BF16))).astype(BF16)
        g_chunk = jnp.exp(lg_row[:, slot:slot + 1] * float(TILE))
        r_ref[slot] = g_chunk * r + both[TILE:]

    pending = None
    for unit in units:
        refs, _, lo = unit
        qk = _dot_nt(refs[0][0, :, lo:lo + HEAD_DIM], refs[0][0, :, GROUP_W + lo:GROUP_W + lo + HEAD_DIM])
        if pending is not None:
            finish(*pending)
        pending = (unit, qk)
        yield
    finish(*pending)


def _tail_kernel(gf_ref, gb_ref, z_ref, rf_ref, rb_ref, rg_ref, x_ref, mod_ref, gdn_g_ref, ret_g_ref,
                 nffn_g_ref, w_ref, wi_ref, wo_ref, fg_ref, o_ref, y_ref, *, d_model, d_ff):
    o = gf_ref[0].astype(F32) + gb_ref[0].astype(F32)
    r = rf_ref[0].astype(F32) + rb_ref[0].astype(F32)
    z = z_ref[0].astype(F32)
    rg = rg_ref[0].astype(F32)
    for h in range(N_HEADS):
        lo = h * HEAD_DIM
        o_h = o[:, lo:lo + HEAD_DIM]
        o_n = o_h * lax.rsqrt(jnp.mean(o_h * o_h, axis=-1, keepdims=True) + NORM_EPS) * gdn_g_ref[...]
        y_ref[:, lo:lo + HEAD_DIM] = (o_n * _silu(z[:, lo:lo + HEAD_DIM])).astype(BF16)
        r_h = r[:, lo:lo + HEAD_DIM]
        mu = jnp.mean(r_h, axis=-1, keepdims=True)
        cen = r_h - mu
        var = jnp.mean(cen * cen, axis=-1, keepdims=True)
        r_n = cen * lax.rsqrt(var + NORM_EPS) * ret_g_ref[...]
        y_ref[:, GROUP_W + lo:GROUP_W + lo + HEAD_DIM] = (r_n * _silu(rg[:, lo:lo + HEAD_DIM])).astype(BF16)
    mod = mod_ref[0]
    x1 = x_ref[0] + mod[:, 2 * d_model:3 * d_model] * _dot(y_ref[...], w_ref[...])
    h2 = _rms_mod(x1, nffn_g_ref[...], mod[:, 3 * d_model:4 * d_model],
                  mod[:, 4 * d_model:5 * d_model]).astype(BF16)
    acc = None
    for lo in range(0, d_ff, FFN_CHUNK):
        hi = min(lo + FFN_CHUNK, d_ff)
        gate = _dot(h2, wi_ref[:, lo:hi])
        up = _dot(h2, wi_ref[:, d_ff + lo:d_ff + hi])
        part = _dot((_silu(gate) * up).astype(BF16), wo_ref[lo:hi, :])
        acc = part if acc is None else acc + part
    x2 = x1 + mod[:, 5 * d_model:6 * d_model] * acc
    ms = jnp.mean(x2 * x2, axis=-1, keepdims=True)
    o_ref[0] = x2 * lax.rsqrt(ms + NORM_EPS) * fg_ref[...]


def _tail(gdn_f, gdn_b, z, ret_f, ret_b, rg, x, mod3, gdn_g, ret_g, nffn_g, w_out, w_ffn_in, w_ffn_out,
          final_g):
    b, seq, d = x.shape
    d_ff = w_ffn_out.shape[0]
    tok = pl.BlockSpec((1, TAIL_TOK, GROUP_W), lambda i, t: (i, t, 0))

    def resident(shape):
        return pl.BlockSpec(shape, lambda i, t: (0,) * len(shape), pipeline_mode=pl.Buffered(1))

    return pl.pallas_call(
        functools.partial(_tail_kernel, d_model=d, d_ff=d_ff),
        grid=(b, seq // TAIL_TOK),
        in_specs=[tok, tok, tok, tok, tok, tok,
                  pl.BlockSpec((1, TAIL_TOK, d), lambda i, t: (i, t, 0)),
                  pl.BlockSpec((1, 1, mod3.shape[2]), lambda i, t: (i, 0, 0)),
                  resident((1, HEAD_DIM)), resident((1, HEAD_DIM)), resident((1, d)),
                  resident(w_out.shape), resident(w_ffn_in.shape), resident(w_ffn_out.shape),
                  resident((1, d))],
        out_specs=pl.BlockSpec((1, TAIL_TOK, d), lambda i, t: (i, t, 0)),
        out_shape=jax.ShapeDtypeStruct((b, seq, d), F32),
        scratch_shapes=[pltpu.VMEM((TAIL_TOK, 2 * GROUP_W), BF16)],
        compiler_params=_cparams(("parallel", "parallel")),
        name="tail",
    )(gdn_f, gdn_b, z, ret_f, ret_b, rg, x, mod3, gdn_g, ret_g, nffn_g, w_out, w_ffn_in, w_ffn_out, final_g)


def _rope_tables(ctx_len, n_lat):
    def angles(pos, n_pairs):
        inv = ROPE_THETA ** (-np.arange(n_pairs, dtype=np.float64) / n_pairs)
        return pos[:, None] * inv[None, :]

    rows = n_lat // GRID_W
    row = np.repeat(np.arange(rows, dtype=np.float64), GRID_W)
    col = np.tile(np.arange(GRID_W, dtype=np.float64), rows)
    zeros = np.zeros((ctx_len,), np.float64)
    p_seq = np.concatenate([np.arange(ctx_len, dtype=np.float64), np.full((n_lat,), float(ctx_len))])
    ang = np.concatenate([angles(p_seq, ROPE_PAIRS[0]),
                          angles(np.concatenate([zeros, row]), ROPE_PAIRS[1]),
                          angles(np.concatenate([zeros, col]), ROPE_PAIRS[2])], axis=-1)
    cos, sin = np.cos(ang).astype(np.float32), np.sin(ang).astype(np.float32)
    return jnp.asarray(np.concatenate([cos, cos, -sin, sin], axis=-1))


def _lane_row(values):
    flat = values.reshape(1, -1).astype(F32)
    return jnp.pad(flat, ((0, 0), (0, GATE_LANES - flat.shape[1])))


def kernel(x, c, ctx, c_ctx, ada_w, ada_b, norm_mix_g, norm_ffn_g, w_in, conv_w, gdn_a_log, gdn_dt_bias,
           gdn_norm_g, ret_decay_logit, ret_norm_g, w_out, w_ffn_in, w_ffn_out, final_g):
    assert ada_w.shape[0] == 1, "single-layer block"
    b, seq, d = x.shape
    ctx_len = ctx.shape[1]
    assert ctx_len == TILE and seq % TILE == 0 and seq % TAIL_TOK == 0 and b + 1 <= 8

    cond = jnp.concatenate([c, c_ctx[None, :], jnp.zeros((8 - b - 1, d), F32)], axis=0)
    mod = _ada(cond, ada_w, ada_b)
    mod3 = mod[:, None, :]

    n_qkvz = 4 * GROUP_W
    n_gate = 4 * N_HEADS
    qkv, z, ab, pr, rg = _inproj(x, ctx, mod3, norm_mix_g, jnp.swapaxes(w_in, 1, 2), conv_w, n_qkvz, n_gate,
                                 _rope_tables(ctx_len, seq))

    dtb_row = _lane_row(gdn_dt_bias[0])
    gdn_f, gdn_b, ret_f, ret_b, w_out_b, w_ffn_in_b, w_ffn_out_b = _mixers(
        qkv, ab, dtb_row, _lane_row(gdn_a_log[0]), pr, _lane_row(ret_decay_logit[0]),
        (w_out, w_ffn_in, w_ffn_out))

    return _tail(gdn_f, gdn_b, z, ret_f, ret_b, rg, x, mod3, gdn_norm_g, ret_norm_g, norm_ffn_g,
                 w_out_b, w_ffn_in_b, w_ffn_out_b, final_g[None, :])
```

```python
import functools

import jax
import jax.numpy as jnp
import numpy as np
from jax import lax
from jax.experimental import pallas as pl
from jax.experimental.pallas import tpu as pltpu

F32 = jnp.float32
BF16 = jnp.bfloat16

HEAD_DIM = 128
N_HEADS = 4
GROUP_W = N_HEADS * HEAD_DIM
CONV_K = 5
GRID_W = 64
ROPE_THETA = 10000.0
ROPE_PAIRS = (16, 24, 24)
NORM_EPS = 1e-6

CHUNK = 64
TILE = 256
CHUNKS_PER_TILE = TILE // CHUNK
HALO = 16
GATE_LANES = 128
TAIL_TOK = 512
FFN_CHUNK = 1024
VMEM_LIMIT = 56 * 1024 * 1024


def _cparams(sem):
    return pltpu.CompilerParams(dimension_semantics=sem, vmem_limit_bytes=VMEM_LIMIT)


def _dot(a, b):
    return jnp.dot(a, b, preferred_element_type=F32)


def _dot_nt(a, b):
    return lax.dot_general(a, b, (((1,), (1,)), ((), ())), preferred_element_type=F32)


def _split(a):
    hi = a.astype(BF16)
    lo = (a - hi.astype(F32)).astype(BF16)
    return hi, lo


def _silu(x):
    return x * jax.nn.sigmoid(x)


def _lane_bcast(col, width):
    return jnp.broadcast_to(col, (col.shape[0], width))


def _ada_kernel(cond_ref, w_ref, b_ref, o_ref):
    a_hi, a_lo = _split(_silu(cond_ref[...]))
    w_hi, w_lo = _split(w_ref[0])
    o_ref[...] = _dot(a_hi, w_hi) + _dot(a_lo, w_hi) + _dot(a_hi, w_lo) + b_ref[...]


def _ada(cond, w, b):
    rows, d = cond.shape
    n = w.shape[2]
    bn = 1536
    return pl.pallas_call(
        _ada_kernel,
        grid=(n // bn,),
        in_specs=[pl.BlockSpec((rows, d), lambda j: (0, 0)),
                  pl.BlockSpec((1, d, bn), lambda j: (0, 0, j)),
                  pl.BlockSpec((1, bn), lambda j: (0, j))],
        out_specs=pl.BlockSpec((rows, bn), lambda j: (0, j)),
        out_shape=jax.ShapeDtypeStruct((rows, n), F32),
        compiler_params=_cparams(("parallel",)),
        name="ada",
    )(cond, w, b)


def _rms_mod(x, g, shift, scale):
    ms = jnp.mean(x * x, axis=-1, keepdims=True)
    return (x * lax.rsqrt(ms + NORM_EPS) * g) * (1.0 + scale) + shift


def _conv_tile(ext_ref, w_ref, o_ref):
    pad = (CONV_K - 1) // 2
    rows = TILE + 2 * HALO
    q_scale = HEAD_DIM ** -0.5
    for part in range(3):
        for hd in range(N_HEADS):
            lo = part * GROUP_W + hd * HEAD_DIM
            slab = ext_ref[:, lo:lo + HEAD_DIM]
            acc = None
            for i in range(CONV_K):
                shifted = slab if i == pad else pltpu.roll(slab, (pad - i) % rows, 0)
                term = shifted[HALO:HALO + TILE] * w_ref[0, i:i + 1, lo:lo + HEAD_DIM]
                acc = term if acc is None else acc + term
            y = _silu(acc)
            if part < 2:
                y = y * lax.rsqrt(jnp.sum(y * y, axis=-1, keepdims=True) + NORM_EPS)
            if part == 0:
                y = y * q_scale
            o_ref[0, :, lo:lo + HEAD_DIM] = y.astype(BF16)


def _inproj_kernel(x_ref, ctx_ref, mod_ref, g_ref, w_ref, cw_ref, rope_ref,
                   qkv_ref, z_ref, ab_ref, pr_ref, rg_ref, wa_ref, wab_ref, wr_ref, ext_ref,
                   *, d_model, n_half, n_gate, tiles_per_sample):
    step = pl.program_id(0)
    t = lax.rem(jnp.minimum(step, pl.num_programs(0) - 2), tiles_per_sample)
    t_conv = lax.rem(jnp.maximum(step - 1, 0), tiles_per_sample)
    width = 3 * GROUP_W

    @pl.when(step == 0)
    def _():
        wa_ref[...] = w_ref[0, :n_half, :].T.astype(BF16)
        wr_ref[...] = w_ref[0, n_half + n_gate:, :].T.astype(BF16)
        lane = lax.broadcasted_iota(jnp.int32, (d_model, GATE_LANES), 1)
        wab_ref[...] = jnp.where(lane < n_gate, w_ref[0, n_half:n_half + GATE_LANES, :].T, 0.0).astype(BF16)
        ext_ref[...] = jnp.zeros_like(ext_ref)

    xin = jnp.where(t == 0, ctx_ref[0], x_ref[0])
    mod = mod_ref[0]
    h = _rms_mod(xin, g_ref[...], mod[:, 0:d_model], mod[:, d_model:2 * d_model]).astype(BF16)
    pa = _dot(h, wa_ref[...])
    z_ref[0] = pa[:, width:].astype(BF16)

    has_next = jnp.logical_and(t_conv >= 1, t_conv < tiles_per_sample - 1)
    ext_ref[HALO + TILE:, :] = jnp.where(has_next, pa[:HALO, :width], 0.0)
    _conv_tile(ext_ref, cw_ref, qkv_ref)
    ext_ref[0:HALO, :] = jnp.where(t >= 2, ext_ref[TILE:HALO + TILE, :], 0.0)
    ext_ref[HALO:HALO + TILE, :] = pa[:, :width]

    ab_ref[0] = _dot(h, wab_ref[...])
    pr = _dot(h, wr_ref[...])
    cos2 = rope_ref[:, :HEAD_DIM]
    sin2 = rope_ref[:, HEAD_DIM:]
    k_scale = HEAD_DIM ** -0.5
    for hd in range(2 * N_HEADS):
        lo = hd * HEAD_DIM
        tt = pr[:, lo:lo + HEAD_DIM]
        rot = tt * cos2 + pltpu.roll(tt, HEAD_DIM // 2, 1) * sin2
        if hd >= N_HEADS:
            rot = rot * k_scale
        pr_ref[0, :, lo:lo + HEAD_DIM] = rot.astype(BF16)
    pr_ref[0, :, 2 * GROUP_W:] = pr[:, 2 * GROUP_W:width].astype(BF16)
    rg_ref[0] = pr[:, width:].astype(BF16)


def _inproj(x, ctx, mod3, g, w_in, conv_w, n_half, n_gate, rope):
    b, seq, d = x.shape
    n_tiles = (ctx.shape[1] + seq) // TILE
    t_total = n_tiles * TILE
    total = b * n_tiles
    width = 3 * GROUP_W
    assert w_in.shape[1] == 2 * n_half + n_gate and n_gate <= GATE_LANES and n_half == width + GROUP_W

    def now(f):
        flat = jnp.minimum(f, total - 1)
        return flat // n_tiles, flat % n_tiles

    def lagged(f):
        flat = jnp.maximum(f - 1, 0)
        return flat // n_tiles, flat % n_tiles

    def tok(cols):
        return pl.BlockSpec((1, TILE, cols), lambda f: (now(f)[0], now(f)[1], 0))

    def lat(cols):
        return pl.BlockSpec((1, TILE, cols), lambda f: (now(f)[0], jnp.maximum(now(f)[1] - 1, 0), 0))

    return pl.pallas_call(
        functools.partial(_inproj_kernel, d_model=d, n_half=n_half, n_gate=n_gate, tiles_per_sample=n_tiles),
        grid=(total + 1,),
        in_specs=[pl.BlockSpec((1, TILE, d), lambda f: (now(f)[0], jnp.maximum(now(f)[1] - 1, 0), 0)),
                  pl.BlockSpec((1, TILE, d), lambda f: (now(f)[0], 0, 0)),
                  pl.BlockSpec((1, 1, mod3.shape[2]), lambda f: (jnp.where(now(f)[1] == 0, b, now(f)[0]), 0, 0)),
                  pl.BlockSpec((1, d), lambda f: (0, 0)),
                  pl.BlockSpec(w_in.shape, lambda f: (0, 0, 0), pipeline_mode=pl.Buffered(1)),
                  pl.BlockSpec(conv_w.shape, lambda f: (0, 0, 0)),
                  pl.BlockSpec((TILE, 2 * HEAD_DIM), lambda f: (now(f)[1], 0))],
        out_specs=[pl.BlockSpec((1, TILE, width), lambda f: (lagged(f)[0], lagged(f)[1], 0)),
                   lat(GROUP_W), tok(GATE_LANES), tok(width), lat(GROUP_W)],
        out_shape=[jax.ShapeDtypeStruct((b, t_total, width), BF16),
                   jax.ShapeDtypeStruct((b, seq, GROUP_W), BF16),
                   jax.ShapeDtypeStruct((b, t_total, GATE_LANES), F32),
                   jax.ShapeDtypeStruct((b, t_total, width), BF16),
                   jax.ShapeDtypeStruct((b, seq, GROUP_W), BF16)],
        scratch_shapes=[pltpu.VMEM((d, n_half), BF16), pltpu.VMEM((d, GATE_LANES), BF16),
                        pltpu.VMEM((d, n_half), BF16), pltpu.VMEM((TILE + 2 * HALO, width), F32)],
        compiler_params=_cparams(("arbitrary",)),
        name="inproj",
    )(x, ctx, mod3, g, w_in, conv_w, rope)


def _block_diag(y, bd_mask):
    yb = y.astype(BF16)
    return jnp.where(bd_mask, jnp.concatenate([yb] * N_HEADS, axis=0), jnp.zeros((), BF16))


def _packed_matmul(x, y_bd):
    return _dot(x.astype(BF16), y_bd)


def _unit_lower_inverse_stages(a_list, eye, m16, m32, m64, bd_mask, out):
    pm = _packed_matmul
    ps = [-jnp.where(m16, a, 0.0) for a in a_list]
    xs = [eye + p for p in ps]
    ps = [p.astype(BF16) for p in ps]
    ps = [pm(p, _block_diag(p, bd_mask)).astype(BF16) for p in ps]
    yield
    for _ in range(2):
        prods = [_dot(jnp.concatenate([x.astype(BF16), p], axis=0), _block_diag(p, bd_mask))
                 for x, p in zip(xs, ps)]
        xs = [x + pr[:CHUNK] for x, pr in zip(xs, prods)]
        ps = [pr[CHUNK:].astype(BF16) for pr in prods]
        yield
    xs = [x + pm(x, _block_diag(p, bd_mask)) for x, p in zip(xs, ps)]
    yield
    for m in (m32, m64):
        zs = [pm(jnp.where(m, a, 0.0), _block_diag(x, bd_mask)).astype(BF16) for a, x in zip(a_list, xs)]
        yield
        xs = [x - pm(x, _block_diag(z, bd_mask)) for x, z in zip(xs, zs)]
        yield
    out.extend(xs)


def _gdn_front_stages(groups, ring_base, mid_base, a_scr, rhs_scr, brow_scr, wq_scr, qkd_scr, kdt_scr,
                      el_scr):
    lane = lax.broadcasted_iota(jnp.int32, (CHUNK, GATE_LANES), 1)
    low_half = lax.broadcasted_iota(jnp.int32, (CHUNK, HEAD_DIM), 1) < CHUNK
    zero_k = jnp.zeros((CHUNK, HEAD_DIM), BF16)
    for g in groups:
        tcum = g["consts"][0]
        g_hi, g_lo = _split(g["gates"])
        g["gc"] = _dot(tcum, g_hi) + _dot(tcum, g_lo)
    yield

    for gi, g in enumerate(groups):
        g_lane0 = N_HEADS if g["rev"] else 0
        b_lane0 = 2 * N_HEADS + g_lane0
        last = 0 if g["rev"] else CHUNK - 1
        gc, gates = g["gc"], g["gates"]
        ct = jnp.where(lane < 2 * N_HEADS, gc, gates).T
        egl_t = jnp.exp(ct[:, last:last + 1] - ct)
        gcb, kb, ms = [], [], []
        for h in range(N_HEADS):
            lo = h * HEAD_DIM
            gl = g_lane0 + h
            ring_unit = (ring_base + gi) * N_HEADS + h
            gcb.append(_lane_bcast(gc[:, gl:gl + 1], HEAD_DIM))
            beta = _lane_bcast(gates[:, b_lane0 + h:b_lane0 + h + 1], HEAD_DIM)
            eg_h = jnp.exp(gcb[h])
            q_h = g["q"][:, lo:lo + HEAD_DIM]
            k_h = g["k"][:, lo:lo + HEAD_DIM].astype(F32)
            kb.append((k_h * beta).astype(BF16))
            rhs_scr[(mid_base + gi) * N_HEADS + h] = jnp.concatenate(
                [g["v"][:, lo:lo + HEAD_DIM], (k_h * eg_h).astype(BF16)], axis=1)
            wq_scr[ring_unit, CHUNK:, :] = (q_h * eg_h).astype(BF16)
            kdt_scr[ring_unit] = (k_h.T * egl_t[gl:gl + 1, :]).astype(BF16)
        d_parts = []
        for p in range(N_HEADS // 2):
            h0, h1 = 2 * p, 2 * p + 1
            lo = h0 * HEAD_DIM
            lhs = jnp.concatenate([g["q"][:, lo:lo + 2 * HEAD_DIM], jnp.concatenate([kb[h0], kb[h1]], axis=1)],
                                  axis=0)
            k0 = g["k"][:, lo:lo + HEAD_DIM]
            k1 = g["k"][:, lo + HEAD_DIM:lo + 2 * HEAD_DIM]
            rhs_bd = jnp.concatenate([jnp.concatenate([k0, zero_k], axis=1),
                                      jnp.concatenate([zero_k, k1], axis=1)], axis=0)
            ms.append(_dot_nt(lhs, rhs_bd))
            g_row = jnp.concatenate([ct[g_lane0 + h0:g_lane0 + h0 + 1, :], ct[g_lane0 + h1:g_lane0 + h1 + 1, :]],
                                    axis=1)
            d_parts.append(jnp.where(low_half, gcb[h0], gcb[h1]) - g_row)
        incl4, strict4 = g["consts"][1:]
        e = jnp.exp(jnp.where(incl4, jnp.concatenate(d_parts, axis=1), 0.0))
        a_scr[mid_base + gi] = jnp.where(strict4, jnp.concatenate([m[CHUNK:] for m in ms], axis=1) * e, 0.0)
        qkd_scr[ring_base + gi] = jnp.where(incl4, jnp.concatenate([m[:CHUNK] for m in ms], axis=1) * e,
                                            0.0).astype(BF16)
        el_scr[ring_base + gi] = jnp.exp(gc[last:last + 1, :])
        brow_scr[mid_base + gi] = jnp.concatenate(
            [ct[b_lane0 + h:b_lane0 + h + 1, :] for h in range(N_HEADS)], axis=1)
        yield


def _gdn_solve_stages(n_groups, masks, ring_base, mid_base, a_scr, rhs_scr, brow_scr, u_scr, wq_scr):
    eye4, m16, m32, m64, bd_mask = masks
    t_invs = []
    yield from _unit_lower_inverse_stages([a_scr[mid_base + gi] for gi in range(n_groups)],
                                          eye4, m16, m32, m64, bd_mask, t_invs)
    for gi, t_inv in enumerate(t_invs):
        t_b = (t_inv * brow_scr[mid_base + gi]).astype(BF16)
        for h in range(N_HEADS):
            sol = _dot(t_b[:, h * CHUNK:(h + 1) * CHUNK], rhs_scr[(mid_base + gi) * N_HEADS + h])
            ring_unit = (ring_base + gi) * N_HEADS + h
            u_scr[ring_unit] = sol[:, :HEAD_DIM]
            wq_scr[ring_unit, :CHUNK, :] = sol[:, HEAD_DIM:].astype(BF16)


def _gdn_scan_stages(slot_base, keep_state, u_scr, wq_scr, qkd_scr, kdt_scr, el_scr, s_ref, of_ref, ob_ref):
    for j in range(CHUNKS_PER_TILE):
        units = [(slot_base + 2 * j + d, h, d * N_HEADS + h) for d in range(2) for h in range(N_HEADS)]
        states = [jnp.where(keep_state, s_ref[slot], 0.0) if j == 0 else s_ref[slot] for _, _, slot in units]
        wss = [_dot(wq_scr[gi * N_HEADS + h], s.astype(BF16)) for (gi, h, _), s in zip(units, states)]
        yield
        v_news = [(u_scr[gi * N_HEADS + h] - ws[:CHUNK]).astype(BF16) for (gi, h, _), ws in zip(units, wss)]
        outs = [ws[CHUNK:] + _dot(qkd_scr[gi][:, h * CHUNK:(h + 1) * CHUNK], vn)
                for (gi, h, _), ws, vn in zip(units, wss, v_news)]
        for (gi, h, slot), s, vn in zip(units, states, v_news):
            s_ref[slot] = el_scr[gi][:, slot:slot + 1] * s + _dot(kdt_scr[gi * N_HEADS + h], vn)
        rf = j * CHUNK
        rb = (CHUNKS_PER_TILE - 1 - j) * CHUNK
        of_ref[0, rf:rf + CHUNK, :] = jnp.concatenate(outs[:N_HEADS], axis=1).astype(BF16)
        ob_ref[0, rb:rb + CHUNK, :] = jnp.concatenate(outs[N_HEADS:], axis=1).astype(BF16)
        yield


def _interleave(*stage_generators):
    live = list(stage_generators)
    while live:
        for gen in list(live):
            try:
                next(gen)
            except StopIteration:
                live.remove(gen)


def _gate_tile(ab, dtb, a_log):
    z = ab + dtb
    softplus = jnp.maximum(z, 0.0) + jnp.log(1.0 + jnp.exp(-jnp.abs(z)))
    lane = lax.broadcasted_iota(jnp.int32, ab.shape, 1)
    return jnp.where(lane < 2 * N_HEADS, -jnp.exp(a_log) * softplus, jax.nn.sigmoid(ab))


def _gdn_kernel(qkvf_ref, abf_ref, qkvb_ref, abb_ref, dtb_ref, alog_ref, rqkvf_ref, rqkvb_ref, logit_ref,
                w1_ref, w2_ref, w3_ref, of_ref, ob_ref, rof_ref, rob_ref, w1b_ref, w2b_ref, w3b_ref,
                s_ref, a_scr, rhs_scr, brow_scr, u_scr, wq_scr, qkd_scr, kdt_scr, el_scr,
                r_ref, decay_ref, xi_ref, zeta_ref, *, tiles_per_sample):
    step = pl.program_id(0)
    lg_row = _log_sigmoid(logit_ref[...])
    for src, dst in ((w1_ref, w1b_ref), (w2_ref, w2b_ref), (w3_ref, w3b_ref)):
        dst[...] = src[0].astype(BF16)
    keep_state = lax.rem(step + tiles_per_sample - 2, tiles_per_sample) != 0
    groups_per_tile = 2 * CHUNKS_PER_TILE
    front_ring = lax.rem(step, 3) * groups_per_tile
    solve_ring = lax.rem(step + 2, 3) * groups_per_tile
    scan_ring = lax.rem(step + 1, 3) * groups_per_tile
    front_mid = lax.rem(step, 2) * groups_per_tile
    solve_mid = groups_per_tile - front_mid

    @pl.when(step == 0)
    def _():
        for scr in (s_ref, a_scr, rhs_scr, brow_scr, u_scr, wq_scr, qkd_scr, kdt_scr, el_scr, r_ref):
            scr[...] = jnp.zeros_like(scr)
        _ret_tables(lg_row, decay_ref, xi_ref, zeta_ref)

    ri = lax.broadcasted_iota(jnp.int32, (CHUNK, CHUNK), 0)
    ci = lax.broadcasted_iota(jnp.int32, (CHUNK, CHUNK), 1)
    ri4 = lax.broadcasted_iota(jnp.int32, (CHUNK, N_HEADS * CHUNK), 0)
    ci4 = lax.broadcasted_iota(jnp.int32, (CHUNK, N_HEADS * CHUNK), 1) % CHUNK
    rbd = lax.broadcasted_iota(jnp.int32, (N_HEADS * CHUNK, N_HEADS * CHUNK), 0) // CHUNK
    cbd = lax.broadcasted_iota(jnp.int32, (N_HEADS * CHUNK, N_HEADS * CHUNK), 1) // CHUNK
    bd_mask = rbd == cbd
    eye4 = (ri4 == ci4).astype(F32)
    m16 = (ri4 // 16) == (ci4 // 16)
    m32 = jnp.logical_and((ri4 // 32) == (ci4 // 32), jnp.logical_not(m16))
    m64 = (ri4 // 32) != (ci4 // 32)

    cf = ((ci <= ri).astype(BF16), ri4 >= ci4, ri4 > ci4)
    cb = ((ci >= ri).astype(BF16), ri4 <= ci4, ri4 < ci4)
    dtb = dtb_ref[...]
    alog = alog_ref[...]

    groups = []
    for j in range(CHUNKS_PER_TILE):
        rf = j * CHUNK
        rb = (CHUNKS_PER_TILE - 1 - j) * CHUNK
        for ref, ab_ref, r0, rev, consts in ((qkvf_ref, abf_ref, rf, False, cf), (qkvb_ref, abb_ref, rb, True, cb)):
            groups.append(dict(q=ref[0, r0:r0 + CHUNK, 0:GROUP_W], k=ref[0, r0:r0 + CHUNK, GROUP_W:2 * GROUP_W],
                               v=ref[0, r0:r0 + CHUNK, 2 * GROUP_W:],
                               gates=_gate_tile(ab_ref[0, r0:r0 + CHUNK, :], dtb, alog), rev=rev, consts=consts))
    _interleave(
        _ret_stages(rqkvf_ref, rqkvb_ref, rof_ref, rob_ref, lg_row, keep_state, r_ref, decay_ref, xi_ref, zeta_ref),
        _gdn_solve_stages(len(groups), (eye4, m16, m32, m64, bd_mask), solve_ring, solve_mid,
                          a_scr, rhs_scr, brow_scr, u_scr, wq_scr),
        _gdn_scan_stages(scan_ring, keep_state, u_scr, wq_scr, qkd_scr, kdt_scr, el_scr, s_ref, of_ref, ob_ref),
        _gdn_front_stages(groups, front_ring, front_mid, a_scr, rhs_scr, brow_scr, wq_scr, qkd_scr, kdt_scr,
                          el_scr))


def _bwd_tile(s, n_tiles):
    return jnp.where(s == 0, 0, n_tiles - s)


def _latent_block(tile, bwd, n_tiles):
    return jnp.where(tile == 0, n_tiles - 2 if bwd else 0, tile - 1)


def _rider(w, n_steps):
    _, rows, cols = w.shape
    block = next(r for r in range(16, rows + 1, 16) if rows % r == 0 and rows // r <= n_steps)
    last = rows // block - 1
    return (pl.BlockSpec((1, block, cols), lambda f: (0, jnp.minimum(f, last), 0)),
            pl.BlockSpec((block, cols), lambda f: (jnp.minimum(f, last), 0)),
            jax.ShapeDtypeStruct((rows, cols), BF16))


def _mixers(qkv, ab, dtb_row, alog_row, pr, logit_row, cast_weights):
    b, t_total, _ = qkv.shape
    n_tiles = t_total // TILE

    total = b * n_tiles
    riders = [_rider(w, total + 2) for w in cast_weights]

    def block_of(f, bwd, lag, latent):
        flat = jnp.maximum(f - 2, 0) if lag else jnp.minimum(f, total - 1)
        pos = flat % n_tiles
        tile = _bwd_tile(pos, n_tiles) if bwd else pos
        return (flat // n_tiles, _latent_block(tile, bwd, n_tiles) if latent else tile, 0)

    def tok(width, bwd, lag=False, latent=False):
        return pl.BlockSpec((1, TILE, width), lambda f: block_of(f, bwd, lag, latent))

    row = pl.BlockSpec((1, GATE_LANES), lambda f: (0, 0))
    per_tile = 2 * CHUNKS_PER_TILE
    n_groups = 3 * per_tile
    n_units = n_groups * N_HEADS
    n_mid = 2 * per_tile
    return pl.pallas_call(
        functools.partial(_gdn_kernel, tiles_per_sample=n_tiles),
        grid=(total + 2,),
        in_specs=[tok(3 * GROUP_W, False), tok(GATE_LANES, False),
                  tok(3 * GROUP_W, True), tok(GATE_LANES, True), row, row,
                  tok(3 * GROUP_W, False, lag=True), tok(3 * GROUP_W, True, lag=True), row]
        + [r[0] for r in riders],
        out_specs=[tok(GROUP_W, bwd, lag=True, latent=True) for bwd in (False, True, False, True)]
        + [r[1] for r in riders],
        out_shape=[jax.ShapeDtypeStruct((b, t_total - TILE, GROUP_W), BF16)] * 4 + [r[2] for r in riders],
        scratch_shapes=[pltpu.VMEM((2 * N_HEADS, HEAD_DIM, HEAD_DIM), F32),
                        pltpu.VMEM((n_mid, CHUNK, N_HEADS * CHUNK), F32),
                        pltpu.VMEM((n_mid * N_HEADS, CHUNK, 2 * HEAD_DIM), BF16),
                        pltpu.VMEM((n_mid, 1, N_HEADS * CHUNK), F32),
                        pltpu.VMEM((n_units, CHUNK, HEAD_DIM), F32),
                        pltpu.VMEM((n_units, 2 * CHUNK, HEAD_DIM), BF16),
                        pltpu.VMEM((n_groups, CHUNK, N_HEADS * CHUNK), BF16),
                        pltpu.VMEM((n_units, HEAD_DIM, CHUNK), BF16),
                        pltpu.VMEM((n_groups, 1, GATE_LANES), F32),
                        pltpu.VMEM((2 * N_HEADS, HEAD_DIM, HEAD_DIM), F32),
                        pltpu.VMEM((2 * N_HEADS, TILE, TILE), F32),
                        pltpu.VMEM((2 * N_HEADS, TILE, HEAD_DIM), F32),
                        pltpu.VMEM((2 * N_HEADS, TILE, HEAD_DIM), F32)],
        compiler_params=_cparams(("arbitrary",)),
        name="mixers",
    )(qkv, ab, qkv, ab, dtb_row, alog_row, pr, pr, logit_row, *cast_weights)


def _log_sigmoid(x):
    return jnp.minimum(x, 0.0) - jnp.log(1.0 + jnp.exp(-jnp.abs(x)))


def _ret_tables(lg_row, decay_ref, xi_ref, zeta_ref):
    ri = lax.broadcasted_iota(jnp.int32, (TILE, TILE), 0)
    ci = lax.broadcasted_iota(jnp.int32, (TILE, TILE), 1)
    row = lax.broadcasted_iota(jnp.int32, (TILE, HEAD_DIM), 0).astype(F32)
    for slot in range(2 * N_HEADS):
        rev = slot >= N_HEADS
        lg = lg_row[:, slot:slot + 1]
        diff = ((ci - ri) if rev else (ri - ci)).astype(F32)
        pos = ((TILE - 1.0) - row) if rev else row
        decay_ref[slot] = jnp.where(diff >= 0, jnp.exp(lg * jnp.maximum(diff, 0.0)), 0.0)
        xi_ref[slot] = jnp.exp(lg * (pos + 1.0))
        zeta_ref[slot] = jnp.exp(lg * ((TILE - 1.0) - pos))


def _ret_stages(qkvf_ref, qkvb_ref, of_ref, ob_ref, lg_row, keep_state, r_ref, decay_ref, xi_ref, zeta_ref):
    units = [(refs, (N_HEADS if rev else 0) + h, h * HEAD_DIM)
             for refs, rev in (((qkvf_ref, of_ref), False), ((qkvb_ref, ob_ref), True))
             for h in range(N_HEADS)]
    qs = [refs[0][0, :, lo:lo + HEAD_DIM] for refs, _, lo in units]
    ks = [refs[0][0, :, GROUP_W + lo:GROUP_W + lo + HEAD_DIM] for refs, _, lo in units]
    vs = [refs[0][0, :, 2 * GROUP_W + lo:2 * GROUP_W + lo + HEAD_DIM] for refs, _, lo in units]
    per_yield = 2
    qks = []
    for i, (q, k) in enumerate(zip(qs, ks)):
        qks.append(_dot_nt(q, k))
        if i % per_yield == per_yield - 1:
            yield
    rs = [jnp.where(keep_state, r_ref[slot], 0.0) for _, slot, _ in units]
    for i, ((refs, slot, lo), q, k, v, qk, r) in enumerate(zip(units, qs, ks, vs, qks, rs)):
        lhs = jnp.concatenate([(qk * decay_ref[slot]).astype(BF16), (k * zeta_ref[slot]).T.astype(BF16)], axis=0)
        both = _dot(lhs, v)
        refs[1][0, :, lo:lo + HEAD_DIM] = (
            both[:TILE] + _dot((q * xi_ref[slot]).astype(BF16), r.astype(BF16))).astype(BF16)
        g_chunk = jnp.exp(lg_row[:, slot:slot + 1] * float(TILE))
        r_ref[slot] = g_chunk * r + both[TILE:]
        if i % per_yield == per_yield - 1:
            yield


def _tail_kernel(gf_ref, gb_ref, z_ref, rf_ref, rb_ref, rg_ref, x_ref, mod_ref, gdn_g_ref, ret_g_ref,
                 nffn_g_ref, w_ref, wi_ref, wo_ref, fg_ref, o_ref, y_ref, *, d_model, d_ff):
    o = gf_ref[0].astype(F32) + gb_ref[0].astype(F32)
    r = rf_ref[0].astype(F32) + rb_ref[0].astype(F32)
    z = z_ref[0].astype(F32)
    rg = rg_ref[0].astype(F32)
    for h in range(N_HEADS):
        lo = h * HEAD_DIM
        o_h = o[:, lo:lo + HEAD_DIM]
        o_n = o_h * lax.rsqrt(jnp.mean(o_h * o_h, axis=-1, keepdims=True) + NORM_EPS) * gdn_g_ref[...]
        y_ref[:, lo:lo + HEAD_DIM] = (o_n * _silu(z[:, lo:lo + HEAD_DIM])).astype(BF16)
        r_h = r[:, lo:lo + HEAD_DIM]
        mu = jnp.mean(r_h, axis=-1, keepdims=True)
        cen = r_h - mu
        var = jnp.mean(cen * cen, axis=-1, keepdims=True)
        r_n = cen * lax.rsqrt(var + NORM_EPS) * ret_g_ref[...]
        y_ref[:, GROUP_W + lo:GROUP_W + lo + HEAD_DIM] = (r_n * _silu(rg[:, lo:lo + HEAD_DIM])).astype(BF16)
    mod = mod_ref[0]
    x1 = x_ref[0] + mod[:, 2 * d_model:3 * d_model] * _dot(y_ref[...], w_ref[...])
    h2 = _rms_mod(x1, nffn_g_ref[...], mod[:, 3 * d_model:4 * d_model],
                  mod[:, 4 * d_model:5 * d_model]).astype(BF16)
    acc = None
    for lo in range(0, d_ff, FFN_CHUNK):
        hi = min(lo + FFN_CHUNK, d_ff)
        gate = _dot(h2, wi_ref[:, lo:hi])
        up = _dot(h2, wi_ref[:, d_ff + lo:d_ff + hi])
        part = _dot((_silu(gate) * up).astype(BF16), wo_ref[lo:hi, :])
        acc = part if acc is None else acc + part
    x2 = x1 + mod[:, 5 * d_model:6 * d_model] * acc
    ms = jnp.mean(x2 * x2, axis=-1, keepdims=True)
    o_ref[0] = x2 * lax.rsqrt(ms + NORM_EPS) * fg_ref[...]


def _tail(gdn_f, gdn_b, z, ret_f, ret_b, rg, x, mod3, gdn_g, ret_g, nffn_g, w_out, w_ffn_in, w_ffn_out,
          final_g):
    b, seq, d = x.shape
    d_ff = w_ffn_out.shape[0]
    tok = pl.BlockSpec((1, TAIL_TOK, GROUP_W), lambda i, t: (i, t, 0))

    def resident(shape):
        return pl.BlockSpec(shape, lambda i, t: (0,) * len(shape), pipeline_mode=pl.Buffered(1))

    return pl.pallas_call(
        functools.partial(_tail_kernel, d_model=d, d_ff=d_ff),
        grid=(b, seq // TAIL_TOK),
        in_specs=[tok, tok, tok, tok, tok, tok,
                  pl.BlockSpec((1, TAIL_TOK, d), lambda i, t: (i, t, 0)),
                  pl.BlockSpec((1, 1, mod3.shape[2]), lambda i, t: (i, 0, 0)),
                  resident((1, HEAD_DIM)), resident((1, HEAD_DIM)), resident((1, d)),
                  resident(w_out.shape), resident(w_ffn_in.shape), resident(w_ffn_out.shape),
                  resident((1, d))],
        out_specs=pl.BlockSpec((1, TAIL_TOK, d), lambda i, t: (i, t, 0)),
        out_shape=jax.ShapeDtypeStruct((b, seq, d), F32),
        scratch_shapes=[pltpu.VMEM((TAIL_TOK, 2 * GROUP_W), BF16)],
        compiler_params=_cparams(("parallel", "parallel")),
        name="tail",
    )(gdn_f, gdn_b, z, ret_f, ret_b, rg, x, mod3, gdn_g, ret_g, nffn_g, w_out, w_ffn_in, w_ffn_out, final_g)


def _rope_tables(ctx_len, n_lat):
    def angles(pos, n_pairs):
        inv = ROPE_THETA ** (-np.arange(n_pairs, dtype=np.float64) / n_pairs)
        return pos[:, None] * inv[None, :]

    rows = n_lat // GRID_W
    row = np.repeat(np.arange(rows, dtype=np.float64), GRID_W)
    col = np.tile(np.arange(GRID_W, dtype=np.float64), rows)
    zeros = np.zeros((ctx_len,), np.float64)
    p_seq = np.concatenate([np.arange(ctx_len, dtype=np.float64), np.full((n_lat,), float(ctx_len))])
    ang = np.concatenate([angles(p_seq, ROPE_PAIRS[0]),
                          angles(np.concatenate([zeros, row]), ROPE_PAIRS[1]),
                          angles(np.concatenate([zeros, col]), ROPE_PAIRS[2])], axis=-1)
    cos, sin = np.cos(ang).astype(np.float32), np.sin(ang).astype(np.float32)
    return jnp.asarray(np.concatenate([cos, cos, -sin, sin], axis=-1))


def _lane_row(values):
    flat = values.reshape(1, -1).astype(F32)
    return jnp.pad(flat, ((0, 0), (0, GATE_LANES - flat.shape[1])))


def kernel(x, c, ctx, c_ctx, ada_w, ada_b, norm_mix_g, norm_ffn_g, w_in, conv_w, gdn_a_log, gdn_dt_bias,
           gdn_norm_g, ret_decay_logit, ret_norm_g, w_out, w_ffn_in, w_ffn_out, final_g):
    assert ada_w.shape[0] == 1, "single-layer block"
    b, seq, d = x.shape
    ctx_len = ctx.shape[1]
    assert ctx_len == TILE and seq % TILE == 0 and seq % TAIL_TOK == 0 and b + 1 <= 8

    cond = jnp.concatenate([c, c_ctx[None, :], jnp.zeros((8 - b - 1, d), F32)], axis=0)
    mod = _ada(cond, ada_w, ada_b)
    mod3 = mod[:, None, :]

    n_qkvz = 4 * GROUP_W
    n_gate = 4 * N_HEADS
    qkv, z, ab, pr, rg = _inproj(x, ctx, mod3, norm_mix_g, jnp.swapaxes(w_in, 1, 2), conv_w, n_qkvz, n_gate,
                                 _rope_tables(ctx_len, seq))

    dtb_row = _lane_row(gdn_dt_bias[0])
    gdn_f, gdn_b, ret_f, ret_b, w_out_b, w_ffn_in_b, w_ffn_out_b = _mixers(
        qkv, ab, dtb_row, _lane_row(gdn_a_log[0]), pr, _lane_row(ret_decay_logit[0]),
        (w_out, w_ffn_in, w_ffn_out))

    return _tail(gdn_f, gdn_b, z, ret_f, ret_b, rg, x, mod3, gdn_norm_g, ret_norm_g, norm_ffn_g,
                 w_out_b, w_ffn_in_b, w_ffn_out_b, final_g[None, :])
```

```python
import functools
import math

import jax
import jax.numpy as jnp
import numpy as np
from jax import lax
from jax.experimental import pallas as pl
from jax.experimental.pallas import tpu as pltpu

F32 = jnp.float32
BF16 = jnp.bfloat16

HEAD_DIM = 128
N_HEADS = 4
GROUP_W = N_HEADS * HEAD_DIM
CONV_K = 5
GRID_W = 64
ROPE_THETA = 10000.0
ROPE_PAIRS = (16, 24, 24)
NORM_EPS = 1e-6

CHUNK = 64
INV_BASE = 8
TILE = 256
CHUNKS_PER_TILE = TILE // CHUNK
HALO = 16
GATE_LANES = 128
TAIL_TOK = 512
FFN_CHUNK = 1024
VMEM_LIMIT = 56 * 1024 * 1024


def _cparams(sem):
    return pltpu.CompilerParams(dimension_semantics=sem, vmem_limit_bytes=VMEM_LIMIT)


def _dot(a, b):
    return jnp.dot(a, b, preferred_element_type=F32)


def _dot_nt(a, b):
    return lax.dot_general(a, b, (((1,), (1,)), ((), ())), preferred_element_type=F32)


def _split(a):
    hi = a.astype(BF16)
    lo = (a - hi.astype(F32)).astype(BF16)
    return hi, lo


def _silu(x):
    return x * jax.nn.sigmoid(x)


def _lane_bcast(col, width):
    return jnp.broadcast_to(col, (col.shape[0], width))


def _ada_kernel(cond_ref, w_ref, b_ref, o_ref):
    a_hi, a_lo = _split(_silu(cond_ref[...]))
    w_hi, w_lo = _split(w_ref[0])
    o_ref[...] = _dot(a_hi, w_hi) + _dot(a_lo, w_hi) + _dot(a_hi, w_lo) + b_ref[...]


def _ada(cond, w, b):
    rows, d = cond.shape
    n = w.shape[2]
    bn = 1536
    return pl.pallas_call(
        _ada_kernel,
        grid=(n // bn,),
        in_specs=[pl.BlockSpec((rows, d), lambda j: (0, 0)),
                  pl.BlockSpec((1, d, bn), lambda j: (0, 0, j)),
                  pl.BlockSpec((1, bn), lambda j: (0, j))],
        out_specs=pl.BlockSpec((rows, bn), lambda j: (0, j)),
        out_shape=jax.ShapeDtypeStruct((rows, n), F32),
        compiler_params=_cparams(("parallel",)),
        name="ada",
    )(cond, w, b)


def _rms_mod(x, g, shift, scale):
    ms = jnp.mean(x * x, axis=-1, keepdims=True)
    return (x * lax.rsqrt(ms + NORM_EPS) * g) * (1.0 + scale) + shift


def _conv_tile(ext_ref, w_ref, o_ref):
    pad = (CONV_K - 1) // 2
    rows = TILE + 2 * HALO
    q_scale = HEAD_DIM ** -0.5
    for part in range(3):
        for hd in range(N_HEADS):
            lo = part * GROUP_W + hd * HEAD_DIM
            slab = ext_ref[:, lo:lo + HEAD_DIM]
            acc = None
            for i in range(CONV_K):
                shifted = slab if i == pad else pltpu.roll(slab, (pad - i) % rows, 0)
                term = shifted[HALO:HALO + TILE] * w_ref[0, i:i + 1, lo:lo + HEAD_DIM]
                acc = term if acc is None else acc + term
            y = _silu(acc)
            if part < 2:
                y = y * lax.rsqrt(jnp.sum(y * y, axis=-1, keepdims=True) + NORM_EPS)
            if part == 0:
                y = y * q_scale
            o_ref[0, :, lo:lo + HEAD_DIM] = y.astype(BF16)


def _inproj_kernel(x_ref, ctx_ref, mod_ref, g_ref, w_ref, cw_ref, rope_ref,
                   qkv_ref, z_ref, ab_ref, pr_ref, rg_ref, wa_ref, wab_ref, wr_ref, ext_ref,
                   *, d_model, n_half, n_gate, tiles_per_sample):
    step = pl.program_id(0)
    t = lax.rem(jnp.minimum(step, pl.num_programs(0) - 2), tiles_per_sample)
    t_conv = lax.rem(jnp.maximum(step - 1, 0), tiles_per_sample)
    width = 3 * GROUP_W

    @pl.when(step == 0)
    def _():
        wa_ref[...] = w_ref[0, :n_half, :].T.astype(BF16)
        wr_ref[...] = w_ref[0, n_half + n_gate:, :].T.astype(BF16)
        lane = lax.broadcasted_iota(jnp.int32, (d_model, GATE_LANES), 1)
        wab_ref[...] = jnp.where(lane < n_gate, w_ref[0, n_half:n_half + GATE_LANES, :].T, 0.0).astype(BF16)
        ext_ref[...] = jnp.zeros_like(ext_ref)

    xin = jnp.where(t == 0, ctx_ref[0], x_ref[0])
    mod = mod_ref[0]
    h = _rms_mod(xin, g_ref[...], mod[:, 0:d_model], mod[:, d_model:2 * d_model]).astype(BF16)
    pa = _dot(h, wa_ref[...])
    z_ref[0] = pa[:, width:].astype(BF16)

    has_next = jnp.logical_and(t_conv >= 1, t_conv < tiles_per_sample - 1)
    ext_ref[HALO + TILE:, :] = jnp.where(has_next, pa[:HALO, :width], 0.0)
    _conv_tile(ext_ref, cw_ref, qkv_ref)
    ext_ref[0:HALO, :] = jnp.where(t >= 2, ext_ref[TILE:HALO + TILE, :], 0.0)
    ext_ref[HALO:HALO + TILE, :] = pa[:, :width]

    ab_ref[0] = _dot(h, wab_ref[...])
    pr = _dot(h, wr_ref[...])
    cos2 = rope_ref[:, :HEAD_DIM]
    sin2 = rope_ref[:, HEAD_DIM:]
    k_scale = HEAD_DIM ** -0.5
    for hd in range(2 * N_HEADS):
        lo = hd * HEAD_DIM
        tt = pr[:, lo:lo + HEAD_DIM]
        rot = tt * cos2 + pltpu.roll(tt, HEAD_DIM // 2, 1) * sin2
        if hd >= N_HEADS:
            rot = rot * k_scale
        pr_ref[0, :, lo:lo + HEAD_DIM] = rot.astype(BF16)
    pr_ref[0, :, 2 * GROUP_W:] = pr[:, 2 * GROUP_W:width].astype(BF16)
    rg_ref[0] = pr[:, width:].astype(BF16)


def _inproj(x, ctx, mod3, g, w_in, conv_w, n_half, n_gate, rope):
    b, seq, d = x.shape
    n_tiles = (ctx.shape[1] + seq) // TILE
    t_total = n_tiles * TILE
    total = b * n_tiles
    width = 3 * GROUP_W
    assert w_in.shape[1] == 2 * n_half + n_gate and n_gate <= GATE_LANES and n_half == width + GROUP_W

    def now(f):
        flat = jnp.minimum(f, total - 1)
        return flat // n_tiles, flat % n_tiles

    def lagged(f):
        flat = jnp.maximum(f - 1, 0)
        return flat // n_tiles, flat % n_tiles

    def tok(cols):
        return pl.BlockSpec((1, TILE, cols), lambda f: (now(f)[0], now(f)[1], 0))

    def lat(cols):
        return pl.BlockSpec((1, TILE, cols), lambda f: (now(f)[0], jnp.maximum(now(f)[1] - 1, 0), 0))

    return pl.pallas_call(
        functools.partial(_inproj_kernel, d_model=d, n_half=n_half, n_gate=n_gate, tiles_per_sample=n_tiles),
        grid=(total + 1,),
        in_specs=[pl.BlockSpec((1, TILE, d), lambda f: (now(f)[0], jnp.maximum(now(f)[1] - 1, 0), 0)),
                  pl.BlockSpec((1, TILE, d), lambda f: (now(f)[0], 0, 0)),
                  pl.BlockSpec((1, 1, mod3.shape[2]), lambda f: (jnp.where(now(f)[1] == 0, b, now(f)[0]), 0, 0)),
                  pl.BlockSpec((1, d), lambda f: (0, 0)),
                  pl.BlockSpec(w_in.shape, lambda f: (0, 0, 0), pipeline_mode=pl.Buffered(1)),
                  pl.BlockSpec(conv_w.shape, lambda f: (0, 0, 0)),
                  pl.BlockSpec((TILE, 2 * HEAD_DIM), lambda f: (now(f)[1], 0))],
        out_specs=[pl.BlockSpec((1, TILE, width), lambda f: (lagged(f)[0], lagged(f)[1], 0)),
                   lat(GROUP_W), tok(GATE_LANES), tok(width), lat(GROUP_W)],
        out_shape=[jax.ShapeDtypeStruct((b, t_total, width), BF16),
                   jax.ShapeDtypeStruct((b, seq, GROUP_W), BF16),
                   jax.ShapeDtypeStruct((b, t_total, GATE_LANES), F32),
                   jax.ShapeDtypeStruct((b, t_total, width), BF16),
                   jax.ShapeDtypeStruct((b, seq, GROUP_W), BF16)],
        scratch_shapes=[pltpu.VMEM((d, n_half), BF16), pltpu.VMEM((d, GATE_LANES), BF16),
                        pltpu.VMEM((d, n_half), BF16), pltpu.VMEM((TILE + 2 * HALO, width), F32)],
        compiler_params=_cparams(("arbitrary",)),
        name="inproj",
    )(x, ctx, mod3, g, w_in, conv_w, rope)


def _block_diag(y, bd_mask):
    yb = y.astype(BF16)
    return jnp.where(bd_mask, jnp.concatenate([yb] * N_HEADS, axis=0), jnp.zeros((), BF16))


def _packed_matmul(x, y_bd):
    return _dot(x.astype(BF16), y_bd)


def _unit_lower_inverse_stages(a_list, eye, base_mask, merge_masks, bd_mask, out):
    pm = _packed_matmul
    ps = [-jnp.where(base_mask, a, 0.0) for a in a_list]
    xs = [eye + p for p in ps]
    ps = [p.astype(BF16) for p in ps]
    ps = [pm(p, _block_diag(p, bd_mask)).astype(BF16) for p in ps]
    yield
    for _ in range(INV_BASE.bit_length() - 3):
        prods = [_dot(jnp.concatenate([x.astype(BF16), p], axis=0), _block_diag(p, bd_mask))
                 for x, p in zip(xs, ps)]
        xs = [x + pr[:CHUNK] for x, pr in zip(xs, prods)]
        ps = [pr[CHUNK:].astype(BF16) for pr in prods]
        yield
    xs = [x + pm(x, _block_diag(p, bd_mask)) for x, p in zip(xs, ps)]
    yield
    for m in merge_masks:
        zs = [pm(jnp.where(m, a, 0.0), _block_diag(x, bd_mask)).astype(BF16) for a, x in zip(a_list, xs)]
        yield
        xs = [x - pm(x, _block_diag(z, bd_mask)) for x, z in zip(xs, zs)]
        yield
    out.extend(xs)


def _gdn_front_stages(groups, ring_base, mid_base, a_scr, rhs_scr, brow_scr, wq_scr, qkd_scr, kdt_scr,
                      el_scr):
    lane = lax.broadcasted_iota(jnp.int32, (CHUNK, GATE_LANES), 1)
    low_half = lax.broadcasted_iota(jnp.int32, (CHUNK, HEAD_DIM), 1) < CHUNK
    zero_k = jnp.zeros((CHUNK, HEAD_DIM), BF16)
    for g in groups:
        tcum = g["consts"][0]
        g_hi, g_lo = _split(g["gates"])
        g["gc"] = _dot(tcum, g_hi) + _dot(tcum, g_lo)
    yield

    for gi, g in enumerate(groups):
        g_lane0 = N_HEADS if g["rev"] else 0
        b_lane0 = 2 * N_HEADS + g_lane0
        last = 0 if g["rev"] else CHUNK - 1
        gc, gates = g["gc"], g["gates"]
        ct = jnp.where(lane < 2 * N_HEADS, gc, gates).T
        egl_t = jnp.exp(ct[:, last:last + 1] - ct)
        gcb, kb, ms = [], [], []
        for h in range(N_HEADS):
            lo = h * HEAD_DIM
            gl = g_lane0 + h
            ring_unit = (ring_base + gi) * N_HEADS + h
            gcb.append(_lane_bcast(gc[:, gl:gl + 1], HEAD_DIM))
            beta = _lane_bcast(gates[:, b_lane0 + h:b_lane0 + h + 1], HEAD_DIM)
            eg_h = jnp.exp(gcb[h])
            q_h = g["q"][:, lo:lo + HEAD_DIM]
            k_h = g["k"][:, lo:lo + HEAD_DIM].astype(F32)
            kb.append((k_h * beta).astype(BF16))
            rhs_scr[(mid_base + gi) * N_HEADS + h] = jnp.concatenate(
                [g["v"][:, lo:lo + HEAD_DIM], (k_h * eg_h).astype(BF16)], axis=1)
            wq_scr[ring_unit, CHUNK:, :] = (q_h * eg_h).astype(BF16)
            kdt_scr[ring_unit] = (k_h.T * egl_t[gl:gl + 1, :]).astype(BF16)
        d_parts = []
        for p in range(N_HEADS // 2):
            h0, h1 = 2 * p, 2 * p + 1
            lo = h0 * HEAD_DIM
            lhs = jnp.concatenate([g["q"][:, lo:lo + 2 * HEAD_DIM], jnp.concatenate([kb[h0], kb[h1]], axis=1)],
                                  axis=0)
            k0 = g["k"][:, lo:lo + HEAD_DIM]
            k1 = g["k"][:, lo + HEAD_DIM:lo + 2 * HEAD_DIM]
            rhs_bd = jnp.concatenate([jnp.concatenate([k0, zero_k], axis=1),
                                      jnp.concatenate([zero_k, k1], axis=1)], axis=0)
            ms.append(_dot_nt(lhs, rhs_bd))
            g_row = jnp.concatenate([ct[g_lane0 + h0:g_lane0 + h0 + 1, :], ct[g_lane0 + h1:g_lane0 + h1 + 1, :]],
                                    axis=1)
            d_parts.append(jnp.where(low_half, gcb[h0], gcb[h1]) - g_row)
        incl4, strict4 = g["consts"][2:]
        e = jnp.exp(jnp.where(incl4, jnp.concatenate(d_parts, axis=1), 0.0))
        a_scr[mid_base + gi] = jnp.where(strict4, jnp.concatenate([m[CHUNK:] for m in ms], axis=1) * e, 0.0)
        qkd_scr[ring_base + gi] = jnp.where(incl4, jnp.concatenate([m[:CHUNK] for m in ms], axis=1) * e,
                                            0.0).astype(BF16)
        el_scr[ring_base + gi] = jnp.exp(gc[last:last + 1, :])
        brow_scr[mid_base + gi] = jnp.concatenate(
            [ct[b_lane0 + h:b_lane0 + h + 1, :] for h in range(N_HEADS)], axis=1)
        yield


def _gdn_solve_stages(n_groups, masks, ring_base, mid_base, a_scr, rhs_scr, brow_scr, u_scr, wq_scr):
    eye4, base_mask, merge_masks, bd_mask = masks
    t_invs = []
    yield from _unit_lower_inverse_stages([a_scr[mid_base + gi] for gi in range(n_groups)],
                                          eye4, base_mask, merge_masks, bd_mask, t_invs)
    for gi, t_inv in enumerate(t_invs):
        t_b = (t_inv * brow_scr[mid_base + gi]).astype(BF16)
        for h in range(N_HEADS):
            sol = _dot(t_b[:, h * CHUNK:(h + 1) * CHUNK], rhs_scr[(mid_base + gi) * N_HEADS + h])
            ring_unit = (ring_base + gi) * N_HEADS + h
            u_scr[ring_unit] = sol[:, :HEAD_DIM]
            wq_scr[ring_unit, :CHUNK, :] = sol[:, HEAD_DIM:].astype(BF16)


def _gdn_scan_stages(slot_base, keep_state, u_scr, wq_scr, qkd_scr, kdt_scr, el_scr, s_ref, of_ref, ob_ref):
    for j in range(CHUNKS_PER_TILE):
        units = [(slot_base + 2 * j + d, h, d * N_HEADS + h) for d in range(2) for h in range(N_HEADS)]
        states = [jnp.where(keep_state, s_ref[slot], 0.0) if j == 0 else s_ref[slot] for _, _, slot in units]
        wss = [_dot(wq_scr[gi * N_HEADS + h], s.astype(BF16)) for (gi, h, _), s in zip(units, states)]
        yield
        v_news = [(u_scr[gi * N_HEADS + h] - ws[:CHUNK]).astype(BF16) for (gi, h, _), ws in zip(units, wss)]
        outs = [ws[CHUNK:] + _dot(qkd_scr[gi][:, h * CHUNK:(h + 1) * CHUNK], vn)
                for (gi, h, _), ws, vn in zip(units, wss, v_news)]
        for (gi, h, slot), s, vn in zip(units, states, v_news):
            s_ref[slot] = el_scr[gi][:, slot:slot + 1] * s + _dot(kdt_scr[gi * N_HEADS + h], vn)
        rf = j * CHUNK
        rb = (CHUNKS_PER_TILE - 1 - j) * CHUNK
        of_ref[0, rf:rf + CHUNK, :] = jnp.concatenate(outs[:N_HEADS], axis=1).astype(BF16)
        ob_ref[0, rb:rb + CHUNK, :] = jnp.concatenate(outs[N_HEADS:], axis=1).astype(BF16)
        yield


def _interleave(*stage_generators):
    live = list(stage_generators)
    while live:
        for gen in list(live):
            try:
                next(gen)
            except StopIteration:
                live.remove(gen)


def _gate_tile(ab, dtb, a_log):
    z = ab + dtb
    softplus = jnp.maximum(z, 0.0) + jnp.log(1.0 + jnp.exp(-jnp.abs(z)))
    lane = lax.broadcasted_iota(jnp.int32, ab.shape, 1)
    return jnp.where(lane < 2 * N_HEADS, -jnp.exp(a_log) * softplus, jax.nn.sigmoid(ab))


def _gdn_kernel(qkvf_ref, abf_ref, qkvb_ref, abb_ref, dtb_ref, alog_ref, rqkvf_ref, rqkvb_ref, logit_ref,
                w1_ref, w2_ref, w3_ref, of_ref, ob_ref, rof_ref, rob_ref, w1b_ref, w2b_ref, w3b_ref,
                s_ref, a_scr, rhs_scr, brow_scr, u_scr, wq_scr, qkd_scr, kdt_scr, el_scr,
                r_ref, decay_ref, xi_ref, zeta_ref, *, tiles_per_sample):
    step = pl.program_id(0)
    lg_row = _log_sigmoid(logit_ref[...])
    for src, dst in ((w1_ref, w1b_ref), (w2_ref, w2b_ref), (w3_ref, w3b_ref)):
        dst[...] = src[0].astype(BF16)
    keep_state = lax.rem(step + tiles_per_sample - 2, tiles_per_sample) != 0
    groups_per_tile = 2 * CHUNKS_PER_TILE
    front_ring = lax.rem(step, 3) * groups_per_tile
    solve_ring = lax.rem(step + 2, 3) * groups_per_tile
    scan_ring = lax.rem(step + 1, 3) * groups_per_tile
    front_mid = lax.rem(step, 2) * groups_per_tile
    solve_mid = groups_per_tile - front_mid

    @pl.when(step == 0)
    def _():
        for scr in (s_ref, a_scr, rhs_scr, brow_scr, u_scr, wq_scr, qkd_scr, kdt_scr, el_scr, r_ref):
            scr[...] = jnp.zeros_like(scr)
        _ret_tables(lg_row, decay_ref, xi_ref, zeta_ref)

    ri = lax.broadcasted_iota(jnp.int32, (CHUNK, CHUNK), 0)
    ci = lax.broadcasted_iota(jnp.int32, (CHUNK, CHUNK), 1)
    ri4 = lax.broadcasted_iota(jnp.int32, (CHUNK, N_HEADS * CHUNK), 0)
    ci4 = lax.broadcasted_iota(jnp.int32, (CHUNK, N_HEADS * CHUNK), 1) % CHUNK
    rbd = lax.broadcasted_iota(jnp.int32, (N_HEADS * CHUNK, N_HEADS * CHUNK), 0) // CHUNK
    cbd = lax.broadcasted_iota(jnp.int32, (N_HEADS * CHUNK, N_HEADS * CHUNK), 1) // CHUNK
    bd_mask = rbd == cbd
    eye4 = (ri4 == ci4).astype(F32)
    base_mask = (ri4 // INV_BASE) == (ci4 // INV_BASE)
    merge_masks = []
    size = 2 * INV_BASE
    while size <= CHUNK:
        merge_masks.append(jnp.logical_and((ri4 // size) == (ci4 // size),
                                           (ri4 // (size // 2)) != (ci4 // (size // 2))))
        size *= 2

    cf = ((ci <= ri).astype(BF16), None, ri4 >= ci4, ri4 > ci4)
    cb = ((ci >= ri).astype(BF16), None, ri4 <= ci4, ri4 < ci4)
    dtb = dtb_ref[...]
    alog = alog_ref[...]

    groups = []
    for j in range(CHUNKS_PER_TILE):
        rf = j * CHUNK
        rb = (CHUNKS_PER_TILE - 1 - j) * CHUNK
        for ref, ab_ref, r0, rev, consts in ((qkvf_ref, abf_ref, rf, False, cf), (qkvb_ref, abb_ref, rb, True, cb)):
            groups.append(dict(q=ref[0, r0:r0 + CHUNK, 0:GROUP_W], k=ref[0, r0:r0 + CHUNK, GROUP_W:2 * GROUP_W],
                               v=ref[0, r0:r0 + CHUNK, 2 * GROUP_W:],
                               gates=_gate_tile(ab_ref[0, r0:r0 + CHUNK, :], dtb, alog), rev=rev, consts=consts))
    _interleave(
        _ret_stages(rqkvf_ref, rqkvb_ref, rof_ref, rob_ref, lg_row, keep_state, r_ref, decay_ref, xi_ref, zeta_ref),
        _gdn_solve_stages(len(groups), (eye4, base_mask, merge_masks, bd_mask), solve_ring, solve_mid,
                          a_scr, rhs_scr, brow_scr, u_scr, wq_scr),
        _gdn_scan_stages(scan_ring, keep_state, u_scr, wq_scr, qkd_scr, kdt_scr, el_scr, s_ref, of_ref, ob_ref),
        _gdn_front_stages(groups, front_ring, front_mid, a_scr, rhs_scr, brow_scr, wq_scr, qkd_scr, kdt_scr,
                          el_scr))


def _bwd_tile(s, n_tiles):
    return jnp.where(s == 0, 0, n_tiles - s)


def _latent_block(tile, bwd, n_tiles):
    return jnp.where(tile == 0, n_tiles - 2 if bwd else 0, tile - 1)


def _rider(w, n_steps):
    _, rows, cols = w.shape
    block = next(r for r in range(16, rows + 1, 16) if rows % r == 0 and rows // r <= n_steps)
    last = rows // block - 1
    return (pl.BlockSpec((1, block, cols), lambda f: (0, jnp.minimum(f, last), 0)),
            pl.BlockSpec((block, cols), lambda f: (jnp.minimum(f, last), 0)),
            jax.ShapeDtypeStruct((rows, cols), BF16))


def _mixers(qkv, ab, dtb_row, alog_row, pr, logit_row, cast_weights):
    b, t_total, _ = qkv.shape
    n_tiles = t_total // TILE

    total = b * n_tiles
    riders = [_rider(w, total + 2) for w in cast_weights]

    def block_of(f, bwd, lag, latent):
        flat = jnp.maximum(f - 2, 0) if lag else jnp.minimum(f, total - 1)
        pos = flat % n_tiles
        tile = _bwd_tile(pos, n_tiles) if bwd else pos
        return (flat // n_tiles, _latent_block(tile, bwd, n_tiles) if latent else tile, 0)

    def tok(width, bwd, lag=False, latent=False):
        return pl.BlockSpec((1, TILE, width), lambda f: block_of(f, bwd, lag, latent))

    row = pl.BlockSpec((1, GATE_LANES), lambda f: (0, 0))
    per_tile = 2 * CHUNKS_PER_TILE
    n_groups = 3 * per_tile
    n_units = n_groups * N_HEADS
    n_mid = 2 * per_tile
    return pl.pallas_call(
        functools.partial(_gdn_kernel, tiles_per_sample=n_tiles),
        grid=(total + 2,),
        in_specs=[tok(3 * GROUP_W, False), tok(GATE_LANES, False),
                  tok(3 * GROUP_W, True), tok(GATE_LANES, True), row, row,
                  tok(3 * GROUP_W, False, lag=True), tok(3 * GROUP_W, True, lag=True), row]
        + [r[0] for r in riders],
        out_specs=[tok(GROUP_W, bwd, lag=True, latent=True) for bwd in (False, True, False, True)]
        + [r[1] for r in riders],
        out_shape=[jax.ShapeDtypeStruct((b, t_total - TILE, GROUP_W), BF16)] * 4 + [r[2] for r in riders],
        scratch_shapes=[pltpu.VMEM((2 * N_HEADS, HEAD_DIM, HEAD_DIM), F32),
                        pltpu.VMEM((n_mid, CHUNK, N_HEADS * CHUNK), F32),
                        pltpu.VMEM((n_mid * N_HEADS, CHUNK, 2 * HEAD_DIM), BF16),
                        pltpu.VMEM((n_mid, 1, N_HEADS * CHUNK), F32),
                        pltpu.VMEM((n_units, CHUNK, HEAD_DIM), F32),
                        pltpu.VMEM((n_units, 2 * CHUNK, HEAD_DIM), BF16),
                        pltpu.VMEM((n_groups, CHUNK, N_HEADS * CHUNK), BF16),
                        pltpu.VMEM((n_units, HEAD_DIM, CHUNK), BF16),
                        pltpu.VMEM((n_groups, 1, GATE_LANES), F32),
                        pltpu.VMEM((2 * N_HEADS, HEAD_DIM, HEAD_DIM), F32),
                        pltpu.VMEM((2 * N_HEADS, TILE, TILE), F32),
                        pltpu.VMEM((2 * N_HEADS, TILE, HEAD_DIM), F32),
                        pltpu.VMEM((2 * N_HEADS, TILE, HEAD_DIM), F32)],
        compiler_params=_cparams(("arbitrary",)),
        name="mixers",
    )(qkv, ab, qkv, ab, dtb_row, alog_row, pr, pr, logit_row, *cast_weights)


def _log_sigmoid(x):
    return jnp.minimum(x, 0.0) - jnp.log(1.0 + jnp.exp(-jnp.abs(x)))


def _ret_tables(lg_row, decay_ref, xi_ref, zeta_ref):
    ri = lax.broadcasted_iota(jnp.int32, (TILE, TILE), 0)
    ci = lax.broadcasted_iota(jnp.int32, (TILE, TILE), 1)
    row = lax.broadcasted_iota(jnp.int32, (TILE, HEAD_DIM), 0).astype(F32)
    for slot in range(2 * N_HEADS):
        rev = slot >= N_HEADS
        lg = lg_row[:, slot:slot + 1]
        diff = ((ci - ri) if rev else (ri - ci)).astype(F32)
        pos = ((TILE - 1.0) - row) if rev else row
        decay_ref[slot] = jnp.where(diff >= 0, jnp.exp(lg * jnp.maximum(diff, 0.0)), 0.0)
        xi_ref[slot] = jnp.exp(lg * (pos + 1.0))
        zeta_ref[slot] = jnp.exp(lg * ((TILE - 1.0) - pos))


def _ret_stages(qkvf_ref, qkvb_ref, of_ref, ob_ref, lg_row, keep_state, r_ref, decay_ref, xi_ref, zeta_ref):
    units = [(refs, (N_HEADS if rev else 0) + h, h * HEAD_DIM)
             for refs, rev in (((qkvf_ref, of_ref), False), ((qkvb_ref, ob_ref), True))
             for h in range(N_HEADS)]
    qs = [refs[0][0, :, lo:lo + HEAD_DIM] for refs, _, lo in units]
    ks = [refs[0][0, :, GROUP_W + lo:GROUP_W + lo + HEAD_DIM] for refs, _, lo in units]
    vs = [refs[0][0, :, 2 * GROUP_W + lo:2 * GROUP_W + lo + HEAD_DIM] for refs, _, lo in units]
    per_yield = 2
    qks = []
    for i, (q, k) in enumerate(zip(qs, ks)):
        qks.append(_dot_nt(q, k))
        if i % per_yield == per_yield - 1:
            yield
    rs = [jnp.where(keep_state, r_ref[slot], 0.0) for _, slot, _ in units]
    for i, ((refs, slot, lo), q, k, v, qk, r) in enumerate(zip(units, qs, ks, vs, qks, rs)):
        lhs = jnp.concatenate([(qk * decay_ref[slot]).astype(BF16), (k * zeta_ref[slot]).T.astype(BF16)], axis=0)
        both = _dot(lhs, v)
        refs[1][0, :, lo:lo + HEAD_DIM] = (
            both[:TILE] + _dot((q * xi_ref[slot]).astype(BF16), r.astype(BF16))).astype(BF16)
        g_chunk = jnp.exp(lg_row[:, slot:slot + 1] * float(TILE))
        r_ref[slot] = g_chunk * r + both[TILE:]
        if i % per_yield == per_yield - 1:
            yield


def _tail_kernel(gf_ref, gb_ref, z_ref, rf_ref, rb_ref, rg_ref, x_ref, mod_ref, gdn_g_ref, ret_g_ref,
                 nffn_g_ref, w_ref, wi_ref, wo_ref, fg_ref, o_ref, y_ref, *, d_model, d_ff):
    o = gf_ref[0].astype(F32) + gb_ref[0].astype(F32)
    r = rf_ref[0].astype(F32) + rb_ref[0].astype(F32)
    z = z_ref[0].astype(F32)
    rg = rg_ref[0].astype(F32)
    for h in range(N_HEADS):
        lo = h * HEAD_DIM
        o_h = o[:, lo:lo + HEAD_DIM]
        o_n = o_h * lax.rsqrt(jnp.mean(o_h * o_h, axis=-1, keepdims=True) + NORM_EPS) * gdn_g_ref[...]
        y_ref[:, lo:lo + HEAD_DIM] = (o_n * _silu(z[:, lo:lo + HEAD_DIM])).astype(BF16)
        r_h = r[:, lo:lo + HEAD_DIM]
        mu = jnp.mean(r_h, axis=-1, keepdims=True)
        cen = r_h - mu
        var = jnp.mean(cen * cen, axis=-1, keepdims=True)
        r_n = cen * lax.rsqrt(var + NORM_EPS) * ret_g_ref[...]
        y_ref[:, GROUP_W + lo:GROUP_W + lo + HEAD_DIM] = (r_n * _silu(rg[:, lo:lo + HEAD_DIM])).astype(BF16)
    mod = mod_ref[0]
    x1 = x_ref[0] + mod[:, 2 * d_model:3 * d_model] * _dot(y_ref[...], w_ref[...])
    h2 = _rms_mod(x1, nffn_g_ref[...], mod[:, 3 * d_model:4 * d_model],
                  mod[:, 4 * d_model:5 * d_model]).astype(BF16)
    acc = None
    for lo in range(0, d_ff, FFN_CHUNK):
        hi = min(lo + FFN_CHUNK, d_ff)
        gate = _dot(h2, wi_ref[:, lo:hi])
        up = _dot(h2, wi_ref[:, d_ff + lo:d_ff + hi])
        part = _dot((_silu(gate) * up).astype(BF16), wo_ref[lo:hi, :])
        acc = part if acc is None else acc + part
    x2 = x1 + mod[:, 5 * d_model:6 * d_model] * acc
    ms = jnp.mean(x2 * x2, axis=-1, keepdims=True)
    o_ref[0] = x2 * lax.rsqrt(ms + NORM_EPS) * fg_ref[...]


def _tail(gdn_f, gdn_b, z, ret_f, ret_b, rg, x, mod3, gdn_g, ret_g, nffn_g, w_out, w_ffn_in, w_ffn_out,
          final_g):
    b, seq, d = x.shape
    d_ff = w_ffn_out.shape[0]
    tok = pl.BlockSpec((1, TAIL_TOK, GROUP_W), lambda i, t: (i, t, 0))

    def resident(shape):
        return pl.BlockSpec(shape, lambda i, t: (0,) * len(shape), pipeline_mode=pl.Buffered(1))

    return pl.pallas_call(
        functools.partial(_tail_kernel, d_model=d, d_ff=d_ff),
        grid=(b, seq // TAIL_TOK),
        in_specs=[tok, tok, tok, tok, tok, tok,
                  pl.BlockSpec((1, TAIL_TOK, d), lambda i, t: (i, t, 0)),
                  pl.BlockSpec((1, 1, mod3.shape[2]), lambda i, t: (i, 0, 0)),
                  resident((1, HEAD_DIM)), resident((1, HEAD_DIM)), resident((1, d)),
                  resident(w_out.shape), resident(w_ffn_in.shape), resident(w_ffn_out.shape),
                  resident((1, d))],
        out_specs=pl.BlockSpec((1, TAIL_TOK, d), lambda i, t: (i, t, 0)),
        out_shape=jax.ShapeDtypeStruct((b, seq, d), F32),
        scratch_shapes=[pltpu.VMEM((TAIL_TOK, 2 * GROUP_W), BF16)],
        compiler_params=_cparams(("parallel", "parallel")),
        name="tail",
    )(gdn_f, gdn_b, z, ret_f, ret_b, rg, x, mod3, gdn_g, ret_g, nffn_g, w_out, w_ffn_in, w_ffn_out, final_g)


def _rope_tables(ctx_len, n_lat):
    def angles(pos, n_pairs):
        inv = ROPE_THETA ** (-np.arange(n_pairs, dtype=np.float64) / n_pairs)
        return pos[:, None] * inv[None, :]

    rows = n_lat // GRID_W
    row = np.repeat(np.arange(rows, dtype=np.float64), GRID_W)
    col = np.tile(np.arange(GRID_W, dtype=np.float64), rows)
    zeros = np.zeros((ctx_len,), np.float64)
    p_seq = np.concatenate([np.arange(ctx_len, dtype=np.float64), np.full((n_lat,), float(ctx_len))])
    ang = np.concatenate([angles(p_seq, ROPE_PAIRS[0]),
                          angles(np.concatenate([zeros, row]), ROPE_PAIRS[1]),
                          angles(np.concatenate([zeros, col]), ROPE_PAIRS[2])], axis=-1)
    cos, sin = np.cos(ang).astype(np.float32), np.sin(ang).astype(np.float32)
    return jnp.asarray(np.concatenate([cos, cos, -sin, sin], axis=-1))


def _lane_row(values):
    flat = values.reshape(1, -1).astype(F32)
    return jnp.pad(flat, ((0, 0), (0, GATE_LANES - flat.shape[1])))


def kernel(x, c, ctx, c_ctx, ada_w, ada_b, norm_mix_g, norm_ffn_g, w_in, conv_w, gdn_a_log, gdn_dt_bias,
           gdn_norm_g, ret_decay_logit, ret_norm_g, w_out, w_ffn_in, w_ffn_out, final_g):
    assert ada_w.shape[0] == 1, "single-layer block"
    b, seq, d = x.shape
    ctx_len = ctx.shape[1]
    assert ctx_len == TILE and seq % TILE == 0 and seq % TAIL_TOK == 0 and b + 1 <= 8

    cond = jnp.concatenate([c, c_ctx[None, :], jnp.zeros((8 - b - 1, d), F32)], axis=0)
    mod = _ada(cond, ada_w, ada_b)
    mod3 = mod[:, None, :]

    n_qkvz = 4 * GROUP_W
    n_gate = 4 * N_HEADS
    qkv, z, ab, pr, rg = _inproj(x, ctx, mod3, norm_mix_g, jnp.swapaxes(w_in, 1, 2), conv_w, n_qkvz, n_gate,
                                 _rope_tables(ctx_len, seq))

    dtb_row = _lane_row(gdn_dt_bias[0])
    gdn_f, gdn_b, ret_f, ret_b, w_out_b, w_ffn_in_b, w_ffn_out_b = _mixers(
        qkv, ab, dtb_row, _lane_row(gdn_a_log[0]), pr, _lane_row(ret_decay_logit[0]),
        (w_out, w_ffn_in, w_ffn_out))

    return _tail(gdn_f, gdn_b, z, ret_f, ret_b, rg, x, mod3, gdn_norm_g, ret_norm_g, norm_ffn_g,
                 w_out_b, w_ffn_in_b, w_ffn_out_b, final_g[None, :])
```

```python
import functools

import jax
import jax.numpy as jnp
import numpy as np
from jax import lax
from jax.experimental import pallas as pl
from jax.experimental.pallas import tpu as pltpu

F32 = jnp.float32
BF16 = jnp.bfloat16

HEAD_DIM = 128
N_HEADS = 4
GROUP_W = N_HEADS * HEAD_DIM
CONV_K = 5
GRID_W = 64
ROPE_THETA = 10000.0
ROPE_PAIRS = (16, 24, 24)
NORM_EPS = 1e-6

CHUNK = 64
INV_BASE = 8
TILE = 256
CHUNKS_PER_TILE = TILE // CHUNK
HALO = 16
GATE_LANES = 128
TAIL_TOK = 512
FFN_CHUNK = 1024
VMEM_LIMIT = 56 * 1024 * 1024


def _cparams(sem):
    return pltpu.CompilerParams(dimension_semantics=sem, vmem_limit_bytes=VMEM_LIMIT)


def _dot(a, b):
    return jnp.dot(a, b, preferred_element_type=F32)


def _dot_nt(a, b):
    return lax.dot_general(a, b, (((1,), (1,)), ((), ())), preferred_element_type=F32)


def _split(a):
    hi = a.astype(BF16)
    lo = (a - hi.astype(F32)).astype(BF16)
    return hi, lo


def _silu(x):
    return x * jax.nn.sigmoid(x)


def _lane_bcast(col, width):
    return jnp.broadcast_to(col, (col.shape[0], width))


def _ada_kernel(cond_ref, w_ref, b_ref, o_ref):
    a_hi, a_lo = _split(_silu(cond_ref[...]))
    w_hi, w_lo = _split(w_ref[0])
    o_ref[...] = _dot(a_hi, w_hi) + _dot(a_lo, w_hi) + _dot(a_hi, w_lo) + b_ref[...]


def _ada(cond, w, b):
    rows, d = cond.shape
    n = w.shape[2]
    bn = 1536
    return pl.pallas_call(
        _ada_kernel,
        grid=(n // bn,),
        in_specs=[pl.BlockSpec((rows, d), lambda j: (0, 0)),
                  pl.BlockSpec((1, d, bn), lambda j: (0, 0, j)),
                  pl.BlockSpec((1, bn), lambda j: (0, j))],
        out_specs=pl.BlockSpec((rows, bn), lambda j: (0, j)),
        out_shape=jax.ShapeDtypeStruct((rows, n), F32),
        compiler_params=_cparams(("parallel",)),
        name="ada",
    )(cond, w, b)


def _rms_mod(x, g, shift, scale):
    ms = jnp.mean(x * x, axis=-1, keepdims=True)
    return (x * lax.rsqrt(ms + NORM_EPS) * g) * (1.0 + scale) + shift


def _conv_tile(ext_ref, w_ref, o_ref):
    pad = (CONV_K - 1) // 2
    rows = TILE + 2 * HALO
    q_scale = HEAD_DIM ** -0.5
    for part in range(3):
        for hd in range(N_HEADS):
            lo = part * GROUP_W + hd * HEAD_DIM
            slab = ext_ref[:, lo:lo + HEAD_DIM]
            acc = None
            for i in range(CONV_K):
                shifted = slab if i == pad else pltpu.roll(slab, (pad - i) % rows, 0)
                term = shifted[HALO:HALO + TILE] * w_ref[0, i:i + 1, lo:lo + HEAD_DIM]
                acc = term if acc is None else acc + term
            y = _silu(acc)
            if part < 2:
                inv_norm = lax.rsqrt(jnp.sum(y * y, axis=-1, keepdims=True) + NORM_EPS)
                y = y * (inv_norm * q_scale if part == 0 else inv_norm)
            o_ref[0, :, lo:lo + HEAD_DIM] = y.astype(BF16)


def _inproj_kernel(x_ref, ctx_ref, mod_ref, g_ref, w_ref, cw_ref, rope_ref,
                   qkv_ref, z_ref, ab_ref, pr_ref, rg_ref, wa_ref, wab_ref, wr_ref, ext_ref,
                   *, d_model, n_half, n_gate, tiles_per_sample):
    step = pl.program_id(0)
    t = lax.rem(jnp.minimum(step, pl.num_programs(0) - 2), tiles_per_sample)
    t_conv = lax.rem(jnp.maximum(step - 1, 0), tiles_per_sample)
    width = 3 * GROUP_W

    @pl.when(step == 0)
    def _():
        wa_ref[...] = w_ref[0, :n_half, :].T.astype(BF16)
        wr_ref[...] = w_ref[0, n_half + n_gate:, :].T.astype(BF16)
        lane = lax.broadcasted_iota(jnp.int32, (d_model, GATE_LANES), 1)
        wab_ref[...] = jnp.where(lane < n_gate, w_ref[0, n_half:n_half + GATE_LANES, :].T, 0.0).astype(BF16)
        ext_ref[...] = jnp.zeros_like(ext_ref)

    xin = jnp.where(t == 0, ctx_ref[0], x_ref[0])
    mod = mod_ref[0]
    h = _rms_mod(xin, g_ref[...], mod[:, 0:d_model], mod[:, d_model:2 * d_model]).astype(BF16)
    pa = _dot(h, wa_ref[...])
    z_ref[0] = pa[:, width:].astype(BF16)

    has_next = jnp.logical_and(t_conv >= 1, t_conv < tiles_per_sample - 1)
    ext_ref[HALO + TILE:, :] = jnp.where(has_next, pa[:HALO, :width], 0.0)
    _conv_tile(ext_ref, cw_ref, qkv_ref)
    ext_ref[0:HALO, :] = jnp.where(t >= 2, ext_ref[TILE:HALO + TILE, :], 0.0)
    ext_ref[HALO:HALO + TILE, :] = pa[:, :width]

    ab_ref[0] = _dot(h, wab_ref[...])
    pr = _dot(h, wr_ref[...])
    cos2 = rope_ref[:, :HEAD_DIM]
    sin2 = rope_ref[:, HEAD_DIM:]
    k_scale = HEAD_DIM ** -0.5
    for hd in range(2 * N_HEADS):
        lo = hd * HEAD_DIM
        tt = pr[:, lo:lo + HEAD_DIM]
        rot = tt * cos2 + pltpu.roll(tt, HEAD_DIM // 2, 1) * sin2
        if hd >= N_HEADS:
            rot = rot * k_scale
        pr_ref[0, :, lo:lo + HEAD_DIM] = rot.astype(BF16)
    pr_ref[0, :, 2 * GROUP_W:] = pr[:, 2 * GROUP_W:width].astype(BF16)
    rg_ref[0] = pr[:, width:].astype(BF16)


def _inproj(x, ctx, mod3, g, w_in, conv_w, n_half, n_gate, rope):
    b, seq, d = x.shape
    n_tiles = (ctx.shape[1] + seq) // TILE
    t_total = n_tiles * TILE
    total = b * n_tiles
    width = 3 * GROUP_W
    assert w_in.shape[1] == 2 * n_half + n_gate and n_gate <= GATE_LANES and n_half == width + GROUP_W

    def now(f):
        flat = jnp.minimum(f, total - 1)
        return flat // n_tiles, flat % n_tiles

    def lagged(f):
        flat = jnp.maximum(f - 1, 0)
        return flat // n_tiles, flat % n_tiles

    def tok(cols):
        return pl.BlockSpec((1, TILE, cols), lambda f: (now(f)[0], now(f)[1], 0))

    def lat(cols):
        return pl.BlockSpec((1, TILE, cols), lambda f: (now(f)[0], jnp.maximum(now(f)[1] - 1, 0), 0))

    return pl.pallas_call(
        functools.partial(_inproj_kernel, d_model=d, n_half=n_half, n_gate=n_gate, tiles_per_sample=n_tiles),
        grid=(total + 1,),
        in_specs=[pl.BlockSpec((1, TILE, d), lambda f: (now(f)[0], jnp.maximum(now(f)[1] - 1, 0), 0)),
                  pl.BlockSpec((1, TILE, d), lambda f: (now(f)[0], 0, 0)),
                  pl.BlockSpec((1, 1, mod3.shape[2]), lambda f: (jnp.where(now(f)[1] == 0, b, now(f)[0]), 0, 0)),
                  pl.BlockSpec((1, d), lambda f: (0, 0)),
                  pl.BlockSpec(w_in.shape, lambda f: (0, 0, 0), pipeline_mode=pl.Buffered(1)),
                  pl.BlockSpec(conv_w.shape, lambda f: (0, 0, 0)),
                  pl.BlockSpec((TILE, 2 * HEAD_DIM), lambda f: (now(f)[1], 0))],
        out_specs=[pl.BlockSpec((1, TILE, width), lambda f: (lagged(f)[0], lagged(f)[1], 0)),
                   lat(GROUP_W), tok(GATE_LANES), tok(width), lat(GROUP_W)],
        out_shape=[jax.ShapeDtypeStruct((b, t_total, width), BF16),
                   jax.ShapeDtypeStruct((b, seq, GROUP_W), BF16),
                   jax.ShapeDtypeStruct((b, t_total, GATE_LANES), F32),
                   jax.ShapeDtypeStruct((b, t_total, width), BF16),
                   jax.ShapeDtypeStruct((b, seq, GROUP_W), BF16)],
        scratch_shapes=[pltpu.VMEM((d, n_half), BF16), pltpu.VMEM((d, GATE_LANES), BF16),
                        pltpu.VMEM((d, n_half), BF16), pltpu.VMEM((TILE + 2 * HALO, width), F32)],
        compiler_params=_cparams(("arbitrary",)),
        name="inproj",
    )(x, ctx, mod3, g, w_in, conv_w, rope)


def _block_diag(y, bd_mask):
    yb = y.astype(BF16)
    return jnp.where(bd_mask, jnp.concatenate([yb] * N_HEADS, axis=0), jnp.zeros((), BF16))


def _packed_matmul(x, y_bd):
    return _dot(x.astype(BF16), y_bd)


def _unit_lower_inverse_stages(a_list, eye, base_mask, merge_masks, bd_mask, out):
    pm = _packed_matmul
    ps = [-jnp.where(base_mask, a, 0.0) for a in a_list]
    xs = [eye + p for p in ps]
    ps = [p.astype(BF16) for p in ps]
    ps = [pm(p, _block_diag(p, bd_mask)).astype(BF16) for p in ps]
    yield
    for _ in range(INV_BASE.bit_length() - 3):
        prods = [_dot(jnp.concatenate([x.astype(BF16), p], axis=0), _block_diag(p, bd_mask))
                 for x, p in zip(xs, ps)]
        xs = [x + pr[:CHUNK] for x, pr in zip(xs, prods)]
        ps = [pr[CHUNK:].astype(BF16) for pr in prods]
        yield
    xs = [x + pm(x, _block_diag(p, bd_mask)) for x, p in zip(xs, ps)]
    yield
    for m in merge_masks:
        zs = [pm(jnp.where(m, a, 0.0), _block_diag(x, bd_mask)).astype(BF16) for a, x in zip(a_list, xs)]
        yield
        xs = [x - pm(x, _block_diag(z, bd_mask)) for x, z in zip(xs, zs)]
        yield
    out.extend(xs)


def _gdn_front_stages(groups, ring_base, mid_base, a_scr, rhs_scr, brow_scr, wq_scr, qkd_scr, kdt_scr,
                      el_scr):
    lane = lax.broadcasted_iota(jnp.int32, (CHUNK, GATE_LANES), 1)
    low_half = lax.broadcasted_iota(jnp.int32, (CHUNK, HEAD_DIM), 1) < CHUNK
    zero_k = jnp.zeros((CHUNK, HEAD_DIM), BF16)
    for g in groups:
        tcum = g["consts"][0]
        g_hi, g_lo = _split(g["gates"])
        g["gc"] = _dot(tcum, g_hi) + _dot(tcum, g_lo)
    yield

    for gi, g in enumerate(groups):
        g_lane0 = N_HEADS if g["rev"] else 0
        b_lane0 = 2 * N_HEADS + g_lane0
        last = 0 if g["rev"] else CHUNK - 1
        gc, gates = g["gc"], g["gates"]
        ct = jnp.where(lane < 2 * N_HEADS, gc, gates).T
        egl_t = jnp.exp(ct[:, last:last + 1] - ct)
        gcb, kb, ms = [], [], []
        for h in range(N_HEADS):
            lo = h * HEAD_DIM
            gl = g_lane0 + h
            ring_unit = (ring_base + gi) * N_HEADS + h
            gcb.append(_lane_bcast(gc[:, gl:gl + 1], HEAD_DIM))
            beta = _lane_bcast(gates[:, b_lane0 + h:b_lane0 + h + 1], HEAD_DIM)
            eg_h = jnp.exp(gcb[h])
            q_h = g["q"][:, lo:lo + HEAD_DIM]
            k_h = g["k"][:, lo:lo + HEAD_DIM].astype(F32)
            kb.append((k_h * beta).astype(BF16))
            rhs_scr[(mid_base + gi) * N_HEADS + h] = jnp.concatenate(
                [g["v"][:, lo:lo + HEAD_DIM], (k_h * eg_h).astype(BF16)], axis=1)
            wq_scr[ring_unit, CHUNK:, :] = (q_h * eg_h).astype(BF16)
            kdt_scr[ring_unit] = (k_h.T * egl_t[gl:gl + 1, :]).astype(BF16)
        d_parts = []
        for p in range(N_HEADS // 2):
            h0, h1 = 2 * p, 2 * p + 1
            lo = h0 * HEAD_DIM
            lhs = jnp.concatenate([g["q"][:, lo:lo + 2 * HEAD_DIM], jnp.concatenate([kb[h0], kb[h1]], axis=1)],
                                  axis=0)
            k0 = g["k"][:, lo:lo + HEAD_DIM]
            k1 = g["k"][:, lo + HEAD_DIM:lo + 2 * HEAD_DIM]
            rhs_bd = jnp.concatenate([jnp.concatenate([k0, zero_k], axis=1),
                                      jnp.concatenate([zero_k, k1], axis=1)], axis=0)
            ms.append(_dot_nt(lhs, rhs_bd))
            g_row = jnp.concatenate([ct[g_lane0 + h0:g_lane0 + h0 + 1, :], ct[g_lane0 + h1:g_lane0 + h1 + 1, :]],
                                    axis=1)
            d_parts.append(jnp.where(low_half, gcb[h0], gcb[h1]) - g_row)
        incl4, strict4 = g["consts"][1:]
        e = jnp.exp(jnp.where(incl4, jnp.concatenate(d_parts, axis=1), 0.0))
        a_scr[mid_base + gi] = jnp.where(strict4, jnp.concatenate([m[CHUNK:] for m in ms], axis=1) * e, 0.0)
        qkd_scr[ring_base + gi] = jnp.where(incl4, jnp.concatenate([m[:CHUNK] for m in ms], axis=1) * e,
                                            0.0).astype(BF16)
        el_scr[ring_base + gi] = jnp.exp(gc[last:last + 1, :])
        brow_scr[mid_base + gi] = jnp.concatenate(
            [ct[b_lane0 + h:b_lane0 + h + 1, :] for h in range(N_HEADS)], axis=1)
        yield


def _gdn_solve_stages(n_groups, masks, ring_base, mid_base, a_scr, rhs_scr, brow_scr, u_scr, wq_scr):
    eye4, base_mask, merge_masks, bd_mask = masks
    t_invs = []
    yield from _unit_lower_inverse_stages([a_scr[mid_base + gi] for gi in range(n_groups)],
                                          eye4, base_mask, merge_masks, bd_mask, t_invs)
    for gi, t_inv in enumerate(t_invs):
        t_b = (t_inv * brow_scr[mid_base + gi]).astype(BF16)
        for h in range(N_HEADS):
            sol = _dot(t_b[:, h * CHUNK:(h + 1) * CHUNK], rhs_scr[(mid_base + gi) * N_HEADS + h])
            ring_unit = (ring_base + gi) * N_HEADS + h
            u_scr[ring_unit] = sol[:, :HEAD_DIM]
            wq_scr[ring_unit, :CHUNK, :] = sol[:, HEAD_DIM:].astype(BF16)


def _gdn_scan_stages(slot_base, keep_state, u_scr, wq_scr, qkd_scr, kdt_scr, el_scr, s_ref, of_ref, ob_ref):
    for j in range(CHUNKS_PER_TILE):
        units = [(slot_base + 2 * j + d, h, d * N_HEADS + h) for d in range(2) for h in range(N_HEADS)]
        states = [jnp.where(keep_state, s_ref[slot], 0.0) if j == 0 else s_ref[slot] for _, _, slot in units]
        wss = [_dot(wq_scr[gi * N_HEADS + h], s.astype(BF16)) for (gi, h, _), s in zip(units, states)]
        yield
        v_news = [(u_scr[gi * N_HEADS + h] - ws[:CHUNK]).astype(BF16) for (gi, h, _), ws in zip(units, wss)]
        outs = [ws[CHUNK:] + _dot(qkd_scr[gi][:, h * CHUNK:(h + 1) * CHUNK], vn)
                for (gi, h, _), ws, vn in zip(units, wss, v_news)]
        for (gi, h, slot), s, vn in zip(units, states, v_news):
            s_ref[slot] = el_scr[gi][:, slot:slot + 1] * s + _dot(kdt_scr[gi * N_HEADS + h], vn)
        rf = j * CHUNK
        rb = (CHUNKS_PER_TILE - 1 - j) * CHUNK
        of_ref[0, rf:rf + CHUNK, :] = jnp.concatenate(outs[:N_HEADS], axis=1).astype(BF16)
        ob_ref[0, rb:rb + CHUNK, :] = jnp.concatenate(outs[N_HEADS:], axis=1).astype(BF16)
        yield


def _interleave(*stage_generators):
    live = list(stage_generators)
    while live:
        for gen in list(live):
            try:
                next(gen)
            except StopIteration:
                live.remove(gen)


def _gate_tile(ab, dtb, a_log):
    z = ab + dtb
    softplus = jnp.maximum(z, 0.0) + jnp.log(1.0 + jnp.exp(-jnp.abs(z)))
    lane = lax.broadcasted_iota(jnp.int32, ab.shape, 1)
    return jnp.where(lane < 2 * N_HEADS, -jnp.exp(a_log) * softplus, jax.nn.sigmoid(ab))


def _gdn_kernel(qkvf_ref, abf_ref, qkvb_ref, abb_ref, dtb_ref, alog_ref, rqkvf_ref, rqkvb_ref, logit_ref,
                w1_ref, w2_ref, w3_ref, of_ref, ob_ref, rof_ref, rob_ref, w1b_ref, w2b_ref, w3b_ref,
                s_ref, a_scr, rhs_scr, brow_scr, u_scr, wq_scr, qkd_scr, kdt_scr, el_scr,
                r_ref, decay_ref, xi_ref, zeta_ref, *, tiles_per_sample):
    step = pl.program_id(0)
    lg_row = _log_sigmoid(logit_ref[...])
    for src, dst in ((w1_ref, w1b_ref), (w2_ref, w2b_ref), (w3_ref, w3b_ref)):
        dst[...] = src[0].astype(BF16)
    keep_state = lax.rem(step + tiles_per_sample - 2, tiles_per_sample) != 0
    groups_per_tile = 2 * CHUNKS_PER_TILE
    front_ring = lax.rem(step, 3) * groups_per_tile
    solve_ring = lax.rem(step + 2, 3) * groups_per_tile
    scan_ring = lax.rem(step + 1, 3) * groups_per_tile
    front_mid = lax.rem(step, 2) * groups_per_tile
    solve_mid = groups_per_tile - front_mid

    @pl.when(step == 0)
    def _():
        for scr in (s_ref, a_scr, rhs_scr, brow_scr, u_scr, wq_scr, qkd_scr, kdt_scr, el_scr, r_ref):
            scr[...] = jnp.zeros_like(scr)
        _ret_tables(lg_row, decay_ref, xi_ref, zeta_ref)

    ri = lax.broadcasted_iota(jnp.int32, (CHUNK, CHUNK), 0)
    ci = lax.broadcasted_iota(jnp.int32, (CHUNK, CHUNK), 1)
    ri4 = lax.broadcasted_iota(jnp.int32, (CHUNK, N_HEADS * CHUNK), 0)
    ci4 = lax.broadcasted_iota(jnp.int32, (CHUNK, N_HEADS * CHUNK), 1) % CHUNK
    rbd = lax.broadcasted_iota(jnp.int32, (N_HEADS * CHUNK, N_HEADS * CHUNK), 0) // CHUNK
    cbd = lax.broadcasted_iota(jnp.int32, (N_HEADS * CHUNK, N_HEADS * CHUNK), 1) // CHUNK
    bd_mask = rbd == cbd
    eye4 = (ri4 == ci4).astype(F32)
    base_mask = (ri4 // INV_BASE) == (ci4 // INV_BASE)
    merge_masks = []
    size = 2 * INV_BASE
    while size <= CHUNK:
        merge_masks.append(jnp.logical_and((ri4 // size) == (ci4 // size),
                                           (ri4 // (size // 2)) != (ci4 // (size // 2))))
        size *= 2

    cf = ((ci <= ri).astype(BF16), ri4 >= ci4, ri4 > ci4)
    cb = ((ci >= ri).astype(BF16), ri4 <= ci4, ri4 < ci4)
    dtb = dtb_ref[...]
    alog = alog_ref[...]

    groups = []
    for j in range(CHUNKS_PER_TILE):
        rf = j * CHUNK
        rb = (CHUNKS_PER_TILE - 1 - j) * CHUNK
        for ref, ab_ref, r0, rev, consts in ((qkvf_ref, abf_ref, rf, False, cf), (qkvb_ref, abb_ref, rb, True, cb)):
            groups.append(dict(q=ref[0, r0:r0 + CHUNK, 0:GROUP_W], k=ref[0, r0:r0 + CHUNK, GROUP_W:2 * GROUP_W],
                               v=ref[0, r0:r0 + CHUNK, 2 * GROUP_W:],
                               gates=_gate_tile(ab_ref[0, r0:r0 + CHUNK, :], dtb, alog), rev=rev, consts=consts))
    _interleave(
        _ret_stages(rqkvf_ref, rqkvb_ref, rof_ref, rob_ref, lg_row, keep_state, r_ref, decay_ref, xi_ref, zeta_ref),
        _gdn_solve_stages(len(groups), (eye4, base_mask, merge_masks, bd_mask), solve_ring, solve_mid,
                          a_scr, rhs_scr, brow_scr, u_scr, wq_scr),
        _gdn_scan_stages(scan_ring, keep_state, u_scr, wq_scr, qkd_scr, kdt_scr, el_scr, s_ref, of_ref, ob_ref),
        _gdn_front_stages(groups, front_ring, front_mid, a_scr, rhs_scr, brow_scr, wq_scr, qkd_scr, kdt_scr,
                          el_scr))


def _bwd_tile(s, n_tiles):
    return jnp.where(s == 0, 0, n_tiles - s)


def _latent_block(tile, bwd, n_tiles):
    return jnp.where(tile == 0, n_tiles - 2 if bwd else 0, tile - 1)


def _rider(w, n_steps):
    _, rows, cols = w.shape
    block = next(r for r in range(16, rows + 1, 16) if rows % r == 0 and rows // r <= n_steps)
    last = rows // block - 1
    return (pl.BlockSpec((1, block, cols), lambda f: (0, jnp.minimum(f, last), 0)),
            pl.BlockSpec((block, cols), lambda f: (jnp.minimum(f, last), 0)),
            jax.ShapeDtypeStruct((rows, cols), BF16))


def _mixers(qkv, ab, dtb_row, alog_row, pr, logit_row, cast_weights):
    b, t_total, _ = qkv.shape
    n_tiles = t_total // TILE

    total = b * n_tiles
    riders = [_rider(w, total + 2) for w in cast_weights]

    def block_of(f, bwd, lag, latent):
        flat = jnp.maximum(f - 2, 0) if lag else jnp.minimum(f, total - 1)
        pos = flat % n_tiles
        tile = _bwd_tile(pos, n_tiles) if bwd else pos
        return (flat // n_tiles, _latent_block(tile, bwd, n_tiles) if latent else tile, 0)

    def tok(width, bwd, lag=False, latent=False):
        return pl.BlockSpec((1, TILE, width), lambda f: block_of(f, bwd, lag, latent))

    row = pl.BlockSpec((1, GATE_LANES), lambda f: (0, 0))
    per_tile = 2 * CHUNKS_PER_TILE
    n_groups = 3 * per_tile
    n_units = n_groups * N_HEADS
    n_mid = 2 * per_tile
    return pl.pallas_call(
        functools.partial(_gdn_kernel, tiles_per_sample=n_tiles),
        grid=(total + 2,),
        in_specs=[tok(3 * GROUP_W, False), tok(GATE_LANES, False),
                  tok(3 * GROUP_W, True), tok(GATE_LANES, True), row, row,
                  tok(3 * GROUP_W, False, lag=True), tok(3 * GROUP_W, True, lag=True), row]
        + [r[0] for r in riders],
        out_specs=[tok(GROUP_W, bwd, lag=True, latent=True) for bwd in (False, True, False, True)]
        + [r[1] for r in riders],
        out_shape=[jax.ShapeDtypeStruct((b, t_total - TILE, GROUP_W), BF16)] * 4 + [r[2] for r in riders],
        scratch_shapes=[pltpu.VMEM((2 * N_HEADS, HEAD_DIM, HEAD_DIM), F32),
                        pltpu.VMEM((n_mid, CHUNK, N_HEADS * CHUNK), F32),
                        pltpu.VMEM((n_mid * N_HEADS, CHUNK, 2 * HEAD_DIM), BF16),
                        pltpu.VMEM((n_mid, 1, N_HEADS * CHUNK), F32),
                        pltpu.VMEM((n_units, CHUNK, HEAD_DIM), F32),
                        pltpu.VMEM((n_units, 2 * CHUNK, HEAD_DIM), BF16),
                        pltpu.VMEM((n_groups, CHUNK, N_HEADS * CHUNK), BF16),
                        pltpu.VMEM((n_units, HEAD_DIM, CHUNK), BF16),
                        pltpu.VMEM((n_groups, 1, GATE_LANES), F32),
                        pltpu.VMEM((2 * N_HEADS, HEAD_DIM, HEAD_DIM), F32),
                        pltpu.VMEM((2 * N_HEADS, TILE, TILE), F32),
                        pltpu.VMEM((2 * N_HEADS, TILE, HEAD_DIM), F32),
                        pltpu.VMEM((2 * N_HEADS, TILE, HEAD_DIM), F32)],
        compiler_params=_cparams(("arbitrary",)),
        name="mixers",
    )(qkv, ab, qkv, ab, dtb_row, alog_row, pr, pr, logit_row, *cast_weights)


def _log_sigmoid(x):
    return jnp.minimum(x, 0.0) - jnp.log(1.0 + jnp.exp(-jnp.abs(x)))


def _ret_tables(lg_row, decay_ref, xi_ref, zeta_ref):
    ri = lax.broadcasted_iota(jnp.int32, (TILE, TILE), 0)
    ci = lax.broadcasted_iota(jnp.int32, (TILE, TILE), 1)
    row = lax.broadcasted_iota(jnp.int32, (TILE, HEAD_DIM), 0).astype(F32)
    for slot in range(2 * N_HEADS):
        rev = slot >= N_HEADS
        lg = lg_row[:, slot:slot + 1]
        diff = ((ci - ri) if rev else (ri - ci)).astype(F32)
        pos = ((TILE - 1.0) - row) if rev else row
        decay_ref[slot] = jnp.where(diff >= 0, jnp.exp(lg * jnp.maximum(diff, 0.0)), 0.0)
        xi_ref[slot] = jnp.exp(lg * (pos + 1.0))
        zeta_ref[slot] = jnp.exp(lg * ((TILE - 1.0) - pos))


def _ret_stages(qkvf_ref, qkvb_ref, of_ref, ob_ref, lg_row, keep_state, r_ref, decay_ref, xi_ref, zeta_ref):
    units = [(refs, (N_HEADS if rev else 0) + h, h * HEAD_DIM)
             for refs, rev in (((qkvf_ref, of_ref), False), ((qkvb_ref, ob_ref), True))
             for h in range(N_HEADS)]
    qs = [refs[0][0, :, lo:lo + HEAD_DIM] for refs, _, lo in units]
    ks = [refs[0][0, :, GROUP_W + lo:GROUP_W + lo + HEAD_DIM] for refs, _, lo in units]
    vs = [refs[0][0, :, 2 * GROUP_W + lo:2 * GROUP_W + lo + HEAD_DIM] for refs, _, lo in units]
    per_yield = 2
    qks = []
    for i, (q, k) in enumerate(zip(qs, ks)):
        qks.append(_dot_nt(q, k))
        if i % per_yield == per_yield - 1:
            yield
    rs = [jnp.where(keep_state, r_ref[slot], 0.0) for _, slot, _ in units]
    for i, ((refs, slot, lo), q, k, v, qk, r) in enumerate(zip(units, qs, ks, vs, qks, rs)):
        lhs = jnp.concatenate([(qk * decay_ref[slot]).astype(BF16), (k * zeta_ref[slot]).T.astype(BF16)], axis=0)
        both = _dot(lhs, v)
        refs[1][0, :, lo:lo + HEAD_DIM] = (
            both[:TILE] + _dot((q * xi_ref[slot]).astype(BF16), r.astype(BF16))).astype(BF16)
        g_chunk = jnp.exp(lg_row[:, slot:slot + 1] * float(TILE))
        r_ref[slot] = g_chunk * r + both[TILE:]
        if i % per_yield == per_yield - 1:
            yield


def _tail_kernel(gf_ref, gb_ref, z_ref, rf_ref, rb_ref, rg_ref, x_ref, mod_ref, gdn_g_ref, ret_g_ref,
                 nffn_g_ref, w_ref, wi_ref, wo_ref, fg_ref, o_ref, y_ref, *, d_model, d_ff):
    o = gf_ref[0].astype(F32) + gb_ref[0].astype(F32)
    r = rf_ref[0].astype(F32) + rb_ref[0].astype(F32)
    z = z_ref[0].astype(F32)
    rg = rg_ref[0].astype(F32)
    for h in range(N_HEADS):
        lo = h * HEAD_DIM
        o_h = o[:, lo:lo + HEAD_DIM]
        o_n = o_h * lax.rsqrt(jnp.mean(o_h * o_h, axis=-1, keepdims=True) + NORM_EPS) * gdn_g_ref[...]
        y_ref[:, lo:lo + HEAD_DIM] = (o_n * _silu(z[:, lo:lo + HEAD_DIM])).astype(BF16)
        r_h = r[:, lo:lo + HEAD_DIM]
        mu = jnp.mean(r_h, axis=-1, keepdims=True)
        cen = r_h - mu
        var = jnp.mean(cen * cen, axis=-1, keepdims=True)
        r_n = cen * lax.rsqrt(var + NORM_EPS) * ret_g_ref[...]
        y_ref[:, GROUP_W + lo:GROUP_W + lo + HEAD_DIM] = (r_n * _silu(rg[:, lo:lo + HEAD_DIM])).astype(BF16)
    mod = mod_ref[0]
    x1 = x_ref[0] + mod[:, 2 * d_model:3 * d_model] * _dot(y_ref[...], w_ref[...])
    h2 = _rms_mod(x1, nffn_g_ref[...], mod[:, 3 * d_model:4 * d_model],
                  mod[:, 4 * d_model:5 * d_model]).astype(BF16)
    acc = None
    for lo in range(0, d_ff, FFN_CHUNK):
        hi = min(lo + FFN_CHUNK, d_ff)
        gate = _dot(h2, wi_ref[:, lo:hi])
        up = _dot(h2, wi_ref[:, d_ff + lo:d_ff + hi])
        part = _dot((_silu(gate) * up).astype(BF16), wo_ref[lo:hi, :])
        acc = part if acc is None else acc + part
    x2 = x1 + mod[:, 5 * d_model:6 * d_model] * acc
    ms = jnp.mean(x2 * x2, axis=-1, keepdims=True)
    o_ref[0] = x2 * lax.rsqrt(ms + NORM_EPS) * fg_ref[...]


def _tail(gdn_f, gdn_b, z, ret_f, ret_b, rg, x, mod3, gdn_g, ret_g, nffn_g, w_out, w_ffn_in, w_ffn_out,
          final_g):
    b, seq, d = x.shape
    d_ff = w_ffn_out.shape[0]
    tok = pl.BlockSpec((1, TAIL_TOK, GROUP_W), lambda i, t: (i, t, 0))

    def resident(shape):
        return pl.BlockSpec(shape, lambda i, t: (0,) * len(shape), pipeline_mode=pl.Buffered(1))

    return pl.pallas_call(
        functools.partial(_tail_kernel, d_model=d, d_ff=d_ff),
        grid=(b, seq // TAIL_TOK),
        in_specs=[tok, tok, tok, tok, tok, tok,
                  pl.BlockSpec((1, TAIL_TOK, d), lambda i, t: (i, t, 0)),
                  pl.BlockSpec((1, 1, mod3.shape[2]), lambda i, t: (i, 0, 0)),
                  resident((1, HEAD_DIM)), resident((1, HEAD_DIM)), resident((1, d)),
                  resident(w_out.shape), resident(w_ffn_in.shape), resident(w_ffn_out.shape),
                  resident((1, d))],
        out_specs=pl.BlockSpec((1, TAIL_TOK, d), lambda i, t: (i, t, 0)),
        out_shape=jax.ShapeDtypeStruct((b, seq, d), F32),
        scratch_shapes=[pltpu.VMEM((TAIL_TOK, 2 * GROUP_W), BF16)],
        compiler_params=_cparams(("parallel", "parallel")),
        name="tail",
    )(gdn_f, gdn_b, z, ret_f, ret_b, rg, x, mod3, gdn_g, ret_g, nffn_g, w_out, w_ffn_in, w_ffn_out, final_g)


def _rope_tables(ctx_len, n_lat):
    def angles(pos, n_pairs):
        inv = ROPE_THETA ** (-np.arange(n_pairs, dtype=np.float64) / n_pairs)
        return pos[:, None] * inv[None, :]

    rows = n_lat // GRID_W
    row = np.repeat(np.arange(rows, dtype=np.float64), GRID_W)
    col = np.tile(np.arange(GRID_W, dtype=np.float64), rows)
    zeros = np.zeros((ctx_len,), np.float64)
    p_seq = np.concatenate([np.arange(ctx_len, dtype=np.float64), np.full((n_lat,), float(ctx_len))])
    ang = np.concatenate([angles(p_seq, ROPE_PAIRS[0]),
                          angles(np.concatenate([zeros, row]), ROPE_PAIRS[1]),
                          angles(np.concatenate([zeros, col]), ROPE_PAIRS[2])], axis=-1)
    cos, sin = np.cos(ang).astype(np.float32), np.sin(ang).astype(np.float32)
    return jnp.asarray(np.concatenate([cos, cos, -sin, sin], axis=-1))


def _lane_row(values):
    flat = values.reshape(1, -1).astype(F32)
    return jnp.pad(flat, ((0, 0), (0, GATE_LANES - flat.shape[1])))


def kernel(x, c, ctx, c_ctx, ada_w, ada_b, norm_mix_g, norm_ffn_g, w_in, conv_w, gdn_a_log, gdn_dt_bias,
           gdn_norm_g, ret_decay_logit, ret_norm_g, w_out, w_ffn_in, w_ffn_out, final_g):
    assert ada_w.shape[0] == 1, "single-layer block"
    b, seq, d = x.shape
    ctx_len = ctx.shape[1]
    assert ctx_len == TILE and seq % TILE == 0 and seq % TAIL_TOK == 0 and b + 1 <= 8

    cond = jnp.concatenate([c, c_ctx[None, :], jnp.zeros((8 - b - 1, d), F32)], axis=0)
    mod = _ada(cond, ada_w, ada_b)
    mod3 = mod[:, None, :]

    n_qkvz = 4 * GROUP_W
    n_gate = 4 * N_HEADS
    qkv, z, ab, pr, rg = _inproj(x, ctx, mod3, norm_mix_g, jnp.swapaxes(w_in, 1, 2), conv_w, n_qkvz, n_gate,
                                 _rope_tables(ctx_len, seq))

    dtb_row = _lane_row(gdn_dt_bias[0])
    gdn_f, gdn_b, ret_f, ret_b, w_out_b, w_ffn_in_b, w_ffn_out_b = _mixers(
        qkv, ab, dtb_row, _lane_row(gdn_a_log[0]), pr, _lane_row(ret_decay_logit[0]),
        (w_out, w_ffn_in, w_ffn_out))

    return _tail(gdn_f, gdn_b, z, ret_f, ret_b, rg, x, mod3, gdn_norm_g, ret_norm_g, norm_ffn_g,
                 w_out_b, w_ffn_in_b, w_ffn_out_b, final_g[None, :])
```

```python
import functools

import jax
import jax.numpy as jnp
import numpy as np
from jax import lax
from jax.experimental import pallas as pl
from jax.experimental.pallas import tpu as pltpu

F32 = jnp.float32
BF16 = jnp.bfloat16

HEAD_DIM = 128
N_HEADS = 4
GROUP_W = N_HEADS * HEAD_DIM
CONV_K = 5
GRID_W = 64
ROPE_THETA = 10000.0
ROPE_PAIRS = (16, 24, 24)
NORM_EPS = 1e-6

CHUNK = 64
INV_BASE = 8
TILE = 256
CHUNKS_PER_TILE = TILE // CHUNK
HALO = 16
GATE_LANES = 128
TAIL_TOK = 512
FFN_CHUNK = 1024
VMEM_LIMIT = 56 * 1024 * 1024


def _cparams(sem):
    return pltpu.CompilerParams(dimension_semantics=sem, vmem_limit_bytes=VMEM_LIMIT)


def _dot(a, b):
    return jnp.dot(a, b, preferred_element_type=F32)


def _dot_nt(a, b):
    return lax.dot_general(a, b, (((1,), (1,)), ((), ())), preferred_element_type=F32)


def _split(a):
    hi = a.astype(BF16)
    lo = (a - hi.astype(F32)).astype(BF16)
    return hi, lo


def _silu(x):
    return x * jax.nn.sigmoid(x)


def _lane_bcast(col, width):
    return jnp.broadcast_to(col, (col.shape[0], width))


def _ada_kernel(cond_ref, w_ref, b_ref, o_ref):
    a_hi, a_lo = _split(_silu(cond_ref[...]))
    w_hi, w_lo = _split(w_ref[0])
    o_ref[...] = _dot(a_hi, w_hi) + _dot(a_lo, w_hi) + _dot(a_hi, w_lo) + b_ref[...]


def _ada(cond, w, b):
    rows, d = cond.shape
    n = w.shape[2]
    bn = 1536
    return pl.pallas_call(
        _ada_kernel,
        grid=(n // bn,),
        in_specs=[pl.BlockSpec((rows, d), lambda j: (0, 0)),
                  pl.BlockSpec((1, d, bn), lambda j: (0, 0, j)),
                  pl.BlockSpec((1, bn), lambda j: (0, j))],
        out_specs=pl.BlockSpec((rows, bn), lambda j: (0, j)),
        out_shape=jax.ShapeDtypeStruct((rows, n), F32),
        compiler_params=_cparams(("parallel",)),
        name="ada",
    )(cond, w, b)


def _rms_mod(x, g, shift, scale):
    ms = jnp.mean(x * x, axis=-1, keepdims=True)
    return (x * lax.rsqrt(ms + NORM_EPS) * g) * (1.0 + scale) + shift


def _conv_tile(ext_ref, w_ref, o_ref, parts=(0, 1, 2)):
    pad = (CONV_K - 1) // 2
    rows = TILE + 2 * HALO
    q_scale = HEAD_DIM ** -0.5
    for part in parts:
        for hd in range(N_HEADS):
            lo = part * GROUP_W + hd * HEAD_DIM
            slab = ext_ref[:, lo:lo + HEAD_DIM]
            acc = None
            for i in range(CONV_K):
                shifted = slab if i == pad else pltpu.roll(slab, (pad - i) % rows, 0)
                term = shifted[HALO:HALO + TILE] * w_ref[0, i:i + 1, lo:lo + HEAD_DIM]
                acc = term if acc is None else acc + term
            y = _silu(acc)
            if part < 2:
                inv_norm = lax.rsqrt(jnp.sum(y * y, axis=-1, keepdims=True) + NORM_EPS)
                y = y * (inv_norm * q_scale if part == 0 else inv_norm)
            o_ref[0, :, lo:lo + HEAD_DIM] = y.astype(BF16)


def _inproj_kernel(x_ref, ctx_ref, mod_ref, g_ref, w_ref, cw_ref, rope_ref,
                   qkv_ref, z_ref, ab_ref, pr_ref, rg_ref, wa_ref, wab_ref, wr_ref, ext_ref,
                   *, d_model, n_half, n_gate, tiles_per_sample):
    step = pl.program_id(0)
    t = lax.rem(jnp.minimum(step, pl.num_programs(0) - 2), tiles_per_sample)
    t_conv = lax.rem(jnp.maximum(step - 1, 0), tiles_per_sample)
    width = 3 * GROUP_W

    @pl.when(step == 0)
    def _():
        wa_ref[...] = w_ref[0, :n_half, :].T.astype(BF16)
        wr_ref[...] = w_ref[0, n_half + n_gate:, :].T.astype(BF16)
        lane = lax.broadcasted_iota(jnp.int32, (d_model, GATE_LANES), 1)
        wab_ref[...] = jnp.where(lane < n_gate, w_ref[0, n_half:n_half + GATE_LANES, :].T, 0.0).astype(BF16)
        ext_ref[...] = jnp.zeros_like(ext_ref)

    xin = jnp.where(t == 0, ctx_ref[0], x_ref[0])
    mod = mod_ref[0]
    h = _rms_mod(xin, g_ref[...], mod[:, 0:d_model], mod[:, d_model:2 * d_model]).astype(BF16)
    has_next = jnp.logical_and(t_conv >= 1, t_conv < tiles_per_sample - 1)
    cos2 = rope_ref[:, :HEAD_DIM]
    sin2 = rope_ref[:, HEAD_DIM:]
    k_scale = HEAD_DIM ** -0.5
    for part in range(4):
        lo = part * GROUP_W
        pa = _dot(h, wa_ref[:, lo:lo + GROUP_W])
        if part == 3:
            z_ref[0] = pa.astype(BF16)
        else:
            ext_ref[HALO + TILE:, lo:lo + GROUP_W] = jnp.where(has_next, pa[:HALO], 0.0)
            _conv_tile(ext_ref, cw_ref, qkv_ref, parts=(part,))
            ext_ref[0:HALO, lo:lo + GROUP_W] = jnp.where(t >= 2, ext_ref[TILE:HALO + TILE, lo:lo + GROUP_W], 0.0)
            ext_ref[HALO:HALO + TILE, lo:lo + GROUP_W] = pa
        pr = _dot(h, wr_ref[:, lo:lo + GROUP_W])
        if part < 2:
            for hd in range(N_HEADS):
                l2 = hd * HEAD_DIM
                tt = pr[:, l2:l2 + HEAD_DIM]
                rot = tt * cos2 + pltpu.roll(tt, HEAD_DIM // 2, 1) * sin2
                if part == 1:
                    rot = rot * k_scale
                pr_ref[0, :, lo + l2:lo + l2 + HEAD_DIM] = rot.astype(BF16)
        elif part == 2:
            pr_ref[0, :, lo:lo + GROUP_W] = pr.astype(BF16)
        else:
            rg_ref[0] = pr.astype(BF16)
    ab_ref[0] = _dot(h, wab_ref[...])


def _inproj(x, ctx, mod3, g, w_in, conv_w, n_half, n_gate, rope):
    b, seq, d = x.shape
    n_tiles = (ctx.shape[1] + seq) // TILE
    t_total = n_tiles * TILE
    total = b * n_tiles
    width = 3 * GROUP_W
    assert w_in.shape[1] == 2 * n_half + n_gate and n_gate <= GATE_LANES and n_half == width + GROUP_W

    def now(f):
        flat = jnp.minimum(f, total - 1)
        return flat // n_tiles, flat % n_tiles

    def lagged(f):
        flat = jnp.maximum(f - 1, 0)
        return flat // n_tiles, flat % n_tiles

    def tok(cols):
        return pl.BlockSpec((1, TILE, cols), lambda f: (now(f)[0], now(f)[1], 0))

    def lat(cols):
        return pl.BlockSpec((1, TILE, cols), lambda f: (now(f)[0], jnp.maximum(now(f)[1] - 1, 0), 0))

    return pl.pallas_call(
        functools.partial(_inproj_kernel, d_model=d, n_half=n_half, n_gate=n_gate, tiles_per_sample=n_tiles),
        grid=(total + 1,),
        in_specs=[pl.BlockSpec((1, TILE, d), lambda f: (now(f)[0], jnp.maximum(now(f)[1] - 1, 0), 0)),
                  pl.BlockSpec((1, TILE, d), lambda f: (now(f)[0], 0, 0)),
                  pl.BlockSpec((1, 1, mod3.shape[2]), lambda f: (jnp.where(now(f)[1] == 0, b, now(f)[0]), 0, 0)),
                  pl.BlockSpec((1, d), lambda f: (0, 0)),
                  pl.BlockSpec(w_in.shape, lambda f: (0, 0, 0), pipeline_mode=pl.Buffered(1)),
                  pl.BlockSpec(conv_w.shape, lambda f: (0, 0, 0)),
                  pl.BlockSpec((TILE, 2 * HEAD_DIM), lambda f: (now(f)[1], 0))],
        out_specs=[pl.BlockSpec((1, TILE, width), lambda f: (lagged(f)[0], lagged(f)[1], 0)),
                   lat(GROUP_W), tok(GATE_LANES), tok(width), lat(GROUP_W)],
        out_shape=[jax.ShapeDtypeStruct((b, t_total, width), BF16),
                   jax.ShapeDtypeStruct((b, seq, GROUP_W), BF16),
                   jax.ShapeDtypeStruct((b, t_total, GATE_LANES), F32),
                   jax.ShapeDtypeStruct((b, t_total, width), BF16),
                   jax.ShapeDtypeStruct((b, seq, GROUP_W), BF16)],
        scratch_shapes=[pltpu.VMEM((d, n_half), BF16), pltpu.VMEM((d, GATE_LANES), BF16),
                        pltpu.VMEM((d, n_half), BF16), pltpu.VMEM((TILE + 2 * HALO, width), F32)],
        compiler_params=_cparams(("arbitrary",)),
        name="inproj",
    )(x, ctx, mod3, g, w_in, conv_w, rope)


def _block_diag(y, bd_mask):
    yb = y.astype(BF16)
    return jnp.where(bd_mask, jnp.concatenate([yb] * N_HEADS, axis=0), jnp.zeros((), BF16))


def _packed_matmul(x, y_bd):
    return _dot(x.astype(BF16), y_bd)


def _unit_lower_inverse_stages(a_list, eye, base_mask, merge_masks, bd_mask, out):
    pm = _packed_matmul
    ps = [-jnp.where(base_mask, a, 0.0) for a in a_list]
    xs = [eye + p for p in ps]
    ps = [p.astype(BF16) for p in ps]
    ps = [pm(p, _block_diag(p, bd_mask)).astype(BF16) for p in ps]
    yield
    for _ in range(INV_BASE.bit_length() - 3):
        prods = [_dot(jnp.concatenate([x.astype(BF16), p], axis=0), _block_diag(p, bd_mask))
                 for x, p in zip(xs, ps)]
        xs = [x + pr[:CHUNK] for x, pr in zip(xs, prods)]
        ps = [pr[CHUNK:].astype(BF16) for pr in prods]
        yield
    xs = [x + pm(x, _block_diag(p, bd_mask)) for x, p in zip(xs, ps)]
    yield
    for m in merge_masks:
        zs = [pm(jnp.where(m, a, 0.0), _block_diag(x, bd_mask)).astype(BF16) for a, x in zip(a_list, xs)]
        yield
        xs = [x - pm(x, _block_diag(z, bd_mask)) for x, z in zip(xs, zs)]
        yield
    out.extend(xs)


def _gdn_front_stages(groups, ring_base, mid_base, a_scr, rhs_scr, brow_scr, wq_scr, qkd_scr, kdt_scr,
                      el_scr):
    lane = lax.broadcasted_iota(jnp.int32, (CHUNK, GATE_LANES), 1)
    low_half = lax.broadcasted_iota(jnp.int32, (CHUNK, HEAD_DIM), 1) < CHUNK
    zero_k = jnp.zeros((CHUNK, HEAD_DIM), BF16)
    for g in groups:
        tcum = g["consts"][0]
        g_hi, g_lo = _split(g["gates"])
        g["gc"] = _dot(tcum, g_hi) + _dot(tcum, g_lo)
    yield

    for gi, g in enumerate(groups):
        g_lane0 = N_HEADS if g["rev"] else 0
        b_lane0 = 2 * N_HEADS + g_lane0
        last = 0 if g["rev"] else CHUNK - 1
        gc, gates = g["gc"], g["gates"]
        ct = jnp.where(lane < 2 * N_HEADS, gc, gates).T
        egl_t = jnp.exp(ct[:, last:last + 1] - ct)
        gcb, kb, ms = [], [], []
        for h in range(N_HEADS):
            lo = h * HEAD_DIM
            gl = g_lane0 + h
            ring_unit = (ring_base + gi) * N_HEADS + h
            gcb.append(_lane_bcast(gc[:, gl:gl + 1], HEAD_DIM))
            beta = _lane_bcast(gates[:, b_lane0 + h:b_lane0 + h + 1], HEAD_DIM)
            eg_h = jnp.exp(gcb[h])
            q_h = g["q"][:, lo:lo + HEAD_DIM]
            k_h = g["k"][:, lo:lo + HEAD_DIM].astype(F32)
            kb.append((k_h * beta).astype(BF16))
            rhs_scr[(mid_base + gi) * N_HEADS + h] = jnp.concatenate(
                [g["v"][:, lo:lo + HEAD_DIM], (k_h * eg_h).astype(BF16)], axis=1)
            wq_scr[ring_unit, CHUNK:, :] = (q_h * eg_h).astype(BF16)
            kdt_scr[ring_unit] = (k_h.T * egl_t[gl:gl + 1, :]).astype(BF16)
        d_parts = []
        for p in range(N_HEADS // 2):
            h0, h1 = 2 * p, 2 * p + 1
            lo = h0 * HEAD_DIM
            lhs = jnp.concatenate([g["q"][:, lo:lo + 2 * HEAD_DIM], jnp.concatenate([kb[h0], kb[h1]], axis=1)],
                                  axis=0)
            k0 = g["k"][:, lo:lo + HEAD_DIM]
            k1 = g["k"][:, lo + HEAD_DIM:lo + 2 * HEAD_DIM]
            rhs_bd = jnp.concatenate([jnp.concatenate([k0, zero_k], axis=1),
                                      jnp.concatenate([zero_k, k1], axis=1)], axis=0)
            ms.append(_dot_nt(lhs, rhs_bd))
            g_row = jnp.concatenate([ct[g_lane0 + h0:g_lane0 + h0 + 1, :], ct[g_lane0 + h1:g_lane0 + h1 + 1, :]],
                                    axis=1)
            d_parts.append(jnp.where(low_half, gcb[h0], gcb[h1]) - g_row)
        incl4, strict4 = g["consts"][1:]
        e = jnp.exp(jnp.where(incl4, jnp.concatenate(d_parts, axis=1), 0.0))
        a_scr[mid_base + gi] = jnp.where(strict4, jnp.concatenate([m[CHUNK:] for m in ms], axis=1) * e, 0.0)
        qkd_scr[ring_base + gi] = jnp.where(incl4, jnp.concatenate([m[:CHUNK] for m in ms], axis=1) * e,
                                            0.0).astype(BF16)
        el_scr[ring_base + gi] = jnp.exp(gc[last:last + 1, :])
        brow_scr[mid_base + gi] = jnp.concatenate(
            [ct[b_lane0 + h:b_lane0 + h + 1, :] for h in range(N_HEADS)], axis=1)
        yield


def _gdn_solve_stages(n_groups, masks, ring_base, mid_base, a_scr, rhs_scr, brow_scr, u_scr, wq_scr):
    eye4, base_mask, merge_masks, bd_mask = masks
    t_invs = []
    yield from _unit_lower_inverse_stages([a_scr[mid_base + gi] for gi in range(n_groups)],
                                          eye4, base_mask, merge_masks, bd_mask, t_invs)
    for gi, t_inv in enumerate(t_invs):
        t_b = (t_inv * brow_scr[mid_base + gi]).astype(BF16)
        for h in range(N_HEADS):
            sol = _dot(t_b[:, h * CHUNK:(h + 1) * CHUNK], rhs_scr[(mid_base + gi) * N_HEADS + h])
            ring_unit = (ring_base + gi) * N_HEADS + h
            u_scr[ring_unit] = sol[:, :HEAD_DIM]
            wq_scr[ring_unit, :CHUNK, :] = sol[:, HEAD_DIM:].astype(BF16)


def _gdn_scan_stages(slot_base, keep_state, u_scr, wq_scr, qkd_scr, kdt_scr, el_scr, s_ref, of_ref, ob_ref):
    for j in range(CHUNKS_PER_TILE):
        units = [(slot_base + 2 * j + d, h, d * N_HEADS + h) for d in range(2) for h in range(N_HEADS)]
        states = [jnp.where(keep_state, s_ref[slot], 0.0) if j == 0 else s_ref[slot] for _, _, slot in units]
        wss = [_dot(wq_scr[gi * N_HEADS + h], s.astype(BF16)) for (gi, h, _), s in zip(units, states)]
        yield
        v_news = [(u_scr[gi * N_HEADS + h] - ws[:CHUNK]).astype(BF16) for (gi, h, _), ws in zip(units, wss)]
        outs = [ws[CHUNK:] + _dot(qkd_scr[gi][:, h * CHUNK:(h + 1) * CHUNK], vn)
                for (gi, h, _), ws, vn in zip(units, wss, v_news)]
        for (gi, h, slot), s, vn in zip(units, states, v_news):
            s_ref[slot] = el_scr[gi][:, slot:slot + 1] * s + _dot(kdt_scr[gi * N_HEADS + h], vn)
        rf = j * CHUNK
        rb = (CHUNKS_PER_TILE - 1 - j) * CHUNK
        of_ref[0, rf:rf + CHUNK, :] = jnp.concatenate(outs[:N_HEADS], axis=1).astype(BF16)
        ob_ref[0, rb:rb + CHUNK, :] = jnp.concatenate(outs[N_HEADS:], axis=1).astype(BF16)
        yield


def _interleave(*stage_generators):
    live = list(stage_generators)
    while live:
        for gen in list(live):
            try:
                next(gen)
            except StopIteration:
                live.remove(gen)


def _gate_tile(ab, dtb, a_log):
    z = ab + dtb
    softplus = jnp.maximum(z, 0.0) + jnp.log(1.0 + jnp.exp(-jnp.abs(z)))
    lane = lax.broadcasted_iota(jnp.int32, ab.shape, 1)
    return jnp.where(lane < 2 * N_HEADS, -jnp.exp(a_log) * softplus, jax.nn.sigmoid(ab))


def _gdn_kernel(qkvf_ref, abf_ref, qkvb_ref, abb_ref, dtb_ref, alog_ref, rqkvf_ref, rqkvb_ref, logit_ref,
                w1_ref, w2_ref, w3_ref, of_ref, ob_ref, rof_ref, rob_ref, w1b_ref, w2b_ref, w3b_ref,
                s_ref, a_scr, rhs_scr, brow_scr, u_scr, wq_scr, qkd_scr, kdt_scr, el_scr,
                r_ref, decay_ref, xi_ref, zeta_ref, *, tiles_per_sample):
    step = pl.program_id(0)
    lg_row = _log_sigmoid(logit_ref[...])
    for src, dst in ((w1_ref, w1b_ref), (w2_ref, w2b_ref), (w3_ref, w3b_ref)):
        dst[...] = src[0].astype(BF16)
    keep_state = lax.rem(step + tiles_per_sample - 2, tiles_per_sample) != 0
    groups_per_tile = 2 * CHUNKS_PER_TILE
    front_ring = lax.rem(step, 3) * groups_per_tile
    solve_ring = lax.rem(step + 2, 3) * groups_per_tile
    scan_ring = lax.rem(step + 1, 3) * groups_per_tile
    front_mid = lax.rem(step, 2) * groups_per_tile
    solve_mid = groups_per_tile - front_mid

    @pl.when(step == 0)
    def _():
        for scr in (s_ref, a_scr, rhs_scr, brow_scr, u_scr, wq_scr, qkd_scr, kdt_scr, el_scr, r_ref):
            scr[...] = jnp.zeros_like(scr)
        _ret_tables(lg_row, decay_ref, xi_ref, zeta_ref)

    ri = lax.broadcasted_iota(jnp.int32, (CHUNK, CHUNK), 0)
    ci = lax.broadcasted_iota(jnp.int32, (CHUNK, CHUNK), 1)
    ri4 = lax.broadcasted_iota(jnp.int32, (CHUNK, N_HEADS * CHUNK), 0)
    ci4 = lax.broadcasted_iota(jnp.int32, (CHUNK, N_HEADS * CHUNK), 1) % CHUNK
    rbd = lax.broadcasted_iota(jnp.int32, (N_HEADS * CHUNK, N_HEADS * CHUNK), 0) // CHUNK
    cbd = lax.broadcasted_iota(jnp.int32, (N_HEADS * CHUNK, N_HEADS * CHUNK), 1) // CHUNK
    bd_mask = rbd == cbd
    eye4 = (ri4 == ci4).astype(F32)
    base_mask = (ri4 // INV_BASE) == (ci4 // INV_BASE)
    merge_masks = []
    size = 2 * INV_BASE
    while size <= CHUNK:
        merge_masks.append(jnp.logical_and((ri4 // size) == (ci4 // size),
                                           (ri4 // (size // 2)) != (ci4 // (size // 2))))
        size *= 2

    cf = ((ci <= ri).astype(BF16), ri4 >= ci4, ri4 > ci4)
    cb = ((ci >= ri).astype(BF16), ri4 <= ci4, ri4 < ci4)
    dtb = dtb_ref[...]
    alog = alog_ref[...]

    groups = []
    for j in range(CHUNKS_PER_TILE):
        rf = j * CHUNK
        rb = (CHUNKS_PER_TILE - 1 - j) * CHUNK
        for ref, ab_ref, r0, rev, consts in ((qkvf_ref, abf_ref, rf, False, cf), (qkvb_ref, abb_ref, rb, True, cb)):
            groups.append(dict(q=ref[0, r0:r0 + CHUNK, 0:GROUP_W], k=ref[0, r0:r0 + CHUNK, GROUP_W:2 * GROUP_W],
                               v=ref[0, r0:r0 + CHUNK, 2 * GROUP_W:],
                               gates=_gate_tile(ab_ref[0, r0:r0 + CHUNK, :], dtb, alog), rev=rev, consts=consts))
    _interleave(
        _ret_stages(rqkvf_ref, rqkvb_ref, rof_ref, rob_ref, lg_row, keep_state, r_ref, decay_ref, xi_ref, zeta_ref),
        _gdn_solve_stages(len(groups), (eye4, base_mask, merge_masks, bd_mask), solve_ring, solve_mid,
                          a_scr, rhs_scr, brow_scr, u_scr, wq_scr),
        _gdn_scan_stages(scan_ring, keep_state, u_scr, wq_scr, qkd_scr, kdt_scr, el_scr, s_ref, of_ref, ob_ref),
        _gdn_front_stages(groups, front_ring, front_mid, a_scr, rhs_scr, brow_scr, wq_scr, qkd_scr, kdt_scr,
                          el_scr))


def _bwd_tile(s, n_tiles):
    return jnp.where(s == 0, 0, n_tiles - s)


def _latent_block(tile, bwd, n_tiles):
    return jnp.where(tile == 0, n_tiles - 2 if bwd else 0, tile - 1)


def _rider(w, n_steps):
    _, rows, cols = w.shape
    block = next(r for r in range(16, rows + 1, 16) if rows % r == 0 and rows // r <= n_steps)
    last = rows // block - 1
    return (pl.BlockSpec((1, block, cols), lambda f: (0, jnp.minimum(f, last), 0)),
            pl.BlockSpec((block, cols), lambda f: (jnp.minimum(f, last), 0)),
            jax.ShapeDtypeStruct((rows, cols), BF16))


def _mixers(qkv, ab, dtb_row, alog_row, pr, logit_row, cast_weights):
    b, t_total, _ = qkv.shape
    n_tiles = t_total // TILE

    total = b * n_tiles
    riders = [_rider(w, total + 2) for w in cast_weights]

    def block_of(f, bwd, lag, latent):
        flat = jnp.maximum(f - 2, 0) if lag else jnp.minimum(f, total - 1)
        pos = flat % n_tiles
        tile = _bwd_tile(pos, n_tiles) if bwd else pos
        return (flat // n_tiles, _latent_block(tile, bwd, n_tiles) if latent else tile, 0)

    def tok(width, bwd, lag=False, latent=False):
        return pl.BlockSpec((1, TILE, width), lambda f: block_of(f, bwd, lag, latent))

    row = pl.BlockSpec((1, GATE_LANES), lambda f: (0, 0))
    per_tile = 2 * CHUNKS_PER_TILE
    n_groups = 3 * per_tile
    n_units = n_groups * N_HEADS
    n_mid = 2 * per_tile
    return pl.pallas_call(
        functools.partial(_gdn_kernel, tiles_per_sample=n_tiles),
        grid=(total + 2,),
        in_specs=[tok(3 * GROUP_W, False), tok(GATE_LANES, False),
                  tok(3 * GROUP_W, True), tok(GATE_LANES, True), row, row,
                  tok(3 * GROUP_W, False, lag=True), tok(3 * GROUP_W, True, lag=True), row]
        + [r[0] for r in riders],
        out_specs=[tok(GROUP_W, bwd, lag=True, latent=True) for bwd in (False, True, False, True)]
        + [r[1] for r in riders],
        out_shape=[jax.ShapeDtypeStruct((b, t_total - TILE, GROUP_W), BF16)] * 4 + [r[2] for r in riders],
        scratch_shapes=[pltpu.VMEM((2 * N_HEADS, HEAD_DIM, HEAD_DIM), F32),
                        pltpu.VMEM((n_mid, CHUNK, N_HEADS * CHUNK), F32),
                        pltpu.VMEM((n_mid * N_HEADS, CHUNK, 2 * HEAD_DIM), BF16),
                        pltpu.VMEM((n_mid, 1, N_HEADS * CHUNK), F32),
                        pltpu.VMEM((n_units, CHUNK, HEAD_DIM), F32),
                        pltpu.VMEM((n_units, 2 * CHUNK, HEAD_DIM), BF16),
                        pltpu.VMEM((n_groups, CHUNK, N_HEADS * CHUNK), BF16),
                        pltpu.VMEM((n_units, HEAD_DIM, CHUNK), BF16),
                        pltpu.VMEM((n_groups, 1, GATE_LANES), F32),
                        pltpu.VMEM((2 * N_HEADS, HEAD_DIM, HEAD_DIM), F32),
                        pltpu.VMEM((2 * N_HEADS, TILE, TILE), F32),
                        pltpu.VMEM((2 * N_HEADS, TILE, HEAD_DIM), F32),
                        pltpu.VMEM((2 * N_HEADS, TILE, HEAD_DIM), F32)],
        compiler_params=_cparams(("arbitrary",)),
        name="mixers",
    )(qkv, ab, qkv, ab, dtb_row, alog_row, pr, pr, logit_row, *cast_weights)


def _log_sigmoid(x):
    return jnp.minimum(x, 0.0) - jnp.log(1.0 + jnp.exp(-jnp.abs(x)))


def _ret_tables(lg_row, decay_ref, xi_ref, zeta_ref):
    ri = lax.broadcasted_iota(jnp.int32, (TILE, TILE), 0)
    ci = lax.broadcasted_iota(jnp.int32, (TILE, TILE), 1)
    row = lax.broadcasted_iota(jnp.int32, (TILE, HEAD_DIM), 0).astype(F32)
    for slot in range(2 * N_HEADS):
        rev = slot >= N_HEADS
        lg = lg_row[:, slot:slot + 1]
        diff = ((ci - ri) if rev else (ri - ci)).astype(F32)
        pos = ((TILE - 1.0) - row) if rev else row
        decay_ref[slot] = jnp.where(diff >= 0, jnp.exp(lg * jnp.maximum(diff, 0.0)), 0.0)
        xi_ref[slot] = jnp.exp(lg * (pos + 1.0))
        zeta_ref[slot] = jnp.exp(lg * ((TILE - 1.0) - pos))


def _ret_stages(qkvf_ref, qkvb_ref, of_ref, ob_ref, lg_row, keep_state, r_ref, decay_ref, xi_ref, zeta_ref):
    units = [(refs, (N_HEADS if rev else 0) + h, h * HEAD_DIM)
             for refs, rev in (((qkvf_ref, of_ref), False), ((qkvb_ref, ob_ref), True))
             for h in range(N_HEADS)]
    qs = [refs[0][0, :, lo:lo + HEAD_DIM] for refs, _, lo in units]
    ks = [refs[0][0, :, GROUP_W + lo:GROUP_W + lo + HEAD_DIM] for refs, _, lo in units]
    vs = [refs[0][0, :, 2 * GROUP_W + lo:2 * GROUP_W + lo + HEAD_DIM] for refs, _, lo in units]
    per_yield = 2
    qks = []
    for i, (q, k) in enumerate(zip(qs, ks)):
        qks.append(_dot_nt(q, k))
        if i % per_yield == per_yield - 1:
            yield
    rs = [jnp.where(keep_state, r_ref[slot], 0.0) for _, slot, _ in units]
    for i, ((refs, slot, lo), q, k, v, qk, r) in enumerate(zip(units, qs, ks, vs, qks, rs)):
        lhs = jnp.concatenate([(qk * decay_ref[slot]).astype(BF16), (k * zeta_ref[slot]).T.astype(BF16)], axis=0)
        both = _dot(lhs, v)
        refs[1][0, :, lo:lo + HEAD_DIM] = (
            both[:TILE] + _dot((q * xi_ref[slot]).astype(BF16), r.astype(BF16))).astype(BF16)
        g_chunk = jnp.exp(lg_row[:, slot:slot + 1] * float(TILE))
        r_ref[slot] = g_chunk * r + both[TILE:]
        if i % per_yield == per_yield - 1:
            yield


def _tail_kernel(gf_ref, gb_ref, z_ref, rf_ref, rb_ref, rg_ref, x_ref, mod_ref, gdn_g_ref, ret_g_ref,
                 nffn_g_ref, w_ref, wi_ref, wo_ref, fg_ref, o_ref, y_ref, *, d_model, d_ff):
    o = gf_ref[0].astype(F32) + gb_ref[0].astype(F32)
    r = rf_ref[0].astype(F32) + rb_ref[0].astype(F32)
    z = z_ref[0].astype(F32)
    rg = rg_ref[0].astype(F32)
    for h in range(N_HEADS):
        lo = h * HEAD_DIM
        o_h = o[:, lo:lo + HEAD_DIM]
        o_n = o_h * lax.rsqrt(jnp.mean(o_h * o_h, axis=-1, keepdims=True) + NORM_EPS) * gdn_g_ref[...]
        y_ref[:, lo:lo + HEAD_DIM] = (o_n * _silu(z[:, lo:lo + HEAD_DIM])).astype(BF16)
        r_h = r[:, lo:lo + HEAD_DIM]
        mu = jnp.mean(r_h, axis=-1, keepdims=True)
        cen = r_h - mu
        var = jnp.mean(cen * cen, axis=-1, keepdims=True)
        r_n = cen * lax.rsqrt(var + NORM_EPS) * ret_g_ref[...]
        y_ref[:, GROUP_W + lo:GROUP_W + lo + HEAD_DIM] = (r_n * _silu(rg[:, lo:lo + HEAD_DIM])).astype(BF16)
    mod = mod_ref[0]
    x1 = x_ref[0] + mod[:, 2 * d_model:3 * d_model] * _dot(y_ref[...], w_ref[...])
    h2 = _rms_mod(x1, nffn_g_ref[...], mod[:, 3 * d_model:4 * d_model],
                  mod[:, 4 * d_model:5 * d_model]).astype(BF16)
    acc = None
    for lo in range(0, d_ff, FFN_CHUNK):
        hi = min(lo + FFN_CHUNK, d_ff)
        gate = _dot(h2, wi_ref[:, lo:hi])
        up = _dot(h2, wi_ref[:, d_ff + lo:d_ff + hi])
        part = _dot((_silu(gate) * up).astype(BF16), wo_ref[lo:hi, :])
        acc = part if acc is None else acc + part
    x2 = x1 + mod[:, 5 * d_model:6 * d_model] * acc
    ms = jnp.mean(x2 * x2, axis=-1, keepdims=True)
    o_ref[0] = x2 * lax.rsqrt(ms + NORM_EPS) * fg_ref[...]


def _tail(gdn_f, gdn_b, z, ret_f, ret_b, rg, x, mod3, gdn_g, ret_g, nffn_g, w_out, w_ffn_in, w_ffn_out,
          final_g):
    b, seq, d = x.shape
    d_ff = w_ffn_out.shape[0]
    tok = pl.BlockSpec((1, TAIL_TOK, GROUP_W), lambda i, t: (i, t, 0))

    def resident(shape):
        return pl.BlockSpec(shape, lambda i, t: (0,) * len(shape), pipeline_mode=pl.Buffered(1))

    return pl.pallas_call(
        functools.partial(_tail_kernel, d_model=d, d_ff=d_ff),
        grid=(b, seq // TAIL_TOK),
        in_specs=[tok, tok, tok, tok, tok, tok,
                  pl.BlockSpec((1, TAIL_TOK, d), lambda i, t: (i, t, 0)),
                  pl.BlockSpec((1, 1, mod3.shape[2]), lambda i, t: (i, 0, 0)),
                  resident((1, HEAD_DIM)), resident((1, HEAD_DIM)), resident((1, d)),
                  resident(w_out.shape), resident(w_ffn_in.shape), resident(w_ffn_out.shape),
                  resident((1, d))],
        out_specs=pl.BlockSpec((1, TAIL_TOK, d), lambda i, t: (i, t, 0)),
        out_shape=jax.ShapeDtypeStruct((b, seq, d), F32),
        scratch_shapes=[pltpu.VMEM((TAIL_TOK, 2 * GROUP_W), BF16)],
        compiler_params=_cparams(("parallel", "parallel")),
        name="tail",
    )(gdn_f, gdn_b, z, ret_f, ret_b, rg, x, mod3, gdn_g, ret_g, nffn_g, w_out, w_ffn_in, w_ffn_out, final_g)


def _rope_tables(ctx_len, n_lat):
    def angles(pos, n_pairs):
        inv = ROPE_THETA ** (-np.arange(n_pairs, dtype=np.float64) / n_pairs)
        return pos[:, None] * inv[None, :]

    rows = n_lat // GRID_W
    row = np.repeat(np.arange(rows, dtype=np.float64), GRID_W)
    col = np.tile(np.arange(GRID_W, dtype=np.float64), rows)
    zeros = np.zeros((ctx_len,), np.float64)
    p_seq = np.concatenate([np.arange(ctx_len, dtype=np.float64), np.full((n_lat,), float(ctx_len))])
    ang = np.concatenate([angles(p_seq, ROPE_PAIRS[0]),
                          angles(np.concatenate([zeros, row]), ROPE_PAIRS[1]),
                          angles(np.concatenate([zeros, col]), ROPE_PAIRS[2])], axis=-1)
    cos, sin = np.cos(ang).astype(np.float32), np.sin(ang).astype(np.float32)
    return jnp.asarray(np.concatenate([cos, cos, -sin, sin], axis=-1))


def _lane_row(values):
    flat = values.reshape(1, -1).astype(F32)
    return jnp.pad(flat, ((0, 0), (0, GATE_LANES - flat.shape[1])))


def kernel(x, c, ctx, c_ctx, ada_w, ada_b, norm_mix_g, norm_ffn_g, w_in, conv_w, gdn_a_log, gdn_dt_bias,
           gdn_norm_g, ret_decay_logit, ret_norm_g, w_out, w_ffn_in, w_ffn_out, final_g):
    assert ada_w.shape[0] == 1, "single-layer block"
    b, seq, d = x.shape
    ctx_len = ctx.shape[1]
    assert ctx_len == TILE and seq % TILE == 0 and seq % TAIL_TOK == 0 and b + 1 <= 8

    cond = jnp.concatenate([c, c_ctx[None, :], jnp.zeros((8 - b - 1, d), F32)], axis=0)
    mod = _ada(cond, ada_w, ada_b)
    mod3 = mod[:, None, :]

    n_qkvz = 4 * GROUP_W
    n_gate = 4 * N_HEADS
    qkv, z, ab, pr, rg = _inproj(x, ctx, mod3, norm_mix_g, jnp.swapaxes(w_in, 1, 2), conv_w, n_qkvz, n_gate,
                                 _rope_tables(ctx_len, seq))

    dtb_row = _lane_row(gdn_dt_bias[0])
    gdn_f, gdn_b, ret_f, ret_b, w_out_b, w_ffn_in_b, w_ffn_out_b = _mixers(
        qkv, ab, dtb_row, _lane_row(gdn_a_log[0]), pr, _lane_row(ret_decay_logit[0]),
        (w_out, w_ffn_in, w_ffn_out))

    return _tail(gdn_f, gdn_b, z, ret_f, ret_b, rg, x, mod3, gdn_norm_g, ret_norm_g, norm_ffn_g,
                 w_out_b, w_ffn_in_b, w_ffn_out_b, final_g[None, :])
```

```python
import functools

import jax
import jax.numpy as jnp
import numpy as np
from jax import lax
from jax.experimental import pallas as pl
from jax.experimental.pallas import tpu as pltpu

F32 = jnp.float32
BF16 = jnp.bfloat16

HEAD_DIM = 128
N_HEADS = 4
GROUP_W = N_HEADS * HEAD_DIM
CONV_K = 5
GRID_W = 64
ROPE_THETA = 10000.0
ROPE_PAIRS = (16, 24, 24)
NORM_EPS = 1e-6

CHUNK = 64
INV_BASE = 8
TILE = 256
CHUNKS_PER_TILE = TILE // CHUNK
HALO = 16
GATE_LANES = 128
TAIL_TOK = 512
FFN_CHUNK = 1024
VMEM_LIMIT = 56 * 1024 * 1024


def _cparams(sem):
    return pltpu.CompilerParams(dimension_semantics=sem, vmem_limit_bytes=VMEM_LIMIT)


def _dot(a, b):
    return jnp.dot(a, b, preferred_element_type=F32)


def _dot_nt(a, b):
    return lax.dot_general(a, b, (((1,), (1,)), ((), ())), preferred_element_type=F32)


def _split(a):
    hi = a.astype(BF16)
    lo = (a - hi.astype(F32)).astype(BF16)
    return hi, lo


def _silu(x):
    return x * jax.nn.sigmoid(x)


def _lane_bcast(col, width):
    return jnp.broadcast_to(col, (col.shape[0], width))


def _ada_kernel(cond_ref, w_ref, b_ref, o_ref):
    a_hi, a_lo = _split(_silu(cond_ref[...]))
    w_hi, w_lo = _split(w_ref[0])
    o_ref[...] = _dot(a_hi, w_hi) + _dot(a_lo, w_hi) + _dot(a_hi, w_lo) + b_ref[...]


def _ada(cond, w, b):
    rows, d = cond.shape
    n = w.shape[2]
    bn = 1536
    return pl.pallas_call(
        _ada_kernel,
        grid=(n // bn,),
        in_specs=[pl.BlockSpec((rows, d), lambda j: (0, 0)),
                  pl.BlockSpec((1, d, bn), lambda j: (0, 0, j)),
                  pl.BlockSpec((1, bn), lambda j: (0, j))],
        out_specs=pl.BlockSpec((rows, bn), lambda j: (0, j)),
        out_shape=jax.ShapeDtypeStruct((rows, n), F32),
        compiler_params=_cparams(("parallel",)),
        name="ada",
    )(cond, w, b)


def _rms_mod(x, g, shift, scale):
    ms = jnp.mean(x * x, axis=-1, keepdims=True)
    return (x * lax.rsqrt(ms + NORM_EPS) * g) * (1.0 + scale) + shift


def _conv_tile(ext_ref, w_ref, o_ref):
    pad = (CONV_K - 1) // 2
    rows = TILE + 2 * HALO
    q_scale = HEAD_DIM ** -0.5
    for part in range(3):
        for hd in range(N_HEADS):
            lo = part * GROUP_W + hd * HEAD_DIM
            slab = ext_ref[:, lo:lo + HEAD_DIM]
            acc = None
            for i in range(CONV_K):
                shifted = slab if i == pad else pltpu.roll(slab, (pad - i) % rows, 0)
                term = shifted[HALO:HALO + TILE] * w_ref[0, i:i + 1, lo:lo + HEAD_DIM]
                acc = term if acc is None else acc + term
            y = _silu(acc)
            if part < 2:
                inv_norm = lax.rsqrt(jnp.sum(y * y, axis=-1, keepdims=True) + NORM_EPS)
                y = y * (inv_norm * q_scale if part == 0 else inv_norm)
            o_ref[0, :, lo:lo + HEAD_DIM] = y.astype(BF16)


def _inproj_kernel(x_ref, ctx_ref, mod_ref, g_ref, w_ref, cw_ref, rope_ref,
                   qkv_ref, z_ref, ab_ref, pr_ref, rg_ref, wa_ref, wab_ref, wr_ref, ext_ref,
                   *, d_model, n_half, n_gate, tiles_per_sample):
    step = pl.program_id(0)
    t = lax.rem(jnp.minimum(step, pl.num_programs(0) - 2), tiles_per_sample)
    t_conv = lax.rem(jnp.maximum(step - 1, 0), tiles_per_sample)
    width = 3 * GROUP_W

    @pl.when(step == 0)
    def _():
        wa_ref[...] = w_ref[0, :n_half, :].T.astype(BF16)
        wr_ref[...] = w_ref[0, n_half + n_gate:, :].T.astype(BF16)
        lane = lax.broadcasted_iota(jnp.int32, (d_model, GATE_LANES), 1)
        wab_ref[...] = jnp.where(lane < n_gate, w_ref[0, n_half:n_half + GATE_LANES, :].T, 0.0).astype(BF16)
        ext_ref[...] = jnp.zeros_like(ext_ref)

    xin = jnp.where(t == 0, ctx_ref[0], x_ref[0])
    mod = mod_ref[0]
    h = _rms_mod(xin, g_ref[...], mod[:, 0:d_model], mod[:, d_model:2 * d_model]).astype(BF16)
    pa = _dot(h, wa_ref[...])
    z_ref[0] = pa[:, width:].astype(BF16)

    has_next = jnp.logical_and(t_conv >= 1, t_conv < tiles_per_sample - 1)
    ext_ref[HALO + TILE:, :] = jnp.where(has_next, pa[:HALO, :width], 0.0)
    _conv_tile(ext_ref, cw_ref, qkv_ref)
    ext_ref[0:HALO, :] = jnp.where(t >= 2, ext_ref[TILE:HALO + TILE, :], 0.0)
    ext_ref[HALO:HALO + TILE, :] = pa[:, :width]

    ab_ref[0] = _dot(h, wab_ref[...])
    pr = _dot(h, wr_ref[...])
    cos2 = rope_ref[:, :HEAD_DIM]
    sin2 = rope_ref[:, HEAD_DIM:]
    k_scale = HEAD_DIM ** -0.5
    for hd in range(2 * N_HEADS):
        lo = hd * HEAD_DIM
        tt = pr[:, lo:lo + HEAD_DIM]
        rot = tt * cos2 + pltpu.roll(tt, HEAD_DIM // 2, 1) * sin2
        if hd >= N_HEADS:
            rot = rot * k_scale
        pr_ref[0, :, lo:lo + HEAD_DIM] = rot.astype(BF16)
    pr_ref[0, :, 2 * GROUP_W:] = pr[:, 2 * GROUP_W:width].astype(BF16)
    rg_ref[0] = pr[:, width:].astype(BF16)


def _inproj(x, ctx, mod3, g, w_in, conv_w, n_half, n_gate, rope):
    b, seq, d = x.shape
    n_tiles = (ctx.shape[1] + seq) // TILE
    t_total = n_tiles * TILE
    total = b * n_tiles
    width = 3 * GROUP_W
    assert w_in.shape[1] == 2 * n_half + n_gate and n_gate <= GATE_LANES and n_half == width + GROUP_W

    def now(f):
        flat = jnp.minimum(f, total - 1)
        return flat // n_tiles, flat % n_tiles

    def lagged(f):
        flat = jnp.maximum(f - 1, 0)
        return flat // n_tiles, flat % n_tiles

    def tok(cols):
        return pl.BlockSpec((1, TILE, cols), lambda f: (now(f)[0], now(f)[1], 0))

    def lat(cols):
        return pl.BlockSpec((1, TILE, cols), lambda f: (now(f)[0], jnp.maximum(now(f)[1] - 1, 0), 0))

    return pl.pallas_call(
        functools.partial(_inproj_kernel, d_model=d, n_half=n_half, n_gate=n_gate, tiles_per_sample=n_tiles),
        grid=(total + 1,),
        in_specs=[pl.BlockSpec((1, TILE, d), lambda f: (now(f)[0], jnp.maximum(now(f)[1] - 1, 0), 0)),
                  pl.BlockSpec((1, TILE, d), lambda f: (now(f)[0], 0, 0)),
                  pl.BlockSpec((1, 1, mod3.shape[2]), lambda f: (jnp.where(now(f)[1] == 0, b, now(f)[0]), 0, 0)),
                  pl.BlockSpec((1, d), lambda f: (0, 0)),
                  pl.BlockSpec(w_in.shape, lambda f: (0, 0, 0), pipeline_mode=pl.Buffered(1)),
                  pl.BlockSpec(conv_w.shape, lambda f: (0, 0, 0)),
                  pl.BlockSpec((TILE, 2 * HEAD_DIM), lambda f: (now(f)[1], 0))],
        out_specs=[pl.BlockSpec((1, TILE, width), lambda f: (lagged(f)[0], lagged(f)[1], 0)),
                   lat(GROUP_W), tok(GATE_LANES), tok(width), lat(GROUP_W)],
        out_shape=[jax.ShapeDtypeStruct((b, t_total, width), BF16),
                   jax.ShapeDtypeStruct((b, seq, GROUP_W), BF16),
                   jax.ShapeDtypeStruct((b, t_total, GATE_LANES), F32),
                   jax.ShapeDtypeStruct((b, t_total, width), BF16),
                   jax.ShapeDtypeStruct((b, seq, GROUP_W), BF16)],
        scratch_shapes=[pltpu.VMEM((d, n_half), BF16), pltpu.VMEM((d, GATE_LANES), BF16),
                        pltpu.VMEM((d, n_half), BF16), pltpu.VMEM((TILE + 2 * HALO, width), F32)],
        compiler_params=_cparams(("arbitrary",)),
        name="inproj",
    )(x, ctx, mod3, g, w_in, conv_w, rope)


def _block_diag(y, bd_mask):
    yb = y.astype(BF16)
    return jnp.where(bd_mask, jnp.concatenate([yb] * N_HEADS, axis=0), jnp.zeros((), BF16))


def _packed_matmul(x, y_bd):
    return _dot(x.astype(BF16), y_bd)


def _unit_lower_inverse_stages(a_list, eye, base_mask, merge_masks, bd_mask, out):
    pm = _packed_matmul
    ps = [-jnp.where(base_mask, a, 0.0) for a in a_list]
    xs = [eye + p for p in ps]
    ps = [p.astype(BF16) for p in ps]
    ps = [pm(p, _block_diag(p, bd_mask)).astype(BF16) for p in ps]
    yield
    for _ in range(INV_BASE.bit_length() - 3):
        prods = [_dot(jnp.concatenate([x.astype(BF16), p], axis=0), _block_diag(p, bd_mask))
                 for x, p in zip(xs, ps)]
        xs = [x + pr[:CHUNK] for x, pr in zip(xs, prods)]
        ps = [pr[CHUNK:].astype(BF16) for pr in prods]
        yield
    xs = [x + pm(x, _block_diag(p, bd_mask)) for x, p in zip(xs, ps)]
    yield
    for m in merge_masks:
        zs = [pm(jnp.where(m, a, 0.0), _block_diag(x, bd_mask)).astype(BF16) for a, x in zip(a_list, xs)]
        yield
        xs = [x - pm(x, _block_diag(z, bd_mask)) for x, z in zip(xs, zs)]
        yield
    out.extend(xs)


def _gdn_front_stages(groups, ring_base, mid_base, a_scr, rhs_scr, brow_scr, wq_scr, qkd_scr, kdt_scr,
                      el_scr):
    lane = lax.broadcasted_iota(jnp.int32, (CHUNK, GATE_LANES), 1)
    low_half = lax.broadcasted_iota(jnp.int32, (CHUNK, HEAD_DIM), 1) < CHUNK
    zero_k = jnp.zeros((CHUNK, HEAD_DIM), BF16)
    for g in groups:
        tcum = g["consts"][0]
        g_hi, g_lo = _split(g["gates"])
        g["gc"] = _dot(tcum, g_hi) + _dot(tcum, g_lo)
    yield

    for gi, g in enumerate(groups):
        g_lane0 = N_HEADS if g["rev"] else 0
        b_lane0 = 2 * N_HEADS + g_lane0
        last = 0 if g["rev"] else CHUNK - 1
        gc, gates = g["gc"], g["gates"]
        ct = jnp.where(lane < 2 * N_HEADS, gc, gates).T
        egl_t = jnp.exp(ct[:, last:last + 1] - ct)
        gcb, kb, ms = [], [], []
        for h in range(N_HEADS):
            lo = h * HEAD_DIM
            gl = g_lane0 + h
            ring_unit = (ring_base + gi) * N_HEADS + h
            gcb.append(_lane_bcast(gc[:, gl:gl + 1], HEAD_DIM))
            beta = _lane_bcast(gates[:, b_lane0 + h:b_lane0 + h + 1], HEAD_DIM)
            eg_h = jnp.exp(gcb[h])
            q_h = g["q"][:, lo:lo + HEAD_DIM]
            k_h = g["k"][:, lo:lo + HEAD_DIM].astype(F32)
            kb.append((k_h * beta).astype(BF16))
            rhs_scr[(mid_base + gi) * N_HEADS + h] = jnp.concatenate(
                [g["v"][:, lo:lo + HEAD_DIM], (k_h * eg_h).astype(BF16)], axis=1)
            wq_scr[ring_unit, CHUNK:, :] = (q_h * eg_h).astype(BF16)
            kdt_scr[ring_unit] = (k_h.T * egl_t[gl:gl + 1, :]).astype(BF16)
        d_parts = []
        for p in range(N_HEADS // 2):
            h0, h1 = 2 * p, 2 * p + 1
            lo = h0 * HEAD_DIM
            lhs = jnp.concatenate([g["q"][:, lo:lo + 2 * HEAD_DIM], jnp.concatenate([kb[h0], kb[h1]], axis=1)],
                                  axis=0)
            k0 = g["k"][:, lo:lo + HEAD_DIM]
            k1 = g["k"][:, lo + HEAD_DIM:lo + 2 * HEAD_DIM]
            rhs_bd = jnp.concatenate([jnp.concatenate([k0, zero_k], axis=1),
                                      jnp.concatenate([zero_k, k1], axis=1)], axis=0)
            ms.append(_dot_nt(lhs, rhs_bd))
            g_row = jnp.concatenate([ct[g_lane0 + h0:g_lane0 + h0 + 1, :], ct[g_lane0 + h1:g_lane0 + h1 + 1, :]],
                                    axis=1)
            d_parts.append(jnp.where(low_half, gcb[h0], gcb[h1]) - g_row)
        incl4, strict4 = g["consts"][1:]
        e = jnp.exp(jnp.where(incl4, jnp.concatenate(d_parts, axis=1), 0.0))
        a_scr[mid_base + gi] = jnp.where(strict4, jnp.concatenate([m[CHUNK:] for m in ms], axis=1) * e, 0.0)
        qkd_scr[ring_base + gi] = jnp.where(incl4, jnp.concatenate([m[:CHUNK] for m in ms], axis=1) * e,
                                            0.0).astype(BF16)
        el_scr[ring_base + gi] = jnp.exp(gc[last:last + 1, :])
        brow_scr[mid_base + gi] = jnp.concatenate(
            [ct[b_lane0 + h:b_lane0 + h + 1, :] for h in range(N_HEADS)], axis=1)
        yield


def _gdn_solve_stages(n_groups, masks, ring_base, mid_base, a_scr, rhs_scr, brow_scr, u_scr, wq_scr):
    eye4, base_mask, merge_masks, bd_mask = masks
    t_invs = []
    yield from _unit_lower_inverse_stages([a_scr[mid_base + gi] for gi in range(n_groups)],
                                          eye4, base_mask, merge_masks, bd_mask, t_invs)
    for gi, t_inv in enumerate(t_invs):
        t_b = (t_inv * brow_scr[mid_base + gi]).astype(BF16)
        for h in range(N_HEADS):
            sol = _dot(t_b[:, h * CHUNK:(h + 1) * CHUNK], rhs_scr[(mid_base + gi) * N_HEADS + h])
            ring_unit = (ring_base + gi) * N_HEADS + h
            u_scr[ring_unit] = sol[:, :HEAD_DIM]
            wq_scr[ring_unit, :CHUNK, :] = sol[:, HEAD_DIM:].astype(BF16)


def _gdn_scan_stages(slot_base, keep_state, u_scr, wq_scr, qkd_scr, kdt_scr, el_scr, s_ref, of_ref, ob_ref):
    for j in range(CHUNKS_PER_TILE):
        units = [(slot_base + 2 * j + d, h, d * N_HEADS + h) for d in range(2) for h in range(N_HEADS)]
        states = [jnp.where(keep_state, s_ref[slot], 0.0) if j == 0 else s_ref[slot] for _, _, slot in units]
        wss = [_dot(wq_scr[gi * N_HEADS + h], s.astype(BF16)) for (gi, h, _), s in zip(units, states)]
        yield
        v_news = [(u_scr[gi * N_HEADS + h] - ws[:CHUNK]).astype(BF16) for (gi, h, _), ws in zip(units, wss)]
        outs = [ws[CHUNK:] + _dot(qkd_scr[gi][:, h * CHUNK:(h + 1) * CHUNK], vn)
                for (gi, h, _), ws, vn in zip(units, wss, v_news)]
        for (gi, h, slot), s, vn in zip(units, states, v_news):
            s_ref[slot] = el_scr[gi][:, slot:slot + 1] * s + _dot(kdt_scr[gi * N_HEADS + h], vn)
        rf = j * CHUNK
        rb = (CHUNKS_PER_TILE - 1 - j) * CHUNK
        of_ref[0, rf:rf + CHUNK, :] = jnp.concatenate(outs[:N_HEADS], axis=1).astype(BF16)
        ob_ref[0, rb:rb + CHUNK, :] = jnp.concatenate(outs[N_HEADS:], axis=1).astype(BF16)
        yield


def _interleave(*stage_generators):
    live = list(stage_generators)
    while live:
        for gen in list(live):
            try:
                next(gen)
            except StopIteration:
                live.remove(gen)


def _gate_tile(ab, dtb, a_log):
    z = ab + dtb
    softplus = jnp.maximum(z, 0.0) + jnp.log(1.0 + jnp.exp(-jnp.abs(z)))
    lane = lax.broadcasted_iota(jnp.int32, ab.shape, 1)
    return jnp.where(lane < 2 * N_HEADS, -jnp.exp(a_log) * softplus, jax.nn.sigmoid(ab))


def _gdn_kernel(qkvf_ref, abf_ref, qkvb_ref, abb_ref, dtb_ref, alog_ref, rqkvf_ref, rqkvb_ref, logit_ref,
                w1_ref, w2_ref, w3_ref, of_ref, ob_ref, rof_ref, rob_ref, w1b_ref, w2b_ref, w3b_ref,
                s_ref, a_scr, rhs_scr, brow_scr, u_scr, wq_scr, qkd_scr, kdt_scr, el_scr,
                r_ref, decay_ref, xi_ref, zeta_ref, *, tiles_per_sample):
    step = pl.program_id(0)
    lg_row = _log_sigmoid(logit_ref[...])
    for src, dst in ((w1_ref, w1b_ref), (w2_ref, w2b_ref), (w3_ref, w3b_ref)):
        dst[...] = src[0].astype(BF16)
    keep_state = lax.rem(step + tiles_per_sample - 2, tiles_per_sample) != 0
    groups_per_tile = 2 * CHUNKS_PER_TILE
    front_ring = lax.rem(step, 3) * groups_per_tile
    solve_ring = lax.rem(step + 2, 3) * groups_per_tile
    scan_ring = lax.rem(step + 1, 3) * groups_per_tile
    front_mid = lax.rem(step, 2) * groups_per_tile
    solve_mid = groups_per_tile - front_mid

    @pl.when(step == 0)
    def _():
        for scr in (s_ref, a_scr, rhs_scr, brow_scr, u_scr, wq_scr, qkd_scr, kdt_scr, el_scr, r_ref):
            scr[...] = jnp.zeros_like(scr)
        _ret_tables(lg_row, decay_ref, xi_ref, zeta_ref)

    ri = lax.broadcasted_iota(jnp.int32, (CHUNK, CHUNK), 0)
    ci = lax.broadcasted_iota(jnp.int32, (CHUNK, CHUNK), 1)
    ri4 = lax.broadcasted_iota(jnp.int32, (CHUNK, N_HEADS * CHUNK), 0)
    ci4 = lax.broadcasted_iota(jnp.int32, (CHUNK, N_HEADS * CHUNK), 1) % CHUNK
    rbd = lax.broadcasted_iota(jnp.int32, (N_HEADS * CHUNK, N_HEADS * CHUNK), 0) // CHUNK
    cbd = lax.broadcasted_iota(jnp.int32, (N_HEADS * CHUNK, N_HEADS * CHUNK), 1) // CHUNK
    bd_mask = rbd == cbd
    eye4 = (ri4 == ci4).astype(F32)
    base_mask = (ri4 // INV_BASE) == (ci4 // INV_BASE)
    merge_masks = []
    size = 2 * INV_BASE
    while size <= CHUNK:
        merge_masks.append(jnp.logical_and((ri4 // size) == (ci4 // size),
                                           (ri4 // (size // 2)) != (ci4 // (size // 2))))
        size *= 2

    cf = ((ci <= ri).astype(BF16), ri4 >= ci4, ri4 > ci4)
    cb = ((ci >= ri).astype(BF16), ri4 <= ci4, ri4 < ci4)
    dtb = dtb_ref[...]
    alog = alog_ref[...]

    groups = []
    for j in range(CHUNKS_PER_TILE):
        rf = j * CHUNK
        rb = (CHUNKS_PER_TILE - 1 - j) * CHUNK
        for ref, ab_ref, r0, rev, consts in ((qkvf_ref, abf_ref, rf, False, cf), (qkvb_ref, abb_ref, rb, True, cb)):
            groups.append(dict(q=ref[0, r0:r0 + CHUNK, 0:GROUP_W], k=ref[0, r0:r0 + CHUNK, GROUP_W:2 * GROUP_W],
                               v=ref[0, r0:r0 + CHUNK, 2 * GROUP_W:],
                               gates=_gate_tile(ab_ref[0, r0:r0 + CHUNK, :], dtb, alog), rev=rev, consts=consts))
    _interleave(
        _ret_stages(rqkvf_ref, rqkvb_ref, rof_ref, rob_ref, lg_row, keep_state, r_ref, decay_ref, xi_ref, zeta_ref),
        _gdn_solve_stages(len(groups), (eye4, base_mask, merge_masks, bd_mask), solve_ring, solve_mid,
                          a_scr, rhs_scr, brow_scr, u_scr, wq_scr),
        _gdn_scan_stages(scan_ring, keep_state, u_scr, wq_scr, qkd_scr, kdt_scr, el_scr, s_ref, of_ref, ob_ref),
        _gdn_front_stages(groups, front_ring, front_mid, a_scr, rhs_scr, brow_scr, wq_scr, qkd_scr, kdt_scr,
                          el_scr))


def _bwd_tile(s, n_tiles):
    return jnp.where(s == 0, 0, n_tiles - s)


def _latent_block(tile, bwd, n_tiles):
    return jnp.where(tile == 0, n_tiles - 2 if bwd else 0, tile - 1)


def _rider(w, n_steps):
    _, rows, cols = w.shape
    block = next(r for r in range(16, rows + 1, 16) if rows % r == 0 and rows // r <= n_steps)
    last = rows // block - 1
    return (pl.BlockSpec((1, block, cols), lambda f: (0, jnp.minimum(f, last), 0)),
            pl.BlockSpec((block, cols), lambda f: (jnp.minimum(f, last), 0)),
            jax.ShapeDtypeStruct((rows, cols), BF16))


def _mixers(qkv, ab, dtb_row, alog_row, pr, logit_row, cast_weights):
    b, t_total, _ = qkv.shape
    n_tiles = t_total // TILE

    total = b * n_tiles
    riders = [_rider(w, total + 2) for w in cast_weights]

    def block_of(f, bwd, lag, latent):
        flat = jnp.maximum(f - 2, 0) if lag else jnp.minimum(f, total - 1)
        pos = flat % n_tiles
        tile = _bwd_tile(pos, n_tiles) if bwd else pos
        return (flat // n_tiles, _latent_block(tile, bwd, n_tiles) if latent else tile, 0)

    def tok(width, bwd, lag=False, latent=False):
        return pl.BlockSpec((1, TILE, width), lambda f: block_of(f, bwd, lag, latent))

    row = pl.BlockSpec((1, GATE_LANES), lambda f: (0, 0))
    per_tile = 2 * CHUNKS_PER_TILE
    n_groups = 3 * per_tile
    n_units = n_groups * N_HEADS
    n_mid = 2 * per_tile
    return pl.pallas_call(
        functools.partial(_gdn_kernel, tiles_per_sample=n_tiles),
        grid=(total + 2,),
        in_specs=[tok(3 * GROUP_W, False), tok(GATE_LANES, False),
                  tok(3 * GROUP_W, True), tok(GATE_LANES, True), row, row,
                  tok(3 * GROUP_W, False, lag=True), tok(3 * GROUP_W, True, lag=True), row]
        + [r[0] for r in riders],
        out_specs=[tok(GROUP_W, bwd, lag=True, latent=True) for bwd in (False, True, False, True)]
        + [r[1] for r in riders],
        out_shape=[jax.ShapeDtypeStruct((b, t_total - TILE, GROUP_W), BF16)] * 4 + [r[2] for r in riders],
        scratch_shapes=[pltpu.VMEM((2 * N_HEADS, HEAD_DIM, HEAD_DIM), F32),
                        pltpu.VMEM((n_mid, CHUNK, N_HEADS * CHUNK), F32),
                        pltpu.VMEM((n_mid * N_HEADS, CHUNK, 2 * HEAD_DIM), BF16),
                        pltpu.VMEM((n_mid, 1, N_HEADS * CHUNK), F32),
                        pltpu.VMEM((n_units, CHUNK, HEAD_DIM), F32),
                        pltpu.VMEM((n_units, 2 * CHUNK, HEAD_DIM), BF16),
                        pltpu.VMEM((n_groups, CHUNK, N_HEADS * CHUNK), BF16),
                        pltpu.VMEM((n_units, HEAD_DIM, CHUNK), BF16),
                        pltpu.VMEM((n_groups, 1, GATE_LANES), F32),
                        pltpu.VMEM((2 * N_HEADS, HEAD_DIM, HEAD_DIM), F32),
                        pltpu.VMEM((2 * N_HEADS, TILE, TILE), F32),
                        pltpu.VMEM((2 * N_HEADS, TILE, HEAD_DIM), F32),
                        pltpu.VMEM((2 * N_HEADS, TILE, HEAD_DIM), F32)],
        compiler_params=_cparams(("arbitrary",)),
        name="mixers",
    )(qkv, ab, qkv, ab, dtb_row, alog_row, pr, pr, logit_row, *cast_weights)


def _log_sigmoid(x):
    return jnp.minimum(x, 0.0) - jnp.log(1.0 + jnp.exp(-jnp.abs(x)))


def _ret_tables(lg_row, decay_ref, xi_ref, zeta_ref):
    ri = lax.broadcasted_iota(jnp.int32, (TILE, TILE), 0)
    ci = lax.broadcasted_iota(jnp.int32, (TILE, TILE), 1)
    row = lax.broadcasted_iota(jnp.int32, (TILE, HEAD_DIM), 0).astype(F32)
    for slot in range(2 * N_HEADS):
        rev = slot >= N_HEADS
        lg = lg_row[:, slot:slot + 1]
        diff = ((ci - ri) if rev else (ri - ci)).astype(F32)
        pos = ((TILE - 1.0) - row) if rev else row
        decay_ref[slot] = jnp.where(diff >= 0, jnp.exp(lg * jnp.maximum(diff, 0.0)), 0.0)
        xi_ref[slot] = jnp.exp(lg * (pos + 1.0))
        zeta_ref[slot] = jnp.exp(lg * ((TILE - 1.0) - pos))


def _ret_stages(qkvf_ref, qkvb_ref, of_ref, ob_ref, lg_row, keep_state, r_ref, decay_ref, xi_ref, zeta_ref):
    units = [(refs, (N_HEADS if rev else 0) + h, h * HEAD_DIM)
             for refs, rev in (((qkvf_ref, of_ref), False), ((qkvb_ref, ob_ref), True))
             for h in range(N_HEADS)]
    qs = [refs[0][0, :, lo:lo + HEAD_DIM] for refs, _, lo in units]
    ks = [refs[0][0, :, GROUP_W + lo:GROUP_W + lo + HEAD_DIM] for refs, _, lo in units]
    vs = [refs[0][0, :, 2 * GROUP_W + lo:2 * GROUP_W + lo + HEAD_DIM] for refs, _, lo in units]
    per_yield = 2
    qks = []
    for i, (q, k) in enumerate(zip(qs, ks)):
        qks.append(_dot_nt(q, k))
        if i % per_yield == per_yield - 1:
            yield
    rs = [jnp.where(keep_state, r_ref[slot], 0.0) for _, slot, _ in units]
    for i, ((refs, slot, lo), q, k, v, qk, r) in enumerate(zip(units, qs, ks, vs, qks, rs)):
        lhs = jnp.concatenate([(qk * decay_ref[slot]).astype(BF16), (k * zeta_ref[slot]).T.astype(BF16)], axis=0)
        both = _dot(lhs, v)
        refs[1][0, :, lo:lo + HEAD_DIM] = (
            both[:TILE] + _dot((q * xi_ref[slot]).astype(BF16), r.astype(BF16))).astype(BF16)
        g_chunk = jnp.exp(lg_row[:, slot:slot + 1] * float(TILE))
        r_ref[slot] = g_chunk * r + both[TILE:]
        if i % per_yield == per_yield - 1:
            yield


def _tail_kernel(gf_ref, gb_ref, z_ref, rf_ref, rb_ref, rg_ref, x_ref, mod_ref, gdn_g_ref, ret_g_ref,
                 nffn_g_ref, w_ref, wi_ref, wo_ref, fg_ref, o_ref, y_ref, *, d_model, d_ff):
    o = gf_ref[0].astype(F32) + gb_ref[0].astype(F32)
    r = rf_ref[0].astype(F32) + rb_ref[0].astype(F32)
    z = z_ref[0].astype(F32)
    rg = rg_ref[0].astype(F32)
    for h in range(N_HEADS):
        lo = h * HEAD_DIM
        o_h = o[:, lo:lo + HEAD_DIM]
        o_n = o_h * lax.rsqrt(jnp.mean(o_h * o_h, axis=-1, keepdims=True) + NORM_EPS) * gdn_g_ref[...]
        y_ref[:, lo:lo + HEAD_DIM] = (o_n * _silu(z[:, lo:lo + HEAD_DIM])).astype(BF16)
        r_h = r[:, lo:lo + HEAD_DIM]
        mu = jnp.mean(r_h, axis=-1, keepdims=True)
        cen = r_h - mu
        var = jnp.mean(cen * cen, axis=-1, keepdims=True)
        r_n = cen * lax.rsqrt(var + NORM_EPS) * ret_g_ref[...]
        y_ref[:, GROUP_W + lo:GROUP_W + lo + HEAD_DIM] = (r_n * _silu(rg[:, lo:lo + HEAD_DIM])).astype(BF16)
    mod = mod_ref[0]
    x1 = x_ref[0] + mod[:, 2 * d_model:3 * d_model] * _dot(y_ref[...], w_ref[...])
    h2 = _rms_mod(x1, nffn_g_ref[...], mod[:, 3 * d_model:4 * d_model],
                  mod[:, 4 * d_model:5 * d_model]).astype(BF16)
    starts = list(range(0, d_ff, FFN_CHUNK))

    def gate_up(lo):
        hi = min(lo + FFN_CHUNK, d_ff)
        return _dot(h2, wi_ref[:, lo:hi]), _dot(h2, wi_ref[:, d_ff + lo:d_ff + hi])

    pending = [gate_up(starts[0])]
    acc = None
    for c, lo in enumerate(starts):
        if c + 1 < len(starts):
            pending.append(gate_up(starts[c + 1]))
        gate, up = pending.pop(0)
        hi = min(lo + FFN_CHUNK, d_ff)
        part = _dot((_silu(gate) * up).astype(BF16), wo_ref[lo:hi, :])
        acc = part if acc is None else acc + part
    x2 = x1 + mod[:, 5 * d_model:6 * d_model] * acc
    ms = jnp.mean(x2 * x2, axis=-1, keepdims=True)
    o_ref[0] = x2 * lax.rsqrt(ms + NORM_EPS) * fg_ref[...]


def _tail(gdn_f, gdn_b, z, ret_f, ret_b, rg, x, mod3, gdn_g, ret_g, nffn_g, w_out, w_ffn_in, w_ffn_out,
          final_g):
    b, seq, d = x.shape
    d_ff = w_ffn_out.shape[0]
    tok = pl.BlockSpec((1, TAIL_TOK, GROUP_W), lambda i, t: (i, t, 0))

    def resident(shape):
        return pl.BlockSpec(shape, lambda i, t: (0,) * len(shape), pipeline_mode=pl.Buffered(1))

    return pl.pallas_call(
        functools.partial(_tail_kernel, d_model=d, d_ff=d_ff),
        grid=(b, seq // TAIL_TOK),
        in_specs=[tok, tok, tok, tok, tok, tok,
                  pl.BlockSpec((1, TAIL_TOK, d), lambda i, t: (i, t, 0)),
                  pl.BlockSpec((1, 1, mod3.shape[2]), lambda i, t: (i, 0, 0)),
                  resident((1, HEAD_DIM)), resident((1, HEAD_DIM)), resident((1, d)),
                  resident(w_out.shape), resident(w_ffn_in.shape), resident(w_ffn_out.shape),
                  resident((1, d))],
        out_specs=pl.BlockSpec((1, TAIL_TOK, d), lambda i, t: (i, t, 0)),
        out_shape=jax.ShapeDtypeStruct((b, seq, d), F32),
        scratch_shapes=[pltpu.VMEM((TAIL_TOK, 2 * GROUP_W), BF16)],
        compiler_params=_cparams(("parallel", "parallel")),
        name="tail",
    )(gdn_f, gdn_b, z, ret_f, ret_b, rg, x, mod3, gdn_g, ret_g, nffn_g, w_out, w_ffn_in, w_ffn_out, final_g)


def _rope_tables(ctx_len, n_lat):
    def angles(pos, n_pairs):
        inv = ROPE_THETA ** (-np.arange(n_pairs, dtype=np.float64) / n_pairs)
        return pos[:, None] * inv[None, :]

    rows = n_lat // GRID_W
    row = np.repeat(np.arange(rows, dtype=np.float64), GRID_W)
    col = np.tile(np.arange(GRID_W, dtype=np.float64), rows)
    zeros = np.zeros((ctx_len,), np.float64)
    p_seq = np.concatenate([np.arange(ctx_len, dtype=np.float64), np.full((n_lat,), float(ctx_len))])
    ang = np.concatenate([angles(p_seq, ROPE_PAIRS[0]),
                          angles(np.concatenate([zeros, row]), ROPE_PAIRS[1]),
                          angles(np.concatenate([zeros, col]), ROPE_PAIRS[2])], axis=-1)
    cos, sin = np.cos(ang).astype(np.float32), np.sin(ang).astype(np.float32)
    return jnp.asarray(np.concatenate([cos, cos, -sin, sin], axis=-1))


def _lane_row(values):
    flat = values.reshape(1, -1).astype(F32)
    return jnp.pad(flat, ((0, 0), (0, GATE_LANES - flat.shape[1])))


def kernel(x, c, ctx, c_ctx, ada_w, ada_b, norm_mix_g, norm_ffn_g, w_in, conv_w, gdn_a_log, gdn_dt_bias,
           gdn_norm_g, ret_decay_logit, ret_norm_g, w_out, w_ffn_in, w_ffn_out, final_g):
    assert ada_w.shape[0] == 1, "single-layer block"
    b, seq, d = x.shape
    ctx_len = ctx.shape[1]
    assert ctx_len == TILE and seq % TILE == 0 and seq % TAIL_TOK == 0 and b + 1 <= 8

    cond = jnp.concatenate([c, c_ctx[None, :], jnp.zeros((8 - b - 1, d), F32)], axis=0)
    mod = _ada(cond, ada_w, ada_b)
    mod3 = mod[:, None, :]

    n_qkvz = 4 * GROUP_W
    n_gate = 4 * N_HEADS
    qkv, z, ab, pr, rg = _inproj(x, ctx, mod3, norm_mix_g, jnp.swapaxes(w_in, 1, 2), conv_w, n_qkvz, n_gate,
                                 _rope_tables(ctx_len, seq))

    dtb_row = _lane_row(gdn_dt_bias[0])
    gdn_f, gdn_b, ret_f, ret_b, w_out_b, w_ffn_in_b, w_ffn_out_b = _mixers(
        qkv, ab, dtb_row, _lane_row(gdn_a_log[0]), pr, _lane_row(ret_decay_logit[0]),
        (w_out, w_ffn_in, w_ffn_out))

    return _tail(gdn_f, gdn_b, z, ret_f, ret_b, rg, x, mod3, gdn_norm_g, ret_norm_g, norm_ffn_g,
                 w_out_b, w_ffn_in_b, w_ffn_out_b, final_g[None, :])
```

```python
import functools

import jax
import jax.numpy as jnp
import numpy as np
from jax import lax
from jax.experimental import pallas as pl
from jax.experimental.pallas import tpu as pltpu

F32 = jnp.float32
BF16 = jnp.bfloat16

HEAD_DIM = 128
N_HEADS = 4
GROUP_W = N_HEADS * HEAD_DIM
CONV_K = 5
GRID_W = 64
ROPE_THETA = 10000.0
ROPE_PAIRS = (16, 24, 24)
NORM_EPS = 1e-6

CHUNK = 64
INV_BASE = 8
TILE = 256
CHUNKS_PER_TILE = TILE // CHUNK
HALO = 16
GATE_LANES = 128
TAIL_TOK = 512
FFN_CHUNK = 1024
VMEM_LIMIT = 56 * 1024 * 1024


def _cparams(sem):
    return pltpu.CompilerParams(dimension_semantics=sem, vmem_limit_bytes=VMEM_LIMIT)


def _dot(a, b):
    return jnp.dot(a, b, preferred_element_type=F32)


def _dot_nt(a, b):
    return lax.dot_general(a, b, (((1,), (1,)), ((), ())), preferred_element_type=F32)


def _split(a):
    hi = a.astype(BF16)
    lo = (a - hi.astype(F32)).astype(BF16)
    return hi, lo


def _silu(x):
    return x * jax.nn.sigmoid(x)


def _lane_bcast(col, width):
    return jnp.broadcast_to(col, (col.shape[0], width))


def _ada_kernel(cond_ref, w_ref, b_ref, o_ref):
    a_hi, a_lo = _split(_silu(cond_ref[...]))
    w_hi, w_lo = _split(w_ref[0])
    o_ref[...] = _dot(a_hi, w_hi) + _dot(a_lo, w_hi) + _dot(a_hi, w_lo) + b_ref[...]


def _ada(cond, w, b):
    rows, d = cond.shape
    n = w.shape[2]
    bn = 1536
    return pl.pallas_call(
        _ada_kernel,
        grid=(n // bn,),
        in_specs=[pl.BlockSpec((rows, d), lambda j: (0, 0)),
                  pl.BlockSpec((1, d, bn), lambda j: (0, 0, j)),
                  pl.BlockSpec((1, bn), lambda j: (0, j))],
        out_specs=pl.BlockSpec((rows, bn), lambda j: (0, j)),
        out_shape=jax.ShapeDtypeStruct((rows, n), F32),
        compiler_params=pltpu.CompilerParams(dimension_semantics=("parallel",), vmem_limit_bytes=VMEM_LIMIT,
                                             allow_input_fusion=[True, False, False]),
        name="ada",
    )(cond, w, b)


def _rms_mod(x, g, shift, scale):
    ms = jnp.mean(x * x, axis=-1, keepdims=True)
    return (x * lax.rsqrt(ms + NORM_EPS) * g) * (1.0 + scale) + shift


def _conv_tile(ext_ref, w_ref, o_ref):
    pad = (CONV_K - 1) // 2
    rows = TILE + 2 * HALO
    q_scale = HEAD_DIM ** -0.5
    for part in range(3):
        for hd in range(N_HEADS):
            lo = part * GROUP_W + hd * HEAD_DIM
            slab = ext_ref[:, lo:lo + HEAD_DIM]
            acc = None
            for i in range(CONV_K):
                shifted = slab if i == pad else pltpu.roll(slab, (pad - i) % rows, 0)
                term = shifted[HALO:HALO + TILE] * w_ref[0, i:i + 1, lo:lo + HEAD_DIM]
                acc = term if acc is None else acc + term
            y = _silu(acc)
            if part < 2:
                inv_norm = lax.rsqrt(jnp.sum(y * y, axis=-1, keepdims=True) + NORM_EPS)
                y = y * (inv_norm * q_scale if part == 0 else inv_norm)
            o_ref[0, :, lo:lo + HEAD_DIM] = y.astype(BF16)


def _inproj_kernel(x_ref, ctx_ref, mod_ref, g_ref, w_ref, cw_ref, rope_ref,
                   qkv_ref, z_ref, ab_ref, pr_ref, rg_ref, wa_ref, wab_ref, wr_ref, ext_ref,
                   *, d_model, n_half, n_gate, tiles_per_sample):
    step = pl.program_id(0)
    t = lax.rem(jnp.minimum(step, pl.num_programs(0) - 2), tiles_per_sample)
    t_conv = lax.rem(jnp.maximum(step - 1, 0), tiles_per_sample)
    width = 3 * GROUP_W

    @pl.when(step == 0)
    def _():
        wa_ref[...] = w_ref[0, :n_half, :].T.astype(BF16)
        wr_ref[...] = w_ref[0, n_half + n_gate:, :].T.astype(BF16)
        lane = lax.broadcasted_iota(jnp.int32, (d_model, GATE_LANES), 1)
        wab_ref[...] = jnp.where(lane < n_gate, w_ref[0, n_half:n_half + GATE_LANES, :].T, 0.0).astype(BF16)
        ext_ref[...] = jnp.zeros_like(ext_ref)

    xin = jnp.where(t == 0, ctx_ref[0], x_ref[0])
    mod = mod_ref[0]
    h = _rms_mod(xin, g_ref[...], mod[:, 0:d_model], mod[:, d_model:2 * d_model]).astype(BF16)
    pa = _dot(h, wa_ref[...])
    z_ref[0] = pa[:, width:].astype(BF16)

    has_next = jnp.logical_and(t_conv >= 1, t_conv < tiles_per_sample - 1)
    ext_ref[HALO + TILE:, :] = jnp.where(has_next, pa[:HALO, :width], 0.0)
    _conv_tile(ext_ref, cw_ref, qkv_ref)
    ext_ref[0:HALO, :] = jnp.where(t >= 2, ext_ref[TILE:HALO + TILE, :], 0.0)
    ext_ref[HALO:HALO + TILE, :] = pa[:, :width]

    ab_ref[0] = _dot(h, wab_ref[...])
    pr = _dot(h, wr_ref[...])
    cos2 = rope_ref[:, :HEAD_DIM]
    sin2 = rope_ref[:, HEAD_DIM:]
    k_scale = HEAD_DIM ** -0.5
    for hd in range(2 * N_HEADS):
        lo = hd * HEAD_DIM
        tt = pr[:, lo:lo + HEAD_DIM]
        rot = tt * cos2 + pltpu.roll(tt, HEAD_DIM // 2, 1) * sin2
        if hd >= N_HEADS:
            rot = rot * k_scale
        pr_ref[0, :, lo:lo + HEAD_DIM] = rot.astype(BF16)
    pr_ref[0, :, 2 * GROUP_W:] = pr[:, 2 * GROUP_W:width].astype(BF16)
    rg_ref[0] = pr[:, width:].astype(BF16)


def _inproj(x, ctx, mod3, g, w_in, conv_w, n_half, n_gate, rope):
    b, seq, d = x.shape
    n_tiles = (ctx.shape[1] + seq) // TILE
    t_total = n_tiles * TILE
    total = b * n_tiles
    width = 3 * GROUP_W
    assert w_in.shape[1] == 2 * n_half + n_gate and n_gate <= GATE_LANES and n_half == width + GROUP_W

    def now(f):
        flat = jnp.minimum(f, total - 1)
        return flat // n_tiles, flat % n_tiles

    def lagged(f):
        flat = jnp.maximum(f - 1, 0)
        return flat // n_tiles, flat % n_tiles

    def tok(cols):
        return pl.BlockSpec((1, TILE, cols), lambda f: (now(f)[0], now(f)[1], 0))

    def lat(cols):
        return pl.BlockSpec((1, TILE, cols), lambda f: (now(f)[0], jnp.maximum(now(f)[1] - 1, 0), 0))

    return pl.pallas_call(
        functools.partial(_inproj_kernel, d_model=d, n_half=n_half, n_gate=n_gate, tiles_per_sample=n_tiles),
        grid=(total + 1,),
        in_specs=[pl.BlockSpec((1, TILE, d), lambda f: (now(f)[0], jnp.maximum(now(f)[1] - 1, 0), 0)),
                  pl.BlockSpec((1, TILE, d), lambda f: (now(f)[0], 0, 0)),
                  pl.BlockSpec((1, 1, mod3.shape[2]), lambda f: (jnp.where(now(f)[1] == 0, b, now(f)[0]), 0, 0)),
                  pl.BlockSpec((1, d), lambda f: (0, 0)),
                  pl.BlockSpec(w_in.shape, lambda f: (0, 0, 0), pipeline_mode=pl.Buffered(1)),
                  pl.BlockSpec(conv_w.shape, lambda f: (0, 0, 0)),
                  pl.BlockSpec((TILE, 2 * HEAD_DIM), lambda f: (now(f)[1], 0))],
        out_specs=[pl.BlockSpec((1, TILE, width), lambda f: (lagged(f)[0], lagged(f)[1], 0)),
                   lat(GROUP_W), tok(GATE_LANES), tok(width), lat(GROUP_W)],
        out_shape=[jax.ShapeDtypeStruct((b, t_total, width), BF16),
                   jax.ShapeDtypeStruct((b, seq, GROUP_W), BF16),
                   jax.ShapeDtypeStruct((b, t_total, GATE_LANES), F32),
                   jax.ShapeDtypeStruct((b, t_total, width), BF16),
                   jax.ShapeDtypeStruct((b, seq, GROUP_W), BF16)],
        scratch_shapes=[pltpu.VMEM((d, n_half), BF16), pltpu.VMEM((d, GATE_LANES), BF16),
                        pltpu.VMEM((d, n_half), BF16), pltpu.VMEM((TILE + 2 * HALO, width), F32)],
        compiler_params=_cparams(("arbitrary",)),
        name="inproj",
    )(x, ctx, mod3, g, w_in, conv_w, rope)


def _block_diag(y, bd_mask):
    yb = y.astype(BF16)
    return jnp.where(bd_mask, jnp.concatenate([yb] * N_HEADS, axis=0), jnp.zeros((), BF16))


def _packed_matmul(x, y_bd):
    return _dot(x.astype(BF16), y_bd)


def _unit_lower_inverse_stages(a_list, eye, base_mask, merge_masks, bd_mask, out):
    pm = _packed_matmul
    ps = [-jnp.where(base_mask, a, 0.0) for a in a_list]
    xs = [eye + p for p in ps]
    ps = [p.astype(BF16) for p in ps]
    ps = [pm(p, _block_diag(p, bd_mask)).astype(BF16) for p in ps]
    yield
    for _ in range(INV_BASE.bit_length() - 3):
        prods = [_dot(jnp.concatenate([x.astype(BF16), p], axis=0), _block_diag(p, bd_mask))
                 for x, p in zip(xs, ps)]
        xs = [x + pr[:CHUNK] for x, pr in zip(xs, prods)]
        ps = [pr[CHUNK:].astype(BF16) for pr in prods]
        yield
    xs = [x + pm(x, _block_diag(p, bd_mask)) for x, p in zip(xs, ps)]
    yield
    for m in merge_masks:
        zs = [pm(jnp.where(m, a, 0.0), _block_diag(x, bd_mask)).astype(BF16) for a, x in zip(a_list, xs)]
        yield
        xs = [x - pm(x, _block_diag(z, bd_mask)) for x, z in zip(xs, zs)]
        yield
    out.extend(xs)


def _gdn_front_stages(groups, ring_base, mid_base, a_scr, rhs_scr, brow_scr, wq_scr, qkd_scr, kdt_scr,
                      el_scr):
    lane = lax.broadcasted_iota(jnp.int32, (CHUNK, GATE_LANES), 1)
    low_half = lax.broadcasted_iota(jnp.int32, (CHUNK, HEAD_DIM), 1) < CHUNK
    zero_k = jnp.zeros((CHUNK, HEAD_DIM), BF16)
    for g in groups:
        tcum = g["consts"][0]
        g_hi, g_lo = _split(g["gates"])
        g["gc"] = _dot(tcum, g_hi) + _dot(tcum, g_lo)
    yield

    for gi, g in enumerate(groups):
        g_lane0 = N_HEADS if g["rev"] else 0
        b_lane0 = 2 * N_HEADS + g_lane0
        last = 0 if g["rev"] else CHUNK - 1
        gc, gates = g["gc"], g["gates"]
        ct = jnp.where(lane < 2 * N_HEADS, gc, gates).T
        egl_t = jnp.exp(ct[:, last:last + 1] - ct)
        gcb, kb, ms = [], [], []
        for h in range(N_HEADS):
            lo = h * HEAD_DIM
            gl = g_lane0 + h
            ring_unit = (ring_base + gi) * N_HEADS + h
            gcb.append(_lane_bcast(gc[:, gl:gl + 1], HEAD_DIM))
            beta = _lane_bcast(gates[:, b_lane0 + h:b_lane0 + h + 1], HEAD_DIM)
            eg_h = jnp.exp(gcb[h])
            q_h = g["q"][:, lo:lo + HEAD_DIM]
            k_h = g["k"][:, lo:lo + HEAD_DIM].astype(F32)
            kb.append((k_h * beta).astype(BF16))
            rhs_scr[(mid_base + gi) * N_HEADS + h] = jnp.concatenate(
                [g["v"][:, lo:lo + HEAD_DIM], (k_h * eg_h).astype(BF16)], axis=1)
            wq_scr[ring_unit, CHUNK:, :] = (q_h * eg_h).astype(BF16)
            kdt_scr[ring_unit] = (k_h.T * egl_t[gl:gl + 1, :]).astype(BF16)
        d_parts = []
        for p in range(N_HEADS // 2):
            h0, h1 = 2 * p, 2 * p + 1
            lo = h0 * HEAD_DIM
            lhs = jnp.concatenate([g["q"][:, lo:lo + 2 * HEAD_DIM], jnp.concatenate([kb[h0], kb[h1]], axis=1)],
                                  axis=0)
            k0 = g["k"][:, lo:lo + HEAD_DIM]
            k1 = g["k"][:, lo + HEAD_DIM:lo + 2 * HEAD_DIM]
            rhs_bd = jnp.concatenate([jnp.concatenate([k0, zero_k], axis=1),
                                      jnp.concatenate([zero_k, k1], axis=1)], axis=0)
            ms.append(_dot_nt(lhs, rhs_bd))
            g_row = jnp.concatenate([ct[g_lane0 + h0:g_lane0 + h0 + 1, :], ct[g_lane0 + h1:g_lane0 + h1 + 1, :]],
                                    axis=1)
            d_parts.append(jnp.where(low_half, gcb[h0], gcb[h1]) - g_row)
        incl4, strict4 = g["consts"][1:]
        e = jnp.exp(jnp.where(incl4, jnp.concatenate(d_parts, axis=1), 0.0))
        a_scr[mid_base + gi] = jnp.where(strict4, jnp.concatenate([m[CHUNK:] for m in ms], axis=1) * e, 0.0)
        qkd_scr[ring_base + gi] = jnp.where(incl4, jnp.concatenate([m[:CHUNK] for m in ms], axis=1) * e,
                                            0.0).astype(BF16)
        el_scr[ring_base + gi] = jnp.exp(gc[last:last + 1, :])
        brow_scr[mid_base + gi] = jnp.concatenate(
            [ct[b_lane0 + h:b_lane0 + h + 1, :] for h in range(N_HEADS)], axis=1)
        yield


def _gdn_solve_stages(n_groups, masks, ring_base, mid_base, a_scr, rhs_scr, brow_scr, u_scr, wq_scr):
    eye4, base_mask, merge_masks, bd_mask = masks
    t_invs = []
    yield from _unit_lower_inverse_stages([a_scr[mid_base + gi] for gi in range(n_groups)],
                                          eye4, base_mask, merge_masks, bd_mask, t_invs)
    for gi, t_inv in enumerate(t_invs):
        t_b = (t_inv * brow_scr[mid_base + gi]).astype(BF16)
        for h in range(N_HEADS):
            sol = _dot(t_b[:, h * CHUNK:(h + 1) * CHUNK], rhs_scr[(mid_base + gi) * N_HEADS + h])
            ring_unit = (ring_base + gi) * N_HEADS + h
            u_scr[ring_unit] = sol[:, :HEAD_DIM]
            wq_scr[ring_unit, :CHUNK, :] = sol[:, HEAD_DIM:].astype(BF16)


def _gdn_scan_stages(slot_base, keep_state, u_scr, wq_scr, qkd_scr, kdt_scr, el_scr, s_ref, of_ref, ob_ref):
    for j in range(CHUNKS_PER_TILE):
        units = [(slot_base + 2 * j + d, h, d * N_HEADS + h) for d in range(2) for h in range(N_HEADS)]
        states = [jnp.where(keep_state, s_ref[slot], 0.0) if j == 0 else s_ref[slot] for _, _, slot in units]
        wss = [_dot(wq_scr[gi * N_HEADS + h], s.astype(BF16)) for (gi, h, _), s in zip(units, states)]
        yield
        v_news = [(u_scr[gi * N_HEADS + h] - ws[:CHUNK]).astype(BF16) for (gi, h, _), ws in zip(units, wss)]
        outs = [ws[CHUNK:] + _dot(qkd_scr[gi][:, h * CHUNK:(h + 1) * CHUNK], vn)
                for (gi, h, _), ws, vn in zip(units, wss, v_news)]
        for (gi, h, slot), s, vn in zip(units, states, v_news):
            s_ref[slot] = el_scr[gi][:, slot:slot + 1] * s + _dot(kdt_scr[gi * N_HEADS + h], vn)
        rf = j * CHUNK
        rb = (CHUNKS_PER_TILE - 1 - j) * CHUNK
        of_ref[0, rf:rf + CHUNK, :] = jnp.concatenate(outs[:N_HEADS], axis=1).astype(BF16)
        ob_ref[0, rb:rb + CHUNK, :] = jnp.concatenate(outs[N_HEADS:], axis=1).astype(BF16)
        yield


def _interleave(*stage_generators):
    live = list(stage_generators)
    while live:
        for gen in list(live):
            try:
                next(gen)
            except StopIteration:
                live.remove(gen)


def _gate_tile(ab, dtb, a_log):
    z = ab + dtb
    softplus = jnp.maximum(z, 0.0) + jnp.log(1.0 + jnp.exp(-jnp.abs(z)))
    lane = lax.broadcasted_iota(jnp.int32, ab.shape, 1)
    return jnp.where(lane < 2 * N_HEADS, -jnp.exp(a_log) * softplus, jax.nn.sigmoid(ab))


def _gdn_kernel(qkvf_ref, abf_ref, qkvb_ref, abb_ref, dtb_ref, alog_ref, rqkvf_ref, rqkvb_ref, logit_ref,
                w1_ref, w2_ref, w3_ref, of_ref, ob_ref, rof_ref, rob_ref, w1b_ref, w2b_ref, w3b_ref,
                s_ref, a_scr, rhs_scr, brow_scr, u_scr, wq_scr, qkd_scr, kdt_scr, el_scr,
                r_ref, decay_ref, xi_ref, zeta_ref, *, tiles_per_sample):
    step = pl.program_id(0)
    lg_row = _log_sigmoid(logit_ref[...])
    for src, dst in ((w1_ref, w1b_ref), (w2_ref, w2b_ref), (w3_ref, w3b_ref)):
        dst[...] = src[0].astype(BF16)
    keep_state = lax.rem(step + tiles_per_sample - 2, tiles_per_sample) != 0
    groups_per_tile = 2 * CHUNKS_PER_TILE
    front_ring = lax.rem(step, 3) * groups_per_tile
    solve_ring = lax.rem(step + 2, 3) * groups_per_tile
    scan_ring = lax.rem(step + 1, 3) * groups_per_tile
    front_mid = lax.rem(step, 2) * groups_per_tile
    solve_mid = groups_per_tile - front_mid

    @pl.when(step == 0)
    def _():
        for scr in (s_ref, a_scr, rhs_scr, brow_scr, u_scr, wq_scr, qkd_scr, kdt_scr, el_scr, r_ref):
            scr[...] = jnp.zeros_like(scr)
        _ret_tables(lg_row, decay_ref, xi_ref, zeta_ref)

    ri = lax.broadcasted_iota(jnp.int32, (CHUNK, CHUNK), 0)
    ci = lax.broadcasted_iota(jnp.int32, (CHUNK, CHUNK), 1)
    ri4 = lax.broadcasted_iota(jnp.int32, (CHUNK, N_HEADS * CHUNK), 0)
    ci4 = lax.broadcasted_iota(jnp.int32, (CHUNK, N_HEADS * CHUNK), 1) % CHUNK
    rbd = lax.broadcasted_iota(jnp.int32, (N_HEADS * CHUNK, N_HEADS * CHUNK), 0) // CHUNK
    cbd = lax.broadcasted_iota(jnp.int32, (N_HEADS * CHUNK, N_HEADS * CHUNK), 1) // CHUNK
    bd_mask = rbd == cbd
    eye4 = (ri4 == ci4).astype(F32)
    base_mask = (ri4 // INV_BASE) == (ci4 // INV_BASE)
    merge_masks = []
    size = 2 * INV_BASE
    while size <= CHUNK:
        merge_masks.append(jnp.logical_and((ri4 // size) == (ci4 // size),
                                           (ri4 // (size // 2)) != (ci4 // (size // 2))))
        size *= 2

    cf = ((ci <= ri).astype(BF16), ri4 >= ci4, ri4 > ci4)
    cb = ((ci >= ri).astype(BF16), ri4 <= ci4, ri4 < ci4)
    dtb = dtb_ref[...]
    alog = alog_ref[...]

    groups = []
    for j in range(CHUNKS_PER_TILE):
        rf = j * CHUNK
        rb = (CHUNKS_PER_TILE - 1 - j) * CHUNK
        for ref, ab_ref, r0, rev, consts in ((qkvf_ref, abf_ref, rf, False, cf), (qkvb_ref, abb_ref, rb, True, cb)):
            groups.append(dict(q=ref[0, r0:r0 + CHUNK, 0:GROUP_W], k=ref[0, r0:r0 + CHUNK, GROUP_W:2 * GROUP_W],
                               v=ref[0, r0:r0 + CHUNK, 2 * GROUP_W:],
                               gates=_gate_tile(ab_ref[0, r0:r0 + CHUNK, :], dtb, alog), rev=rev, consts=consts))
    _interleave(
        _ret_stages(rqkvf_ref, rqkvb_ref, rof_ref, rob_ref, lg_row, keep_state, r_ref, decay_ref, xi_ref, zeta_ref),
        _gdn_solve_stages(len(groups), (eye4, base_mask, merge_masks, bd_mask), solve_ring, solve_mid,
                          a_scr, rhs_scr, brow_scr, u_scr, wq_scr),
        _gdn_scan_stages(scan_ring, keep_state, u_scr, wq_scr, qkd_scr, kdt_scr, el_scr, s_ref, of_ref, ob_ref),
        _gdn_front_stages(groups, front_ring, front_mid, a_scr, rhs_scr, brow_scr, wq_scr, qkd_scr, kdt_scr,
                          el_scr))


def _bwd_tile(s, n_tiles):
    return jnp.where(s == 0, 0, n_tiles - s)


def _latent_block(tile, bwd, n_tiles):
    return jnp.where(tile == 0, n_tiles - 2 if bwd else 0, tile - 1)


def _rider(w, n_steps):
    _, rows, cols = w.shape
    block = next(r for r in range(16, rows + 1, 16) if rows % r == 0 and rows // r <= n_steps)
    last = rows // block - 1
    return (pl.BlockSpec((1, block, cols), lambda f: (0, jnp.minimum(f, last), 0)),
            pl.BlockSpec((block, cols), lambda f: (jnp.minimum(f, last), 0)),
            jax.ShapeDtypeStruct((rows, cols), BF16))


def _mixers(qkv, ab, dtb_row, alog_row, pr, logit_row, cast_weights):
    b, t_total, _ = qkv.shape
    n_tiles = t_total // TILE

    total = b * n_tiles
    riders = [_rider(w, total + 2) for w in cast_weights]

    def block_of(f, bwd, lag, latent):
        flat = jnp.maximum(f - 2, 0) if lag else jnp.minimum(f, total - 1)
        pos = flat % n_tiles
        tile = _bwd_tile(pos, n_tiles) if bwd else pos
        return (flat // n_tiles, _latent_block(tile, bwd, n_tiles) if latent else tile, 0)

    def tok(width, bwd, lag=False, latent=False):
        return pl.BlockSpec((1, TILE, width), lambda f: block_of(f, bwd, lag, latent))

    row = pl.BlockSpec((1, GATE_LANES), lambda f: (0, 0))
    per_tile = 2 * CHUNKS_PER_TILE
    n_groups = 3 * per_tile
    n_units = n_groups * N_HEADS
    n_mid = 2 * per_tile
    return pl.pallas_call(
        functools.partial(_gdn_kernel, tiles_per_sample=n_tiles),
        grid=(total + 2,),
        in_specs=[tok(3 * GROUP_W, False), tok(GATE_LANES, False),
                  tok(3 * GROUP_W, True), tok(GATE_LANES, True), row, row,
                  tok(3 * GROUP_W, False, lag=True), tok(3 * GROUP_W, True, lag=True), row]
        + [r[0] for r in riders],
        out_specs=[tok(GROUP_W, bwd, lag=True, latent=True) for bwd in (False, True, False, True)]
        + [r[1] for r in riders],
        out_shape=[jax.ShapeDtypeStruct((b, t_total - TILE, GROUP_W), BF16)] * 4 + [r[2] for r in riders],
        scratch_shapes=[pltpu.VMEM((2 * N_HEADS, HEAD_DIM, HEAD_DIM), F32),
                        pltpu.VMEM((n_mid, CHUNK, N_HEADS * CHUNK), F32),
                        pltpu.VMEM((n_mid * N_HEADS, CHUNK, 2 * HEAD_DIM), BF16),
                        pltpu.VMEM((n_mid, 1, N_HEADS * CHUNK), F32),
                        pltpu.VMEM((n_units, CHUNK, HEAD_DIM), F32),
                        pltpu.VMEM((n_units, 2 * CHUNK, HEAD_DIM), BF16),
                        pltpu.VMEM((n_groups, CHUNK, N_HEADS * CHUNK), BF16),
                        pltpu.VMEM((n_units, HEAD_DIM, CHUNK), BF16),
                        pltpu.VMEM((n_groups, 1, GATE_LANES), F32),
                        pltpu.VMEM((2 * N_HEADS, HEAD_DIM, HEAD_DIM), F32),
                        pltpu.VMEM((2 * N_HEADS, TILE, TILE), F32),
                        pltpu.VMEM((2 * N_HEADS, TILE, HEAD_DIM), F32),
                        pltpu.VMEM((2 * N_HEADS, TILE, HEAD_DIM), F32)],
        compiler_params=_cparams(("arbitrary",)),
        name="mixers",
    )(qkv, ab, qkv, ab, dtb_row, alog_row, pr, pr, logit_row, *cast_weights)


def _log_sigmoid(x):
    return jnp.minimum(x, 0.0) - jnp.log(1.0 + jnp.exp(-jnp.abs(x)))


def _ret_tables(lg_row, decay_ref, xi_ref, zeta_ref):
    ri = lax.broadcasted_iota(jnp.int32, (TILE, TILE), 0)
    ci = lax.broadcasted_iota(jnp.int32, (TILE, TILE), 1)
    row = lax.broadcasted_iota(jnp.int32, (TILE, HEAD_DIM), 0).astype(F32)
    for slot in range(2 * N_HEADS):
        rev = slot >= N_HEADS
        lg = lg_row[:, slot:slot + 1]
        diff = ((ci - ri) if rev else (ri - ci)).astype(F32)
        pos = ((TILE - 1.0) - row) if rev else row
        decay_ref[slot] = jnp.where(diff >= 0, jnp.exp(lg * jnp.maximum(diff, 0.0)), 0.0)
        xi_ref[slot] = jnp.exp(lg * (pos + 1.0))
        zeta_ref[slot] = jnp.exp(lg * ((TILE - 1.0) - pos))


def _ret_stages(qkvf_ref, qkvb_ref, of_ref, ob_ref, lg_row, keep_state, r_ref, decay_ref, xi_ref, zeta_ref):
    units = [(refs, (N_HEADS if rev else 0) + h, h * HEAD_DIM)
             for refs, rev in (((qkvf_ref, of_ref), False), ((qkvb_ref, ob_ref), True))
             for h in range(N_HEADS)]
    qs = [refs[0][0, :, lo:lo + HEAD_DIM] for refs, _, lo in units]
    ks = [refs[0][0, :, GROUP_W + lo:GROUP_W + lo + HEAD_DIM] for refs, _, lo in units]
    vs = [refs[0][0, :, 2 * GROUP_W + lo:2 * GROUP_W + lo + HEAD_DIM] for refs, _, lo in units]
    per_yield = 2
    qks = []
    for i, (q, k) in enumerate(zip(qs, ks)):
        qks.append(_dot_nt(q, k))
        if i % per_yield == per_yield - 1:
            yield
    rs = [jnp.where(keep_state, r_ref[slot], 0.0) for _, slot, _ in units]
    for i, ((refs, slot, lo), q, k, v, qk, r) in enumerate(zip(units, qs, ks, vs, qks, rs)):
        lhs = jnp.concatenate([(qk * decay_ref[slot]).astype(BF16), (k * zeta_ref[slot]).T.astype(BF16)], axis=0)
        both = _dot(lhs, v)
        refs[1][0, :, lo:lo + HEAD_DIM] = (
            both[:TILE] + _dot((q * xi_ref[slot]).astype(BF16), r.astype(BF16))).astype(BF16)
        g_chunk = jnp.exp(lg_row[:, slot:slot + 1] * float(TILE))
        r_ref[slot] = g_chunk * r + both[TILE:]
        if i % per_yield == per_yield - 1:
            yield


def _tail_kernel(gf_ref, gb_ref, z_ref, rf_ref, rb_ref, rg_ref, x_ref, mod_ref, gdn_g_ref, ret_g_ref,
                 nffn_g_ref, w_ref, wi_ref, wo_ref, fg_ref, o_ref, y_ref, *, d_model, d_ff):
    o = gf_ref[0].astype(F32) + gb_ref[0].astype(F32)
    r = rf_ref[0].astype(F32) + rb_ref[0].astype(F32)
    z = z_ref[0].astype(F32)
    rg = rg_ref[0].astype(F32)
    for h in range(N_HEADS):
        lo = h * HEAD_DIM
        o_h = o[:, lo:lo + HEAD_DIM]
        o_n = o_h * lax.rsqrt(jnp.mean(o_h * o_h, axis=-1, keepdims=True) + NORM_EPS) * gdn_g_ref[...]
        y_ref[:, lo:lo + HEAD_DIM] = (o_n * _silu(z[:, lo:lo + HEAD_DIM])).astype(BF16)
        r_h = r[:, lo:lo + HEAD_DIM]
        mu = jnp.mean(r_h, axis=-1, keepdims=True)
        cen = r_h - mu
        var = jnp.mean(cen * cen, axis=-1, keepdims=True)
        r_n = cen * lax.rsqrt(var + NORM_EPS) * ret_g_ref[...]
        y_ref[:, GROUP_W + lo:GROUP_W + lo + HEAD_DIM] = (r_n * _silu(rg[:, lo:lo + HEAD_DIM])).astype(BF16)
    mod = mod_ref[0]
    x1 = x_ref[0] + mod[:, 2 * d_model:3 * d_model] * _dot(y_ref[...], w_ref[...])
    h2 = _rms_mod(x1, nffn_g_ref[...], mod[:, 3 * d_model:4 * d_model],
                  mod[:, 4 * d_model:5 * d_model]).astype(BF16)
    starts = list(range(0, d_ff, FFN_CHUNK))

    def gate_up(lo):
        hi = min(lo + FFN_CHUNK, d_ff)
        return _dot(h2, wi_ref[:, lo:hi]), _dot(h2, wi_ref[:, d_ff + lo:d_ff + hi])

    pending = [gate_up(starts[0])]
    acc = None
    for c, lo in enumerate(starts):
        if c + 1 < len(starts):
            pending.append(gate_up(starts[c + 1]))
        gate, up = pending.pop(0)
        hi = min(lo + FFN_CHUNK, d_ff)
        part = _dot((_silu(gate) * up).astype(BF16), wo_ref[lo:hi, :])
        acc = part if acc is None else acc + part
    x2 = x1 + mod[:, 5 * d_model:6 * d_model] * acc
    ms = jnp.mean(x2 * x2, axis=-1, keepdims=True)
    o_ref[0] = x2 * lax.rsqrt(ms + NORM_EPS) * fg_ref[...]


def _tail(gdn_f, gdn_b, z, ret_f, ret_b, rg, x, mod3, gdn_g, ret_g, nffn_g, w_out, w_ffn_in, w_ffn_out,
          final_g):
    b, seq, d = x.shape
    d_ff = w_ffn_out.shape[0]
    tok = pl.BlockSpec((1, TAIL_TOK, GROUP_W), lambda i, t: (i, t, 0))

    def resident(shape):
        return pl.BlockSpec(shape, lambda i, t: (0,) * len(shape), pipeline_mode=pl.Buffered(1))

    return pl.pallas_call(
        functools.partial(_tail_kernel, d_model=d, d_ff=d_ff),
        grid=(b, seq // TAIL_TOK),
        in_specs=[tok, tok, tok, tok, tok, tok,
                  pl.BlockSpec((1, TAIL_TOK, d), lambda i, t: (i, t, 0)),
                  pl.BlockSpec((1, 1, mod3.shape[2]), lambda i, t: (i, 0, 0)),
                  resident((1, HEAD_DIM)), resident((1, HEAD_DIM)), resident((1, d)),
                  resident(w_out.shape), resident(w_ffn_in.shape), resident(w_ffn_out.shape),
                  resident((1, d))],
        out_specs=pl.BlockSpec((1, TAIL_TOK, d), lambda i, t: (i, t, 0)),
        out_shape=jax.ShapeDtypeStruct((b, seq, d), F32),
        scratch_shapes=[pltpu.VMEM((TAIL_TOK, 2 * GROUP_W), BF16)],
        compiler_params=_cparams(("parallel", "parallel")),
        name="tail",
    )(gdn_f, gdn_b, z, ret_f, ret_b, rg, x, mod3, gdn_g, ret_g, nffn_g, w_out, w_ffn_in, w_ffn_out, final_g)


def _rope_tables(ctx_len, n_lat):
    def angles(pos, n_pairs):
        inv = ROPE_THETA ** (-np.arange(n_pairs, dtype=np.float64) / n_pairs)
        return pos[:, None] * inv[None, :]

    rows = n_lat // GRID_W
    row = np.repeat(np.arange(rows, dtype=np.float64), GRID_W)
    col = np.tile(np.arange(GRID_W, dtype=np.float64), rows)
    zeros = np.zeros((ctx_len,), np.float64)
    p_seq = np.concatenate([np.arange(ctx_len, dtype=np.float64), np.full((n_lat,), float(ctx_len))])
    ang = np.concatenate([angles(p_seq, ROPE_PAIRS[0]),
                          angles(np.concatenate([zeros, row]), ROPE_PAIRS[1]),
                          angles(np.concatenate([zeros, col]), ROPE_PAIRS[2])], axis=-1)
    cos, sin = np.cos(ang).astype(np.float32), np.sin(ang).astype(np.float32)
    return jnp.asarray(np.concatenate([cos, cos, -sin, sin], axis=-1))


def _lane_row(values):
    flat = values.reshape(1, -1).astype(F32)
    return jnp.pad(flat, ((0, 0), (0, GATE_LANES - flat.shape[1])))


def kernel(x, c, ctx, c_ctx, ada_w, ada_b, norm_mix_g, norm_ffn_g, w_in, conv_w, gdn_a_log, gdn_dt_bias,
           gdn_norm_g, ret_decay_logit, ret_norm_g, w_out, w_ffn_in, w_ffn_out, final_g):
    assert ada_w.shape[0] == 1, "single-layer block"
    b, seq, d = x.shape
    ctx_len = ctx.shape[1]
    assert ctx_len == TILE and seq % TILE == 0 and seq % TAIL_TOK == 0 and b + 1 <= 8

    cond = jnp.concatenate([c, c_ctx[None, :], jnp.zeros((8 - b - 1, d), F32)], axis=0)
    mod = _ada(cond, ada_w, ada_b)
    mod3 = mod[:, None, :]

    n_qkvz = 4 * GROUP_W
    n_gate = 4 * N_HEADS
    qkv, z, ab, pr, rg = _inproj(x, ctx, mod3, norm_mix_g, jnp.swapaxes(w_in, 1, 2), conv_w, n_qkvz, n_gate,
                                 _rope_tables(ctx_len, seq))

    dtb_row = _lane_row(gdn_dt_bias[0])
    gdn_f, gdn_b, ret_f, ret_b, w_out_b, w_ffn_in_b, w_ffn_out_b = _mixers(
        qkv, ab, dtb_row, _lane_row(gdn_a_log[0]), pr, _lane_row(ret_decay_logit[0]),
        (w_out, w_ffn_in, w_ffn_out))

    return _tail(gdn_f, gdn_b, z, ret_f, ret_b, rg, x, mod3, gdn_norm_g, ret_norm_g, norm_ffn_g,
                 w_out_b, w_ffn_in_b, w_ffn_out_b, final_g[None, :])
```

```python
import functools

import jax
import jax.numpy as jnp
import numpy as np
from jax import lax
from jax.experimental import pallas as pl
from jax.experimental.pallas import tpu as pltpu

F32 = jnp.float32
BF16 = jnp.bfloat16

HEAD_DIM = 128
N_HEADS = 4
GROUP_W = N_HEADS * HEAD_DIM
CONV_K = 5
GRID_W = 64
ROPE_THETA = 10000.0
ROPE_PAIRS = (16, 24, 24)
NORM_EPS = 1e-6

CHUNK = 64
INV_BASE = 8
TILE = 256
CHUNKS_PER_TILE = TILE // CHUNK
HALO = 16
GATE_LANES = 128
TAIL_TOK = 512
FFN_CHUNK = 1024
VMEM_LIMIT = 56 * 1024 * 1024


def _cparams(sem):
    return pltpu.CompilerParams(dimension_semantics=sem, vmem_limit_bytes=VMEM_LIMIT)


def _dot(a, b):
    return jnp.dot(a, b, preferred_element_type=F32)


def _dot_nt(a, b):
    return lax.dot_general(a, b, (((1,), (1,)), ((), ())), preferred_element_type=F32)


def _split(a):
    hi = a.astype(BF16)
    lo = (a - hi.astype(F32)).astype(BF16)
    return hi, lo


def _silu(x):
    return x * jax.nn.sigmoid(x)


def _lane_bcast(col, width):
    return jnp.broadcast_to(col, (col.shape[0], width))


def _ada_kernel(cond_ref, w_ref, b_ref, o_ref):
    a_hi, a_lo = _split(_silu(cond_ref[...]))
    w_hi, w_lo = _split(w_ref[0])
    o_ref[...] = _dot(a_hi, w_hi) + _dot(a_lo, w_hi) + _dot(a_hi, w_lo) + b_ref[...]


def _ada(cond, w, b):
    rows, d = cond.shape
    n = w.shape[2]
    bn = 1536
    return pl.pallas_call(
        _ada_kernel,
        grid=(n // bn,),
        in_specs=[pl.BlockSpec((rows, d), lambda j: (0, 0)),
                  pl.BlockSpec((1, d, bn), lambda j: (0, 0, j)),
                  pl.BlockSpec((1, bn), lambda j: (0, j))],
        out_specs=pl.BlockSpec((rows, bn), lambda j: (0, j)),
        out_shape=jax.ShapeDtypeStruct((rows, n), F32),
        compiler_params=pltpu.CompilerParams(dimension_semantics=("parallel",), vmem_limit_bytes=VMEM_LIMIT,
                                             allow_input_fusion=[True, False, False]),
        name="ada",
    )(cond, w, b)


def _rms_mod(x, g, shift, scale):
    ms = jnp.mean(x * x, axis=-1, keepdims=True)
    return (x * lax.rsqrt(ms + NORM_EPS) * g) * (1.0 + scale) + shift


def _conv_tile(ext_ref, w_ref, o_ref):
    pad = (CONV_K - 1) // 2
    rows = TILE + 2 * HALO
    q_scale = HEAD_DIM ** -0.5
    for part in range(3):
        for hd in range(N_HEADS):
            lo = part * GROUP_W + hd * HEAD_DIM
            slab = ext_ref[:, lo:lo + HEAD_DIM]
            acc = None
            for i in range(CONV_K):
                shifted = slab if i == pad else pltpu.roll(slab, (pad - i) % rows, 0)
                term = shifted[HALO:HALO + TILE] * w_ref[0, i:i + 1, lo:lo + HEAD_DIM]
                acc = term if acc is None else acc + term
            y = _silu(acc)
            if part < 2:
                inv_norm = lax.rsqrt(jnp.sum(y * y, axis=-1, keepdims=True) + NORM_EPS)
                y = y * (inv_norm * q_scale if part == 0 else inv_norm)
            o_ref[0, :, lo:lo + HEAD_DIM] = y.astype(BF16)


def _inproj_kernel(x_ref, ctx_ref, mod_ref, g_ref, w_ref, cw_ref, rope_ref,
                   qkv_ref, z_ref, ab_ref, pr_ref, rg_ref, wa_ref, wab_ref, wr_ref, ext_ref,
                   *, d_model, n_half, n_gate, tiles_per_sample):
    step = pl.program_id(0)
    t = lax.rem(jnp.minimum(step, pl.num_programs(0) - 2), tiles_per_sample)
    t_conv = lax.rem(jnp.maximum(step - 1, 0), tiles_per_sample)
    width = 3 * GROUP_W

    @pl.when(step == 0)
    def _():
        wa_ref[...] = w_ref[0, :n_half, :].T.astype(BF16)
        wr_ref[...] = w_ref[0, n_half + n_gate:, :].T.astype(BF16)
        lane = lax.broadcasted_iota(jnp.int32, (d_model, GATE_LANES), 1)
        wab_ref[...] = jnp.where(lane < n_gate, w_ref[0, n_half:n_half + GATE_LANES, :].T, 0.0).astype(BF16)
        ext_ref[...] = jnp.zeros_like(ext_ref)

    xin = jnp.where(t == 0, ctx_ref[0], x_ref[0])
    mod = mod_ref[0]
    h = _rms_mod(xin, g_ref[...], mod[:, 0:d_model], mod[:, d_model:2 * d_model]).astype(BF16)
    pa = _dot(h, wa_ref[...])
    z_ref[0] = pa[:, width:].astype(BF16)

    has_next = jnp.logical_and(t_conv >= 1, t_conv < tiles_per_sample - 1)
    ext_ref[HALO + TILE:, :] = jnp.where(has_next, pa[:HALO, :width], 0.0)
    _conv_tile(ext_ref, cw_ref, qkv_ref)
    ext_ref[0:HALO, :] = jnp.where(t >= 2, ext_ref[TILE:HALO + TILE, :], 0.0)
    ext_ref[HALO:HALO + TILE, :] = pa[:, :width]

    ab_ref[0] = _dot(h, wab_ref[...])
    pr = _dot(h, wr_ref[...])
    cos2 = rope_ref[:, :HEAD_DIM]
    sin2 = rope_ref[:, HEAD_DIM:]
    k_scale = HEAD_DIM ** -0.5
    for hd in range(2 * N_HEADS):
        lo = hd * HEAD_DIM
        tt = pr[:, lo:lo + HEAD_DIM]
        rot = tt * cos2 + pltpu.roll(tt, HEAD_DIM // 2, 1) * sin2
        if hd >= N_HEADS:
            rot = rot * k_scale
        pr_ref[0, :, lo:lo + HEAD_DIM] = rot.astype(BF16)
    pr_ref[0, :, 2 * GROUP_W:] = pr[:, 2 * GROUP_W:width].astype(BF16)
    rg_ref[0] = pr[:, width:].astype(BF16)


def _inproj(x, ctx, mod3, g, w_in, conv_w, n_half, n_gate, rope):
    b, seq, d = x.shape
    n_tiles = (ctx.shape[1] + seq) // TILE
    t_total = n_tiles * TILE
    total = b * n_tiles
    width = 3 * GROUP_W
    assert w_in.shape[1] == 2 * n_half + n_gate and n_gate <= GATE_LANES and n_half == width + GROUP_W

    def now(f):
        flat = jnp.minimum(f, total - 1)
        return flat // n_tiles, flat % n_tiles

    def lagged(f):
        flat = jnp.maximum(f - 1, 0)
        return flat // n_tiles, flat % n_tiles

    def tok(cols):
        return pl.BlockSpec((1, TILE, cols), lambda f: (now(f)[0], now(f)[1], 0))

    def lat(cols):
        return pl.BlockSpec((1, TILE, cols), lambda f: (now(f)[0], jnp.maximum(now(f)[1] - 1, 0), 0))

    return pl.pallas_call(
        functools.partial(_inproj_kernel, d_model=d, n_half=n_half, n_gate=n_gate, tiles_per_sample=n_tiles),
        grid=(total + 1,),
        in_specs=[pl.BlockSpec((1, TILE, d), lambda f: (now(f)[0], jnp.maximum(now(f)[1] - 1, 0), 0)),
                  pl.BlockSpec((1, TILE, d), lambda f: (now(f)[0], 0, 0)),
                  pl.BlockSpec((1, 1, mod3.shape[2]), lambda f: (jnp.where(now(f)[1] == 0, b, now(f)[0]), 0, 0)),
                  pl.BlockSpec((1, d), lambda f: (0, 0)),
                  pl.BlockSpec(w_in.shape, lambda f: (0, 0, 0), pipeline_mode=pl.Buffered(1)),
                  pl.BlockSpec(conv_w.shape, lambda f: (0, 0, 0)),
                  pl.BlockSpec((TILE, 2 * HEAD_DIM), lambda f: (now(f)[1], 0))],
        out_specs=[pl.BlockSpec((1, TILE, width), lambda f: (lagged(f)[0], lagged(f)[1], 0)),
                   lat(GROUP_W), tok(GATE_LANES), tok(width), lat(GROUP_W)],
        out_shape=[jax.ShapeDtypeStruct((b, t_total, width), BF16),
                   jax.ShapeDtypeStruct((b, seq, GROUP_W), BF16),
                   jax.ShapeDtypeStruct((b, t_total, GATE_LANES), F32),
                   jax.ShapeDtypeStruct((b, t_total, width), BF16),
                   jax.ShapeDtypeStruct((b, seq, GROUP_W), BF16)],
        scratch_shapes=[pltpu.VMEM((d, n_half), BF16), pltpu.VMEM((d, GATE_LANES), BF16),
                        pltpu.VMEM((d, n_half), BF16), pltpu.VMEM((TILE + 2 * HALO, width), F32)],
        compiler_params=_cparams(("arbitrary",)),
        name="inproj",
    )(x, ctx, mod3, g, w_in, conv_w, rope)


def _block_diag(y, bd_mask):
    yb = y.astype(BF16)
    return jnp.where(bd_mask, jnp.concatenate([yb] * N_HEADS, axis=0), jnp.zeros((), BF16))


def _packed_matmul(x, y_bd):
    return _dot(x.astype(BF16), y_bd)


def _unit_lower_inverse_stages(a_list, eye, base_mask, merge_masks, bd_mask, out):
    pm = _packed_matmul
    ps = [-jnp.where(base_mask, a, 0.0) for a in a_list]
    xs = [eye + p for p in ps]
    ps = [p.astype(BF16) for p in ps]
    ps = [pm(p, _block_diag(p, bd_mask)).astype(BF16) for p in ps]
    yield
    for _ in range(INV_BASE.bit_length() - 3):
        prods = [_dot(jnp.concatenate([x.astype(BF16), p], axis=0), _block_diag(p, bd_mask))
                 for x, p in zip(xs, ps)]
        xs = [x + pr[:CHUNK] for x, pr in zip(xs, prods)]
        ps = [pr[CHUNK:].astype(BF16) for pr in prods]
        yield
    xs = [x + pm(x, _block_diag(p, bd_mask)) for x, p in zip(xs, ps)]
    yield
    for m in merge_masks:
        zs = [pm(jnp.where(m, a, 0.0), _block_diag(x, bd_mask)).astype(BF16) for a, x in zip(a_list, xs)]
        yield
        xs = [x - pm(x, _block_diag(z, bd_mask)) for x, z in zip(xs, zs)]
        yield
    out.extend(xs)


def _gdn_front_stages(groups, ring_base, mid_base, a_scr, rhs_scr, brow_scr, wq_scr, qkd_scr, kdt_scr,
                      el_scr):
    lane = lax.broadcasted_iota(jnp.int32, (CHUNK, GATE_LANES), 1)
    low_half = lax.broadcasted_iota(jnp.int32, (CHUNK, HEAD_DIM), 1) < CHUNK
    zero_k = jnp.zeros((CHUNK, HEAD_DIM), BF16)
    for g in groups:
        tcum = g["consts"][0]
        g_hi, g_lo = _split(g["gates"])
        g["gc"] = _dot(tcum, g_hi) + _dot(tcum, g_lo)
    yield

    for gi, g in enumerate(groups):
        g_lane0 = N_HEADS if g["rev"] else 0
        b_lane0 = 2 * N_HEADS + g_lane0
        last = 0 if g["rev"] else CHUNK - 1
        gc, gates = g["gc"], g["gates"]
        ct = jnp.where(lane < 2 * N_HEADS, gc, gates).T
        egl_t = jnp.exp(ct[:, last:last + 1] - ct)
        gcb, kb, ms = [], [], []
        for h in range(N_HEADS):
            lo = h * HEAD_DIM
            gl = g_lane0 + h
            ring_unit = (ring_base + gi) * N_HEADS + h
            gcb.append(_lane_bcast(gc[:, gl:gl + 1], HEAD_DIM))
            beta = _lane_bcast(gates[:, b_lane0 + h:b_lane0 + h + 1], HEAD_DIM)
            eg_h = jnp.exp(gcb[h])
            q_h = g["q"][:, lo:lo + HEAD_DIM]
            k_h = g["k"][:, lo:lo + HEAD_DIM].astype(F32)
            kb.append((k_h * beta).astype(BF16))
            rhs_scr[(mid_base + gi) * N_HEADS + h] = jnp.concatenate(
                [g["v"][:, lo:lo + HEAD_DIM], (k_h * eg_h).astype(BF16)], axis=1)
            wq_scr[ring_unit, CHUNK:, :] = (q_h * eg_h).astype(BF16)
            kdt_scr[ring_unit] = (k_h.T * egl_t[gl:gl + 1, :]).astype(BF16)
        d_parts = []
        for p in range(N_HEADS // 2):
            h0, h1 = 2 * p, 2 * p + 1
            lo = h0 * HEAD_DIM
            lhs = jnp.concatenate([g["q"][:, lo:lo + 2 * HEAD_DIM], jnp.concatenate([kb[h0], kb[h1]], axis=1)],
                                  axis=0)
            k0 = g["k"][:, lo:lo + HEAD_DIM]
            k1 = g["k"][:, lo + HEAD_DIM:lo + 2 * HEAD_DIM]
            rhs_bd = jnp.concatenate([jnp.concatenate([k0, zero_k], axis=1),
                                      jnp.concatenate([zero_k, k1], axis=1)], axis=0)
            ms.append(_dot_nt(lhs, rhs_bd))
            g_row = jnp.concatenate([ct[g_lane0 + h0:g_lane0 + h0 + 1, :], ct[g_lane0 + h1:g_lane0 + h1 + 1, :]],
                                    axis=1)
            d_parts.append(jnp.where(low_half, gcb[h0], gcb[h1]) - g_row)
        incl4, strict4 = g["consts"][1:]
        e = jnp.exp(jnp.where(incl4, jnp.concatenate(d_parts, axis=1), 0.0))
        a_scr[mid_base + gi] = jnp.where(strict4, jnp.concatenate([m[CHUNK:] for m in ms], axis=1) * e, 0.0)
        qkd_scr[ring_base + gi] = jnp.where(incl4, jnp.concatenate([m[:CHUNK] for m in ms], axis=1) * e,
                                            0.0).astype(BF16)
        el_scr[ring_base + gi] = jnp.exp(gc[last:last + 1, :])
        brow_scr[mid_base + gi] = jnp.concatenate(
            [ct[b_lane0 + h:b_lane0 + h + 1, :] for h in range(N_HEADS)], axis=1)
        yield


def _gdn_solve_stages(n_groups, masks, ring_base, mid_base, a_scr, rhs_scr, brow_scr, u_scr, wq_scr):
    eye4, base_mask, merge_masks, bd_mask = masks
    t_invs = []
    yield from _unit_lower_inverse_stages([a_scr[mid_base + gi] for gi in range(n_groups)],
                                          eye4, base_mask, merge_masks, bd_mask, t_invs)
    for gi, t_inv in enumerate(t_invs):
        t_b = (t_inv * brow_scr[mid_base + gi]).astype(BF16)
        for h in range(N_HEADS):
            sol = _dot(t_b[:, h * CHUNK:(h + 1) * CHUNK], rhs_scr[(mid_base + gi) * N_HEADS + h])
            ring_unit = (ring_base + gi) * N_HEADS + h
            u_scr[ring_unit] = sol[:, :HEAD_DIM]
            wq_scr[ring_unit, :CHUNK, :] = sol[:, HEAD_DIM:].astype(BF16)


def _gdn_scan_stages(slot_base, keep_state, u_scr, wq_scr, qkd_scr, kdt_scr, el_scr, s_ref, of_ref, ob_ref):
    for j in range(CHUNKS_PER_TILE):
        units = [(slot_base + 2 * j + d, h, d * N_HEADS + h) for d in range(2) for h in range(N_HEADS)]
        states = [jnp.where(keep_state, s_ref[slot], 0.0) if j == 0 else s_ref[slot] for _, _, slot in units]
        wss = [_dot(wq_scr[gi * N_HEADS + h], s.astype(BF16)) for (gi, h, _), s in zip(units, states)]
        yield
        v_news = [(u_scr[gi * N_HEADS + h] - ws[:CHUNK]).astype(BF16) for (gi, h, _), ws in zip(units, wss)]
        outs = [ws[CHUNK:] + _dot(qkd_scr[gi][:, h * CHUNK:(h + 1) * CHUNK], vn)
                for (gi, h, _), ws, vn in zip(units, wss, v_news)]
        for (gi, h, slot), s, vn in zip(units, states, v_news):
            s_ref[slot] = el_scr[gi][:, slot:slot + 1] * s + _dot(kdt_scr[gi * N_HEADS + h], vn)
        rf = j * CHUNK
        rb = (CHUNKS_PER_TILE - 1 - j) * CHUNK
        of_ref[0, rf:rf + CHUNK, :] = jnp.concatenate(outs[:N_HEADS], axis=1).astype(BF16)
        ob_ref[0, rb:rb + CHUNK, :] = jnp.concatenate(outs[N_HEADS:], axis=1).astype(BF16)
        yield


def _interleave(*stage_generators):
    live = list(stage_generators)
    while live:
        for gen in list(live):
            try:
                next(gen)
            except StopIteration:
                live.remove(gen)


def _gate_tile(ab, dtb, a_log):
    z = ab + dtb
    softplus = jnp.maximum(z, 0.0) + jnp.log(1.0 + jnp.exp(-jnp.abs(z)))
    lane = lax.broadcasted_iota(jnp.int32, ab.shape, 1)
    return jnp.where(lane < 2 * N_HEADS, -jnp.exp(a_log) * softplus, jax.nn.sigmoid(ab))


def _gdn_kernel(qkvf_ref, abf_ref, qkvb_ref, abb_ref, dtb_ref, alog_ref, rqkvf_ref, rqkvb_ref, logit_ref,
                w1_ref, w2_ref, w3_ref, of_ref, ob_ref, rof_ref, rob_ref, w1b_ref, w2b_ref, w3b_ref,
                s_ref, a_scr, rhs_scr, brow_scr, u_scr, wq_scr, qkd_scr, kdt_scr, el_scr,
                r_ref, decay_ref, xi_ref, zeta_ref, *, tiles_per_sample):
    step = pl.program_id(0)
    lg_row = _log_sigmoid(logit_ref[...])
    for src, dst in ((w1_ref, w1b_ref), (w2_ref, w2b_ref), (w3_ref, w3b_ref)):
        dst[...] = src[0].astype(BF16)
    keep_state = lax.rem(step + tiles_per_sample - 2, tiles_per_sample) != 0
    groups_per_tile = 2 * CHUNKS_PER_TILE
    front_ring = lax.rem(step, 3) * groups_per_tile
    solve_ring = lax.rem(step + 2, 3) * groups_per_tile
    scan_ring = lax.rem(step + 1, 3) * groups_per_tile
    front_mid = lax.rem(step, 2) * groups_per_tile
    solve_mid = groups_per_tile - front_mid

    @pl.when(step == 0)
    def _():
        for scr in (s_ref, a_scr, rhs_scr, brow_scr, u_scr, wq_scr, qkd_scr, kdt_scr, el_scr, r_ref):
            scr[...] = jnp.zeros_like(scr)
        _ret_tables(lg_row, decay_ref, xi_ref, zeta_ref)

    ri = lax.broadcasted_iota(jnp.int32, (CHUNK, CHUNK), 0)
    ci = lax.broadcasted_iota(jnp.int32, (CHUNK, CHUNK), 1)
    ri4 = lax.broadcasted_iota(jnp.int32, (CHUNK, N_HEADS * CHUNK), 0)
    ci4 = lax.broadcasted_iota(jnp.int32, (CHUNK, N_HEADS * CHUNK), 1) % CHUNK
    rbd = lax.broadcasted_iota(jnp.int32, (N_HEADS * CHUNK, N_HEADS * CHUNK), 0) // CHUNK
    cbd = lax.broadcasted_iota(jnp.int32, (N_HEADS * CHUNK, N_HEADS * CHUNK), 1) // CHUNK
    bd_mask = rbd == cbd
    eye4 = (ri4 == ci4).astype(F32)
    base_mask = (ri4 // INV_BASE) == (ci4 // INV_BASE)
    merge_masks = []
    size = 2 * INV_BASE
    while size <= CHUNK:
        merge_masks.append(jnp.logical_and((ri4 // size) == (ci4 // size),
                                           (ri4 // (size // 2)) != (ci4 // (size // 2))))
        size *= 2

    cf = ((ci <= ri).astype(BF16), ri4 >= ci4, ri4 > ci4)
    cb = ((ci >= ri).astype(BF16), ri4 <= ci4, ri4 < ci4)
    dtb = dtb_ref[...]
    alog = alog_ref[...]

    groups = []
    for j in range(CHUNKS_PER_TILE):
        rf = j * CHUNK
        rb = (CHUNKS_PER_TILE - 1 - j) * CHUNK
        for ref, ab_ref, r0, rev, consts in ((qkvf_ref, abf_ref, rf, False, cf), (qkvb_ref, abb_ref, rb, True, cb)):
            groups.append(dict(q=ref[0, r0:r0 + CHUNK, 0:GROUP_W], k=ref[0, r0:r0 + CHUNK, GROUP_W:2 * GROUP_W],
                               v=ref[0, r0:r0 + CHUNK, 2 * GROUP_W:],
                               gates=_gate_tile(ab_ref[0, r0:r0 + CHUNK, :], dtb, alog), rev=rev, consts=consts))
    _interleave(
        _ret_stages(rqkvf_ref, rqkvb_ref, rof_ref, rob_ref, lg_row, keep_state, r_ref, decay_ref, xi_ref, zeta_ref),
        _gdn_solve_stages(len(groups), (eye4, base_mask, merge_masks, bd_mask), solve_ring, solve_mid,
                          a_scr, rhs_scr, brow_scr, u_scr, wq_scr),
        _gdn_scan_stages(scan_ring, keep_state, u_scr, wq_scr, qkd_scr, kdt_scr, el_scr, s_ref, of_ref, ob_ref),
        _gdn_front_stages(groups, front_ring, front_mid, a_scr, rhs_scr, brow_scr, wq_scr, qkd_scr, kdt_scr,
                          el_scr))


def _bwd_tile(s, n_tiles):
    return jnp.where(s == 0, 0, n_tiles - s)


def _latent_block(tile, bwd, n_tiles):
    return jnp.where(tile == 0, n_tiles - 2 if bwd else 0, tile - 1)


def _rider(w, n_steps):
    _, rows, cols = w.shape
    block = next(r for r in range(16, rows + 1, 16) if rows % r == 0 and rows // r <= n_steps)
    last = rows // block - 1
    return (pl.BlockSpec((1, block, cols), lambda f: (0, jnp.minimum(f, last), 0)),
            pl.BlockSpec((block, cols), lambda f: (jnp.minimum(f, last), 0)),
            jax.ShapeDtypeStruct((rows, cols), BF16))


def _mixers(qkv, ab, dtb_row, alog_row, pr, logit_row, cast_weights):
    b, t_total, _ = qkv.shape
    n_tiles = t_total // TILE

    total = b * n_tiles
    riders = [_rider(w, total + 2) for w in cast_weights]

    def block_of(f, bwd, lag, latent):
        flat = jnp.maximum(f - 2, 0) if lag else jnp.minimum(f, total - 1)
        pos = flat % n_tiles
        tile = _bwd_tile(pos, n_tiles) if bwd else pos
        return (flat // n_tiles, _latent_block(tile, bwd, n_tiles) if latent else tile, 0)

    def tok(width, bwd, lag=False, latent=False):
        return pl.BlockSpec((1, TILE, width), lambda f: block_of(f, bwd, lag, latent))

    row = pl.BlockSpec((1, GATE_LANES), lambda f: (0, 0))
    per_tile = 2 * CHUNKS_PER_TILE
    n_groups = 3 * per_tile
    n_units = n_groups * N_HEADS
    n_mid = 2 * per_tile
    return pl.pallas_call(
        functools.partial(_gdn_kernel, tiles_per_sample=n_tiles),
        grid=(total + 2,),
        in_specs=[tok(3 * GROUP_W, False), tok(GATE_LANES, False),
                  tok(3 * GROUP_W, True), tok(GATE_LANES, True), row, row,
                  tok(3 * GROUP_W, False, lag=True), tok(3 * GROUP_W, True, lag=True), row]
        + [r[0] for r in riders],
        out_specs=[tok(GROUP_W, bwd, lag=True, latent=True) for bwd in (False, True, False, True)]
        + [r[1] for r in riders],
        out_shape=[jax.ShapeDtypeStruct((b, t_total - TILE, GROUP_W), BF16)] * 4 + [r[2] for r in riders],
        scratch_shapes=[pltpu.VMEM((2 * N_HEADS, HEAD_DIM, HEAD_DIM), F32),
                        pltpu.VMEM((n_mid, CHUNK, N_HEADS * CHUNK), F32),
                        pltpu.VMEM((n_mid * N_HEADS, CHUNK, 2 * HEAD_DIM), BF16),
                        pltpu.VMEM((n_mid, 1, N_HEADS * CHUNK), F32),
                        pltpu.VMEM((n_units, CHUNK, HEAD_DIM), F32),
                        pltpu.VMEM((n_units, 2 * CHUNK, HEAD_DIM), BF16),
                        pltpu.VMEM((n_groups, CHUNK, N_HEADS * CHUNK), BF16),
                        pltpu.VMEM((n_units, HEAD_DIM, CHUNK), BF16),
                        pltpu.VMEM((n_groups, 1, GATE_LANES), F32),
                        pltpu.VMEM((2 * N_HEADS, HEAD_DIM, HEAD_DIM), F32),
                        pltpu.VMEM((2 * N_HEADS, TILE, TILE), F32),
                        pltpu.VMEM((2 * N_HEADS, TILE, HEAD_DIM), F32),
                        pltpu.VMEM((2 * N_HEADS, TILE, HEAD_DIM), F32)],
        compiler_params=pltpu.CompilerParams(
            dimension_semantics=("arbitrary",), vmem_limit_bytes=VMEM_LIMIT,
            allow_input_fusion=[False, False, False, False, True, True, False, False, True, False, False, False]),
        name="mixers",
    )(qkv, ab, qkv, ab, dtb_row, alog_row, pr, pr, logit_row, *cast_weights)


def _log_sigmoid(x):
    return jnp.minimum(x, 0.0) - jnp.log(1.0 + jnp.exp(-jnp.abs(x)))


def _ret_tables(lg_row, decay_ref, xi_ref, zeta_ref):
    ri = lax.broadcasted_iota(jnp.int32, (TILE, TILE), 0)
    ci = lax.broadcasted_iota(jnp.int32, (TILE, TILE), 1)
    row = lax.broadcasted_iota(jnp.int32, (TILE, HEAD_DIM), 0).astype(F32)
    for slot in range(2 * N_HEADS):
        rev = slot >= N_HEADS
        lg = lg_row[:, slot:slot + 1]
        diff = ((ci - ri) if rev else (ri - ci)).astype(F32)
        pos = ((TILE - 1.0) - row) if rev else row
        decay_ref[slot] = jnp.where(diff >= 0, jnp.exp(lg * jnp.maximum(diff, 0.0)), 0.0)
        xi_ref[slot] = jnp.exp(lg * (pos + 1.0))
        zeta_ref[slot] = jnp.exp(lg * ((TILE - 1.0) - pos))


def _ret_stages(qkvf_ref, qkvb_ref, of_ref, ob_ref, lg_row, keep_state, r_ref, decay_ref, xi_ref, zeta_ref):
    units = [(refs, (N_HEADS if rev else 0) + h, h * HEAD_DIM)
             for refs, rev in (((qkvf_ref, of_ref), False), ((qkvb_ref, ob_ref), True))
             for h in range(N_HEADS)]
    qs = [refs[0][0, :, lo:lo + HEAD_DIM] for refs, _, lo in units]
    ks = [refs[0][0, :, GROUP_W + lo:GROUP_W + lo + HEAD_DIM] for refs, _, lo in units]
    vs = [refs[0][0, :, 2 * GROUP_W + lo:2 * GROUP_W + lo + HEAD_DIM] for refs, _, lo in units]
    per_yield = 2
    qks = []
    for i, (q, k) in enumerate(zip(qs, ks)):
        qks.append(_dot_nt(q, k))
        if i % per_yield == per_yield - 1:
            yield
    rs = [jnp.where(keep_state, r_ref[slot], 0.0) for _, slot, _ in units]
    for i, ((refs, slot, lo), q, k, v, qk, r) in enumerate(zip(units, qs, ks, vs, qks, rs)):
        lhs = jnp.concatenate([(qk * decay_ref[slot]).astype(BF16), (k * zeta_ref[slot]).T.astype(BF16)], axis=0)
        both = _dot(lhs, v)
        refs[1][0, :, lo:lo + HEAD_DIM] = (
            both[:TILE] + _dot((q * xi_ref[slot]).astype(BF16), r.astype(BF16))).astype(BF16)
        g_chunk = jnp.exp(lg_row[:, slot:slot + 1] * float(TILE))
        r_ref[slot] = g_chunk * r + both[TILE:]
        if i % per_yield == per_yield - 1:
            yield


def _tail_kernel(gf_ref, gb_ref, z_ref, rf_ref, rb_ref, rg_ref, x_ref, mod_ref, gdn_g_ref, ret_g_ref,
                 nffn_g_ref, w_ref, wi_ref, wo_ref, fg_ref, o_ref, y_ref, *, d_model, d_ff):
    o = gf_ref[0].astype(F32) + gb_ref[0].astype(F32)
    r = rf_ref[0].astype(F32) + rb_ref[0].astype(F32)
    z = z_ref[0].astype(F32)
    rg = rg_ref[0].astype(F32)
    for h in range(N_HEADS):
        lo = h * HEAD_DIM
        o_h = o[:, lo:lo + HEAD_DIM]
        o_n = o_h * lax.rsqrt(jnp.mean(o_h * o_h, axis=-1, keepdims=True) + NORM_EPS) * gdn_g_ref[...]
        y_ref[:, lo:lo + HEAD_DIM] = (o_n * _silu(z[:, lo:lo + HEAD_DIM])).astype(BF16)
        r_h = r[:, lo:lo + HEAD_DIM]
        mu = jnp.mean(r_h, axis=-1, keepdims=True)
        cen = r_h - mu
        var = jnp.mean(cen * cen, axis=-1, keepdims=True)
        r_n = cen * lax.rsqrt(var + NORM_EPS) * ret_g_ref[...]
        y_ref[:, GROUP_W + lo:GROUP_W + lo + HEAD_DIM] = (r_n * _silu(rg[:, lo:lo + HEAD_DIM])).astype(BF16)
    mod = mod_ref[0]
    x1 = x_ref[0] + mod[:, 2 * d_model:3 * d_model] * _dot(y_ref[...], w_ref[...])
    h2 = _rms_mod(x1, nffn_g_ref[...], mod[:, 3 * d_model:4 * d_model],
                  mod[:, 4 * d_model:5 * d_model]).astype(BF16)
    starts = list(range(0, d_ff, FFN_CHUNK))

    def gate_up(lo):
        hi = min(lo + FFN_CHUNK, d_ff)
        return _dot(h2, wi_ref[:, lo:hi]), _dot(h2, wi_ref[:, d_ff + lo:d_ff + hi])

    pending = [gate_up(starts[0])]
    acc = None
    for c, lo in enumerate(starts):
        if c + 1 < len(starts):
            pending.append(gate_up(starts[c + 1]))
        gate, up = pending.pop(0)
        hi = min(lo + FFN_CHUNK, d_ff)
        part = _dot((_silu(gate) * up).astype(BF16), wo_ref[lo:hi, :])
        acc = part if acc is None else acc + part
    x2 = x1 + mod[:, 5 * d_model:6 * d_model] * acc
    ms = jnp.mean(x2 * x2, axis=-1, keepdims=True)
    o_ref[0] = x2 * lax.rsqrt(ms + NORM_EPS) * fg_ref[...]


def _tail(gdn_f, gdn_b, z, ret_f, ret_b, rg, x, mod3, gdn_g, ret_g, nffn_g, w_out, w_ffn_in, w_ffn_out,
          final_g):
    b, seq, d = x.shape
    d_ff = w_ffn_out.shape[0]
    tok = pl.BlockSpec((1, TAIL_TOK, GROUP_W), lambda i, t: (i, t, 0))

    def resident(shape):
        return pl.BlockSpec(shape, lambda i, t: (0,) * len(shape), pipeline_mode=pl.Buffered(1))

    return pl.pallas_call(
        functools.partial(_tail_kernel, d_model=d, d_ff=d_ff),
        grid=(b, seq // TAIL_TOK),
        in_specs=[tok, tok, tok, tok, tok, tok,
                  pl.BlockSpec((1, TAIL_TOK, d), lambda i, t: (i, t, 0)),
                  pl.BlockSpec((1, 1, mod3.shape[2]), lambda i, t: (i, 0, 0)),
                  resident((1, HEAD_DIM)), resident((1, HEAD_DIM)), resident((1, d)),
                  resident(w_out.shape), resident(w_ffn_in.shape), resident(w_ffn_out.shape),
                  resident((1, d))],
        out_specs=pl.BlockSpec((1, TAIL_TOK, d), lambda i, t: (i, t, 0)),
        out_shape=jax.ShapeDtypeStruct((b, seq, d), F32),
        scratch_shapes=[pltpu.VMEM((TAIL_TOK, 2 * GROUP_W), BF16)],
        compiler_params=_cparams(("parallel", "parallel")),
        name="tail",
    )(gdn_f, gdn_b, z, ret_f, ret_b, rg, x, mod3, gdn_g, ret_g, nffn_g, w_out, w_ffn_in, w_ffn_out, final_g)


def _rope_tables(ctx_len, n_lat):
    def angles(pos, n_pairs):
        inv = ROPE_THETA ** (-np.arange(n_pairs, dtype=np.float64) / n_pairs)
        return pos[:, None] * inv[None, :]

    rows = n_lat // GRID_W
    row = np.repeat(np.arange(rows, dtype=np.float64), GRID_W)
    col = np.tile(np.arange(GRID_W, dtype=np.float64), rows)
    zeros = np.zeros((ctx_len,), np.float64)
    p_seq = np.concatenate([np.arange(ctx_len, dtype=np.float64), np.full((n_lat,), float(ctx_len))])
    ang = np.concatenate([angles(p_seq, ROPE_PAIRS[0]),
                          angles(np.concatenate([zeros, row]), ROPE_PAIRS[1]),
                          angles(np.concatenate([zeros, col]), ROPE_PAIRS[2])], axis=-1)
    cos, sin = np.cos(ang).astype(np.float32), np.sin(ang).astype(np.float32)
    return jnp.asarray(np.concatenate([cos, cos, -sin, sin], axis=-1))


def _lane_row(values):
    flat = values.reshape(1, -1).astype(F32)
    return jnp.pad(flat, ((0, 0), (0, GATE_LANES - flat.shape[1])))


def kernel(x, c, ctx, c_ctx, ada_w, ada_b, norm_mix_g, norm_ffn_g, w_in, conv_w, gdn_a_log, gdn_dt_bias,
           gdn_norm_g, ret_decay_logit, ret_norm_g, w_out, w_ffn_in, w_ffn_out, final_g):
    assert ada_w.shape[0] == 1, "single-layer block"
    b, seq, d = x.shape
    ctx_len = ctx.shape[1]
    assert ctx_len == TILE and seq % TILE == 0 and seq % TAIL_TOK == 0 and b + 1 <= 8

    cond = jnp.concatenate([c, c_ctx[None, :], jnp.zeros((8 - b - 1, d), F32)], axis=0)
    mod = _ada(cond, ada_w, ada_b)
    mod3 = mod[:, None, :]

    n_qkvz = 4 * GROUP_W
    n_gate = 4 * N_HEADS
    qkv, z, ab, pr, rg = _inproj(x, ctx, mod3, norm_mix_g, jnp.swapaxes(w_in, 1, 2), conv_w, n_qkvz, n_gate,
                                 _rope_tables(ctx_len, seq))

    dtb_row = _lane_row(gdn_dt_bias[0])
    gdn_f, gdn_b, ret_f, ret_b, w_out_b, w_ffn_in_b, w_ffn_out_b = _mixers(
        qkv, ab, dtb_row, _lane_row(gdn_a_log[0]), pr, _lane_row(ret_decay_logit[0]),
        (w_out, w_ffn_in, w_ffn_out))

    return _tail(gdn_f, gdn_b, z, ret_f, ret_b, rg, x, mod3, gdn_norm_g, ret_norm_g, norm_ffn_g,
                 w_out_b, w_ffn_in_b, w_ffn_out_b, final_g[None, :])
```
